```python
import math
import jax
import jax.numpy as jnp
from jax import lax
import numpy as np

D_MODEL = 2048
BATCH = 2
SEQ = 4096
DEPTH = 1

HEAD_DIM = 64
ATT_HEADS = 16
RWKV_HEADS = 16
ATT_WIDTH = ATT_HEADS * HEAD_DIM
RWKV_WIDTH = RWKV_HEADS * HEAD_DIM
MIX_WIDTH = ATT_WIDTH + RWKV_WIDTH
MOBA_BLOCK = 256
MOBA_TOPK = 3
Q_CHUNK = 128
REL_BUCKETS = 32
REL_MAX_DIST = 128
DECAY_LORA = 64
AAA_LORA = 64
GATE_LORA = 160
RWKV_COLS = 3 * RWKV_WIDTH + DECAY_LORA + AAA_LORA + GATE_LORA
IN_COLS = 3 * ATT_WIDTH + RWKV_COLS
D_FF = 5632
PLE_DIM = 256
LN_EPS = 1e-5
GN_EPS = 64e-5
NEG = -1e30
DEEPNORM_ALPHA = (2 * DEPTH) ** 0.25
DEEPNORM_BETA = (8 * DEPTH) ** -0.25

kernel_name = "hymba_moba_rwkv7_macaron_deepnorm"


def layer_norm(x, g, b):
    xf = x.astype(jnp.float32)
    mu = xf.mean(-1, keepdims=True)
    var = jnp.square(xf - mu).mean(-1, keepdims=True)
    return (xf - mu) * lax.rsqrt(var + LN_EPS) * g + b


def swiglu(x, w_gate, w_up, w_down):
    return (jax.nn.silu(x @ w_gate) * (x @ w_up)) @ w_down


def rel_bucket(dist):
    max_exact = REL_BUCKETS // 2
    n = jnp.maximum(dist, 0)
    nf = jnp.maximum(n, 1).astype(jnp.float32)
    large = max_exact + (jnp.log(nf / max_exact) / math.log(REL_MAX_DIST / max_exact)
                         * (REL_BUCKETS - max_exact)).astype(jnp.int32)
    large = jnp.minimum(large, REL_BUCKETS - 1)
    return jnp.where(n < max_exact, n, large)


def moba_attention(q, k, v, rel_bias):
    B, H, S, Dh = q.shape
    nb = -(-S // MOBA_BLOCK)
    pad = nb * MOBA_BLOCK - S
    kp = jnp.pad(k, ((0, 0), (0, 0), (0, pad), (0, 0)))
    vp = jnp.pad(v, ((0, 0), (0, 0), (0, pad), (0, 0)))
    k_blk = kp.reshape(B, H, nb, MOBA_BLOCK, Dh)
    v_blk = vp.reshape(B, H, nb, MOBA_BLOCK, Dh)
    k_mean = k_blk.mean(axis=3)
    topk = min(MOBA_TOPK, nb)
    bias_hb = rel_bias.T.astype(jnp.float32)
    head_idx = jnp.arange(H)[None, :, None, None, None]
    offs = jnp.arange(MOBA_BLOCK)
    scale = Dh ** -0.5
    gather = jax.vmap(jax.vmap(lambda blocks, idx: blocks[idx]))

    def chunk(c):
        q0 = c * Q_CHUNK
        qc = lax.dynamic_slice_in_dim(q, q0, Q_CHUNK, axis=2)
        q_pos = q0 + jnp.arange(Q_CHUNK)
        own = q0 // MOBA_BLOCK
        gate = jnp.einsum('bhqd,bhnd->bhqn', qc, k_mean)
        gate = jnp.where(jnp.arange(nb) < own, gate, -jnp.inf)
        _, sel = lax.top_k(gate, topk)
        sel_ok = jnp.arange(topk) < own
        ks = gather(k_blk, sel)
        vs = gather(v_blk, sel)
        s_sel = jnp.einsum('bhqd,bhqnld->bhqnl', qc, ks) * scale
        k_pos_sel = sel[..., None] * MOBA_BLOCK + offs
        b_sel = bias_hb[head_idx, rel_bucket(q_pos[None, None, :, None, None] - k_pos_sel)]
        s_sel = jnp.where(sel_ok[:, None], s_sel + b_sel, NEG)
        k_own = lax.dynamic_slice_in_dim(kp, own * MOBA_BLOCK, MOBA_BLOCK, axis=2)
        v_own = lax.dynamic_slice_in_dim(vp, own * MOBA_BLOCK, MOBA_BLOCK, axis=2)
        dist_own = q_pos[:, None] - (own * MOBA_BLOCK + offs)[None, :]
        s_own = jnp.einsum('bhqd,bhld->bhql', qc, k_own) * scale
        s_own = jnp.where(dist_own >= 0, s_own + bias_hb[:, rel_bucket(dist_own)], NEG)
        n_sel = topk * MOBA_BLOCK
        logits = jnp.concatenate([s_sel.reshape(B, H, Q_CHUNK, n_sel), s_own], axis=-1)
        probs = jax.nn.softmax(logits, axis=-1)
        p_sel = probs[..., :n_sel].reshape(B, H, Q_CHUNK, topk, MOBA_BLOCK)
        p_own = probs[..., n_sel:]
        return (jnp.einsum('bhqnl,bhqnld->bhqd', p_sel, vs)
                + jnp.einsum('bhql,bhld->bhqd', p_own, v_own))

    outs = lax.map(chunk, jnp.arange(S // Q_CHUNK))
    return outs.transpose(1, 2, 0, 3, 4).reshape(B, H, S, Dh)


def rwkv7_time_mix(u, shift_mix, w0, w2, a0, a2, g2, k_k, k_a, r_k, gn_g, gn_b):
    u = u.astype(jnp.float32)
    B, S, _ = u.shape
    H, N, C = RWKV_HEADS, HEAD_DIM, RWKV_WIDTH
    prev = jnp.pad(u, ((0, 0), (1, 0), (0, 0)))[:, :-1]
    u = u + (prev - u) * shift_mix
    r = u[..., 0:C]
    k = u[..., C:2 * C]
    v = u[..., 2 * C:3 * C]
    wd = u[..., 3 * C:3 * C + DECAY_LORA]
    ad = u[..., 3 * C + DECAY_LORA:3 * C + DECAY_LORA + AAA_LORA]
    gd = u[..., 3 * C + DECAY_LORA + AAA_LORA:]
    w_log = -jax.nn.softplus(-(w0 + jnp.tanh(wd) @ w2)) - 0.5
    decay = jnp.exp(-jnp.exp(w_log))
    a = jax.nn.sigmoid(a0 + ad @ a2)
    g = jax.nn.sigmoid(gd) @ g2
    kk = (k * k_k).reshape(B, S, H, N)
    kk = kk / jnp.maximum(jnp.linalg.norm(kk, axis=-1, keepdims=True), 1e-12)
    k = k * (1.0 + (a - 1.0) * k_a)
    hd = lambda t: t.reshape(B, S, H, N)
    r_h, k_h, v_h, w_h, a_h = hd(r), hd(k), hd(v), hd(decay), hd(a)

    def step(state, inp):
        r_t, w_t, k_t, v_t, aa_t, bb_t = inp
        sa = jnp.einsum('bhij,bhj->bhi', state, aa_t)
        state = (state * w_t[:, :, None, :] + sa[..., None] * bb_t[:, :, None, :]
                 + v_t[..., None] * k_t[:, :, None, :])
        return state, jnp.einsum('bhij,bhj->bhi', state, r_t)

    xs = tuple(t.transpose(1, 0, 2, 3) for t in (r_h, w_h, k_h, v_h, -kk, kk * a_h))
    _, ys = lax.scan(step, jnp.zeros((B, H, N, N), jnp.float32), xs)
    y = ys.transpose(1, 0, 2, 3)
    mu = y.mean(-1, keepdims=True)
    var = jnp.square(y - mu).mean(-1, keepdims=True)
    y = ((y - mu) * lax.rsqrt(var + GN_EPS)).reshape(B, S, C) * gn_g + gn_b
    bonus = (r_h * k_h * r_k).sum(-1, keepdims=True) * v_h
    return (y + bonus.reshape(B, S, C)) * g


def setup_inputs(seed: int = 0) -> dict:
    key = jax.random.key(seed)
    ks = iter(jax.random.split(key, 40))
    nrm = lambda shape, s: jax.random.normal(next(ks), shape, jnp.float32) * s
    gain = lambda shape: 1.0 + nrm(shape, 0.02)
    L = DEPTH
    col_scale = jnp.concatenate([
        jnp.ones((2 * ATT_WIDTH,), jnp.float32),
        jnp.full((ATT_WIDTH,), DEEPNORM_BETA, jnp.float32),
        jnp.ones((2 * RWKV_WIDTH,), jnp.float32),
        jnp.full((RWKV_WIDTH,), DEEPNORM_BETA, jnp.float32),
        jnp.ones((DECAY_LORA + AAA_LORA + GATE_LORA,), jnp.float32)])
    return {
        "x": nrm((BATCH, SEQ, D_MODEL), 1.0),
        "p": nrm((DEPTH, BATCH, SEQ, PLE_DIM), 1.0),
        "ffn1_w_gate": nrm((L, D_MODEL, D_FF), D_MODEL ** -0.5),
        "ffn1_w_up": nrm((L, D_MODEL, D_FF), D_MODEL ** -0.5),
        "ffn1_w_down": nrm((L, D_FF, D_MODEL), DEEPNORM_BETA * D_FF ** -0.5),
        "ln1_g": gain((L, D_MODEL)),
        "ln1_b": nrm((L, D_MODEL), 0.02),
        "w_in": nrm((L, D_MODEL, IN_COLS), D_MODEL ** -0.5) * col_scale,
        "rel_bias": nrm((REL_BUCKETS, ATT_HEADS), 0.5),
        "shift_mix": jax.random.uniform(next(ks), (L, RWKV_COLS), jnp.float32),
        "decay_w0": jax.random.uniform(next(ks), (L, RWKV_WIDTH), jnp.float32, -6.0, 1.0),
        "decay_w2": nrm((L, DECAY_LORA, RWKV_WIDTH), 0.5 * DECAY_LORA ** -0.5),
        "a_a0": nrm((L, RWKV_WIDTH), 0.1),
        "a_a2": nrm((L, AAA_LORA, RWKV_WIDTH), 0.5 * AAA_LORA ** -0.5),
        "gate_g2": nrm((L, GATE_LORA, RWKV_WIDTH), GATE_LORA ** -0.5),
        "k_k": 0.85 + nrm((L, RWKV_WIDTH), 0.05),
        "k_a": 1.0 + nrm((L, RWKV_WIDTH), 0.05),
        "r_k": nrm((L, RWKV_HEADS, HEAD_DIM), 0.1),
        "gn_g": gain((L, RWKV_WIDTH)),
        "gn_b": nrm((L, RWKV_WIDTH), 0.02),
        "w_out": nrm((L, MIX_WIDTH, D_MODEL), DEEPNORM_BETA * MIX_WIDTH ** -0.5),
        "ln2_g": gain((L, D_MODEL)),
        "ln2_b": nrm((L, D_MODEL), 0.02),
        "ffn2_w_gate": nrm((L, D_MODEL, D_FF), D_MODEL ** -0.5),
        "ffn2_w_up": nrm((L, D_MODEL, D_FF), D_MODEL ** -0.5),
        "ffn2_w_down": nrm((L, D_FF, D_MODEL), DEEPNORM_BETA * D_FF ** -0.5),
        "ln3_g": gain((L, D_MODEL)),
        "ln3_b": nrm((L, D_MODEL), 0.02),
        "ple_w_up": nrm((L, PLE_DIM, D_MODEL), DEEPNORM_BETA * PLE_DIM ** -0.5),
        "ple_w_gate": nrm((L, D_MODEL, D_MODEL), D_MODEL ** -0.5),
        "ple_b_gate": nrm((L, D_MODEL), 0.02),
        "ln4_g": gain((L, D_MODEL)),
        "ln4_b": nrm((L, D_MODEL), 0.02),
    }


def reference(x, p, ffn1_w_gate, ffn1_w_up, ffn1_w_down, ln1_g, ln1_b, w_in, rel_bias,
              shift_mix, decay_w0, decay_w2, a_a0, a_a2, gate_g2, k_k, k_a, r_k, gn_g, gn_b,
              w_out, ln2_g, ln2_b, ffn2_w_gate, ffn2_w_up, ffn2_w_down, ln3_g, ln3_b,
              ple_w_up, ple_w_gate, ple_b_gate, ln4_g, ln4_b):
    B, S, _ = x.shape
    h = x.astype(jnp.float32)
    alpha = DEEPNORM_ALPHA
    for i in range(DEPTH):
        h = layer_norm(alpha * h + 0.5 * swiglu(h, ffn1_w_gate[i], ffn1_w_up[i], ffn1_w_down[i]),
                       ln1_g[i], ln1_b[i])
        u = (h @ w_in[i]).astype(jnp.float32)
        heads = lambda t: t.reshape(B, S, ATT_HEADS, HEAD_DIM).transpose(0, 2, 1, 3)
        q = heads(u[..., 0:ATT_WIDTH])
        k = heads(u[..., ATT_WIDTH:2 * ATT_WIDTH])
        v = heads(u[..., 2 * ATT_WIDTH:3 * ATT_WIDTH])
        att = moba_attention(q, k, v, rel_bias)
        att = att.transpose(0, 2, 1, 3).reshape(B, S, ATT_WIDTH)
        rw = rwkv7_time_mix(u[..., 3 * ATT_WIDTH:], shift_mix[i], decay_w0[i], decay_w2[i],
                            a_a0[i], a_a2[i], gate_g2[i], k_k[i], k_a[i], r_k[i], gn_g[i], gn_b[i])
        mix = jnp.concatenate([att, rw], axis=-1) @ w_out[i]
        h = layer_norm(alpha * h + mix, ln2_g[i], ln2_b[i])
        h = layer_norm(alpha * h + 0.5 * swiglu(h, ffn2_w_gate[i], ffn2_w_up[i], ffn2_w_down[i]),
                       ln3_g[i], ln3_b[i])
        e = p[i].astype(jnp.float32) @ ple_w_up[i]
        gate = jax.nn.sigmoid(h @ ple_w_gate[i] + ple_b_gate[i])
        h = layer_norm(alpha * h + gate * e, ln4_g[i], ln4_b[i])
    return h.astype(x.dtype)
```

```python
import functools
import math

import jax
import jax.numpy as jnp
from jax import lax
from jax.experimental import pallas as pl
from jax.experimental.pallas import tpu as pltpu

D_MODEL = 2048
DEPTH = 1
HEAD_DIM = 64
ATT_HEADS = 16
RWKV_HEADS = 16
ATT_WIDTH = ATT_HEADS * HEAD_DIM
RWKV_WIDTH = RWKV_HEADS * HEAD_DIM
MOBA_BLOCK = 256
MOBA_TOPK = 3
REL_BUCKETS = 32
REL_MAX_DIST = 128
DECAY_LORA = 64
AAA_LORA = 64
GATE_LORA = 160
D_FF = 5632
PLE_DIM = 256
LN_EPS = 1e-5
GN_EPS = 64e-5
NEG = -1e30
ALPHA = (2 * DEPTH) ** 0.25

LANE = 128
PAIR = LANE // HEAD_DIM
CHUNK = 64
QUAD = 256
HEADS_PER_QUAD = QUAD // HEAD_DIM
TAIL_COLS = 512
TAIL_WD, TAIL_AD, TAIL_GD = 0, 128, 256
GD_PAD = 256
VMEM_LIMIT = 56 * 1024 * 1024

F32 = jnp.float32
BF16 = jnp.bfloat16
HI = lax.Precision.HIGHEST


def _cparams(sem):
    return pltpu.CompilerParams(dimension_semantics=sem, vmem_limit_bytes=VMEM_LIMIT)


def _layer_norm(y, g, b):
    mu = jnp.mean(y, axis=-1, keepdims=True)
    d = y - mu
    var = jnp.mean(d * d, axis=-1, keepdims=True)
    return d * lax.rsqrt(var + LN_EPS) * g + b


def _dot(a, b):
    return jnp.dot(a, b, preferred_element_type=F32)


def _dot_hi(a, b):
    return jnp.dot(a, b, preferred_element_type=F32, precision=HI)


def _dot_tb(a, b, precision=None):
    return lax.dot_general(a, b, (((1,), (1,)), ((), ())), preferred_element_type=F32,
                           precision=precision)


def _split3(x):
    hi = x.astype(BF16)
    r1 = x - hi.astype(F32)
    mid = r1.astype(BF16)
    lo = (r1 - mid.astype(F32)).astype(BF16)
    return hi, mid, lo


def _dot3_exact_rhs(x, w_bf16):
    hi, mid, lo = _split3(x)
    return _dot(hi, w_bf16) + _dot(mid, w_bf16) + _dot(lo, w_bf16)


def _dot3_exact_lhs(w_bf16, x):
    hi, mid, lo = _split3(x)
    return _dot(w_bf16, hi) + _dot(w_bf16, mid) + _dot(w_bf16, lo)


def _ffn_ln_kernel(x_ref, wg_ref, wu_ref, wd_ref, g_ref, b_ref, o_ref, xb_ref, acc_ref):
    f = pl.program_id(1)

    @pl.when(f == 0)
    def _():
        xb_ref[...] = x_ref[...].astype(BF16)
        acc_ref[...] = jnp.zeros_like(acc_ref)

    xb = xb_ref[...]
    gate = _dot(xb, wg_ref[...])
    up = _dot(xb, wu_ref[...])
    act = gate * jax.nn.sigmoid(gate) * up
    acc_ref[...] += _dot(act.astype(BF16), wd_ref[...])

    @pl.when(f == pl.num_programs(1) - 1)
    def _():
        y = ALPHA * x_ref[...] + 0.5 * acc_ref[...]
        o_ref[...] = _layer_norm(y, g_ref[...], b_ref[...])


def _ffn_ln(x, wg, wu, wd, g, b, tm=512, tf=512):
    t, d = x.shape
    ff = wg.shape[1]
    return pl.pallas_call(
        _ffn_ln_kernel,
        grid=(t // tm, ff // tf),
        in_specs=[
            pl.BlockSpec((tm, d), lambda i, f: (i, 0)),
            pl.BlockSpec((d, tf), lambda i, f: (0, f)),
            pl.BlockSpec((d, tf), lambda i, f: (0, f)),
            pl.BlockSpec((tf, d), lambda i, f: (f, 0)),
            pl.BlockSpec((1, d), lambda i, f: (0, 0)),
            pl.BlockSpec((1, d), lambda i, f: (0, 0)),
        ],
        out_specs=pl.BlockSpec((tm, d), lambda i, f: (i, 0)),
        out_shape=jax.ShapeDtypeStruct((t, d), F32),
        scratch_shapes=[pltpu.VMEM((tm, d), BF16), pltpu.VMEM((tm, d), F32)],
        compiler_params=_cparams(("parallel", "arbitrary")),
        name="ffn_ln",
    )(x, wg, wu, wd, g, b)


def _proj_kernel(x_ref, w_ref, o_ref, xb_ref):
    @pl.when(pl.program_id(1) == 0)
    def _():
        xb_ref[...] = x_ref[...].astype(BF16)

    o_ref[...] = _dot(xb_ref[...], w_ref[...])


def _proj(x, w, tm=512, tn=512):
    t, d = x.shape
    n = w.shape[1]
    return pl.pallas_call(
        _proj_kernel,
        grid=(t // tm, n // tn),
        in_specs=[
            pl.BlockSpec((tm, d), lambda i, j: (i, 0)),
            pl.BlockSpec((d, tn), lambda i, j: (0, j)),
        ],
        out_specs=pl.BlockSpec((tm, tn), lambda i, j: (i, j)),
        out_shape=jax.ShapeDtypeStruct((t, n), F32),
        scratch_shapes=[pltpu.VMEM((tm, d), BF16)],
        compiler_params=_cparams(("parallel", "arbitrary")),
        name="proj",
    )(x, w)


def _rel_bucket(dist):
    max_exact = REL_BUCKETS // 2
    n = jnp.maximum(dist, 0)
    nf = jnp.maximum(n, 1).astype(F32)
    large = max_exact + (jnp.log(nf / max_exact) / math.log(REL_MAX_DIST / max_exact)
                         * (REL_BUCKETS - max_exact)).astype(jnp.int32)
    large = jnp.minimum(large, REL_BUCKETS - 1)
    return jnp.where(n < max_exact, n, large)


def _bias_tab_kernel(rb_ref, o_ref):
    p = pl.program_id(0)
    key = lax.broadcasted_iota(jnp.int32, (MOBA_BLOCK, MOBA_BLOCK), 0)
    qry = lax.broadcasted_iota(jnp.int32, (MOBA_BLOCK, MOBA_BLOCK), 1)
    for kind in range(2):
        dist = qry - key + kind * MOBA_BLOCK
        bucket = _rel_bucket(dist)
        for hh in range(PAIR):
            h = PAIR * p + hh
            acc = jnp.zeros((MOBA_BLOCK, MOBA_BLOCK), F32)
            for bkt in range(REL_BUCKETS):
                acc = jnp.where(bucket == bkt, rb_ref[bkt, h], acc)
            if kind == 0:
                acc = jnp.where(dist >= 0, acc, NEG)
            o_ref[0, 2 * hh + kind] = acc


def _bias_tab(rel_bias):
    npair = ATT_HEADS // PAIR
    return pl.pallas_call(
        _bias_tab_kernel,
        grid=(npair,),
        in_specs=[pl.BlockSpec(memory_space=pltpu.SMEM)],
        out_specs=pl.BlockSpec((1, 2 * PAIR, MOBA_BLOCK, MOBA_BLOCK), lambda p: (p, 0, 0, 0)),
        out_shape=jax.ShapeDtypeStruct((npair, 2 * PAIR, MOBA_BLOCK, MOBA_BLOCK), F32),
        compiler_params=_cparams(("arbitrary",)),
        name="bias_tab",
    )(rel_bias)


def _moba_kernel(rb_ref, q_ref, k_ref, v_ref, tab_ref, o_ref, kb_scr, vt_scr, kmean_scr, selm_scr):
    p = pl.program_id(1)
    i = pl.program_id(2)
    nb = kb_scr.shape[0]
    blk = MOBA_BLOCK

    @pl.when(i == 0)
    def _():
        for n in range(nb):
            kblk = k_ref[0, n * blk:(n + 1) * blk, :]
            kb_scr[n] = kblk.astype(BF16)
            kmean_scr[n:n + 1, :] = jnp.mean(kblk, axis=0, keepdims=True)
            vt_scr[n] = v_ref[0, n * blk:(n + 1) * blk, :].T.astype(BF16)

    q = q_ref[0] * (HEAD_DIM ** -0.5)
    lane = lax.broadcasted_iota(jnp.int32, q.shape, 1)
    kmean = kmean_scr[...]
    nidx = lax.broadcasted_iota(jnp.int32, (nb, blk), 0)
    n_adj = jnp.maximum(i - 1, 0)

    qh, b_far = [], []
    for hh in range(PAIR):
        in_head = (lane >= hh * HEAD_DIM) & (lane < (hh + 1) * HEAD_DIM)
        q_h = jnp.where(in_head, q, 0.0)
        qh.append(q_h.astype(BF16))
        gate = _dot_tb(kmean, q_h, precision=HI)
        rank = jnp.zeros((nb, blk), jnp.int32)
        for m in range(nb):
            gm = gate[m:m + 1, :]
            beats = (gm > gate) | ((gm == gate) & (m < nidx))
            rank = rank + jnp.where(beats & (m < i), 1, 0)
        sel = (nidx < i) & (rank < MOBA_TOPK)
        selm = jnp.where(sel, 0.0, NEG)
        for n in range(nb):
            selm_scr[hh * nb + n] = selm[n:n + 1, :]
        b_far.append(rb_ref[REL_BUCKETS - 1, PAIR * p + hh])

    def scores(hh, n):
        return _dot_tb(kb_scr[n], qh[hh])

    def vt(hh, n):
        return vt_scr[n, hh * HEAD_DIM:(hh + 1) * HEAD_DIM, :]

    def update(s, m, l, acc, v_t):
        m_new = jnp.maximum(m, jnp.max(s, axis=0, keepdims=True))
        alpha = jnp.exp(m - m_new)
        pr = jnp.exp(s - m_new)
        l = alpha * l + jnp.sum(pr, axis=0, keepdims=True)
        acc = alpha * acc + _dot(v_t, pr.astype(BF16))
        return m_new, l, acc

    state = []
    for hh in range(PAIR):
        s = scores(hh, i) + tab_ref[0, 2 * hh]
        m = jnp.max(s, axis=0, keepdims=True)
        pr = jnp.exp(s - m)
        l = jnp.sum(pr, axis=0, keepdims=True)
        acc = _dot(vt(hh, i), pr.astype(BF16))
        s = scores(hh, n_adj) + tab_ref[0, 2 * hh + 1] + selm_scr[hh * nb + n_adj]
        state.extend(update(s, m, l, acc, vt(hh, n_adj)))

    def far_body(n, carry):
        out = []
        for hh in range(PAIR):
            m, l, acc = carry[3 * hh:3 * hh + 3]
            s = scores(hh, n) + (selm_scr[hh * nb + n] + b_far[hh])
            out.extend(update(s, m, l, acc, vt(hh, n)))
        return tuple(out)

    state = lax.fori_loop(0, n_adj, far_body, tuple(state))
    out_t = jnp.concatenate([state[3 * hh + 2] / state[3 * hh + 1] for hh in range(PAIR)], axis=0)
    o_ref[0] = out_t.T


def _moba(rel_bias, u_main, tab, batch, seq):
    npair = ATT_HEADS // PAIR
    nb = seq // MOBA_BLOCK
    kcol = ATT_WIDTH // LANE
    return pl.pallas_call(
        _moba_kernel,
        grid=(batch, npair, nb),
        in_specs=[
            pl.BlockSpec(memory_space=pltpu.SMEM),
            pl.BlockSpec((1, MOBA_BLOCK, LANE), lambda b, p, i: (b, i, p)),
            pl.BlockSpec((1, seq, LANE), lambda b, p, i: (b, 0, kcol + p)),
            pl.BlockSpec((1, seq, LANE), lambda b, p, i: (b, 0, 2 * kcol + p)),
            pl.BlockSpec((1, 2 * PAIR, MOBA_BLOCK, MOBA_BLOCK), lambda b, p, i: (p, 0, 0, 0)),
        ],
        out_specs=pl.BlockSpec((1, MOBA_BLOCK, LANE), lambda b, p, i: (b, i, p)),
        out_shape=jax.ShapeDtypeStruct((batch, seq, ATT_WIDTH), F32),
        scratch_shapes=[
            pltpu.VMEM((nb, MOBA_BLOCK, LANE), BF16),
            pltpu.VMEM((nb, LANE, MOBA_BLOCK), BF16),
            pltpu.VMEM((nb, LANE), F32),
            pltpu.VMEM((PAIR * nb, 1, MOBA_BLOCK), F32),
        ],
        compiler_params=_cparams(("parallel", "parallel", "arbitrary")),
        name="moba",
    )(rel_bias, u_main, u_main, u_main, tab)


def _rwkv_prep_kernel(ur_ref, uk_ref, uv_ref, ut_ref, pr_ref, pk_ref, pv_ref, pt_ref,
                      mixm_ref, mixt_ref, w0_ref, w2_ref, a0_ref, a2_ref, g2_ref, kk_ref, ka_ref,
                      esum_ref,
                      r_out, k_out, v_out, lw_out, aa_out, bb_out, g_out, *, tiles_per_seq):
    first = (pl.program_id(0) % tiles_per_seq) == 0
    width = RWKV_WIDTH

    def shifted(x_ref, p_ref, mix):
        x = x_ref[...]
        prev = pltpu.roll(x, 1, 0)
        row = lax.broadcasted_iota(jnp.int32, x.shape, 0)
        last = jnp.where(first, 0.0, p_ref[7:8, :])
        prev = jnp.where(row == 0, last, prev)
        return x + (prev - x) * mix

    r = shifted(ur_ref, pr_ref, mixm_ref[:, 0:width])
    k = shifted(uk_ref, pk_ref, mixm_ref[:, width:2 * width])
    v = shifted(uv_ref, pv_ref, mixm_ref[:, 2 * width:3 * width])
    tail = shifted(ut_ref, pt_ref, mixt_ref[...])
    wd = tail[:, TAIL_WD:TAIL_WD + LANE]
    ad = tail[:, TAIL_AD:TAIL_AD + LANE]
    gd = tail[:, TAIL_GD:TAIL_GD + GD_PAD]

    z = -(w0_ref[...] + _dot_hi(jnp.tanh(wd), w2_ref[...]))
    softplus = jnp.maximum(z, 0.0) + jnp.log(1.0 + jnp.exp(-jnp.abs(z)))
    w_log = -softplus - 0.5
    lw_out[...] = -jnp.exp(w_log)
    a = jax.nn.sigmoid(a0_ref[...] + _dot_hi(ad, a2_ref[...]))
    g_out[...] = _dot_hi(jax.nn.sigmoid(gd), g2_ref[...])
    kk = k * kk_ref[...]
    ss = _dot3_exact_rhs(kk * kk, esum_ref[...])
    kk = kk / jnp.maximum(jnp.sqrt(ss), 1e-12)
    r_out[...] = r
    k_out[...] = k * (1.0 + (a - 1.0) * ka_ref[...])
    v_out[...] = v
    aa_out[...] = -kk
    bb_out[...] = kk * a


def _rwkv_prep(u_main, u_tail, mix_main, mix_tail, w0, w2, a0, a2, g2, k_k, k_a, esum, seq, tm=256):
    t = u_main.shape[0]
    width = RWKV_WIDTH
    col0 = 3 * ATT_WIDTH // width
    tiles_per_seq = seq // tm
    rows8 = tm // 8

    def main_spec(c):
        return pl.BlockSpec((tm, width), lambda i: (i, col0 + c))

    def prev_spec(c):
        return pl.BlockSpec((8, width), lambda i: (jnp.maximum(i * rows8 - 1, 0), col0 + c))

    full = lambda shape: pl.BlockSpec(shape, lambda i: (0,) * len(shape))
    out_spec = pl.BlockSpec((tm, width), lambda i: (i, 0))
    return pl.pallas_call(
        functools.partial(_rwkv_prep_kernel, tiles_per_seq=tiles_per_seq),
        grid=(t // tm,),
        in_specs=[
            main_spec(0), main_spec(1), main_spec(2),
            pl.BlockSpec((tm, TAIL_COLS), lambda i: (i, 0)),
            prev_spec(0), prev_spec(1), prev_spec(2),
            pl.BlockSpec((8, TAIL_COLS), lambda i: (jnp.maximum(i * rows8 - 1, 0), 0)),
            full((1, 3 * width)), full((1, TAIL_COLS)),
            full((1, width)), full((LANE, width)), full((1, width)), full((LANE, width)),
            full((GD_PAD, width)), full((1, width)), full((1, width)),
            full((width, width)),
        ],
        out_specs=[out_spec] * 7,
        out_shape=[jax.ShapeDtypeStruct((t, width), F32)] * 7,
        compiler_params=_cparams(("parallel",)),
        name="rwkv_prep",
    )(u_main, u_main, u_main, u_tail, u_main, u_main, u_main, u_tail,
      mix_main, mix_tail, w0, w2, a0, a2, g2, k_k, k_a, esum)


def _rwkv_chunk_kernel(r_ref, k_ref, v_ref, lw_ref, aa_ref, bb_ref, g_ref,
                       rk_ref, gng_ref, gnb_ref, ltri_ref, eq_ref, o_ref, st_scr):
    c = pl.program_id(1)

    @pl.when(c == 0)
    def _():
        st_scr[...] = jnp.zeros_like(st_scr)

    ltri = ltri_ref[...]
    eq = eq_ref[...]
    row_h = lax.broadcasted_iota(jnp.int32, (QUAD, QUAD), 0) // HEAD_DIM
    col_h = lax.broadcasted_iota(jnp.int32, (QUAD, QUAD), 1) // HEAD_DIM
    bdmask = row_h == col_h
    eye = (lax.broadcasted_iota(jnp.int32, (QUAD, QUAD), 0)
           == lax.broadcasted_iota(jnp.int32, (QUAD, QUAD), 1))
    t_idx = lax.broadcasted_iota(jnp.int32, (CHUNK, QUAD), 0)
    s_idx = lax.broadcasted_iota(jnp.int32, (CHUNK, QUAD), 1) % CHUNK
    strict = t_idx > s_idx
    incl = t_idx >= s_idx
    reps = QUAD // CHUNK

    def bds(x):
        return jnp.where(bdmask, jnp.concatenate([x] * reps, axis=0), 0.0)

    def mm(a, b):
        return _dot_hi(a, b)

    for qd in range(RWKV_WIDTH // QUAD):
        sl = slice(qd * QUAD, (qd + 1) * QUAD)
        r = r_ref[0, :, sl]
        k = k_ref[0, :, sl]
        v = v_ref[0, :, sl]
        lw = lw_ref[0, :, sl]
        aa = aa_ref[0, :, sl]
        bb = bb_ref[0, :, sl]

        cum = _dot3_exact_lhs(ltri, lw)
        ctot = cum[CHUNK - 1:CHUNK, :]
        e_neg = jnp.exp(-cum)
        e_tol = jnp.exp(ctot - cum)
        a_t = aa * jnp.exp(cum - lw)
        r_t = r * jnp.exp(cum)
        b_c = bb * e_neg
        k_c = k * e_neg
        b_h = bb * e_tol
        k_h = k * e_tol
        gamma = jnp.exp(ctot)

        lhs = jnp.concatenate([a_t, r_t], axis=0)
        pb = _dot_tb(lhs, bds(b_c), precision=HI)
        pk = _dot_tb(lhs, bds(k_c), precision=HI)
        a_ab = jnp.where(strict, pb[:CHUNK], 0.0)
        a_rb = jnp.where(incl, pb[CHUNK:], 0.0)
        a_ak = jnp.where(strict, pk[:CHUNK], 0.0)
        a_rk = jnp.where(incl, pk[CHUNK:], 0.0)

        bd_v = bds(v)
        w = a_t
        u = mm(a_ak, bd_v)
        n = a_ab
        levels = int(math.log2(CHUNK))
        for lvl in range(levels):
            w = w + mm(n, bds(w))
            u = u + mm(n, bds(u))
            if lvl < levels - 1:
                n = mm(n, bds(n))

        q_t = r_t + mm(a_rb, bds(w))
        y0 = mm(a_rb, bds(u)) + mm(a_rk, bd_v)
        st = st_scr[qd]
        y = mm(q_t, st) + y0

        bk_t = jnp.concatenate([b_h, k_h], axis=0).T
        w0 = jnp.concatenate([w, jnp.zeros_like(w)], axis=0)
        g_mat = jnp.where(bdmask, mm(bk_t, w0), 0.0) + jnp.where(eye, gamma, 0.0)
        h_mat = jnp.where(bdmask, mm(bk_t, jnp.concatenate([u, v], axis=0)), 0.0)
        st_scr[qd] = mm(g_mat, st) + h_mat

        inv_n = 1.0 / HEAD_DIM
        mu = _dot3_exact_rhs(y, eq) * inv_n
        d = y - mu
        var = _dot3_exact_rhs(d * d, eq) * inv_n
        yn = d * lax.rsqrt(var + GN_EPS) * gng_ref[:, sl] + gnb_ref[:, sl]
        bonus = _dot3_exact_rhs(r * k * rk_ref[:, sl], eq) * v
        o_ref[0, :, sl] = (yn + bonus) * g_ref[0, :, sl]


def _rwkv_chunk(r, k, v, lw, aa, bb, g, r_k, gn_g, gn_b, ltri, eq):
    batch, seq, width = r.shape
    tok = pl.BlockSpec((1, CHUNK, width), lambda b, c: (b, c, 0))
    full = lambda shape: pl.BlockSpec(shape, lambda b, c: (0,) * len(shape))
    return pl.pallas_call(
        _rwkv_chunk_kernel,
        grid=(batch, seq // CHUNK),
        in_specs=[tok] * 7 + [full((1, width))] * 3 + [full((CHUNK, CHUNK)), full((QUAD, QUAD))],
        out_specs=tok,
        out_shape=jax.ShapeDtypeStruct((batch, seq, width), F32),
        scratch_shapes=[pltpu.VMEM((width // QUAD, QUAD, QUAD), F32)],
        compiler_params=_cparams(("parallel", "arbitrary")),
        name="rwkv_chunk",
    )(r, k, v, lw, aa, bb, g, r_k, gn_g, gn_b, ltri, eq)


def _out_ln_kernel(h_ref, att_ref, rw_ref, wa_ref, wr_ref, g_ref, b_ref, o_ref):
    mix = _dot(att_ref[...].astype(BF16), wa_ref[...]) + _dot(rw_ref[...].astype(BF16), wr_ref[...])
    o_ref[...] = _layer_norm(ALPHA * h_ref[...] + mix, g_ref[...], b_ref[...])


def _out_ln(h, att, rw, w_att, w_rw, g, b, tm=512):
    t, d = h.shape
    row = lambda n: pl.BlockSpec((tm, n), lambda i: (i, 0))
    full = lambda shape: pl.BlockSpec(shape, lambda i: (0,) * len(shape))
    return pl.pallas_call(
        _out_ln_kernel,
        grid=(t // tm,),
        in_specs=[row(d), row(att.shape[1]), row(rw.shape[1]), full(w_att.shape), full(w_rw.shape),
                  full((1, d)), full((1, d))],
        out_specs=row(d),
        out_shape=jax.ShapeDtypeStruct((t, d), F32),
        compiler_params=_cparams(("parallel",)),
        name="out_ln",
    )(h, att, rw, w_att, w_rw, g, b)


def _ple_ln_kernel(h_ref, p_ref, wg_ref, bg_ref, wp_ref, g_ref, b_ref, o_ref):
    h = h_ref[...]
    e = _dot(p_ref[...].astype(BF16), wp_ref[...])
    gate = jax.nn.sigmoid(_dot(h.astype(BF16), wg_ref[...]) + bg_ref[...])
    o_ref[...] = _layer_norm(ALPHA * h + gate * e, g_ref[...], b_ref[...])


def _ple_ln(h, p, w_gate, b_gate, w_up, g, b, tm=512):
    t, d = h.shape
    row = lambda n: pl.BlockSpec((tm, n), lambda i: (i, 0))
    full = lambda shape: pl.BlockSpec(shape, lambda i: (0,) * len(shape))
    return pl.pallas_call(
        _ple_ln_kernel,
        grid=(t // tm,),
        in_specs=[row(d), row(p.shape[1]), full(w_gate.shape), full((1, d)), full(w_up.shape),
                  full((1, d)), full((1, d))],
        out_specs=row(d),
        out_shape=jax.ShapeDtypeStruct((t, d), F32),
        compiler_params=_cparams(("parallel",)),
        name="ple_ln",
    )(h, p, w_gate, b_gate, w_up, g, b)


def _pad_rows(w, rows):
    return jnp.pad(w, ((0, rows - w.shape[0]), (0, 0)))


def _pack_tail(cols, axis):
    wd, ad, gd = jnp.split(cols, [DECAY_LORA, DECAY_LORA + AAA_LORA], axis=axis)

    def pad(x, n):
        widths = [(0, 0)] * x.ndim
        widths[axis] = (0, n - x.shape[axis])
        return jnp.pad(x, widths)

    return jnp.concatenate([pad(wd, LANE), pad(ad, LANE), pad(gd, GD_PAD)], axis=axis)


def kernel(x, p, ffn1_w_gate, ffn1_w_up, ffn1_w_down, ln1_g, ln1_b, w_in, rel_bias, shift_mix, decay_w0, decay_w2, a_a0, a_a2, gate_g2, k_k, k_a, r_k, gn_g, gn_b, w_out, ln2_g, ln2_b, ffn2_w_gate, ffn2_w_up, ffn2_w_down, ln3_g, ln3_b, ple_w_up, ple_w_gate, ple_b_gate, ln4_g, ln4_b):
    batch, seq, d = x.shape
    t = batch * seq
    row = lambda a: a.reshape(1, -1)
    main_cols = 3 * ATT_WIDTH + 3 * RWKV_WIDTH

    heads = jnp.arange(RWKV_WIDTH) // HEAD_DIM
    esum = (heads[:, None] == heads[None, :]).astype(BF16)
    ltri = (jnp.arange(CHUNK)[:, None] >= jnp.arange(CHUNK)[None, :]).astype(BF16)
    tab = _bias_tab(rel_bias)

    h = x.reshape(t, d).astype(F32)
    for i in range(DEPTH):
        h = _ffn_ln(h, ffn1_w_gate[i].astype(BF16), ffn1_w_up[i].astype(BF16),
                    ffn1_w_down[i].astype(BF16), row(ln1_g[i]), row(ln1_b[i]))
        w_main = w_in[i][:, :main_cols].astype(BF16)
        w_tail = _pack_tail(w_in[i][:, main_cols:], axis=1).astype(BF16)
        u_main = _proj(h, w_main)
        u_tail = _proj(h, w_tail, tn=TAIL_COLS)
        att = _moba(rel_bias, u_main.reshape(batch, seq, main_cols), tab, batch, seq)
        mix = shift_mix[i]
        prep = _rwkv_prep(
            u_main, u_tail, row(mix[:3 * RWKV_WIDTH]), row(_pack_tail(mix[3 * RWKV_WIDTH:], axis=0)),
            row(decay_w0[i]), _pad_rows(decay_w2[i], LANE), row(a_a0[i]), _pad_rows(a_a2[i], LANE),
            _pad_rows(gate_g2[i], GD_PAD), row(k_k[i]), row(k_a[i]), esum, seq)
        prep = [a.reshape(batch, seq, RWKV_WIDTH) for a in prep]
        rw = _rwkv_chunk(*prep, row(r_k[i]), row(gn_g[i]), row(gn_b[i]), ltri, esum[:QUAD, :QUAD])
        wo = w_out[i].astype(BF16)
        h = _out_ln(h, att.reshape(t, ATT_WIDTH), rw.reshape(t, RWKV_WIDTH),
                    wo[:ATT_WIDTH], wo[ATT_WIDTH:], row(ln2_g[i]), row(ln2_b[i]))
        h = _ffn_ln(h, ffn2_w_gate[i].astype(BF16), ffn2_w_up[i].astype(BF16),
                    ffn2_w_down[i].astype(BF16), row(ln3_g[i]), row(ln3_b[i]))
        h = _ple_ln(h, p[i].reshape(t, PLE_DIM).astype(F32), ple_w_gate[i].astype(BF16),
                    row(ple_b_gate[i]), ple_w_up[i].astype(BF16), row(ln4_g[i]), row(ln4_b[i]))
    return h.reshape(batch, seq, d).astype(x.dtype)
```

```python
import functools
import itertools
import math

import jax
import jax.numpy as jnp
from jax import lax
from jax.experimental import pallas as pl
from jax.experimental.pallas import tpu as pltpu

D_MODEL = 2048
DEPTH = 1
HEAD_DIM = 64
ATT_HEADS = 16
RWKV_HEADS = 16
ATT_WIDTH = ATT_HEADS * HEAD_DIM
RWKV_WIDTH = RWKV_HEADS * HEAD_DIM
MOBA_BLOCK = 256
MOBA_TOPK = 3
MOBA_UNROLL = 4
REL_BUCKETS = 32
REL_MAX_DIST = 128
DECAY_LORA = 64
AAA_LORA = 64
GATE_LORA = 160
D_FF = 5632
PLE_DIM = 256
LN_EPS = 1e-5
GN_EPS = 64e-5
NEG = -1e30
ALPHA = (2 * DEPTH) ** 0.25

LANE = 128
PAIR = LANE // HEAD_DIM
CHUNK = 64
QUAD = 256
HEADS_PER_QUAD = QUAD // HEAD_DIM
TAIL_COLS = 512
TAIL_WD, TAIL_AD, TAIL_GD = 0, 128, 256
GD_PAD = 256
VMEM_LIMIT = 56 * 1024 * 1024

F32 = jnp.float32
BF16 = jnp.bfloat16
HI = lax.Precision.HIGHEST


def _cparams(sem):
    return pltpu.CompilerParams(dimension_semantics=sem, vmem_limit_bytes=VMEM_LIMIT)


def _layer_norm(y, g, b):
    mu = jnp.mean(y, axis=-1, keepdims=True)
    d = y - mu
    var = jnp.mean(d * d, axis=-1, keepdims=True)
    return d * lax.rsqrt(var + LN_EPS) * g + b


def _dot(a, b):
    return jnp.dot(a, b, preferred_element_type=F32)


def _dot_hi(a, b):
    return jnp.dot(a, b, preferred_element_type=F32, precision=HI)


def _dot_tb(a, b, precision=None):
    return lax.dot_general(a, b, (((1,), (1,)), ((), ())), preferred_element_type=F32,
                           precision=precision)


def _split3(x):
    hi = x.astype(BF16)
    r1 = x - hi.astype(F32)
    mid = r1.astype(BF16)
    lo = (r1 - mid.astype(F32)).astype(BF16)
    return hi, mid, lo


def _dot3_exact_rhs(x, w_bf16):
    hi, mid, lo = _split3(x)
    return _dot(hi, w_bf16) + _dot(mid, w_bf16) + _dot(lo, w_bf16)


def _dot3_exact_lhs(w_bf16, x):
    hi, mid, lo = _split3(x)
    return _dot(w_bf16, hi) + _dot(w_bf16, mid) + _dot(w_bf16, lo)


def _ffn_ln_kernel(x_ref, wg_ref, wu_ref, wd_ref, g_ref, b_ref, o_ref, xb_ref, acc_ref):
    f = pl.program_id(1)

    @pl.when(f == 0)
    def _():
        xb_ref[...] = x_ref[...].astype(BF16)
        acc_ref[...] = jnp.zeros_like(acc_ref)

    xb = xb_ref[...]
    gate = _dot(xb, wg_ref[...])
    up = _dot(xb, wu_ref[...])
    act = gate * jax.nn.sigmoid(gate) * up
    acc_ref[...] += _dot(act.astype(BF16), wd_ref[...])

    @pl.when(f == pl.num_programs(1) - 1)
    def _():
        y = ALPHA * x_ref[...] + 0.5 * acc_ref[...]
        o_ref[...] = _layer_norm(y, g_ref[...], b_ref[...])


def _ffn_ln(x, wg, wu, wd, g, b, tm=512, tf=512):
    t, d = x.shape
    ff = wg.shape[1]
    return pl.pallas_call(
        _ffn_ln_kernel,
        grid=(t // tm, ff // tf),
        in_specs=[
            pl.BlockSpec((tm, d), lambda i, f: (i, 0)),
            pl.BlockSpec((d, tf), lambda i, f: (0, f)),
            pl.BlockSpec((d, tf), lambda i, f: (0, f)),
            pl.BlockSpec((tf, d), lambda i, f: (f, 0)),
            pl.BlockSpec((1, d), lambda i, f: (0, 0)),
            pl.BlockSpec((1, d), lambda i, f: (0, 0)),
        ],
        out_specs=pl.BlockSpec((tm, d), lambda i, f: (i, 0)),
        out_shape=jax.ShapeDtypeStruct((t, d), F32),
        scratch_shapes=[pltpu.VMEM((tm, d), BF16), pltpu.VMEM((tm, d), F32)],
        compiler_params=_cparams(("parallel", "arbitrary")),
        name="ffn_ln",
    )(x, wg, wu, wd, g, b)


def _proj_kernel(x_ref, w_ref, o_ref, xb_ref):
    @pl.when(pl.program_id(1) == 0)
    def _():
        xb_ref[...] = x_ref[...].astype(BF16)

    o_ref[...] = _dot(xb_ref[...], w_ref[...])


def _proj(x, w, tm=1024, tn=1024):
    t, d = x.shape
    n = w.shape[1]
    return pl.pallas_call(
        _proj_kernel,
        grid=(t // tm, n // tn),
        in_specs=[
            pl.BlockSpec((tm, d), lambda i, j: (i, 0)),
            pl.BlockSpec((d, tn), lambda i, j: (0, j)),
        ],
        out_specs=pl.BlockSpec((tm, tn), lambda i, j: (i, j)),
        out_shape=jax.ShapeDtypeStruct((t, n), F32),
        scratch_shapes=[pltpu.VMEM((tm, d), BF16)],
        compiler_params=_cparams(("parallel", "arbitrary")),
        name="proj",
    )(x, w)


def _rel_bucket(dist):
    max_exact = REL_BUCKETS // 2
    n = jnp.maximum(dist, 0)
    nf = jnp.maximum(n, 1).astype(F32)
    large = max_exact + (jnp.log(nf / max_exact) / math.log(REL_MAX_DIST / max_exact)
                         * (REL_BUCKETS - max_exact)).astype(jnp.int32)
    large = jnp.minimum(large, REL_BUCKETS - 1)
    return jnp.where(n < max_exact, n, large)


def _bias_tab_kernel(rb_ref, o_ref):
    p = pl.program_id(0)
    key = lax.broadcasted_iota(jnp.int32, (MOBA_BLOCK, MOBA_BLOCK), 0)
    qry = lax.broadcasted_iota(jnp.int32, (MOBA_BLOCK, MOBA_BLOCK), 1)
    for kind in range(2):
        dist = qry - key + kind * MOBA_BLOCK
        bucket = _rel_bucket(dist)
        for hh in range(PAIR):
            h = PAIR * p + hh
            acc = jnp.zeros((MOBA_BLOCK, MOBA_BLOCK), F32)
            for bkt in range(REL_BUCKETS):
                acc = jnp.where(bucket == bkt, rb_ref[bkt, h], acc)
            if kind == 0:
                acc = jnp.where(dist >= 0, acc, NEG)
            o_ref[0, 2 * hh + kind] = acc


def _bias_tab(rel_bias):
    npair = ATT_HEADS // PAIR
    return pl.pallas_call(
        _bias_tab_kernel,
        grid=(npair,),
        in_specs=[pl.BlockSpec(memory_space=pltpu.SMEM)],
        out_specs=pl.BlockSpec((1, 2 * PAIR, MOBA_BLOCK, MOBA_BLOCK), lambda p: (p, 0, 0, 0)),
        out_shape=jax.ShapeDtypeStruct((npair, 2 * PAIR, MOBA_BLOCK, MOBA_BLOCK), F32),
        compiler_params=_cparams(("arbitrary",)),
        name="bias_tab",
    )(rel_bias)


def _moba_kernel(rb_ref, q_ref, k_ref, v_ref, tab_ref, o_ref, kb_scr, vt_scr, kmean_scr, selm_scr):
    p = pl.program_id(1)
    i = pl.program_id(2)
    nb = kb_scr.shape[0]
    blk = MOBA_BLOCK

    @pl.when(i == 0)
    def _():
        for n in range(nb):
            kblk = k_ref[0, n * blk:(n + 1) * blk, :]
            kb_scr[n] = kblk.astype(BF16)
            kmean_scr[n:n + 1, :] = jnp.mean(kblk, axis=0, keepdims=True)
            vt_scr[n] = v_ref[0, n * blk:(n + 1) * blk, :].T.astype(BF16)

    q = q_ref[0] * (HEAD_DIM ** -0.5)
    lane = lax.broadcasted_iota(jnp.int32, q.shape, 1)
    kmean = kmean_scr[...]
    nidx = lax.broadcasted_iota(jnp.int32, (nb, blk), 0)
    n_adj = jnp.maximum(i - 1, 0)

    qh, b_far = [], []
    for hh in range(PAIR):
        in_head = (lane >= hh * HEAD_DIM) & (lane < (hh + 1) * HEAD_DIM)
        q_h = jnp.where(in_head, q, 0.0)
        qh.append(q_h.astype(BF16))
        gate = _dot_tb(kmean, q_h, precision=HI)
        rank = jnp.zeros((nb, blk), jnp.int32)
        for m in range(nb):
            gm = gate[m:m + 1, :]
            beats = (gm > gate) | ((gm == gate) & (m < nidx))
            rank = rank + jnp.where(beats & (m < i), 1, 0)
        sel = (nidx < i) & (rank < MOBA_TOPK)
        selm = jnp.where(sel, 0.0, NEG)
        for n in range(nb):
            selm_scr[hh * nb + n] = selm[n:n + 1, :]
        b_far.append(rb_ref[REL_BUCKETS - 1, PAIR * p + hh])

    def scores(hh, n):
        return _dot_tb(kb_scr[n], qh[hh])

    def vt(hh, n):
        return vt_scr[n, hh * HEAD_DIM:(hh + 1) * HEAD_DIM, :]

    def head_update(hh, blocks, prev, out):
        s_list = [scores(hh, n) + bias for n, bias in blocks]
        yield
        m_new = jnp.max(s_list[0], axis=0, keepdims=True)
        for s in s_list[1:]:
            m_new = jnp.maximum(m_new, jnp.max(s, axis=0, keepdims=True))
        if prev is not None:
            m_new = jnp.maximum(prev[0], m_new)
        yield
        l, acc = None, None
        for s, (n, _) in zip(s_list, blocks):
            pr = jnp.exp(s - m_new)
            l_s = jnp.sum(pr, axis=0, keepdims=True)
            acc_s = _dot(vt(hh, n), pr.astype(BF16))
            l = l_s if l is None else l + l_s
            acc = acc_s if acc is None else acc + acc_s
        yield
        if prev is not None:
            alpha = jnp.exp(prev[0] - m_new)
            l = alpha * prev[1] + l
            acc = alpha * prev[2] + acc
        out[hh] = (m_new, l, acc)

    def run_heads(block_fn, prev):
        out = [None] * PAIR
        gens = [head_update(hh, block_fn(hh), None if prev is None else prev[hh], out)
                for hh in range(PAIR)]
        for _ in itertools.zip_longest(*gens):
            pass
        return out

    state = run_heads(
        lambda hh: [(i, tab_ref[0, 2 * hh]),
                    (n_adj, tab_ref[0, 2 * hh + 1] + selm_scr[hh * nb + n_adj])], None)

    def far_body(j, carry):
        def blocks(hh):
            out = []
            for u in range(MOBA_UNROLL):
                n = j * MOBA_UNROLL + u
                nc = jnp.minimum(n, nb - 1)
                out.append((nc, jnp.where(n < n_adj, selm_scr[hh * nb + nc] + b_far[hh], NEG)))
            return out

        prev = [carry[3 * hh:3 * hh + 3] for hh in range(PAIR)]
        return tuple(x for st in run_heads(blocks, prev) for x in st)

    trips = (n_adj + MOBA_UNROLL - 1) // MOBA_UNROLL
    flat = lax.fori_loop(0, trips, far_body, tuple(x for st in state for x in st))
    out_t = jnp.concatenate([flat[3 * hh + 2] / flat[3 * hh + 1] for hh in range(PAIR)], axis=0)
    o_ref[0] = out_t.T


def _moba(rel_bias, u_main, tab, batch, seq):
    npair = ATT_HEADS // PAIR
    nb = seq // MOBA_BLOCK
    kcol = ATT_WIDTH // LANE
    return pl.pallas_call(
        _moba_kernel,
        grid=(batch, npair, nb),
        in_specs=[
            pl.BlockSpec(memory_space=pltpu.SMEM),
            pl.BlockSpec((1, MOBA_BLOCK, LANE), lambda b, p, i: (b, i, p)),
            pl.BlockSpec((1, seq, LANE), lambda b, p, i: (b, 0, kcol + p)),
            pl.BlockSpec((1, seq, LANE), lambda b, p, i: (b, 0, 2 * kcol + p)),
            pl.BlockSpec((1, 2 * PAIR, MOBA_BLOCK, MOBA_BLOCK), lambda b, p, i: (p, 0, 0, 0)),
        ],
        out_specs=pl.BlockSpec((1, MOBA_BLOCK, LANE), lambda b, p, i: (b, i, p)),
        out_shape=jax.ShapeDtypeStruct((batch, seq, ATT_WIDTH), F32),
        scratch_shapes=[
            pltpu.VMEM((nb, MOBA_BLOCK, LANE), BF16),
            pltpu.VMEM((nb, LANE, MOBA_BLOCK), BF16),
            pltpu.VMEM((nb, LANE), F32),
            pltpu.VMEM((PAIR * nb, 1, MOBA_BLOCK), F32),
        ],
        compiler_params=_cparams(("parallel", "parallel", "arbitrary")),
        name="moba",
    )(rel_bias, u_main, u_main, u_main, tab)


def _rwkv_prep_kernel(ur_ref, uk_ref, uv_ref, ut_ref, pr_ref, pk_ref, pv_ref, pt_ref,
                      mixm_ref, mixt_ref, w0_ref, w2_ref, a0_ref, a2_ref, g2_ref, kk_ref, ka_ref,
                      esum_ref,
                      r_out, k_out, v_out, lw_out, aa_out, bb_out, g_out, *, tiles_per_seq):
    first = (pl.program_id(0) % tiles_per_seq) == 0
    width = RWKV_WIDTH

    def shifted(x_ref, p_ref, mix):
        x = x_ref[...]
        prev = pltpu.roll(x, 1, 0)
        row = lax.broadcasted_iota(jnp.int32, x.shape, 0)
        last = jnp.where(first, 0.0, p_ref[7:8, :])
        prev = jnp.where(row == 0, last, prev)
        return x + (prev - x) * mix

    r = shifted(ur_ref, pr_ref, mixm_ref[:, 0:width])
    k = shifted(uk_ref, pk_ref, mixm_ref[:, width:2 * width])
    v = shifted(uv_ref, pv_ref, mixm_ref[:, 2 * width:3 * width])
    tail = shifted(ut_ref, pt_ref, mixt_ref[...])
    wd = tail[:, TAIL_WD:TAIL_WD + LANE]
    ad = tail[:, TAIL_AD:TAIL_AD + LANE]
    gd = tail[:, TAIL_GD:TAIL_GD + GD_PAD]

    z = -(w0_ref[...] + _dot_hi(jnp.tanh(wd), w2_ref[...]))
    softplus = jnp.maximum(z, 0.0) + jnp.log(1.0 + jnp.exp(-jnp.abs(z)))
    w_log = -softplus - 0.5
    lw_out[...] = -jnp.exp(w_log)
    a = jax.nn.sigmoid(a0_ref[...] + _dot_hi(ad, a2_ref[...]))
    g_out[...] = _dot_hi(jax.nn.sigmoid(gd), g2_ref[...])
    kk = k * kk_ref[...]
    ss = _dot3_exact_rhs(kk * kk, esum_ref[...])
    kk = kk / jnp.maximum(jnp.sqrt(ss), 1e-12)
    r_out[...] = r
    k_out[...] = k * (1.0 + (a - 1.0) * ka_ref[...])
    v_out[...] = v
    aa_out[...] = -kk
    bb_out[...] = kk * a


def _rwkv_prep(u_main, u_tail, mix_main, mix_tail, w0, w2, a0, a2, g2, k_k, k_a, esum, seq, tm=256):
    t = u_main.shape[0]
    width = RWKV_WIDTH
    col0 = 3 * ATT_WIDTH // width
    tiles_per_seq = seq // tm
    rows8 = tm // 8

    def main_spec(c):
        return pl.BlockSpec((tm, width), lambda i: (i, col0 + c))

    def prev_spec(c):
        return pl.BlockSpec((8, width), lambda i: (jnp.maximum(i * rows8 - 1, 0), col0 + c))

    full = lambda shape: pl.BlockSpec(shape, lambda i: (0,) * len(shape))
    out_spec = pl.BlockSpec((tm, width), lambda i: (i, 0))
    return pl.pallas_call(
        functools.partial(_rwkv_prep_kernel, tiles_per_seq=tiles_per_seq),
        grid=(t // tm,),
        in_specs=[
            main_spec(0), main_spec(1), main_spec(2),
            pl.BlockSpec((tm, TAIL_COLS), lambda i: (i, 0)),
            prev_spec(0), prev_spec(1), prev_spec(2),
            pl.BlockSpec((8, TAIL_COLS), lambda i: (jnp.maximum(i * rows8 - 1, 0), 0)),
            full((1, 3 * width)), full((1, TAIL_COLS)),
            full((1, width)), full((LANE, width)), full((1, width)), full((LANE, width)),
            full((GD_PAD, width)), full((1, width)), full((1, width)),
            full((width, width)),
        ],
        out_specs=[out_spec] * 7,
        out_shape=[jax.ShapeDtypeStruct((t, width), F32)] * 7,
        compiler_params=_cparams(("parallel",)),
        name="rwkv_prep",
    )(u_main, u_main, u_main, u_tail, u_main, u_main, u_main, u_tail,
      mix_main, mix_tail, w0, w2, a0, a2, g2, k_k, k_a, esum)


def _rwkv_chunk_kernel(r_ref, k_ref, v_ref, lw_ref, aa_ref, bb_ref, g_ref,
                       rk_ref, gng_ref, gnb_ref, ltri_ref, eq_ref, o_ref, st_scr):
    c = pl.program_id(1)

    @pl.when(c == 0)
    def _():
        st_scr[...] = jnp.zeros_like(st_scr)

    ltri = ltri_ref[...]
    eq = eq_ref[...]
    row_h = lax.broadcasted_iota(jnp.int32, (QUAD, QUAD), 0) // HEAD_DIM
    col_h = lax.broadcasted_iota(jnp.int32, (QUAD, QUAD), 1) // HEAD_DIM
    bdmask = row_h == col_h
    eye = (lax.broadcasted_iota(jnp.int32, (QUAD, QUAD), 0)
           == lax.broadcasted_iota(jnp.int32, (QUAD, QUAD), 1))
    t_idx = lax.broadcasted_iota(jnp.int32, (CHUNK, QUAD), 0)
    s_idx = lax.broadcasted_iota(jnp.int32, (CHUNK, QUAD), 1) % CHUNK
    strict = t_idx > s_idx
    incl = t_idx >= s_idx
    reps = QUAD // CHUNK

    def bds(x):
        xb = x.astype(BF16)
        return jnp.where(bdmask, jnp.concatenate([xb] * reps, axis=0), jnp.zeros((), BF16))

    def mm(a, b):
        return _dot(a.astype(BF16), b.astype(BF16))

    def quad_stages(qd):
        sl = slice(qd * QUAD, (qd + 1) * QUAD)
        r = r_ref[0, :, sl]
        k = k_ref[0, :, sl]
        v = v_ref[0, :, sl]
        lw = lw_ref[0, :, sl]
        aa = aa_ref[0, :, sl]
        bb = bb_ref[0, :, sl]

        cum = _dot3_exact_lhs(ltri, lw)
        yield
        ctot = cum[CHUNK - 1:CHUNK, :]
        e_neg = jnp.exp(-cum)
        e_tol = jnp.exp(ctot - cum)
        a_t = aa * jnp.exp(cum - lw)
        r_t = r * jnp.exp(cum)
        b_c = bb * e_neg
        k_c = k * e_neg
        b_h = bb * e_tol
        k_h = k * e_tol
        gamma = jnp.exp(ctot)

        lhs = jnp.concatenate([a_t, r_t], axis=0).astype(BF16)
        pb = _dot_tb(lhs, bds(b_c))
        pk = _dot_tb(lhs, bds(k_c))
        bk_t = jnp.concatenate([b_h, k_h], axis=0).T.astype(BF16)
        yield
        a_ab = jnp.where(strict, pb[:CHUNK], 0.0)
        a_rb = jnp.where(incl, pb[CHUNK:], 0.0).astype(BF16)
        a_ak = jnp.where(strict, pk[:CHUNK], 0.0)
        a_rk = jnp.where(incl, pk[CHUNK:], 0.0)

        bd_v = bds(v)
        w = a_t
        u = mm(a_ak, bd_v)
        n = a_ab
        yield
        levels = int(math.log2(CHUNK))
        for lvl in range(levels):
            n16 = n.astype(BF16)
            w = w + mm(n16, bds(w))
            u = u + mm(n16, bds(u))
            if lvl < levels - 1:
                n = mm(n16, bds(n))
            yield

        q_t = r_t + mm(a_rb, bds(w))
        y0 = mm(a_rb, bds(u)) + mm(a_rk, bd_v)
        w0 = jnp.concatenate([w, jnp.zeros_like(w)], axis=0)
        g_mat = jnp.where(bdmask, mm(bk_t, w0), 0.0) + jnp.where(eye, gamma, 0.0)
        h_mat = jnp.where(bdmask, mm(bk_t, jnp.concatenate([u, v], axis=0)), 0.0)
        yield
        st16 = st_scr[qd].astype(BF16)
        y = mm(q_t, st16) + y0
        st_scr[qd] = mm(g_mat, st16) + h_mat
        yield

        inv_n = 1.0 / HEAD_DIM
        mu = mm(y, eq) * inv_n
        bonus = mm(r * k * rk_ref[:, sl], eq) * v
        yield
        d = y - mu
        var = mm(d * d, eq) * inv_n
        yield
        yn = d * lax.rsqrt(var + GN_EPS) * gng_ref[:, sl] + gnb_ref[:, sl]
        o_ref[0, :, sl] = (yn + bonus) * g_ref[0, :, sl]

    for _ in itertools.zip_longest(*[quad_stages(qd) for qd in range(RWKV_WIDTH // QUAD)]):
        pass


def _rwkv_chunk(r, k, v, lw, aa, bb, g, r_k, gn_g, gn_b, ltri, eq):
    batch, seq, width = r.shape
    tok = pl.BlockSpec((1, CHUNK, width), lambda b, c: (b, c, 0))
    full = lambda shape: pl.BlockSpec(shape, lambda b, c: (0,) * len(shape))
    return pl.pallas_call(
        _rwkv_chunk_kernel,
        grid=(batch, seq // CHUNK),
        in_specs=[tok] * 7 + [full((1, width))] * 3 + [full((CHUNK, CHUNK)), full((QUAD, QUAD))],
        out_specs=tok,
        out_shape=jax.ShapeDtypeStruct((batch, seq, width), F32),
        scratch_shapes=[pltpu.VMEM((width // QUAD, QUAD, QUAD), F32)],
        compiler_params=_cparams(("parallel", "arbitrary")),
        name="rwkv_chunk",
    )(r, k, v, lw, aa, bb, g, r_k, gn_g, gn_b, ltri, eq)


def _out_ln_kernel(h_ref, att_ref, rw_ref, wa_ref, wr_ref, g_ref, b_ref, o_ref):
    mix = _dot(att_ref[...].astype(BF16), wa_ref[...]) + _dot(rw_ref[...].astype(BF16), wr_ref[...])
    o_ref[...] = _layer_norm(ALPHA * h_ref[...] + mix, g_ref[...], b_ref[...])


def _out_ln(h, att, rw, w_att, w_rw, g, b, tm=512):
    t, d = h.shape
    row = lambda n: pl.BlockSpec((tm, n), lambda i: (i, 0))
    full = lambda shape: pl.BlockSpec(shape, lambda i: (0,) * len(shape))
    return pl.pallas_call(
        _out_ln_kernel,
        grid=(t // tm,),
        in_specs=[row(d), row(att.shape[1]), row(rw.shape[1]), full(w_att.shape), full(w_rw.shape),
                  full((1, d)), full((1, d))],
        out_specs=row(d),
        out_shape=jax.ShapeDtypeStruct((t, d), F32),
        compiler_params=_cparams(("parallel",)),
        name="out_ln",
    )(h, att, rw, w_att, w_rw, g, b)


def _ple_ln_kernel(h_ref, p_ref, wg_ref, bg_ref, wp_ref, g_ref, b_ref, o_ref):
    h = h_ref[...]
    e = _dot(p_ref[...].astype(BF16), wp_ref[...])
    gate = jax.nn.sigmoid(_dot(h.astype(BF16), wg_ref[...]) + bg_ref[...])
    o_ref[...] = _layer_norm(ALPHA * h + gate * e, g_ref[...], b_ref[...])


def _ple_ln(h, p, w_gate, b_gate, w_up, g, b, tm=512):
    t, d = h.shape
    row = lambda n: pl.BlockSpec((tm, n), lambda i: (i, 0))
    full = lambda shape: pl.BlockSpec(shape, lambda i: (0,) * len(shape))
    return pl.pallas_call(
        _ple_ln_kernel,
        grid=(t // tm,),
        in_specs=[row(d), row(p.shape[1]), full(w_gate.shape), full((1, d)), full(w_up.shape),
                  full((1, d)), full((1, d))],
        out_specs=row(d),
        out_shape=jax.ShapeDtypeStruct((t, d), F32),
        compiler_params=_cparams(("parallel",)),
        name="ple_ln",
    )(h, p, w_gate, b_gate, w_up, g, b)


def _pad_rows(w, rows):
    return jnp.pad(w, ((0, rows - w.shape[0]), (0, 0)))


def _pack_tail(cols, axis):
    wd, ad, gd = jnp.split(cols, [DECAY_LORA, DECAY_LORA + AAA_LORA], axis=axis)

    def pad(x, n):
        widths = [(0, 0)] * x.ndim
        widths[axis] = (0, n - x.shape[axis])
        return jnp.pad(x, widths)

    return jnp.concatenate([pad(wd, LANE), pad(ad, LANE), pad(gd, GD_PAD)], axis=axis)


def kernel(x, p, ffn1_w_gate, ffn1_w_up, ffn1_w_down, ln1_g, ln1_b, w_in, rel_bias, shift_mix, decay_w0, decay_w2, a_a0, a_a2, gate_g2, k_k, k_a, r_k, gn_g, gn_b, w_out, ln2_g, ln2_b, ffn2_w_gate, ffn2_w_up, ffn2_w_down, ln3_g, ln3_b, ple_w_up, ple_w_gate, ple_b_gate, ln4_g, ln4_b):
    batch, seq, d = x.shape
    t = batch * seq
    row = lambda a: a.reshape(1, -1)
    main_cols = 3 * ATT_WIDTH + 3 * RWKV_WIDTH

    heads = jnp.arange(RWKV_WIDTH) // HEAD_DIM
    esum = (heads[:, None] == heads[None, :]).astype(BF16)
    ltri = (jnp.arange(CHUNK)[:, None] >= jnp.arange(CHUNK)[None, :]).astype(BF16)
    tab = _bias_tab(rel_bias)

    h = x.reshape(t, d).astype(F32)
    for i in range(DEPTH):
        h = _ffn_ln(h, ffn1_w_gate[i].astype(BF16), ffn1_w_up[i].astype(BF16),
                    ffn1_w_down[i].astype(BF16), row(ln1_g[i]), row(ln1_b[i]))
        w_main = w_in[i][:, :main_cols].astype(BF16)
        w_tail = _pack_tail(w_in[i][:, main_cols:], axis=1).astype(BF16)
        u_main = _proj(h, w_main)
        u_tail = _proj(h, w_tail, tn=TAIL_COLS)
        att = _moba(rel_bias, u_main.reshape(batch, seq, main_cols), tab, batch, seq)
        mix = shift_mix[i]
        prep = _rwkv_prep(
            u_main, u_tail, row(mix[:3 * RWKV_WIDTH]), row(_pack_tail(mix[3 * RWKV_WIDTH:], axis=0)),
            row(decay_w0[i]), _pad_rows(decay_w2[i], LANE), row(a_a0[i]), _pad_rows(a_a2[i], LANE),
            _pad_rows(gate_g2[i], GD_PAD), row(k_k[i]), row(k_a[i]), esum, seq)
        prep = [a.reshape(batch, seq, RWKV_WIDTH) for a in prep]
        rw = _rwkv_chunk(*prep, row(r_k[i]), row(gn_g[i]), row(gn_b[i]), ltri, esum[:QUAD, :QUAD])
        wo = w_out[i].astype(BF16)
        h = _out_ln(h, att.reshape(t, ATT_WIDTH), rw.reshape(t, RWKV_WIDTH),
                    wo[:ATT_WIDTH], wo[ATT_WIDTH:], row(ln2_g[i]), row(ln2_b[i]))
        h = _ffn_ln(h, ffn2_w_gate[i].astype(BF16), ffn2_w_up[i].astype(BF16),
                    ffn2_w_down[i].astype(BF16), row(ln3_g[i]), row(ln3_b[i]))
        h = _ple_ln(h, p[i].reshape(t, PLE_DIM).astype(F32), ple_w_gate[i].astype(BF16),
                    row(ple_b_gate[i]), ple_w_up[i].astype(BF16), row(ln4_g[i]), row(ln4_b[i]))
    return h.reshape(batch, seq, d).astype(x.dtype)
```

```python
import functools
import itertools
import math

import jax
import jax.numpy as jnp
from jax import lax
from jax.experimental import pallas as pl
from jax.experimental.pallas import tpu as pltpu

D_MODEL = 2048
DEPTH = 1
HEAD_DIM = 64
ATT_HEADS = 16
RWKV_HEADS = 16
ATT_WIDTH = ATT_HEADS * HEAD_DIM
RWKV_WIDTH = RWKV_HEADS * HEAD_DIM
MOBA_BLOCK = 256
MOBA_TOPK = 3
MOBA_UNROLL = 4
LOG2E = 1.4426950408889634
REL_BUCKETS = 32
REL_MAX_DIST = 128
DECAY_LORA = 64
AAA_LORA = 64
GATE_LORA = 160
D_FF = 5632
PLE_DIM = 256
LN_EPS = 1e-5
GN_EPS = 64e-5
NEG = -1e30
ALPHA = (2 * DEPTH) ** 0.25

LANE = 128
PAIR = LANE // HEAD_DIM
CHUNK = 64
QUAD = 256
HEADS_PER_QUAD = QUAD // HEAD_DIM
ONES_ROWS = 16
VT_ROWS = HEAD_DIM + ONES_ROWS
TAIL_COLS = 512
TAIL_WD, TAIL_AD, TAIL_GD = 0, 128, 256
GD_PAD = 256
VMEM_LIMIT = 56 * 1024 * 1024

F32 = jnp.float32
BF16 = jnp.bfloat16
HI = lax.Precision.HIGHEST


def _cparams(sem):
    return pltpu.CompilerParams(dimension_semantics=sem, vmem_limit_bytes=VMEM_LIMIT)


def _layer_norm(y, g, b):
    mu = jnp.mean(y, axis=-1, keepdims=True)
    d = y - mu
    var = jnp.mean(d * d, axis=-1, keepdims=True)
    return d * lax.rsqrt(var + LN_EPS) * g + b


def _dot(a, b):
    return jnp.dot(a, b, preferred_element_type=F32)


def _dot_hi(a, b):
    return jnp.dot(a, b, preferred_element_type=F32, precision=HI)


def _dot_tb(a, b, precision=None):
    return lax.dot_general(a, b, (((1,), (1,)), ((), ())), preferred_element_type=F32,
                           precision=precision)


def _split3(x):
    hi = x.astype(BF16)
    r1 = x - hi.astype(F32)
    mid = r1.astype(BF16)
    lo = (r1 - mid.astype(F32)).astype(BF16)
    return hi, mid, lo


def _dot3_exact_rhs(x, w_bf16):
    hi, mid, lo = _split3(x)
    return _dot(hi, w_bf16) + _dot(mid, w_bf16) + _dot(lo, w_bf16)


def _dot3_exact_lhs(w_bf16, x):
    hi, mid, lo = _split3(x)
    return _dot(w_bf16, hi) + _dot(w_bf16, mid) + _dot(w_bf16, lo)


def _ffn_ln_kernel(x_ref, wg_ref, wu_ref, wd_ref, g_ref, b_ref, o_ref, xb_ref, acc_ref):
    f = pl.program_id(1)

    @pl.when(f == 0)
    def _():
        xb_ref[...] = x_ref[...].astype(BF16)
        acc_ref[...] = jnp.zeros_like(acc_ref)

    xb = xb_ref[...]
    gate = _dot(xb, wg_ref[...])
    up = _dot(xb, wu_ref[...])
    act = gate * jax.nn.sigmoid(gate) * up
    acc_ref[...] += _dot(act.astype(BF16), wd_ref[...])

    @pl.when(f == pl.num_programs(1) - 1)
    def _():
        y = ALPHA * x_ref[...] + 0.5 * acc_ref[...]
        o_ref[...] = _layer_norm(y, g_ref[...], b_ref[...])


def _ffn_ln(x, wg, wu, wd, g, b, tm=512, tf=512):
    t, d = x.shape
    ff = wg.shape[1]
    return pl.pallas_call(
        _ffn_ln_kernel,
        grid=(t // tm, ff // tf),
        in_specs=[
            pl.BlockSpec((tm, d), lambda i, f: (i, 0)),
            pl.BlockSpec((d, tf), lambda i, f: (0, f)),
            pl.BlockSpec((d, tf), lambda i, f: (0, f)),
            pl.BlockSpec((tf, d), lambda i, f: (f, 0)),
            pl.BlockSpec((1, d), lambda i, f: (0, 0)),
            pl.BlockSpec((1, d), lambda i, f: (0, 0)),
        ],
        out_specs=pl.BlockSpec((tm, d), lambda i, f: (i, 0)),
        out_shape=jax.ShapeDtypeStruct((t, d), F32),
        scratch_shapes=[pltpu.VMEM((tm, d), BF16), pltpu.VMEM((tm, d), F32)],
        compiler_params=_cparams(("parallel", "arbitrary")),
        name="ffn_ln",
    )(x, wg, wu, wd, g, b)


def _proj_kernel(x_ref, w_ref, o_ref, xb_ref):
    @pl.when(pl.program_id(1) == 0)
    def _():
        xb_ref[...] = x_ref[...].astype(BF16)

    o_ref[...] = _dot(xb_ref[...], w_ref[...])


def _proj(x, w, tm=1024, tn=1024):
    t, d = x.shape
    n = w.shape[1]
    return pl.pallas_call(
        _proj_kernel,
        grid=(t // tm, n // tn),
        in_specs=[
            pl.BlockSpec((tm, d), lambda i, j: (i, 0)),
            pl.BlockSpec((d, tn), lambda i, j: (0, j)),
        ],
        out_specs=pl.BlockSpec((tm, tn), lambda i, j: (i, j)),
        out_shape=jax.ShapeDtypeStruct((t, n), F32),
        scratch_shapes=[pltpu.VMEM((tm, d), BF16)],
        compiler_params=_cparams(("parallel", "arbitrary")),
        name="proj",
    )(x, w)


def _rel_bucket(dist):
    max_exact = REL_BUCKETS // 2
    n = jnp.maximum(dist, 0)
    nf = jnp.maximum(n, 1).astype(F32)
    large = max_exact + (jnp.log(nf / max_exact) / math.log(REL_MAX_DIST / max_exact)
                         * (REL_BUCKETS - max_exact)).astype(jnp.int32)
    large = jnp.minimum(large, REL_BUCKETS - 1)
    return jnp.where(n < max_exact, n, large)


def _bias_tab_kernel(rb_ref, o_ref):
    p = pl.program_id(0)
    key = lax.broadcasted_iota(jnp.int32, (MOBA_BLOCK, MOBA_BLOCK), 0)
    qry = lax.broadcasted_iota(jnp.int32, (MOBA_BLOCK, MOBA_BLOCK), 1)
    for kind in range(2):
        dist = qry - key + kind * MOBA_BLOCK
        bucket = _rel_bucket(dist)
        for hh in range(PAIR):
            h = PAIR * p + hh
            acc = jnp.zeros((MOBA_BLOCK, MOBA_BLOCK), F32)
            for bkt in range(REL_BUCKETS):
                acc = jnp.where(bucket == bkt, rb_ref[bkt, h], acc)
            acc = acc * LOG2E
            if kind == 0:
                acc = jnp.where(dist >= 0, acc, NEG)
            o_ref[0, 2 * hh + kind] = acc


def _bias_tab(rel_bias):
    npair = ATT_HEADS // PAIR
    return pl.pallas_call(
        _bias_tab_kernel,
        grid=(npair,),
        in_specs=[pl.BlockSpec(memory_space=pltpu.SMEM)],
        out_specs=pl.BlockSpec((1, 2 * PAIR, MOBA_BLOCK, MOBA_BLOCK), lambda p: (p, 0, 0, 0)),
        out_shape=jax.ShapeDtypeStruct((npair, 2 * PAIR, MOBA_BLOCK, MOBA_BLOCK), F32),
        compiler_params=_cparams(("arbitrary",)),
        name="bias_tab",
    )(rel_bias)


def _moba_kernel(rb_ref, q_ref, k_ref, v_ref, tab_ref, o_ref, kb_scr, vt_scr, kmean_scr):
    p = pl.program_id(1)
    i = pl.program_id(2)
    nb = kb_scr.shape[0]
    blk = MOBA_BLOCK

    @pl.when(i == 0)
    def _():
        lane_k = lax.broadcasted_iota(jnp.int32, (blk, LANE), 1)
        lane_m = lax.broadcasted_iota(jnp.int32, (1, LANE), 1)
        ones = jnp.ones((ONES_ROWS, blk), BF16)
        for n in range(nb):
            kblk = k_ref[0, n * blk:(n + 1) * blk, :]
            kb_scr[n, :, :LANE] = kblk.astype(BF16)
            hot = (lane_k == n) | (lane_k == nb + n) | (lane_k == 2 * nb + n)
            kb_scr[n, :, LANE:] = jnp.where(hot, 1.0, 0.0).astype(BF16)
            kmean = jnp.mean(kblk, axis=0, keepdims=True)
            v_t = v_ref[0, n * blk:(n + 1) * blk, :].T.astype(BF16)
            for hh in range(PAIR):
                in_head = (lane_m >= hh * HEAD_DIM) & (lane_m < (hh + 1) * HEAD_DIM)
                kmean_scr[hh * nb + n:hh * nb + n + 1, :] = jnp.where(in_head, kmean, 0.0)
                vt_scr[n, hh * VT_ROWS:hh * VT_ROWS + HEAD_DIM, :] = v_t[hh * HEAD_DIM:(hh + 1) * HEAD_DIM, :]
                vt_scr[n, hh * VT_ROWS + HEAD_DIM:(hh + 1) * VT_ROWS, :] = ones

    q8 = q_ref[0] * (HEAD_DIM ** -0.5)
    qs = q8 * LOG2E
    lane = lax.broadcasted_iota(jnp.int32, q8.shape, 1)
    nidx = lax.broadcasted_iota(jnp.int32, (nb, blk), 0)
    n_adj = jnp.maximum(i - 1, 0)
    gate_all = _dot_tb(kmean_scr[...], q8, precision=HI)

    q_tail, q_far, adj_mask = [], [], []
    for hh in range(PAIR):
        gate = gate_all[hh * nb:(hh + 1) * nb]
        rank = jnp.zeros((nb, blk), jnp.int32)
        for m in range(nb):
            gm = gate[m:m + 1, :]
            beats = (gm > gate) | ((gm == gate) & (m < nidx))
            rank = rank + jnp.where(beats & (m < i), 1, 0)
        sel = (nidx < i) & (rank < MOBA_TOPK)
        b_far = rb_ref[REL_BUCKETS - 1, PAIR * p + hh] * LOG2E
        rowb = jnp.where(sel & (nidx < n_adj), b_far, NEG)
        adj_mask.append(jnp.max(jnp.where(sel & (nidx == i - 1), 0.0, NEG), axis=0, keepdims=True))
        hi, mid, lo = _split3(rowb)
        parts = jnp.concatenate([hi.astype(F32), mid.astype(F32), lo.astype(F32),
                                 jnp.zeros((LANE - 3 * nb, blk), F32)], axis=0)
        in_head = (lane >= hh * HEAD_DIM) & (lane < (hh + 1) * HEAD_DIM)
        q_h = jnp.where(in_head, qs, 0.0).astype(BF16)
        q_tail.append(q_h)
        q_far.append(jnp.concatenate([q_h, parts.T.astype(BF16)], axis=1))

    def vt(hh, n):
        return vt_scr[n, hh * VT_ROWS:(hh + 1) * VT_ROWS, :]

    def head_update(hh, blocks, prev, out):
        s_list = [thunk() for _, thunk in blocks]
        yield
        m_list = [jnp.max(s, axis=0, keepdims=True) for s in s_list]
        yield
        acc_list = [_dot(vt(hh, n), jnp.exp2(s - m_s).astype(BF16))
                    for s, m_s, (n, _) in zip(s_list, m_list, blocks)]
        m_new = m_list[0] if prev is None else jnp.maximum(prev[0], m_list[0])
        for m_s in m_list[1:]:
            m_new = jnp.maximum(m_new, m_s)
        yield
        acc = None if prev is None else jnp.exp2(prev[0] - m_new) * prev[1]
        for m_s, acc_s in zip(m_list, acc_list):
            term = jnp.exp2(m_s - m_new) * acc_s
            acc = term if acc is None else acc + term
        out[hh] = (m_new, acc)

    def run_heads(block_fn, prev):
        out = [None] * PAIR
        gens = [head_update(hh, block_fn(hh), None if prev is None else prev[hh], out)
                for hh in range(PAIR)]
        for _ in itertools.zip_longest(*gens):
            pass
        return out

    def tail_blocks(hh):
        def own():
            return _dot_tb(kb_scr[i, :, :LANE], q_tail[hh]) + tab_ref[0, 2 * hh]

        def prev_blk():
            return _dot_tb(kb_scr[n_adj, :, :LANE], q_tail[hh]) + tab_ref[0, 2 * hh + 1] + adj_mask[hh]

        return [(i, own), (n_adj, prev_blk)]

    state = run_heads(tail_blocks, None)

    def far_body(j, carry):
        def blocks(hh):
            out = []
            for u in range(MOBA_UNROLL):
                nc = jnp.minimum(j * MOBA_UNROLL + u, nb - 1)
                out.append((nc, lambda nc=nc: _dot_tb(kb_scr[nc], q_far[hh])))
            return out

        prev = [carry[2 * hh:2 * hh + 2] for hh in range(PAIR)]
        return tuple(x for st in run_heads(blocks, prev) for x in st)

    trips = (n_adj + MOBA_UNROLL - 1) // MOBA_UNROLL
    flat = lax.fori_loop(0, trips, far_body, tuple(x for st in state for x in st))
    out_t = jnp.concatenate(
        [flat[2 * hh + 1][:HEAD_DIM] / flat[2 * hh + 1][HEAD_DIM:HEAD_DIM + 1] for hh in range(PAIR)], axis=0)
    o_ref[0] = out_t.T


def _moba(rel_bias, u_main, tab, batch, seq):
    npair = ATT_HEADS // PAIR
    nb = seq // MOBA_BLOCK
    assert 3 * nb <= LANE and nb % MOBA_UNROLL == 0
    kcol = ATT_WIDTH // LANE
    return pl.pallas_call(
        _moba_kernel,
        grid=(batch, npair, nb),
        in_specs=[
            pl.BlockSpec(memory_space=pltpu.SMEM),
            pl.BlockSpec((1, MOBA_BLOCK, LANE), lambda b, p, i: (b, i, p)),
            pl.BlockSpec((1, seq, LANE), lambda b, p, i: (b, 0, kcol + p)),
            pl.BlockSpec((1, seq, LANE), lambda b, p, i: (b, 0, 2 * kcol + p)),
            pl.BlockSpec((1, 2 * PAIR, MOBA_BLOCK, MOBA_BLOCK), lambda b, p, i: (p, 0, 0, 0)),
        ],
        out_specs=pl.BlockSpec((1, MOBA_BLOCK, LANE), lambda b, p, i: (b, i, p)),
        out_shape=jax.ShapeDtypeStruct((batch, seq, ATT_WIDTH), F32),
        scratch_shapes=[
            pltpu.VMEM((nb, MOBA_BLOCK, 2 * LANE), BF16),
            pltpu.VMEM((nb, PAIR * VT_ROWS, MOBA_BLOCK), BF16),
            pltpu.VMEM((PAIR * nb, LANE), F32),
        ],
        compiler_params=_cparams(("parallel", "parallel", "arbitrary")),
        name="moba",
    )(rel_bias, u_main, u_main, u_main, tab)


def _rwkv_prep_kernel(ur_ref, uk_ref, uv_ref, ut_ref, pr_ref, pk_ref, pv_ref, pt_ref,
                      mixm_ref, mixt_ref, w0_ref, w2_ref, a0_ref, a2_ref, g2_ref, kk_ref, ka_ref,
                      esum_ref,
                      r_out, k_out, v_out, lw_out, aa_out, bb_out, g_out, *, tiles_per_seq):
    first = (pl.program_id(0) % tiles_per_seq) == 0
    width = RWKV_WIDTH

    def shifted(x_ref, p_ref, mix):
        x = x_ref[...]
        prev = pltpu.roll(x, 1, 0)
        row = lax.broadcasted_iota(jnp.int32, x.shape, 0)
        last = jnp.where(first, 0.0, p_ref[7:8, :])
        prev = jnp.where(row == 0, last, prev)
        return x + (prev - x) * mix

    r = shifted(ur_ref, pr_ref, mixm_ref[:, 0:width])
    k = shifted(uk_ref, pk_ref, mixm_ref[:, width:2 * width])
    v = shifted(uv_ref, pv_ref, mixm_ref[:, 2 * width:3 * width])
    tail = shifted(ut_ref, pt_ref, mixt_ref[...])
    wd = tail[:, TAIL_WD:TAIL_WD + LANE]
    ad = tail[:, TAIL_AD:TAIL_AD + LANE]
    gd = tail[:, TAIL_GD:TAIL_GD + GD_PAD]

    z = -(w0_ref[...] + _dot(jnp.tanh(wd).astype(BF16), w2_ref[...]))
    softplus = jnp.maximum(z, 0.0) + jnp.log(1.0 + jnp.exp(-jnp.abs(z)))
    w_log = -softplus - 0.5
    lw_out[...] = -jnp.exp(w_log)
    a = jax.nn.sigmoid(a0_ref[...] + _dot(ad.astype(BF16), a2_ref[...]))
    g_out[...] = _dot(jax.nn.sigmoid(gd).astype(BF16), g2_ref[...])
    kk = k * kk_ref[...]
    kk2 = kk * kk
    ss = jnp.concatenate([_dot3_exact_rhs(kk2[:, c:c + QUAD], esum_ref[...])
                          for c in range(0, width, QUAD)], axis=1)
    kk = kk / jnp.maximum(jnp.sqrt(ss), 1e-12)
    r_out[...] = r
    k_out[...] = k * (1.0 + (a - 1.0) * ka_ref[...])
    v_out[...] = v
    aa_out[...] = -kk
    bb_out[...] = kk * a


def _rwkv_prep(u_main, u_tail, mix_main, mix_tail, w0, w2, a0, a2, g2, k_k, k_a, esum, seq, tm=256):
    t = u_main.shape[0]
    width = RWKV_WIDTH
    col0 = 3 * ATT_WIDTH // width
    tiles_per_seq = seq // tm
    rows8 = tm // 8

    def main_spec(c):
        return pl.BlockSpec((tm, width), lambda i: (i, col0 + c))

    def prev_spec(c):
        return pl.BlockSpec((8, width), lambda i: (jnp.maximum(i * rows8 - 1, 0), col0 + c))

    full = lambda shape: pl.BlockSpec(shape, lambda i: (0,) * len(shape))
    out_spec = pl.BlockSpec((tm, width), lambda i: (i, 0))
    return pl.pallas_call(
        functools.partial(_rwkv_prep_kernel, tiles_per_seq=tiles_per_seq),
        grid=(t // tm,),
        in_specs=[
            main_spec(0), main_spec(1), main_spec(2),
            pl.BlockSpec((tm, TAIL_COLS), lambda i: (i, 0)),
            prev_spec(0), prev_spec(1), prev_spec(2),
            pl.BlockSpec((8, TAIL_COLS), lambda i: (jnp.maximum(i * rows8 - 1, 0), 0)),
            full((1, 3 * width)), full((1, TAIL_COLS)),
            full((1, width)), full((LANE, width)), full((1, width)), full((LANE, width)),
            full((GD_PAD, width)), full((1, width)), full((1, width)),
            full((QUAD, QUAD)),
        ],
        out_specs=[out_spec] * 7,
        out_shape=[jax.ShapeDtypeStruct((t, width), F32)] * 7,
        compiler_params=_cparams(("parallel",)),
        name="rwkv_prep",
    )(u_main, u_main, u_main, u_tail, u_main, u_main, u_main, u_tail,
      mix_main, mix_tail, w0, w2, a0, a2, g2, k_k, k_a, esum)


def _rwkv_chunk_kernel(r_ref, k_ref, v_ref, lw_ref, aa_ref, bb_ref, g_ref,
                       rk_ref, gng_ref, gnb_ref, eq_ref, o_ref, st_scr):
    c = pl.program_id(0)
    nquad = RWKV_WIDTH // QUAD

    @pl.when(c == 0)
    def _():
        st_scr[...] = jnp.zeros_like(st_scr)

    eq = eq_ref[...]
    levels = int(math.log2(CHUNK))
    row_h = lax.broadcasted_iota(jnp.int32, (QUAD, QUAD), 0) // HEAD_DIM
    col_h = lax.broadcasted_iota(jnp.int32, (QUAD, QUAD), 1) // HEAD_DIM
    bdmask = row_h == col_h
    eye = (lax.broadcasted_iota(jnp.int32, (QUAD, QUAD), 0)
           == lax.broadcasted_iota(jnp.int32, (QUAD, QUAD), 1))
    t_idx = lax.broadcasted_iota(jnp.int32, (CHUNK, QUAD), 0)
    s_idx = lax.broadcasted_iota(jnp.int32, (CHUNK, QUAD), 1) % CHUNK
    strict = t_idx > s_idx
    incl = t_idx >= s_idx
    both = jnp.concatenate([strict, incl], axis=0)
    reps = QUAD // CHUNK

    def bds(x):
        xb = x.astype(BF16)
        return jnp.where(bdmask, jnp.concatenate([xb] * reps, axis=0), jnp.zeros((), BF16))

    def mm(a, b):
        return _dot(a.astype(BF16), b.astype(BF16))

    def quad_stages(bi, qd):
        sl = slice(qd * QUAD, (qd + 1) * QUAD)
        st_ref = st_scr.at[bi * nquad + qd]
        r = r_ref[bi, :, sl]
        k = k_ref[bi, :, sl]
        v = v_ref[bi, :, sl]
        lw = lw_ref[bi, :, sl]
        aa = aa_ref[bi, :, sl]
        bb = bb_ref[bi, :, sl]

        cum = lw
        for sh in (1 << b for b in range(levels)):
            cum = cum + jnp.where(t_idx >= sh, pltpu.roll(cum, sh, 0), 0.0)
        ctot = cum[CHUNK - 1:CHUNK, :]
        e_neg = jnp.exp(-cum)
        e_tol = jnp.exp(ctot - cum)
        a_t = aa * jnp.exp(cum - lw)
        r_t = r * jnp.exp(cum)
        b_c = bb * e_neg
        k_c = k * e_neg
        b_h = bb * e_tol
        k_h = k * e_tol
        gamma = jnp.exp(ctot)

        lhs = jnp.concatenate([a_t, r_t], axis=0).astype(BF16)
        pb = _dot_tb(lhs, bds(b_c))
        pk = _dot_tb(lhs, bds(k_c))
        bk_t = jnp.concatenate([b_h, k_h], axis=0).T.astype(BF16)
        yield
        a_rb = jnp.where(incl, pb[CHUNK:], 0.0).astype(BF16)
        akrk = mm(jnp.where(both, pk, 0.0), bds(v))
        n = jnp.where(strict, pb[:CHUNK], 0.0)
        t_inv = jnp.where(t_idx == s_idx, 1.0, 0.0) + n
        n = mm(n, bds(n))
        yield
        for lvl in range(1, levels):
            bd_n = bds(n)
            if lvl < levels - 1:
                tn = mm(jnp.concatenate([t_inv, n], axis=0), bd_n)
                t_inv = t_inv + tn[:CHUNK]
                n = tn[CHUNK:]
            else:
                t_inv = t_inv + mm(t_inv, bd_n)
            yield
        w = mm(t_inv, bds(a_t))
        u = mm(t_inv, bds(akrk[:CHUNK]))
        yield
        q_t = r_t + mm(a_rb, bds(w))
        y0 = mm(a_rb, bds(u)) + akrk[CHUNK:]
        w0 = jnp.concatenate([w, jnp.zeros_like(w)], axis=0)
        g_mat = jnp.where(bdmask, mm(bk_t, w0), 0.0) + jnp.where(eye, gamma, 0.0)
        h_mat = jnp.where(bdmask, mm(bk_t, jnp.concatenate([u, v], axis=0)), 0.0)
        yield
        gy = mm(jnp.concatenate([g_mat, q_t], axis=0), st_ref[...])
        st_ref[...] = gy[:QUAD] + h_mat
        y = gy[QUAD:] + y0
        yield

        inv_n = 1.0 / HEAD_DIM
        stats = mm(jnp.concatenate([y, r * k * rk_ref[:, sl]], axis=0), eq)
        mu = stats[:CHUNK] * inv_n
        bonus = stats[CHUNK:] * v
        yield
        d = y - mu
        var = mm(d * d, eq) * inv_n
        yield
        yn = d * lax.rsqrt(var + GN_EPS) * gng_ref[:, sl] + gnb_ref[:, sl]
        o_ref[bi, :, sl] = (yn + bonus) * g_ref[bi, :, sl]

    tiles = [quad_stages(bi, qd) for bi in range(r_ref.shape[0]) for qd in range(nquad)]
    for _ in itertools.zip_longest(*tiles):
        pass


def _rwkv_chunk(r, k, v, lw, aa, bb, g, r_k, gn_g, gn_b, eq):
    batch, seq, width = r.shape
    tok = pl.BlockSpec((batch, CHUNK, width), lambda c: (0, c, 0))
    full = lambda shape: pl.BlockSpec(shape, lambda c: (0,) * len(shape))
    return pl.pallas_call(
        _rwkv_chunk_kernel,
        grid=(seq // CHUNK,),
        in_specs=[tok] * 7 + [full((1, width))] * 3 + [full((QUAD, QUAD))],
        out_specs=tok,
        out_shape=jax.ShapeDtypeStruct((batch, seq, width), F32),
        scratch_shapes=[pltpu.VMEM((batch * width // QUAD, QUAD, QUAD), F32)],
        compiler_params=_cparams(("arbitrary",)),
        name="rwkv_chunk",
    )(r, k, v, lw, aa, bb, g, r_k, gn_g, gn_b, eq)


def _out_ln_kernel(h_ref, att_ref, rw_ref, wa_ref, wr_ref, g_ref, b_ref, o_ref):
    mix = _dot(att_ref[...].astype(BF16), wa_ref[...]) + _dot(rw_ref[...].astype(BF16), wr_ref[...])
    o_ref[...] = _layer_norm(ALPHA * h_ref[...] + mix, g_ref[...], b_ref[...])


def _out_ln(h, att, rw, w_att, w_rw, g, b, tm=512):
    t, d = h.shape
    row = lambda n: pl.BlockSpec((tm, n), lambda i: (i, 0))
    full = lambda shape: pl.BlockSpec(shape, lambda i: (0,) * len(shape))
    return pl.pallas_call(
        _out_ln_kernel,
        grid=(t // tm,),
        in_specs=[row(d), row(att.shape[1]), row(rw.shape[1]), full(w_att.shape), full(w_rw.shape),
                  full((1, d)), full((1, d))],
        out_specs=row(d),
        out_shape=jax.ShapeDtypeStruct((t, d), F32),
        compiler_params=_cparams(("parallel",)),
        name="out_ln",
    )(h, att, rw, w_att, w_rw, g, b)


def _ple_ln_kernel(h_ref, p_ref, wg_ref, bg_ref, wp_ref, g_ref, b_ref, o_ref):
    h = h_ref[...]
    e = _dot(p_ref[...].astype(BF16), wp_ref[...])
    gate = jax.nn.sigmoid(_dot(h.astype(BF16), wg_ref[...]) + bg_ref[...])
    o_ref[...] = _layer_norm(ALPHA * h + gate * e, g_ref[...], b_ref[...])


def _ple_ln(h, p, w_gate, b_gate, w_up, g, b, tm=512):
    t, d = h.shape
    row = lambda n: pl.BlockSpec((tm, n), lambda i: (i, 0))
    full = lambda shape: pl.BlockSpec(shape, lambda i: (0,) * len(shape))
    return pl.pallas_call(
        _ple_ln_kernel,
        grid=(t // tm,),
        in_specs=[row(d), row(p.shape[1]), full(w_gate.shape), full((1, d)), full(w_up.shape),
                  full((1, d)), full((1, d))],
        out_specs=row(d),
        out_shape=jax.ShapeDtypeStruct((t, d), F32),
        compiler_params=_cparams(("parallel",)),
        name="ple_ln",
    )(h, p, w_gate, b_gate, w_up, g, b)


def _pad_rows(w, rows):
    return jnp.pad(w, ((0, rows - w.shape[0]), (0, 0)))


def _pack_tail(cols, axis):
    wd, ad, gd = jnp.split(cols, [DECAY_LORA, DECAY_LORA + AAA_LORA], axis=axis)

    def pad(x, n):
        widths = [(0, 0)] * x.ndim
        widths[axis] = (0, n - x.shape[axis])
        return jnp.pad(x, widths)

    return jnp.concatenate([pad(wd, LANE), pad(ad, LANE), pad(gd, GD_PAD)], axis=axis)


def kernel(x, p, ffn1_w_gate, ffn1_w_up, ffn1_w_down, ln1_g, ln1_b, w_in, rel_bias, shift_mix, decay_w0, decay_w2, a_a0, a_a2, gate_g2, k_k, k_a, r_k, gn_g, gn_b, w_out, ln2_g, ln2_b, ffn2_w_gate, ffn2_w_up, ffn2_w_down, ln3_g, ln3_b, ple_w_up, ple_w_gate, ple_b_gate, ln4_g, ln4_b):
    batch, seq, d = x.shape
    t = batch * seq
    row = lambda a: a.reshape(1, -1)
    main_cols = 3 * ATT_WIDTH + 3 * RWKV_WIDTH

    heads = jnp.arange(QUAD) // HEAD_DIM
    esum = (heads[:, None] == heads[None, :]).astype(BF16)
    tab = _bias_tab(rel_bias)

    h = x.reshape(t, d).astype(F32)
    for i in range(DEPTH):
        h = _ffn_ln(h, ffn1_w_gate[i].astype(BF16), ffn1_w_up[i].astype(BF16),
                    ffn1_w_down[i].astype(BF16), row(ln1_g[i]), row(ln1_b[i]))
        w_main = w_in[i][:, :main_cols].astype(BF16)
        w_tail = _pack_tail(w_in[i][:, main_cols:], axis=1).astype(BF16)
        u_main = _proj(h, w_main)
        u_tail = _proj(h, w_tail, tn=TAIL_COLS)
        att = _moba(rel_bias, u_main.reshape(batch, seq, main_cols), tab, batch, seq)
        mix = shift_mix[i]
        prep = _rwkv_prep(
            u_main, u_tail, row(mix[:3 * RWKV_WIDTH]), row(_pack_tail(mix[3 * RWKV_WIDTH:], axis=0)),
            row(decay_w0[i]), _pad_rows(decay_w2[i], LANE).astype(BF16), row(a_a0[i]),
            _pad_rows(a_a2[i], LANE).astype(BF16), _pad_rows(gate_g2[i], GD_PAD).astype(BF16),
            row(k_k[i]), row(k_a[i]), esum, seq)
        prep = [a.reshape(batch, seq, RWKV_WIDTH) for a in prep]
        rw = _rwkv_chunk(*prep, row(r_k[i]), row(gn_g[i]), row(gn_b[i]), esum)
        wo = w_out[i].astype(BF16)
        h = _out_ln(h, att.reshape(t, ATT_WIDTH), rw.reshape(t, RWKV_WIDTH),
                    wo[:ATT_WIDTH], wo[ATT_WIDTH:], row(ln2_g[i]), row(ln2_b[i]))
        h = _ffn_ln(h, ffn2_w_gate[i].astype(BF16), ffn2_w_up[i].astype(BF16),
                    ffn2_w_down[i].astype(BF16), row(ln3_g[i]), row(ln3_b[i]))
        h = _ple_ln(h, p[i].reshape(t, PLE_DIM).astype(F32), ple_w_gate[i].astype(BF16),
                    row(ple_b_gate[i]), ple_w_up[i].astype(BF16), row(ln4_g[i]), row(ln4_b[i]))
    return h.reshape(batch, seq, d).astype(x.dtype)
```

```python
import functools
import itertools
import math

import jax
import jax.numpy as jnp
from jax import lax
from jax.experimental import pallas as pl
from jax.experimental.pallas import tpu as pltpu

D_MODEL = 2048
DEPTH = 1
HEAD_DIM = 64
ATT_HEADS = 16
RWKV_HEADS = 16
ATT_WIDTH = ATT_HEADS * HEAD_DIM
RWKV_WIDTH = RWKV_HEADS * HEAD_DIM
MOBA_BLOCK = 256
MOBA_TOPK = 3
MOBA_UNROLL = 4
MOBA_QBLOCKS = 2
LOG2E = 1.4426950408889634
REL_BUCKETS = 32
REL_MAX_DIST = 128
DECAY_LORA = 64
AAA_LORA = 64
GATE_LORA = 160
D_FF = 5632
PLE_DIM = 256
LN_EPS = 1e-5
GN_EPS = 64e-5
NEG = -1e30
ALPHA = (2 * DEPTH) ** 0.25

LANE = 128
PAIR = LANE // HEAD_DIM
CHUNK = 64
QUAD = 256
HEADS_PER_QUAD = QUAD // HEAD_DIM
ONES_ROWS = 16
VT_ROWS = HEAD_DIM + ONES_ROWS
TAIL_COLS = 512
TAIL_WD, TAIL_AD, TAIL_GD = 0, 128, 256
GD_PAD = 256
VMEM_LIMIT = 56 * 1024 * 1024

F32 = jnp.float32
BF16 = jnp.bfloat16
HI = lax.Precision.HIGHEST


def _cparams(sem):
    return pltpu.CompilerParams(dimension_semantics=sem, vmem_limit_bytes=VMEM_LIMIT)


def _layer_norm(y, g, b):
    mu = jnp.mean(y, axis=-1, keepdims=True)
    d = y - mu
    var = jnp.mean(d * d, axis=-1, keepdims=True)
    return d * lax.rsqrt(var + LN_EPS) * g + b


def _dot(a, b):
    return jnp.dot(a, b, preferred_element_type=F32)


def _dot_hi(a, b):
    return jnp.dot(a, b, preferred_element_type=F32, precision=HI)


def _dot_tb(a, b, precision=None):
    return lax.dot_general(a, b, (((1,), (1,)), ((), ())), preferred_element_type=F32,
                           precision=precision)


def _split3(x):
    hi = x.astype(BF16)
    r1 = x - hi.astype(F32)
    mid = r1.astype(BF16)
    lo = (r1 - mid.astype(F32)).astype(BF16)
    return hi, mid, lo


def _dot3_exact_rhs(x, w_bf16):
    hi, mid, lo = _split3(x)
    return _dot(hi, w_bf16) + _dot(mid, w_bf16) + _dot(lo, w_bf16)


def _dot3_exact_lhs(w_bf16, x):
    hi, mid, lo = _split3(x)
    return _dot(w_bf16, hi) + _dot(w_bf16, mid) + _dot(w_bf16, lo)


def _ffn_ln_kernel(x_ref, wg_ref, wu_ref, wd_ref, g_ref, b_ref, o_ref, xb_ref, acc_ref):
    f = pl.program_id(1)

    @pl.when(f == 0)
    def _():
        xb_ref[...] = x_ref[...].astype(BF16)
        acc_ref[...] = jnp.zeros_like(acc_ref)

    xb = xb_ref[...]
    gate = _dot(xb, wg_ref[...])
    up = _dot(xb, wu_ref[...])
    act = gate * jax.nn.sigmoid(gate) * up
    acc_ref[...] += _dot(act.astype(BF16), wd_ref[...])

    @pl.when(f == pl.num_programs(1) - 1)
    def _():
        y = ALPHA * x_ref[...] + 0.5 * acc_ref[...]
        o_ref[...] = _layer_norm(y, g_ref[...], b_ref[...])


def _ffn_ln(x, wg, wu, wd, g, b, tm=512, tf=512):
    t, d = x.shape
    ff = wg.shape[1]
    return pl.pallas_call(
        _ffn_ln_kernel,
        grid=(t // tm, ff // tf),
        in_specs=[
            pl.BlockSpec((tm, d), lambda i, f: (i, 0)),
            pl.BlockSpec((d, tf), lambda i, f: (0, f)),
            pl.BlockSpec((d, tf), lambda i, f: (0, f)),
            pl.BlockSpec((tf, d), lambda i, f: (f, 0)),
            pl.BlockSpec((1, d), lambda i, f: (0, 0)),
            pl.BlockSpec((1, d), lambda i, f: (0, 0)),
        ],
        out_specs=pl.BlockSpec((tm, d), lambda i, f: (i, 0)),
        out_shape=jax.ShapeDtypeStruct((t, d), F32),
        scratch_shapes=[pltpu.VMEM((tm, d), BF16), pltpu.VMEM((tm, d), F32)],
        compiler_params=_cparams(("parallel", "arbitrary")),
        name="ffn_ln",
    )(x, wg, wu, wd, g, b)


def _proj_kernel(x_ref, w_ref, o_ref, xb_ref):
    @pl.when(pl.program_id(1) == 0)
    def _():
        xb_ref[...] = x_ref[...].astype(BF16)

    o_ref[...] = _dot(xb_ref[...], w_ref[...].astype(BF16))


def _proj(x, w, n, tm=1024, tn=1024):
    t, d = x.shape
    return pl.pallas_call(
        _proj_kernel,
        grid=(t // tm, n // tn),
        in_specs=[
            pl.BlockSpec((tm, d), lambda i, j: (i, 0)),
            pl.BlockSpec((d, tn), lambda i, j: (0, j)),
        ],
        out_specs=pl.BlockSpec((tm, tn), lambda i, j: (i, j)),
        out_shape=jax.ShapeDtypeStruct((t, n), F32),
        scratch_shapes=[pltpu.VMEM((tm, d), BF16)],
        compiler_params=_cparams(("parallel", "arbitrary")),
        name="proj",
    )(x, w)


def _rel_bucket(dist):
    max_exact = REL_BUCKETS // 2
    n = jnp.maximum(dist, 0)
    nf = jnp.maximum(n, 1).astype(F32)
    large = max_exact + (jnp.log(nf / max_exact) / math.log(REL_MAX_DIST / max_exact)
                         * (REL_BUCKETS - max_exact)).astype(jnp.int32)
    large = jnp.minimum(large, REL_BUCKETS - 1)
    return jnp.where(n < max_exact, n, large)


def _bias_tab_kernel(rb_ref, o_ref):
    p = pl.program_id(0)
    key = lax.broadcasted_iota(jnp.int32, (MOBA_BLOCK, MOBA_BLOCK), 0)
    qry = lax.broadcasted_iota(jnp.int32, (MOBA_BLOCK, MOBA_BLOCK), 1)
    for kind in range(2):
        dist = qry - key + kind * MOBA_BLOCK
        bucket = _rel_bucket(dist)
        for hh in range(PAIR):
            h = PAIR * p + hh
            acc = jnp.zeros((MOBA_BLOCK, MOBA_BLOCK), F32)
            for bkt in range(REL_BUCKETS):
                acc = jnp.where(bucket == bkt, rb_ref[bkt, h], acc)
            acc = acc * LOG2E
            if kind == 0:
                acc = jnp.where(dist >= 0, acc, NEG)
            o_ref[0, 2 * hh + kind] = acc


def _bias_tab(rel_bias):
    npair = ATT_HEADS // PAIR
    return pl.pallas_call(
        _bias_tab_kernel,
        grid=(npair,),
        in_specs=[pl.BlockSpec(memory_space=pltpu.SMEM)],
        out_specs=pl.BlockSpec((1, 2 * PAIR, MOBA_BLOCK, MOBA_BLOCK), lambda p: (p, 0, 0, 0)),
        out_shape=jax.ShapeDtypeStruct((npair, 2 * PAIR, MOBA_BLOCK, MOBA_BLOCK), F32),
        compiler_params=_cparams(("arbitrary",)),
        name="bias_tab",
    )(rel_bias)


def _moba_kernel(rb_ref, q_ref, k_ref, v_ref, tab_ref, o_ref, kb_scr, vt_scr, kmean_scr):
    p = pl.program_id(1)
    i0 = pl.program_id(2) * MOBA_QBLOCKS
    nb = kb_scr.shape[0]
    blk = MOBA_BLOCK

    @pl.when(i0 == 0)
    def _():
        lane_k = lax.broadcasted_iota(jnp.int32, (blk, LANE), 1)
        lane_m = lax.broadcasted_iota(jnp.int32, (1, LANE), 1)
        ones = jnp.ones((ONES_ROWS, blk), BF16)
        for n in range(nb):
            kblk = k_ref[0, n * blk:(n + 1) * blk, :]
            kb_scr[n, :, :LANE] = kblk.astype(BF16)
            hot = (lane_k == n) | (lane_k == nb + n) | (lane_k == 2 * nb + n)
            kb_scr[n, :, LANE:] = jnp.where(hot, 1.0, 0.0).astype(BF16)
            kmean = jnp.mean(kblk, axis=0, keepdims=True)
            v_t = v_ref[0, n * blk:(n + 1) * blk, :].T.astype(BF16)
            for hh in range(PAIR):
                in_head = (lane_m >= hh * HEAD_DIM) & (lane_m < (hh + 1) * HEAD_DIM)
                kmean_scr[hh * nb + n:hh * nb + n + 1, :] = jnp.where(in_head, kmean, 0.0)
                vt_scr[n, hh * VT_ROWS:hh * VT_ROWS + HEAD_DIM, :] = v_t[hh * HEAD_DIM:(hh + 1) * HEAD_DIM, :]
                vt_scr[n, hh * VT_ROWS + HEAD_DIM:(hh + 1) * VT_ROWS, :] = ones

    q8 = q_ref[0] * (HEAD_DIM ** -0.5)
    qs = q8 * LOG2E
    nq = q8.shape[0]
    lane = lax.broadcasted_iota(jnp.int32, q8.shape, 1)
    nidx = lax.broadcasted_iota(jnp.int32, (nb, nq), 0)
    own = i0 + lax.broadcasted_iota(jnp.int32, (1, nq), 1) // blk
    gate_all = _dot_tb(kmean_scr[...], q8, precision=HI)

    q_tail, q_far, adj_mask = [], [], []
    for hh in range(PAIR):
        gate = gate_all[hh * nb:(hh + 1) * nb]
        avail = nidx < own
        sel = jnp.zeros((nb, nq), jnp.bool_)
        for _ in range(MOBA_TOPK):
            best = jnp.max(jnp.where(avail, gate, -jnp.inf), axis=0, keepdims=True)
            first = jnp.min(jnp.where(avail & (gate == best), nidx, nb), axis=0, keepdims=True)
            pick = nidx == first
            sel = sel | pick
            avail = avail & jnp.logical_not(pick)
        b_far = rb_ref[REL_BUCKETS - 1, PAIR * p + hh] * LOG2E
        rowb = jnp.where(sel & (nidx < own - 1), b_far, NEG)
        adj_mask.append(jnp.max(jnp.where(sel & (nidx == own - 1), 0.0, NEG), axis=0, keepdims=True))
        hi, mid, lo = _split3(rowb)
        parts = jnp.concatenate([hi.astype(F32), mid.astype(F32), lo.astype(F32),
                                 jnp.zeros((LANE - 3 * nb, nq), F32)], axis=0)
        in_head = (lane >= hh * HEAD_DIM) & (lane < (hh + 1) * HEAD_DIM)
        q_h = jnp.where(in_head, qs, 0.0).astype(BF16)
        q_tail.append(q_h)
        q_far.append(jnp.concatenate([q_h, parts.T.astype(BF16)], axis=1))

    def vt(hh, n):
        return vt_scr[n, hh * VT_ROWS:(hh + 1) * VT_ROWS, :]

    def head_update(key, hh, blocks, prev, out):
        s_list = [thunk() for _, thunk in blocks]
        yield
        m_list = [jnp.max(s, axis=0, keepdims=True) for s in s_list]
        yield
        acc_list = [_dot(vt(hh, n), jnp.exp2(s - m_s).astype(BF16))
                    for s, m_s, (n, _) in zip(s_list, m_list, blocks)]
        m_new = m_list[0] if prev is None else jnp.maximum(prev[0], m_list[0])
        for m_s in m_list[1:]:
            m_new = jnp.maximum(m_new, m_s)
        yield
        acc = None if prev is None else jnp.exp2(prev[0] - m_new) * prev[1]
        for m_s, acc_s in zip(m_list, acc_list):
            term = jnp.exp2(m_s - m_new) * acc_s
            acc = term if acc is None else acc + term
        out[key] = (m_new, acc)

    def run_chains(gens):
        for _ in itertools.zip_longest(*gens):
            pass

    def tail_blocks(hh, c):
        cols = slice(c * blk, (c + 1) * blk)
        i_c = i0 + c
        n_adj = jnp.maximum(i_c - 1, 0)
        q_c = q_tail[hh][cols]

        def own_blk():
            return _dot_tb(kb_scr[i_c, :, :LANE], q_c) + tab_ref[0, 2 * hh]

        def prev_blk():
            return _dot_tb(kb_scr[n_adj, :, :LANE], q_c) + tab_ref[0, 2 * hh + 1] + adj_mask[hh][:, cols]

        return [(i_c, own_blk), (n_adj, prev_blk)]

    tails = {}
    run_chains([head_update((hh, c), hh, tail_blocks(hh, c), None, tails)
                for hh in range(PAIR) for c in range(MOBA_QBLOCKS)])
    state = [tuple(jnp.concatenate([tails[hh, c][s] for c in range(MOBA_QBLOCKS)], axis=1) for s in range(2))
             for hh in range(PAIR)]

    def far_body(j, carry):
        def blocks(hh):
            out = []
            for u in range(MOBA_UNROLL):
                nc = jnp.minimum(j * MOBA_UNROLL + u, nb - 1)
                out.append((nc, lambda nc=nc: _dot_tb(kb_scr[nc], q_far[hh])))
            return out

        out = [None] * PAIR
        run_chains([head_update(hh, hh, blocks(hh), carry[2 * hh:2 * hh + 2], out) for hh in range(PAIR)])
        return tuple(x for st in out for x in st)

    n_far = jnp.maximum(i0 + MOBA_QBLOCKS - 2, 0)
    trips = (n_far + MOBA_UNROLL - 1) // MOBA_UNROLL
    flat = lax.fori_loop(0, trips, far_body, tuple(x for st in state for x in st))
    out_t = jnp.concatenate(
        [flat[2 * hh + 1][:HEAD_DIM] / flat[2 * hh + 1][HEAD_DIM:HEAD_DIM + 1] for hh in range(PAIR)], axis=0)
    o_ref[0] = out_t.T


def _moba(rel_bias, u_main, tab, batch, seq):
    npair = ATT_HEADS // PAIR
    nb = seq // MOBA_BLOCK
    assert 3 * nb <= LANE and nb % MOBA_UNROLL == 0 and nb % MOBA_QBLOCKS == 0
    kcol = ATT_WIDTH // LANE
    qrows = MOBA_QBLOCKS * MOBA_BLOCK
    return pl.pallas_call(
        _moba_kernel,
        grid=(batch, npair, nb // MOBA_QBLOCKS),
        in_specs=[
            pl.BlockSpec(memory_space=pltpu.SMEM),
            pl.BlockSpec((1, qrows, LANE), lambda b, p, i: (b, i, p)),
            pl.BlockSpec((1, seq, LANE), lambda b, p, i: (b, 0, kcol + p)),
            pl.BlockSpec((1, seq, LANE), lambda b, p, i: (b, 0, 2 * kcol + p)),
            pl.BlockSpec((1, 2 * PAIR, MOBA_BLOCK, MOBA_BLOCK), lambda b, p, i: (p, 0, 0, 0)),
        ],
        out_specs=pl.BlockSpec((1, qrows, LANE), lambda b, p, i: (b, i, p)),
        out_shape=jax.ShapeDtypeStruct((batch, seq, ATT_WIDTH), F32),
        scratch_shapes=[
            pltpu.VMEM((nb, MOBA_BLOCK, 2 * LANE), BF16),
            pltpu.VMEM((nb, PAIR * VT_ROWS, MOBA_BLOCK), BF16),
            pltpu.VMEM((PAIR * nb, LANE), F32),
        ],
        compiler_params=_cparams(("parallel", "parallel", "arbitrary")),
        name="moba",
    )(rel_bias, u_main, u_main, u_main, tab)


def _rwkv_prep_kernel(ur_ref, uk_ref, uv_ref, ut_ref, pr_ref, pk_ref, pv_ref, pt_ref,
                      mixm_ref, mixt_ref, w0_ref, w2_ref, a0_ref, a2_ref, g2_ref, kk_ref, ka_ref,
                      esum_ref,
                      r_out, k_out, v_out, lw_out, aa_out, bb_out, g_out, *, tiles_per_seq):
    first = (pl.program_id(0) % tiles_per_seq) == 0
    width = RWKV_WIDTH

    def shifted(x_ref, p_ref, mix):
        x = x_ref[...]
        prev = pltpu.roll(x, 1, 0)
        row = lax.broadcasted_iota(jnp.int32, x.shape, 0)
        last = jnp.where(first, 0.0, p_ref[7:8, :])
        prev = jnp.where(row == 0, last, prev)
        return x + (prev - x) * mix

    r = shifted(ur_ref, pr_ref, mixm_ref[:, 0:width])
    k = shifted(uk_ref, pk_ref, mixm_ref[:, width:2 * width])
    v = shifted(uv_ref, pv_ref, mixm_ref[:, 2 * width:3 * width])
    tail = shifted(ut_ref, pt_ref, mixt_ref[...])
    wd = tail[:, TAIL_WD:TAIL_WD + LANE]
    ad = tail[:, TAIL_AD:TAIL_AD + LANE]
    gd = tail[:, TAIL_GD:TAIL_GD + GD_PAD]

    z = -(w0_ref[...] + _dot(jnp.tanh(wd).astype(BF16), w2_ref[...]))
    softplus = jnp.maximum(z, 0.0) + jnp.log(1.0 + jnp.exp(-jnp.abs(z)))
    w_log = -softplus - 0.5
    lw_out[...] = -jnp.exp(w_log)
    a = jax.nn.sigmoid(a0_ref[...] + _dot(ad.astype(BF16), a2_ref[...]))
    g_out[...] = _dot(jax.nn.sigmoid(gd).astype(BF16), g2_ref[...])
    kk = k * kk_ref[...]
    kk2 = kk * kk
    ss = jnp.concatenate([_dot3_exact_rhs(kk2[:, c:c + QUAD], esum_ref[...])
                          for c in range(0, width, QUAD)], axis=1)
    kk = kk / jnp.maximum(jnp.sqrt(ss), 1e-12)
    r_out[...] = r
    k_out[...] = k * (1.0 + (a - 1.0) * ka_ref[...])
    v_out[...] = v
    aa_out[...] = -kk
    bb_out[...] = kk * a


def _rwkv_prep(u_main, u_tail, mix_main, mix_tail, w0, w2, a0, a2, g2, k_k, k_a, esum, seq, tm=256):
    t = u_main.shape[0]
    width = RWKV_WIDTH
    col0 = 3 * ATT_WIDTH // width
    tiles_per_seq = seq // tm
    rows8 = tm // 8

    def main_spec(c):
        return pl.BlockSpec((tm, width), lambda i: (i, col0 + c))

    def prev_spec(c):
        return pl.BlockSpec((8, width), lambda i: (jnp.maximum(i * rows8 - 1, 0), col0 + c))

    full = lambda shape: pl.BlockSpec(shape, lambda i: (0,) * len(shape))
    out_spec = pl.BlockSpec((tm, width), lambda i: (i, 0))
    return pl.pallas_call(
        functools.partial(_rwkv_prep_kernel, tiles_per_seq=tiles_per_seq),
        grid=(t // tm,),
        in_specs=[
            main_spec(0), main_spec(1), main_spec(2),
            pl.BlockSpec((tm, TAIL_COLS), lambda i: (i, 0)),
            prev_spec(0), prev_spec(1), prev_spec(2),
            pl.BlockSpec((8, TAIL_COLS), lambda i: (jnp.maximum(i * rows8 - 1, 0), 0)),
            full((1, 3 * width)), full((1, TAIL_COLS)),
            full((1, width)), full((LANE, width)), full((1, width)), full((LANE, width)),
            full((GD_PAD, width)), full((1, width)), full((1, width)),
            full((QUAD, QUAD)),
        ],
        out_specs=[out_spec] * 7,
        out_shape=[jax.ShapeDtypeStruct((t, width), F32)] * 7,
        compiler_params=_cparams(("parallel",)),
        name="rwkv_prep",
    )(u_main, u_main, u_main, u_tail, u_main, u_main, u_main, u_tail,
      mix_main, mix_tail, w0, w2, a0, a2, g2, k_k, k_a, esum)


def _rwkv_chunk_kernel(r_ref, k_ref, v_ref, lw_ref, aa_ref, bb_ref, g_ref,
                       rk_ref, gng_ref, gnb_ref, eq_ref, o_ref, st_scr):
    c = pl.program_id(0)
    nquad = RWKV_WIDTH // QUAD

    @pl.when(c == 0)
    def _():
        st_scr[...] = jnp.zeros_like(st_scr)

    eq = eq_ref[...]
    levels = int(math.log2(CHUNK))
    row_h = lax.broadcasted_iota(jnp.int32, (QUAD, QUAD), 0) // HEAD_DIM
    col_h = lax.broadcasted_iota(jnp.int32, (QUAD, QUAD), 1) // HEAD_DIM
    bdmask = row_h == col_h
    eye = (lax.broadcasted_iota(jnp.int32, (QUAD, QUAD), 0)
           == lax.broadcasted_iota(jnp.int32, (QUAD, QUAD), 1))
    t_idx = lax.broadcasted_iota(jnp.int32, (CHUNK, QUAD), 0)
    s_idx = lax.broadcasted_iota(jnp.int32, (CHUNK, QUAD), 1) % CHUNK
    strict = t_idx > s_idx
    incl = t_idx >= s_idx
    both = jnp.concatenate([strict, incl], axis=0)
    reps = QUAD // CHUNK

    def bds(x):
        xb = x.astype(BF16)
        return jnp.where(bdmask, jnp.concatenate([xb] * reps, axis=0), jnp.zeros((), BF16))

    def mm(a, b):
        return _dot(a.astype(BF16), b.astype(BF16))

    def quad_stages(bi, qd):
        sl = slice(qd * QUAD, (qd + 1) * QUAD)
        st_ref = st_scr.at[bi * nquad + qd]
        r = r_ref[bi, :, sl]
        k = k_ref[bi, :, sl]
        v = v_ref[bi, :, sl]
        lw = lw_ref[bi, :, sl]
        aa = aa_ref[bi, :, sl]
        bb = bb_ref[bi, :, sl]

        cum = lw
        for sh in (1 << b for b in range(levels)):
            cum = cum + jnp.where(t_idx >= sh, pltpu.roll(cum, sh, 0), 0.0)
        ctot = cum[CHUNK - 1:CHUNK, :]
        e_neg = jnp.exp(-cum)
        e_tol = jnp.exp(ctot - cum)
        a_t = aa * jnp.exp(cum - lw)
        r_t = r * jnp.exp(cum)
        b_c = bb * e_neg
        k_c = k * e_neg
        b_h = bb * e_tol
        k_h = k * e_tol
        gamma = jnp.exp(ctot)

        lhs = jnp.concatenate([a_t, r_t], axis=0).astype(BF16)
        pb = _dot_tb(lhs, bds(b_c))
        pk = _dot_tb(lhs, bds(k_c))
        bk_t = jnp.concatenate([b_h, k_h], axis=0).T.astype(BF16)
        yield
        a_rb = jnp.where(incl, pb[CHUNK:], 0.0).astype(BF16)
        akrk = mm(jnp.where(both, pk, 0.0), bds(v))
        n = jnp.where(strict, pb[:CHUNK], 0.0)
        t_inv = jnp.where(t_idx == s_idx, 1.0, 0.0) + n
        n = mm(n, bds(n))
        yield
        for lvl in range(1, levels):
            bd_n = bds(n)
            if lvl < levels - 1:
                tn = mm(jnp.concatenate([t_inv, n], axis=0), bd_n)
                t_inv = t_inv + tn[:CHUNK]
                n = tn[CHUNK:]
            else:
                t_inv = t_inv + mm(t_inv, bd_n)
            yield
        w = mm(t_inv, bds(a_t))
        u = mm(t_inv, bds(akrk[:CHUNK]))
        yield
        q_t = r_t + mm(a_rb, bds(w))
        y0 = mm(a_rb, bds(u)) + akrk[CHUNK:]
        w0 = jnp.concatenate([w, jnp.zeros_like(w)], axis=0)
        g_mat = jnp.where(bdmask, mm(bk_t, w0), 0.0) + jnp.where(eye, gamma, 0.0)
        h_mat = jnp.where(bdmask, mm(bk_t, jnp.concatenate([u, v], axis=0)), 0.0)
        yield
        gy = mm(jnp.concatenate([g_mat, q_t], axis=0), st_ref[...])
        st_ref[...] = gy[:QUAD] + h_mat
        y = gy[QUAD:] + y0
        yield

        inv_n = 1.0 / HEAD_DIM
        stats = mm(jnp.concatenate([y, r * k * rk_ref[:, sl]], axis=0), eq)
        mu = stats[:CHUNK] * inv_n
        bonus = stats[CHUNK:] * v
        yield
        d = y - mu
        var = mm(d * d, eq) * inv_n
        yield
        yn = d * lax.rsqrt(var + GN_EPS) * gng_ref[:, sl] + gnb_ref[:, sl]
        o_ref[bi, :, sl] = (yn + bonus) * g_ref[bi, :, sl]

    tiles = [quad_stages(bi, qd) for bi in range(r_ref.shape[0]) for qd in range(nquad)]
    for _ in itertools.zip_longest(*tiles):
        pass


def _rwkv_chunk(r, k, v, lw, aa, bb, g, r_k, gn_g, gn_b, eq):
    batch, seq, width = r.shape
    tok = pl.BlockSpec((batch, CHUNK, width), lambda c: (0, c, 0))
    full = lambda shape: pl.BlockSpec(shape, lambda c: (0,) * len(shape))
    return pl.pallas_call(
        _rwkv_chunk_kernel,
        grid=(seq // CHUNK,),
        in_specs=[tok] * 7 + [full((1, width))] * 3 + [full((QUAD, QUAD))],
        out_specs=tok,
        out_shape=jax.ShapeDtypeStruct((batch, seq, width), F32),
        scratch_shapes=[pltpu.VMEM((batch * width // QUAD, QUAD, QUAD), F32)],
        compiler_params=_cparams(("arbitrary",)),
        name="rwkv_chunk",
    )(r, k, v, lw, aa, bb, g, r_k, gn_g, gn_b, eq)


def _out_ln_kernel(h_ref, att_ref, rw_ref, wa_ref, wr_ref, g_ref, b_ref, o_ref):
    mix = _dot(att_ref[...].astype(BF16), wa_ref[...]) + _dot(rw_ref[...].astype(BF16), wr_ref[...])
    o_ref[...] = _layer_norm(ALPHA * h_ref[...] + mix, g_ref[...], b_ref[...])


def _out_ln(h, att, rw, w_att, w_rw, g, b, tm=512):
    t, d = h.shape
    row = lambda n: pl.BlockSpec((tm, n), lambda i: (i, 0))
    full = lambda shape: pl.BlockSpec(shape, lambda i: (0,) * len(shape))
    return pl.pallas_call(
        _out_ln_kernel,
        grid=(t // tm,),
        in_specs=[row(d), row(att.shape[1]), row(rw.shape[1]), full(w_att.shape), full(w_rw.shape),
                  full((1, d)), full((1, d))],
        out_specs=row(d),
        out_shape=jax.ShapeDtypeStruct((t, d), F32),
        compiler_params=_cparams(("parallel",)),
        name="out_ln",
    )(h, att, rw, w_att, w_rw, g, b)


def _ple_ln_kernel(h_ref, p_ref, wg_ref, bg_ref, wp_ref, g_ref, b_ref, o_ref):
    h = h_ref[...]
    e = _dot(p_ref[...].astype(BF16), wp_ref[...])
    gate = jax.nn.sigmoid(_dot(h.astype(BF16), wg_ref[...]) + bg_ref[...])
    o_ref[...] = _layer_norm(ALPHA * h + gate * e, g_ref[...], b_ref[...])


def _ple_ln(h, p, w_gate, b_gate, w_up, g, b, tm=512):
    t, d = h.shape
    row = lambda n: pl.BlockSpec((tm, n), lambda i: (i, 0))
    full = lambda shape: pl.BlockSpec(shape, lambda i: (0,) * len(shape))
    return pl.pallas_call(
        _ple_ln_kernel,
        grid=(t // tm,),
        in_specs=[row(d), row(p.shape[1]), full(w_gate.shape), full((1, d)), full(w_up.shape),
                  full((1, d)), full((1, d))],
        out_specs=row(d),
        out_shape=jax.ShapeDtypeStruct((t, d), F32),
        compiler_params=_cparams(("parallel",)),
        name="ple_ln",
    )(h, p, w_gate, b_gate, w_up, g, b)


def _pad_rows(w, rows):
    return jnp.pad(w, ((0, rows - w.shape[0]), (0, 0)))


def _pack_tail(cols, axis):
    wd, ad, gd = jnp.split(cols, [DECAY_LORA, DECAY_LORA + AAA_LORA], axis=axis)

    def pad(x, n):
        widths = [(0, 0)] * x.ndim
        widths[axis] = (0, n - x.shape[axis])
        return jnp.pad(x, widths)

    return jnp.concatenate([pad(wd, LANE), pad(ad, LANE), pad(gd, GD_PAD)], axis=axis)


def kernel(x, p, ffn1_w_gate, ffn1_w_up, ffn1_w_down, ln1_g, ln1_b, w_in, rel_bias, shift_mix, decay_w0, decay_w2, a_a0, a_a2, gate_g2, k_k, k_a, r_k, gn_g, gn_b, w_out, ln2_g, ln2_b, ffn2_w_gate, ffn2_w_up, ffn2_w_down, ln3_g, ln3_b, ple_w_up, ple_w_gate, ple_b_gate, ln4_g, ln4_b):
    batch, seq, d = x.shape
    t = batch * seq
    row = lambda a: a.reshape(1, -1)
    main_cols = 3 * ATT_WIDTH + 3 * RWKV_WIDTH

    heads = jnp.arange(QUAD) // HEAD_DIM
    esum = (heads[:, None] == heads[None, :]).astype(BF16)
    tab = _bias_tab(rel_bias)

    h = x.reshape(t, d).astype(F32)
    for i in range(DEPTH):
        h = _ffn_ln(h, ffn1_w_gate[i].astype(BF16), ffn1_w_up[i].astype(BF16),
                    ffn1_w_down[i].astype(BF16), row(ln1_g[i]), row(ln1_b[i]))
        u_main = _proj(h, w_in[i], main_cols)
        u_tail = _proj(h, _pack_tail(w_in[i][:, main_cols:], axis=1), TAIL_COLS, tn=TAIL_COLS)
        att = _moba(rel_bias, u_main.reshape(batch, seq, main_cols), tab, batch, seq)
        mix = shift_mix[i]
        prep = _rwkv_prep(
            u_main, u_tail, row(mix[:3 * RWKV_WIDTH]), row(_pack_tail(mix[3 * RWKV_WIDTH:], axis=0)),
            row(decay_w0[i]), _pad_rows(decay_w2[i], LANE).astype(BF16), row(a_a0[i]),
            _pad_rows(a_a2[i], LANE).astype(BF16), _pad_rows(gate_g2[i], GD_PAD).astype(BF16),
            row(k_k[i]), row(k_a[i]), esum, seq)
        prep = [a.reshape(batch, seq, RWKV_WIDTH) for a in prep]
        rw = _rwkv_chunk(*prep, row(r_k[i]), row(gn_g[i]), row(gn_b[i]), esum)
        wo = w_out[i].astype(BF16)
        h = _out_ln(h, att.reshape(t, ATT_WIDTH), rw.reshape(t, RWKV_WIDTH),
                    wo[:ATT_WIDTH], wo[ATT_WIDTH:], row(ln2_g[i]), row(ln2_b[i]))
        h = _ffn_ln(h, ffn2_w_gate[i].astype(BF16), ffn2_w_up[i].astype(BF16),
                    ffn2_w_down[i].astype(BF16), row(ln3_g[i]), row(ln3_b[i]))
        h = _ple_ln(h, p[i].reshape(t, PLE_DIM).astype(F32), ple_w_gate[i].astype(BF16),
                    row(ple_b_gate[i]), ple_w_up[i].astype(BF16), row(ln4_g[i]), row(ln4_b[i]))
    return h.reshape(batch, seq, d).astype(x.dtype)
```

```python
import functools
import itertools
import math

import jax
import jax.numpy as jnp
from jax import lax
from jax.experimental import pallas as pl
from jax.experimental.pallas import tpu as pltpu

D_MODEL = 2048
DEPTH = 1
HEAD_DIM = 64
ATT_HEADS = 16
RWKV_HEADS = 16
ATT_WIDTH = ATT_HEADS * HEAD_DIM
RWKV_WIDTH = RWKV_HEADS * HEAD_DIM
MOBA_BLOCK = 256
MOBA_TOPK = 3
MOBA_UNROLL = 4
MOBA_QBLOCKS = 2
LOG2E = 1.4426950408889634
REL_BUCKETS = 32
REL_MAX_DIST = 128
DECAY_LORA = 64
AAA_LORA = 64
GATE_LORA = 160
D_FF = 5632
PLE_DIM = 256
LN_EPS = 1e-5
GN_EPS = 64e-5
NEG = -1e30
ALPHA = (2 * DEPTH) ** 0.25

LANE = 128
PAIR = LANE // HEAD_DIM
CHUNK = 64
QUAD = 256
HEADS_PER_QUAD = QUAD // HEAD_DIM
ONES_ROWS = 16
VT_ROWS = HEAD_DIM + ONES_ROWS
TAIL_COLS = 512
TAIL_WD, TAIL_AD, TAIL_GD = 0, 128, 256
GD_PAD = 256
VMEM_LIMIT = 56 * 1024 * 1024

F32 = jnp.float32
BF16 = jnp.bfloat16
HI = lax.Precision.HIGHEST


def _cparams(sem):
    return pltpu.CompilerParams(dimension_semantics=sem, vmem_limit_bytes=VMEM_LIMIT)


def _layer_norm(y, g, b):
    mu = jnp.mean(y, axis=-1, keepdims=True)
    d = y - mu
    var = jnp.mean(d * d, axis=-1, keepdims=True)
    return d * lax.rsqrt(var + LN_EPS) * g + b


def _dot(a, b):
    return jnp.dot(a, b, preferred_element_type=F32)


def _dot_hi(a, b):
    return jnp.dot(a, b, preferred_element_type=F32, precision=HI)


def _dot_tb(a, b, precision=None):
    return lax.dot_general(a, b, (((1,), (1,)), ((), ())), preferred_element_type=F32,
                           precision=precision)


def _split3(x):
    hi = x.astype(BF16)
    r1 = x - hi.astype(F32)
    mid = r1.astype(BF16)
    lo = (r1 - mid.astype(F32)).astype(BF16)
    return hi, mid, lo


def _dot3_exact_rhs(x, w_bf16):
    hi, mid, lo = _split3(x)
    return _dot(hi, w_bf16) + _dot(mid, w_bf16) + _dot(lo, w_bf16)


def _dot3_exact_lhs(w_bf16, x):
    hi, mid, lo = _split3(x)
    return _dot(w_bf16, hi) + _dot(w_bf16, mid) + _dot(w_bf16, lo)


def _ffn_ln_kernel(*refs, n_cast):
    x_ref, wg_ref, wu_ref, wd_ref, g_ref, b_ref = refs[:6]
    cast_in = refs[6:6 + n_cast]
    o_ref = refs[6 + n_cast]
    cast_out = refs[7 + n_cast:7 + 2 * n_cast]
    xb_ref, acc_ref = refs[7 + 2 * n_cast:]
    f = pl.program_id(1)

    @pl.when(f == 0)
    def _():
        xb_ref[...] = x_ref[...].astype(BF16)
        acc_ref[...] = jnp.zeros_like(acc_ref)

    xb = xb_ref[...]
    gate = _dot(xb, wg_ref[...])
    up = _dot(xb, wu_ref[...])
    act = gate * jax.nn.sigmoid(gate) * up
    acc_ref[...] += _dot(act.astype(BF16), wd_ref[...])

    for src_ref, dst_ref in zip(cast_in, cast_out):
        dst_ref[...] = src_ref[...].astype(BF16)

    @pl.when(f == pl.num_programs(1) - 1)
    def _():
        y = ALPHA * x_ref[...] + 0.5 * acc_ref[...]
        o_ref[...] = _layer_norm(y, g_ref[...], b_ref[...])


def _cast_spec(shape, n_i, n_f):
    rows, cols = shape
    if rows % (16 * n_i) == 0 and cols % (LANE * n_f) == 0:
        return pl.BlockSpec((rows // n_i, cols // n_f), lambda i, f: (i, f))
    if rows % (16 * n_f) == 0 and cols % (LANE * n_i) == 0:
        return pl.BlockSpec((rows // n_f, cols // n_i), lambda i, f: (f, i))
    assert rows % (16 * n_i) == 0, shape
    return pl.BlockSpec((rows // n_i, cols), lambda i, f: (i, 0))


def _ffn_ln(x, wg, wu, wd, g, b, to_cast=(), tm=512, tf=512):
    t, d = x.shape
    ff = wg.shape[1]
    n_i, n_f = t // tm, ff // tf
    cast_specs = [_cast_spec(w.shape, n_i, n_f) for w in to_cast]
    outs = pl.pallas_call(
        functools.partial(_ffn_ln_kernel, n_cast=len(to_cast)),
        grid=(n_i, n_f),
        in_specs=[
            pl.BlockSpec((tm, d), lambda i, f: (i, 0)),
            pl.BlockSpec((d, tf), lambda i, f: (0, f)),
            pl.BlockSpec((d, tf), lambda i, f: (0, f)),
            pl.BlockSpec((tf, d), lambda i, f: (f, 0)),
            pl.BlockSpec((1, d), lambda i, f: (0, 0)),
            pl.BlockSpec((1, d), lambda i, f: (0, 0)),
        ] + cast_specs,
        out_specs=[pl.BlockSpec((tm, d), lambda i, f: (i, 0))] + cast_specs,
        out_shape=[jax.ShapeDtypeStruct((t, d), F32)]
        + [jax.ShapeDtypeStruct(w.shape, BF16) for w in to_cast],
        scratch_shapes=[pltpu.VMEM((tm, d), BF16), pltpu.VMEM((tm, d), F32)],
        compiler_params=_cparams(("parallel", "arbitrary")),
        name="ffn_ln",
    )(x, wg, wu, wd, g, b, *to_cast)
    return outs[0], list(outs[1:])


def _proj_kernel(x_ref, w_ref, o_ref, xb_ref):
    @pl.when(pl.program_id(1) == 0)
    def _():
        xb_ref[...] = x_ref[...].astype(BF16)

    o_ref[...] = _dot_tb(xb_ref[...], w_ref[...].astype(BF16))


def _proj(x, w_t, n, tm=1024, tn=1024):
    t, d = x.shape
    return pl.pallas_call(
        _proj_kernel,
        grid=(t // tm, n // tn),
        in_specs=[
            pl.BlockSpec((tm, d), lambda i, j: (i, 0)),
            pl.BlockSpec((tn, d), lambda i, j: (j, 0)),
        ],
        out_specs=pl.BlockSpec((tm, tn), lambda i, j: (i, j)),
        out_shape=jax.ShapeDtypeStruct((t, n), F32),
        scratch_shapes=[pltpu.VMEM((tm, d), BF16)],
        compiler_params=_cparams(("parallel", "arbitrary")),
        name="proj",
    )(x, w_t)


def _rel_bucket(dist):
    max_exact = REL_BUCKETS // 2
    n = jnp.maximum(dist, 0)
    nf = jnp.maximum(n, 1).astype(F32)
    large = max_exact + (jnp.log(nf / max_exact) / math.log(REL_MAX_DIST / max_exact)
                         * (REL_BUCKETS - max_exact)).astype(jnp.int32)
    large = jnp.minimum(large, REL_BUCKETS - 1)
    return jnp.where(n < max_exact, n, large)


def _bias_tab_kernel(rb_ref, o_ref):
    p = pl.program_id(0)
    key = lax.broadcasted_iota(jnp.int32, (MOBA_BLOCK, MOBA_BLOCK), 0)
    qry = lax.broadcasted_iota(jnp.int32, (MOBA_BLOCK, MOBA_BLOCK), 1)
    for kind in range(2):
        dist = qry - key + kind * MOBA_BLOCK
        bucket = _rel_bucket(dist)
        for hh in range(PAIR):
            h = PAIR * p + hh
            acc = jnp.zeros((MOBA_BLOCK, MOBA_BLOCK), F32)
            for bkt in range(REL_BUCKETS):
                acc = jnp.where(bucket == bkt, rb_ref[bkt, h], acc)
            acc = acc * LOG2E
            if kind == 0:
                acc = jnp.where(dist >= 0, acc, NEG)
            o_ref[0, 2 * hh + kind] = acc


def _bias_tab(rel_bias):
    npair = ATT_HEADS // PAIR
    return pl.pallas_call(
        _bias_tab_kernel,
        grid=(npair,),
        in_specs=[pl.BlockSpec(memory_space=pltpu.SMEM)],
        out_specs=pl.BlockSpec((1, 2 * PAIR, MOBA_BLOCK, MOBA_BLOCK), lambda p: (p, 0, 0, 0)),
        out_shape=jax.ShapeDtypeStruct((npair, 2 * PAIR, MOBA_BLOCK, MOBA_BLOCK), F32),
        compiler_params=_cparams(("arbitrary",)),
        name="bias_tab",
    )(rel_bias)


def _moba_kernel(rb_ref, q_ref, k_ref, v_ref, tab_ref, o_ref, kb_scr, vt_scr, kmean_scr):
    p = pl.program_id(1)
    i0 = pl.program_id(2) * MOBA_QBLOCKS
    nb = kb_scr.shape[0]
    blk = MOBA_BLOCK

    @pl.when(i0 == 0)
    def _():
        lane_k = lax.broadcasted_iota(jnp.int32, (blk, LANE), 1)
        lane_m = lax.broadcasted_iota(jnp.int32, (1, LANE), 1)
        ones = jnp.ones((ONES_ROWS, blk), BF16)
        for n in range(nb):
            kblk = k_ref[0, n * blk:(n + 1) * blk, :]
            kb_scr[n, :, :LANE] = kblk.astype(BF16)
            hot = (lane_k == n) | (lane_k == nb + n) | (lane_k == 2 * nb + n)
            kb_scr[n, :, LANE:] = jnp.where(hot, 1.0, 0.0).astype(BF16)
            kmean = jnp.mean(kblk, axis=0, keepdims=True)
            v_t = v_ref[0, n * blk:(n + 1) * blk, :].T.astype(BF16)
            for hh in range(PAIR):
                in_head = (lane_m >= hh * HEAD_DIM) & (lane_m < (hh + 1) * HEAD_DIM)
                kmean_scr[hh * nb + n:hh * nb + n + 1, :] = jnp.where(in_head, kmean, 0.0)
                vt_scr[n, hh * VT_ROWS:hh * VT_ROWS + HEAD_DIM, :] = v_t[hh * HEAD_DIM:(hh + 1) * HEAD_DIM, :]
                vt_scr[n, hh * VT_ROWS + HEAD_DIM:(hh + 1) * VT_ROWS, :] = ones

    q8 = q_ref[0] * (HEAD_DIM ** -0.5)
    qs = q8 * LOG2E
    nq = q8.shape[0]
    lane = lax.broadcasted_iota(jnp.int32, q8.shape, 1)
    nidx = lax.broadcasted_iota(jnp.int32, (nb, nq), 0)
    own = i0 + lax.broadcasted_iota(jnp.int32, (1, nq), 1) // blk
    gate_all = _dot_tb(kmean_scr[...], q8, precision=HI)

    q_tail, q_far, adj_mask = [], [], []
    for hh in range(PAIR):
        gate = gate_all[hh * nb:(hh + 1) * nb]
        avail = nidx < own
        sel = jnp.zeros((nb, nq), jnp.bool_)
        for _ in range(MOBA_TOPK):
            best = jnp.max(jnp.where(avail, gate, -jnp.inf), axis=0, keepdims=True)
            first = jnp.min(jnp.where(avail & (gate == best), nidx, nb), axis=0, keepdims=True)
            pick = nidx == first
            sel = sel | pick
            avail = avail & jnp.logical_not(pick)
        b_far = rb_ref[REL_BUCKETS - 1, PAIR * p + hh] * LOG2E
        rowb = jnp.where(sel & (nidx < own - 1), b_far, NEG)
        adj_mask.append(jnp.max(jnp.where(sel & (nidx == own - 1), 0.0, NEG), axis=0, keepdims=True))
        hi, mid, lo = _split3(rowb)
        parts = jnp.concatenate([hi.astype(F32), mid.astype(F32), lo.astype(F32),
                                 jnp.zeros((LANE - 3 * nb, nq), F32)], axis=0)
        in_head = (lane >= hh * HEAD_DIM) & (lane < (hh + 1) * HEAD_DIM)
        q_h = jnp.where(in_head, qs, 0.0).astype(BF16)
        q_tail.append(q_h)
        q_far.append(jnp.concatenate([q_h, parts.T.astype(BF16)], axis=1))

    def vt(hh, n):
        return vt_scr[n, hh * VT_ROWS:(hh + 1) * VT_ROWS, :]

    def head_update(key, hh, blocks, prev, out):
        s_list = [thunk() for _, thunk in blocks]
        yield
        m_list = [jnp.max(s, axis=0, keepdims=True) for s in s_list]
        yield
        acc_list = [_dot(vt(hh, n), jnp.exp2(s - m_s).astype(BF16))
                    for s, m_s, (n, _) in zip(s_list, m_list, blocks)]
        m_new = m_list[0] if prev is None else jnp.maximum(prev[0], m_list[0])
        for m_s in m_list[1:]:
            m_new = jnp.maximum(m_new, m_s)
        yield
        acc = None if prev is None else jnp.exp2(prev[0] - m_new) * prev[1]
        for m_s, acc_s in zip(m_list, acc_list):
            term = jnp.exp2(m_s - m_new) * acc_s
            acc = term if acc is None else acc + term
        out[key] = (m_new, acc)

    def run_chains(gens):
        for _ in itertools.zip_longest(*gens):
            pass

    def tail_blocks(hh, c):
        cols = slice(c * blk, (c + 1) * blk)
        i_c = i0 + c
        n_adj = jnp.maximum(i_c - 1, 0)
        q_c = q_tail[hh][cols]

        def own_blk():
            return _dot_tb(kb_scr[i_c, :, :LANE], q_c) + tab_ref[0, 2 * hh]

        def prev_blk():
            return _dot_tb(kb_scr[n_adj, :, :LANE], q_c) + tab_ref[0, 2 * hh + 1] + adj_mask[hh][:, cols]

        return [(i_c, own_blk), (n_adj, prev_blk)]

    tails = {}
    run_chains([head_update((hh, c), hh, tail_blocks(hh, c), None, tails)
                for hh in range(PAIR) for c in range(MOBA_QBLOCKS)])
    state = [tuple(jnp.concatenate([tails[hh, c][s] for c in range(MOBA_QBLOCKS)], axis=1) for s in range(2))
             for hh in range(PAIR)]

    def far_body(j, carry):
        def blocks(hh):
            out = []
            for u in range(MOBA_UNROLL):
                nc = jnp.minimum(j * MOBA_UNROLL + u, nb - 1)
                out.append((nc, lambda nc=nc: _dot_tb(kb_scr[nc], q_far[hh])))
            return out

        out = [None] * PAIR
        run_chains([head_update(hh, hh, blocks(hh), carry[2 * hh:2 * hh + 2], out) for hh in range(PAIR)])
        return tuple(x for st in out for x in st)

    n_far = jnp.maximum(i0 + MOBA_QBLOCKS - 2, 0)
    trips = (n_far + MOBA_UNROLL - 1) // MOBA_UNROLL
    flat = lax.fori_loop(0, trips, far_body, tuple(x for st in state for x in st))
    out_t = jnp.concatenate(
        [flat[2 * hh + 1][:HEAD_DIM] / flat[2 * hh + 1][HEAD_DIM:HEAD_DIM + 1] for hh in range(PAIR)], axis=0)
    o_ref[0] = out_t.T


def _moba(rel_bias, u_main, tab, batch, seq):
    npair = ATT_HEADS // PAIR
    nb = seq // MOBA_BLOCK
    assert 3 * nb <= LANE and nb % MOBA_UNROLL == 0 and nb % MOBA_QBLOCKS == 0
    kcol = ATT_WIDTH // LANE
    qrows = MOBA_QBLOCKS * MOBA_BLOCK
    return pl.pallas_call(
        _moba_kernel,
        grid=(batch, npair, nb // MOBA_QBLOCKS),
        in_specs=[
            pl.BlockSpec(memory_space=pltpu.SMEM),
            pl.BlockSpec((1, qrows, LANE), lambda b, p, i: (b, i, p)),
            pl.BlockSpec((1, seq, LANE), lambda b, p, i: (b, 0, kcol + p)),
            pl.BlockSpec((1, seq, LANE), lambda b, p, i: (b, 0, 2 * kcol + p)),
            pl.BlockSpec((1, 2 * PAIR, MOBA_BLOCK, MOBA_BLOCK), lambda b, p, i: (p, 0, 0, 0)),
        ],
        out_specs=pl.BlockSpec((1, qrows, LANE), lambda b, p, i: (b, i, p)),
        out_shape=jax.ShapeDtypeStruct((batch, seq, ATT_WIDTH), F32),
        scratch_shapes=[
            pltpu.VMEM((nb, MOBA_BLOCK, 2 * LANE), BF16),
            pltpu.VMEM((nb, PAIR * VT_ROWS, MOBA_BLOCK), BF16),
            pltpu.VMEM((PAIR * nb, LANE), F32),
        ],
        compiler_params=_cparams(("parallel", "parallel", "arbitrary")),
        name="moba",
    )(rel_bias, u_main, u_main, u_main, tab)


def _rwkv_prep_kernel(ur_ref, uk_ref, uv_ref, ut_ref, pr_ref, pk_ref, pv_ref, pt_ref,
                      mixm_ref, mixt_ref, w0_ref, w2_ref, a0_ref, a2_ref, g2_ref, kk_ref, ka_ref,
                      esum_ref,
                      r_out, k_out, v_out, lw_out, aa_out, bb_out, g_out, *, tiles_per_seq):
    first = (pl.program_id(0) % tiles_per_seq) == 0
    width = RWKV_WIDTH

    def shifted(x_ref, p_ref, mix):
        x = x_ref[...]
        prev = pltpu.roll(x, 1, 0)
        row = lax.broadcasted_iota(jnp.int32, x.shape, 0)
        last = jnp.where(first, 0.0, p_ref[7:8, :])
        prev = jnp.where(row == 0, last, prev)
        return x + (prev - x) * mix

    r = shifted(ur_ref, pr_ref, mixm_ref[:, 0:width])
    k = shifted(uk_ref, pk_ref, mixm_ref[:, width:2 * width])
    v = shifted(uv_ref, pv_ref, mixm_ref[:, 2 * width:3 * width])
    tail = shifted(ut_ref, pt_ref, mixt_ref[...])
    wd = tail[:, TAIL_WD:TAIL_WD + LANE]
    ad = tail[:, TAIL_AD:TAIL_AD + LANE]
    gd = tail[:, TAIL_GD:TAIL_GD + GD_PAD]

    z = -(w0_ref[...] + _dot(jnp.tanh(wd).astype(BF16), w2_ref[...]))
    softplus = jnp.maximum(z, 0.0) + jnp.log(1.0 + jnp.exp(-jnp.abs(z)))
    w_log = -softplus - 0.5
    lw_out[...] = -jnp.exp(w_log)
    a = jax.nn.sigmoid(a0_ref[...] + _dot(ad.astype(BF16), a2_ref[...]))
    g_out[...] = _dot(jax.nn.sigmoid(gd).astype(BF16), g2_ref[...])
    kk = k * kk_ref[...]
    kk2 = kk * kk
    ss = jnp.concatenate([_dot3_exact_rhs(kk2[:, c:c + QUAD], esum_ref[...])
                          for c in range(0, width, QUAD)], axis=1)
    kk = kk / jnp.maximum(jnp.sqrt(ss), 1e-12)
    r_out[...] = r
    k_out[...] = k * (1.0 + (a - 1.0) * ka_ref[...])
    v_out[...] = v
    aa_out[...] = -kk
    bb_out[...] = kk * a


def _rwkv_prep(u_main, u_tail, mix_main, mix_tail, w0, w2, a0, a2, g2, k_k, k_a, esum, seq, tm=256):
    t = u_main.shape[0]
    width = RWKV_WIDTH
    col0 = 3 * ATT_WIDTH // width
    tiles_per_seq = seq // tm
    rows8 = tm // 8

    def main_spec(c):
        return pl.BlockSpec((tm, width), lambda i: (i, col0 + c))

    def prev_spec(c):
        return pl.BlockSpec((8, width), lambda i: (jnp.maximum(i * rows8 - 1, 0), col0 + c))

    full = lambda shape: pl.BlockSpec(shape, lambda i: (0,) * len(shape))
    out_spec = pl.BlockSpec((tm, width), lambda i: (i, 0))
    return pl.pallas_call(
        functools.partial(_rwkv_prep_kernel, tiles_per_seq=tiles_per_seq),
        grid=(t // tm,),
        in_specs=[
            main_spec(0), main_spec(1), main_spec(2),
            pl.BlockSpec((tm, TAIL_COLS), lambda i: (i, 0)),
            prev_spec(0), prev_spec(1), prev_spec(2),
            pl.BlockSpec((8, TAIL_COLS), lambda i: (jnp.maximum(i * rows8 - 1, 0), 0)),
            full((1, 3 * width)), full((1, TAIL_COLS)),
            full((1, width)), full((LANE, width)), full((1, width)), full((LANE, width)),
            full((GD_PAD, width)), full((1, width)), full((1, width)),
            full((QUAD, QUAD)),
        ],
        out_specs=[out_spec] * 7,
        out_shape=[jax.ShapeDtypeStruct((t, width), F32)] * 7,
        compiler_params=_cparams(("parallel",)),
        name="rwkv_prep",
    )(u_main, u_main, u_main, u_tail, u_main, u_main, u_main, u_tail,
      mix_main, mix_tail, w0, w2, a0, a2, g2, k_k, k_a, esum)


def _rwkv_chunk_kernel(r_ref, k_ref, v_ref, lw_ref, aa_ref, bb_ref, g_ref,
                       rk_ref, gng_ref, gnb_ref, eq_ref, o_ref, st_scr):
    c = pl.program_id(0)
    nquad = RWKV_WIDTH // QUAD

    @pl.when(c == 0)
    def _():
        st_scr[...] = jnp.zeros_like(st_scr)

    eq = eq_ref[...]
    levels = int(math.log2(CHUNK))
    row_h = lax.broadcasted_iota(jnp.int32, (QUAD, QUAD), 0) // HEAD_DIM
    col_h = lax.broadcasted_iota(jnp.int32, (QUAD, QUAD), 1) // HEAD_DIM
    bdmask = row_h == col_h
    eye = (lax.broadcasted_iota(jnp.int32, (QUAD, QUAD), 0)
           == lax.broadcasted_iota(jnp.int32, (QUAD, QUAD), 1))
    t_idx = lax.broadcasted_iota(jnp.int32, (CHUNK, QUAD), 0)
    s_idx = lax.broadcasted_iota(jnp.int32, (CHUNK, QUAD), 1) % CHUNK
    strict = t_idx > s_idx
    incl = t_idx >= s_idx
    both = jnp.concatenate([strict, incl], axis=0)
    reps = QUAD // CHUNK

    def bds(x):
        xb = x.astype(BF16)
        return jnp.where(bdmask, jnp.concatenate([xb] * reps, axis=0), jnp.zeros((), BF16))

    def mm(a, b):
        return _dot(a.astype(BF16), b.astype(BF16))

    def quad_stages(bi, qd):
        sl = slice(qd * QUAD, (qd + 1) * QUAD)
        st_ref = st_scr.at[bi * nquad + qd]
        r = r_ref[bi, :, sl]
        k = k_ref[bi, :, sl]
        v = v_ref[bi, :, sl]
        lw = lw_ref[bi, :, sl]
        aa = aa_ref[bi, :, sl]
        bb = bb_ref[bi, :, sl]

        cum = lw
        for sh in (1 << b for b in range(levels)):
            cum = cum + jnp.where(t_idx >= sh, pltpu.roll(cum, sh, 0), 0.0)
        ctot = cum[CHUNK - 1:CHUNK, :]
        e_neg = jnp.exp(-cum)
        e_tol = jnp.exp(ctot - cum)
        a_t = aa * jnp.exp(cum - lw)
        r_t = r * jnp.exp(cum)
        b_c = bb * e_neg
        k_c = k * e_neg
        b_h = bb * e_tol
        k_h = k * e_tol
        gamma = jnp.exp(ctot)

        lhs = jnp.concatenate([a_t, r_t], axis=0).astype(BF16)
        pb = _dot_tb(lhs, bds(b_c))
        pk = _dot_tb(lhs, bds(k_c))
        bk_t = jnp.concatenate([b_h, k_h], axis=0).T.astype(BF16)
        yield
        a_rb = jnp.where(incl, pb[CHUNK:], 0.0).astype(BF16)
        akrk = mm(jnp.where(both, pk, 0.0), bds(v))
        n = jnp.where(strict, pb[:CHUNK], 0.0)
        t_inv = jnp.where(t_idx == s_idx, 1.0, 0.0) + n
        n = mm(n, bds(n))
        yield
        for lvl in range(1, levels):
            bd_n = bds(n)
            if lvl < levels - 1:
                tn = mm(jnp.concatenate([t_inv, n], axis=0), bd_n)
                t_inv = t_inv + tn[:CHUNK]
                n = tn[CHUNK:]
            else:
                t_inv = t_inv + mm(t_inv, bd_n)
            yield
        w = mm(t_inv, bds(a_t))
        u = mm(t_inv, bds(akrk[:CHUNK]))
        yield
        q_t = r_t + mm(a_rb, bds(w))
        y0 = mm(a_rb, bds(u)) + akrk[CHUNK:]
        w0 = jnp.concatenate([w, jnp.zeros_like(w)], axis=0)
        g_mat = jnp.where(bdmask, mm(bk_t, w0), 0.0) + jnp.where(eye, gamma, 0.0)
        h_mat = jnp.where(bdmask, mm(bk_t, jnp.concatenate([u, v], axis=0)), 0.0)
        yield
        gy = mm(jnp.concatenate([g_mat, q_t], axis=0), st_ref[...])
        st_ref[...] = gy[:QUAD] + h_mat
        y = gy[QUAD:] + y0
        yield

        inv_n = 1.0 / HEAD_DIM
        stats = mm(jnp.concatenate([y, r * k * rk_ref[:, sl]], axis=0), eq)
        mu = stats[:CHUNK] * inv_n
        bonus = stats[CHUNK:] * v
        yield
        d = y - mu
        var = mm(d * d, eq) * inv_n
        yield
        yn = d * lax.rsqrt(var + GN_EPS) * gng_ref[:, sl] + gnb_ref[:, sl]
        o_ref[bi, :, sl] = (yn + bonus) * g_ref[bi, :, sl]

    tiles = [quad_stages(bi, qd) for bi in range(r_ref.shape[0]) for qd in range(nquad)]
    for _ in itertools.zip_longest(*tiles):
        pass


def _rwkv_chunk(r, k, v, lw, aa, bb, g, r_k, gn_g, gn_b, eq):
    batch, seq, width = r.shape
    tok = pl.BlockSpec((batch, CHUNK, width), lambda c: (0, c, 0))
    full = lambda shape: pl.BlockSpec(shape, lambda c: (0,) * len(shape))
    return pl.pallas_call(
        _rwkv_chunk_kernel,
        grid=(seq // CHUNK,),
        in_specs=[tok] * 7 + [full((1, width))] * 3 + [full((QUAD, QUAD))],
        out_specs=tok,
        out_shape=jax.ShapeDtypeStruct((batch, seq, width), F32),
        scratch_shapes=[pltpu.VMEM((batch * width // QUAD, QUAD, QUAD), F32)],
        compiler_params=_cparams(("arbitrary",)),
        name="rwkv_chunk",
    )(r, k, v, lw, aa, bb, g, r_k, gn_g, gn_b, eq)


def _out_ln_kernel(h_ref, att_ref, rw_ref, wa_ref, wr_ref, g_ref, b_ref, o_ref):
    mix = _dot(att_ref[...].astype(BF16), wa_ref[...]) + _dot(rw_ref[...].astype(BF16), wr_ref[...])
    o_ref[...] = _layer_norm(ALPHA * h_ref[...] + mix, g_ref[...], b_ref[...])


def _out_ln(h, att, rw, w_out, g, b, tm=512):
    t, d = h.shape
    wa, wr = att.shape[1], rw.shape[1]
    assert wa == wr and w_out.shape[0] == wa + wr
    row = lambda n: pl.BlockSpec((tm, n), lambda i: (i, 0))
    full = lambda shape: pl.BlockSpec(shape, lambda i: (0,) * len(shape))
    return pl.pallas_call(
        _out_ln_kernel,
        grid=(t // tm,),
        in_specs=[row(d), row(wa), row(wr),
                  pl.BlockSpec((wa, d), lambda i: (0, 0)), pl.BlockSpec((wr, d), lambda i: (1, 0)),
                  full((1, d)), full((1, d))],
        out_specs=row(d),
        out_shape=jax.ShapeDtypeStruct((t, d), F32),
        compiler_params=_cparams(("parallel",)),
        name="out_ln",
    )(h, att, rw, w_out, w_out, g, b)


def _ple_ln_kernel(h_ref, p_ref, wg_ref, bg_ref, wp_ref, g_ref, b_ref, o_ref):
    h = h_ref[...]
    e = _dot(p_ref[...].astype(BF16), wp_ref[...])
    gate = jax.nn.sigmoid(_dot(h.astype(BF16), wg_ref[...]) + bg_ref[...])
    o_ref[...] = _layer_norm(ALPHA * h + gate * e, g_ref[...], b_ref[...])


def _ple_ln(h, p, w_gate, b_gate, w_up, g, b, tm=512):
    t, d = h.shape
    row = lambda n: pl.BlockSpec((tm, n), lambda i: (i, 0))
    full = lambda shape: pl.BlockSpec(shape, lambda i: (0,) * len(shape))
    return pl.pallas_call(
        _ple_ln_kernel,
        grid=(t // tm,),
        in_specs=[row(d), row(p.shape[1]), full(w_gate.shape), full((1, d)), full(w_up.shape),
                  full((1, d)), full((1, d))],
        out_specs=row(d),
        out_shape=jax.ShapeDtypeStruct((t, d), F32),
        compiler_params=_cparams(("parallel",)),
        name="ple_ln",
    )(h, p, w_gate, b_gate, w_up, g, b)


def _pad_rows(w, rows):
    return jnp.pad(w, ((0, rows - w.shape[0]), (0, 0)))


def _pack_tail(cols, axis):
    wd, ad, gd = jnp.split(cols, [DECAY_LORA, DECAY_LORA + AAA_LORA], axis=axis)

    def pad(x, n):
        widths = [(0, 0)] * x.ndim
        widths[axis] = (0, n - x.shape[axis])
        return jnp.pad(x, widths)

    return jnp.concatenate([pad(wd, LANE), pad(ad, LANE), pad(gd, GD_PAD)], axis=axis)


def kernel(x, p, ffn1_w_gate, ffn1_w_up, ffn1_w_down, ln1_g, ln1_b, w_in, rel_bias, shift_mix, decay_w0, decay_w2, a_a0, a_a2, gate_g2, k_k, k_a, r_k, gn_g, gn_b, w_out, ln2_g, ln2_b, ffn2_w_gate, ffn2_w_up, ffn2_w_down, ln3_g, ln3_b, ple_w_up, ple_w_gate, ple_b_gate, ln4_g, ln4_b):
    batch, seq, d = x.shape
    t = batch * seq
    row = lambda a: a.reshape(1, -1)
    main_cols = 3 * ATT_WIDTH + 3 * RWKV_WIDTH

    heads = jnp.arange(QUAD) // HEAD_DIM
    esum = (heads[:, None] == heads[None, :]).astype(BF16)
    tab = _bias_tab(rel_bias)

    h = x.reshape(t, d).astype(F32)
    for i in range(DEPTH):
        later_weights = (ffn2_w_gate[i], ffn2_w_up[i], ffn2_w_down[i], w_out[i], ple_w_gate[i], ple_w_up[i])
        h, (wg2, wu2, wd2, wo, wpg, wpu) = _ffn_ln(
            h, ffn1_w_gate[i].astype(BF16), ffn1_w_up[i].astype(BF16), ffn1_w_down[i].astype(BF16),
            row(ln1_g[i]), row(ln1_b[i]), to_cast=later_weights)
        w_in_t = w_in[i].T
        u_main = _proj(h, w_in_t, main_cols)
        u_tail = _proj(h, _pack_tail(w_in_t[main_cols:], axis=0), TAIL_COLS, tn=TAIL_COLS)
        att = _moba(rel_bias, u_main.reshape(batch, seq, main_cols), tab, batch, seq)
        mix = shift_mix[i]
        prep = _rwkv_prep(
            u_main, u_tail, row(mix[:3 * RWKV_WIDTH]), row(_pack_tail(mix[3 * RWKV_WIDTH:], axis=0)),
            row(decay_w0[i]), _pad_rows(decay_w2[i], LANE).astype(BF16), row(a_a0[i]),
            _pad_rows(a_a2[i], LANE).astype(BF16), _pad_rows(gate_g2[i], GD_PAD).astype(BF16),
            row(k_k[i]), row(k_a[i]), esum, seq)
        prep = [a.reshape(batch, seq, RWKV_WIDTH) for a in prep]
        rw = _rwkv_chunk(*prep, row(r_k[i]), row(gn_g[i]), row(gn_b[i]), esum)
        h = _out_ln(h, att.reshape(t, ATT_WIDTH), rw.reshape(t, RWKV_WIDTH), wo,
                    row(ln2_g[i]), row(ln2_b[i]))
        h, _ = _ffn_ln(h, wg2, wu2, wd2, row(ln3_g[i]), row(ln3_b[i]))
        h = _ple_ln(h, p[i].reshape(t, PLE_DIM).astype(F32), wpg, row(ple_b_gate[i]), wpu,
                    row(ln4_g[i]), row(ln4_b[i]))
    return h.reshape(batch, seq, d).astype(x.dtype)
```

```python
import functools
import itertools
import math

import jax
import jax.numpy as jnp
from jax import lax
from jax.experimental import pallas as pl
from jax.experimental.pallas import tpu as pltpu

D_MODEL = 2048
DEPTH = 1
HEAD_DIM = 64
ATT_HEADS = 16
RWKV_HEADS = 16
ATT_WIDTH = ATT_HEADS * HEAD_DIM
RWKV_WIDTH = RWKV_HEADS * HEAD_DIM
MOBA_BLOCK = 256
MOBA_TOPK = 3
MOBA_UNROLL = 4
MOBA_QBLOCKS = 2
LOG2E = 1.4426950408889634
REL_BUCKETS = 32
REL_MAX_DIST = 128
DECAY_LORA = 64
AAA_LORA = 64
GATE_LORA = 160
D_FF = 5632
PLE_DIM = 256
LN_EPS = 1e-5
GN_EPS = 64e-5
NEG = -1e30
ALPHA = (2 * DEPTH) ** 0.25

LANE = 128
PAIR = LANE // HEAD_DIM
CHUNK = 64
QUAD = 256
HEADS_PER_QUAD = QUAD // HEAD_DIM
ONES_ROWS = 16
VT_ROWS = HEAD_DIM + ONES_ROWS
TAIL_COLS = 512
TAIL_WD, TAIL_AD, TAIL_GD = 0, 128, 256
GD_PAD = 256
VMEM_LIMIT = 58 * 1024 * 1024

F32 = jnp.float32
BF16 = jnp.bfloat16
HI = lax.Precision.HIGHEST


def _cparams(sem):
    return pltpu.CompilerParams(dimension_semantics=sem, vmem_limit_bytes=VMEM_LIMIT)


def _layer_norm(y, g, b):
    mu = jnp.mean(y, axis=-1, keepdims=True)
    d = y - mu
    var = jnp.mean(d * d, axis=-1, keepdims=True)
    return d * lax.rsqrt(var + LN_EPS) * g + b


def _dot(a, b):
    return jnp.dot(a, b, preferred_element_type=F32)


def _dot_hi(a, b):
    return jnp.dot(a, b, preferred_element_type=F32, precision=HI)


def _dot_tb(a, b, precision=None):
    return lax.dot_general(a, b, (((1,), (1,)), ((), ())), preferred_element_type=F32,
                           precision=precision)


def _split3(x):
    hi = x.astype(BF16)
    r1 = x - hi.astype(F32)
    mid = r1.astype(BF16)
    lo = (r1 - mid.astype(F32)).astype(BF16)
    return hi, mid, lo


def _dot3_exact_rhs(x, w_bf16):
    hi, mid, lo = _split3(x)
    return _dot(hi, w_bf16) + _dot(mid, w_bf16) + _dot(lo, w_bf16)


def _dot3_exact_lhs(w_bf16, x):
    hi, mid, lo = _split3(x)
    return _dot(w_bf16, hi) + _dot(w_bf16, mid) + _dot(w_bf16, lo)


def _ffn_ln_kernel(*refs, n_cast):
    x_ref, wg_ref, wu_ref, wd_ref, g_ref, b_ref = refs[:6]
    cast_in = refs[6:6 + n_cast]
    o_ref = refs[6 + n_cast]
    cast_out = refs[7 + n_cast:7 + 2 * n_cast]
    (xb_ref,) = refs[7 + 2 * n_cast:]
    f = pl.program_id(1)

    @pl.when(f == 0)
    def _():
        xb_ref[...] = x_ref[...].astype(BF16)
        o_ref[...] = jnp.zeros_like(o_ref)

    xb = xb_ref[...]
    gate = _dot(xb, wg_ref[...])
    up = _dot(xb, wu_ref[...])
    act = gate * jax.nn.sigmoid(gate) * up
    o_ref[...] += _dot(act.astype(BF16), wd_ref[...])

    for src_ref, dst_ref in zip(cast_in, cast_out):
        dst_ref[...] = src_ref[...].astype(BF16)

    @pl.when(f == pl.num_programs(1) - 1)
    def _():
        y = ALPHA * x_ref[...] + 0.5 * o_ref[...]
        o_ref[...] = _layer_norm(y, g_ref[...], b_ref[...])


def _cast_spec(shape, n_i, n_f):
    rows, cols = shape
    if rows % (16 * n_i) == 0 and cols % (LANE * n_f) == 0:
        return pl.BlockSpec((rows // n_i, cols // n_f), lambda i, f: (i, f))
    if rows % (16 * n_f) == 0 and cols % (LANE * n_i) == 0:
        return pl.BlockSpec((rows // n_f, cols // n_i), lambda i, f: (f, i))
    assert rows % (16 * n_i) == 0, shape
    return pl.BlockSpec((rows // n_i, cols), lambda i, f: (i, 0))


def _ffn_ln(x, wg, wu, wd, g, b, to_cast=(), tm=512, tf=512):
    t, d = x.shape
    ff = wg.shape[1]
    n_i, n_f = t // tm, ff // tf
    to_cast = [w if isinstance(w, tuple) else (w, w.shape[0]) for w in to_cast]
    cast_shapes = [(rows, w.shape[1]) for w, rows in to_cast]
    to_cast = [w for w, _ in to_cast]
    cast_specs = [_cast_spec(shape, n_i, n_f) for shape in cast_shapes]
    outs = pl.pallas_call(
        functools.partial(_ffn_ln_kernel, n_cast=len(to_cast)),
        grid=(n_i, n_f),
        in_specs=[
            pl.BlockSpec((tm, d), lambda i, f: (i, 0)),
            pl.BlockSpec((d, tf), lambda i, f: (0, f)),
            pl.BlockSpec((d, tf), lambda i, f: (0, f)),
            pl.BlockSpec((tf, d), lambda i, f: (f, 0)),
            pl.BlockSpec((1, d), lambda i, f: (0, 0)),
            pl.BlockSpec((1, d), lambda i, f: (0, 0)),
        ] + cast_specs,
        out_specs=[pl.BlockSpec((tm, d), lambda i, f: (i, 0))] + cast_specs,
        out_shape=[jax.ShapeDtypeStruct((t, d), F32)]
        + [jax.ShapeDtypeStruct(shape, BF16) for shape in cast_shapes],
        scratch_shapes=[pltpu.VMEM((tm, d), BF16)],
        compiler_params=_cparams(("parallel", "arbitrary")),
        name="ffn_ln",
    )(x, wg, wu, wd, g, b, *to_cast)
    return outs[0], list(outs[1:])


def _proj_kernel(x_ref, w_ref, o_ref, xb_ref):
    @pl.when(pl.program_id(1) == 0)
    def _():
        xb_ref[...] = x_ref[...].astype(BF16)

    o_ref[...] = _dot_tb(xb_ref[...], w_ref[...].astype(BF16))


def _proj(x, w_t, n, tm=1024, tn=1024):
    t, d = x.shape
    return pl.pallas_call(
        _proj_kernel,
        grid=(t // tm, n // tn),
        in_specs=[
            pl.BlockSpec((tm, d), lambda i, j: (i, 0)),
            pl.BlockSpec((tn, d), lambda i, j: (j, 0)),
        ],
        out_specs=pl.BlockSpec((tm, tn), lambda i, j: (i, j)),
        out_shape=jax.ShapeDtypeStruct((t, n), F32),
        scratch_shapes=[pltpu.VMEM((tm, d), BF16)],
        compiler_params=_cparams(("parallel", "arbitrary")),
        name="proj",
    )(x, w_t)


def _rel_bucket(dist):
    max_exact = REL_BUCKETS // 2
    n = jnp.maximum(dist, 0)
    nf = jnp.maximum(n, 1).astype(F32)
    large = max_exact + (jnp.log(nf / max_exact) / math.log(REL_MAX_DIST / max_exact)
                         * (REL_BUCKETS - max_exact)).astype(jnp.int32)
    large = jnp.minimum(large, REL_BUCKETS - 1)
    return jnp.where(n < max_exact, n, large)


def _bias_tab_kernel(rb_ref, o_ref):
    p = pl.program_id(0)
    key = lax.broadcasted_iota(jnp.int32, (MOBA_BLOCK, MOBA_BLOCK), 0)
    qry = lax.broadcasted_iota(jnp.int32, (MOBA_BLOCK, MOBA_BLOCK), 1)
    for kind in range(2):
        dist = qry - key + kind * MOBA_BLOCK
        bucket = _rel_bucket(dist)
        for hh in range(PAIR):
            h = PAIR * p + hh
            acc = jnp.zeros((MOBA_BLOCK, MOBA_BLOCK), F32)
            for bkt in range(REL_BUCKETS):
                acc = jnp.where(bucket == bkt, rb_ref[bkt, h], acc)
            acc = acc * LOG2E
            if kind == 0:
                acc = jnp.where(dist >= 0, acc, NEG)
            o_ref[0, 2 * hh + kind] = acc


def _bias_tab(rel_bias):
    npair = ATT_HEADS // PAIR
    return pl.pallas_call(
        _bias_tab_kernel,
        grid=(npair,),
        in_specs=[pl.BlockSpec(memory_space=pltpu.SMEM)],
        out_specs=pl.BlockSpec((1, 2 * PAIR, MOBA_BLOCK, MOBA_BLOCK), lambda p: (p, 0, 0, 0)),
        out_shape=jax.ShapeDtypeStruct((npair, 2 * PAIR, MOBA_BLOCK, MOBA_BLOCK), F32),
        compiler_params=_cparams(("arbitrary",)),
        name="bias_tab",
    )(rel_bias)


def _moba_kernel(rb_ref, q_ref, k_ref, v_ref, tab_ref, o_ref, kb_scr, vt_scr, kmean_scr):
    p = pl.program_id(1)
    i0 = pl.program_id(2) * MOBA_QBLOCKS
    nb = kb_scr.shape[0]
    blk = MOBA_BLOCK

    @pl.when(i0 == 0)
    def _():
        lane_k = lax.broadcasted_iota(jnp.int32, (blk, LANE), 1)
        lane_m = lax.broadcasted_iota(jnp.int32, (1, LANE), 1)
        ones = jnp.ones((ONES_ROWS, blk), BF16)
        for n in range(nb):
            kblk = k_ref[0, n * blk:(n + 1) * blk, :]
            kb_scr[n, :, :LANE] = kblk.astype(BF16)
            hot = (lane_k == n) | (lane_k == nb + n) | (lane_k == 2 * nb + n)
            kb_scr[n, :, LANE:] = jnp.where(hot, 1.0, 0.0).astype(BF16)
            kmean = jnp.mean(kblk, axis=0, keepdims=True)
            v_t = v_ref[0, n * blk:(n + 1) * blk, :].T.astype(BF16)
            for hh in range(PAIR):
                in_head = (lane_m >= hh * HEAD_DIM) & (lane_m < (hh + 1) * HEAD_DIM)
                kmean_scr[hh * nb + n:hh * nb + n + 1, :] = jnp.where(in_head, kmean, 0.0)
                vt_scr[n, hh * VT_ROWS:hh * VT_ROWS + HEAD_DIM, :] = v_t[hh * HEAD_DIM:(hh + 1) * HEAD_DIM, :]
                vt_scr[n, hh * VT_ROWS + HEAD_DIM:(hh + 1) * VT_ROWS, :] = ones

    q8 = q_ref[0] * (HEAD_DIM ** -0.5)
    qs = q8 * LOG2E
    nq = q8.shape[0]
    lane = lax.broadcasted_iota(jnp.int32, q8.shape, 1)
    nidx = lax.broadcasted_iota(jnp.int32, (nb, nq), 0)
    own = i0 + lax.broadcasted_iota(jnp.int32, (1, nq), 1) // blk
    gate_all = _dot_tb(kmean_scr[...], q8, precision=HI)

    q_tail, q_far, adj_mask = [], [], []
    for hh in range(PAIR):
        gate = gate_all[hh * nb:(hh + 1) * nb]
        avail = nidx < own
        sel = jnp.zeros((nb, nq), jnp.bool_)
        for _ in range(MOBA_TOPK):
            best = jnp.max(jnp.where(avail, gate, -jnp.inf), axis=0, keepdims=True)
            first = jnp.min(jnp.where(avail & (gate == best), nidx, nb), axis=0, keepdims=True)
            pick = nidx == first
            sel = sel | pick
            avail = avail & jnp.logical_not(pick)
        b_far = rb_ref[REL_BUCKETS - 1, PAIR * p + hh] * LOG2E
        rowb = jnp.where(sel & (nidx < own - 1), b_far, NEG)
        adj_mask.append(jnp.max(jnp.where(sel & (nidx == own - 1), 0.0, NEG), axis=0, keepdims=True))
        hi, mid, lo = _split3(rowb)
        parts = jnp.concatenate([hi.astype(F32), mid.astype(F32), lo.astype(F32),
                                 jnp.zeros((LANE - 3 * nb, nq), F32)], axis=0)
        in_head = (lane >= hh * HEAD_DIM) & (lane < (hh + 1) * HEAD_DIM)
        q_h = jnp.where(in_head, qs, 0.0).astype(BF16)
        q_tail.append(q_h)
        q_far.append(jnp.concatenate([q_h, parts.T.astype(BF16)], axis=1))

    def vt(hh, n):
        return vt_scr[n, hh * VT_ROWS:(hh + 1) * VT_ROWS, :]

    def head_update(key, hh, blocks, prev, out):
        s_list = [thunk() for _, thunk in blocks]
        yield
        m_list = [jnp.max(s, axis=0, keepdims=True) for s in s_list]
        yield
        acc_list = [_dot(vt(hh, n), jnp.exp2(s - m_s).astype(BF16))
                    for s, m_s, (n, _) in zip(s_list, m_list, blocks)]
        m_new = m_list[0] if prev is None else jnp.maximum(prev[0], m_list[0])
        for m_s in m_list[1:]:
            m_new = jnp.maximum(m_new, m_s)
        yield
        acc = None if prev is None else jnp.exp2(prev[0] - m_new) * prev[1]
        for m_s, acc_s in zip(m_list, acc_list):
            term = jnp.exp2(m_s - m_new) * acc_s
            acc = term if acc is None else acc + term
        out[key] = (m_new, acc)

    def run_chains(gens):
        for _ in itertools.zip_longest(*gens):
            pass

    def tail_blocks(hh, c):
        cols = slice(c * blk, (c + 1) * blk)
        i_c = i0 + c
        n_adj = jnp.maximum(i_c - 1, 0)
        q_c = q_tail[hh][cols]

        def own_blk():
            return _dot_tb(kb_scr[i_c, :, :LANE], q_c) + tab_ref[0, 2 * hh]

        def prev_blk():
            return _dot_tb(kb_scr[n_adj, :, :LANE], q_c) + tab_ref[0, 2 * hh + 1] + adj_mask[hh][:, cols]

        return [(i_c, own_blk), (n_adj, prev_blk)]

    tails = {}
    run_chains([head_update((hh, c), hh, tail_blocks(hh, c), None, tails)
                for hh in range(PAIR) for c in range(MOBA_QBLOCKS)])
    state = [tuple(jnp.concatenate([tails[hh, c][s] for c in range(MOBA_QBLOCKS)], axis=1) for s in range(2))
             for hh in range(PAIR)]

    def far_body(j, carry):
        def blocks(hh):
            out = []
            for u in range(MOBA_UNROLL):
                nc = jnp.minimum(j * MOBA_UNROLL + u, nb - 1)
                out.append((nc, lambda nc=nc: _dot_tb(kb_scr[nc], q_far[hh])))
            return out

        out = [None] * PAIR
        run_chains([head_update(hh, hh, blocks(hh), carry[2 * hh:2 * hh + 2], out) for hh in range(PAIR)])
        return tuple(x for st in out for x in st)

    n_far = jnp.maximum(i0 + MOBA_QBLOCKS - 2, 0)
    trips = (n_far + MOBA_UNROLL - 1) // MOBA_UNROLL
    flat = lax.fori_loop(0, trips, far_body, tuple(x for st in state for x in st))
    out_t = jnp.concatenate(
        [flat[2 * hh + 1][:HEAD_DIM] / flat[2 * hh + 1][HEAD_DIM:HEAD_DIM + 1] for hh in range(PAIR)], axis=0)
    o_ref[0] = out_t.T


def _moba(rel_bias, u_main, tab, batch, seq):
    npair = ATT_HEADS // PAIR
    nb = seq // MOBA_BLOCK
    assert 3 * nb <= LANE and nb % MOBA_UNROLL == 0 and nb % MOBA_QBLOCKS == 0
    kcol = ATT_WIDTH // LANE
    qrows = MOBA_QBLOCKS * MOBA_BLOCK
    return pl.pallas_call(
        _moba_kernel,
        grid=(batch, npair, nb // MOBA_QBLOCKS),
        in_specs=[
            pl.BlockSpec(memory_space=pltpu.SMEM),
            pl.BlockSpec((1, qrows, LANE), lambda b, p, i: (b, i, p)),
            pl.BlockSpec((1, seq, LANE), lambda b, p, i: (b, 0, kcol + p)),
            pl.BlockSpec((1, seq, LANE), lambda b, p, i: (b, 0, 2 * kcol + p)),
            pl.BlockSpec((1, 2 * PAIR, MOBA_BLOCK, MOBA_BLOCK), lambda b, p, i: (p, 0, 0, 0)),
        ],
        out_specs=pl.BlockSpec((1, qrows, LANE), lambda b, p, i: (b, i, p)),
        out_shape=jax.ShapeDtypeStruct((batch, seq, ATT_WIDTH), F32),
        scratch_shapes=[
            pltpu.VMEM((nb, MOBA_BLOCK, 2 * LANE), BF16),
            pltpu.VMEM((nb, PAIR * VT_ROWS, MOBA_BLOCK), BF16),
            pltpu.VMEM((PAIR * nb, LANE), F32),
        ],
        compiler_params=_cparams(("parallel", "parallel", "arbitrary")),
        name="moba",
    )(rel_bias, u_main, u_main, u_main, tab)


def _rwkv_prep_kernel(ur_ref, uk_ref, uv_ref, ut_ref, pr_ref, pk_ref, pv_ref, pt_ref,
                      mixm_ref, mixt_ref, w0_ref, w2_ref, a0_ref, a2_ref, g2_ref, kk_ref, ka_ref,
                      esum_ref,
                      r_out, k_out, v_out, lw_out, aa_out, bb_out, g_out, *, tiles_per_seq):
    first = (pl.program_id(0) % tiles_per_seq) == 0
    width = RWKV_WIDTH

    def shifted(x_ref, p_ref, mix):
        x = x_ref[...]
        prev = pltpu.roll(x, 1, 0)
        row = lax.broadcasted_iota(jnp.int32, x.shape, 0)
        last = jnp.where(first, 0.0, p_ref[7:8, :])
        prev = jnp.where(row == 0, last, prev)
        return x + (prev - x) * mix

    r = shifted(ur_ref, pr_ref, mixm_ref[:, 0:width])
    k = shifted(uk_ref, pk_ref, mixm_ref[:, width:2 * width])
    v = shifted(uv_ref, pv_ref, mixm_ref[:, 2 * width:3 * width])
    tail = shifted(ut_ref, pt_ref, mixt_ref[...])
    wd = tail[:, TAIL_WD:TAIL_WD + LANE]
    ad = tail[:, TAIL_AD:TAIL_AD + LANE]
    gd = tail[:, TAIL_GD:TAIL_GD + GD_PAD]

    z = -(w0_ref[...] + _dot(jnp.tanh(wd).astype(BF16), w2_ref[...]))
    softplus = jnp.maximum(z, 0.0) + jnp.log(1.0 + jnp.exp(-jnp.abs(z)))
    w_log = -softplus - 0.5
    lw_out[...] = -jnp.exp(w_log)
    a = jax.nn.sigmoid(a0_ref[...] + _dot(ad.astype(BF16), a2_ref[...]))
    g_out[...] = _dot(jax.nn.sigmoid(gd).astype(BF16), g2_ref[...]).astype(BF16)
    kk = k * kk_ref[...]
    kk2 = kk * kk
    ss = jnp.concatenate([_dot3_exact_rhs(kk2[:, c:c + QUAD], esum_ref[...])
                          for c in range(0, width, QUAD)], axis=1)
    kk = kk / jnp.maximum(jnp.sqrt(ss), 1e-12)
    r_out[...] = r.astype(BF16)
    k_out[...] = (k * (1.0 + (a - 1.0) * ka_ref[...])).astype(BF16)
    v_out[...] = v.astype(BF16)
    aa_out[...] = (-kk).astype(BF16)
    bb_out[...] = (kk * a).astype(BF16)


def _rwkv_prep(u_main, u_tail, mix_main, mix_tail, w0, w2, a0, a2, g2, k_k, k_a, esum, seq, tm=256):
    t = u_main.shape[0]
    width = RWKV_WIDTH
    col0 = 3 * ATT_WIDTH // width
    tiles_per_seq = seq // tm
    rows8 = tm // 8

    def main_spec(c):
        return pl.BlockSpec((tm, width), lambda i: (i, col0 + c))

    def prev_spec(c):
        return pl.BlockSpec((8, width), lambda i: (jnp.maximum(i * rows8 - 1, 0), col0 + c))

    full = lambda shape: pl.BlockSpec(shape, lambda i: (0,) * len(shape))
    out_spec = pl.BlockSpec((tm, width), lambda i: (i, 0))
    return pl.pallas_call(
        functools.partial(_rwkv_prep_kernel, tiles_per_seq=tiles_per_seq),
        grid=(t // tm,),
        in_specs=[
            main_spec(0), main_spec(1), main_spec(2),
            pl.BlockSpec((tm, TAIL_COLS), lambda i: (i, 0)),
            prev_spec(0), prev_spec(1), prev_spec(2),
            pl.BlockSpec((8, TAIL_COLS), lambda i: (jnp.maximum(i * rows8 - 1, 0), 0)),
            full((1, 3 * width)), full((1, TAIL_COLS)),
            full((1, width)), full((LANE, width)), full((1, width)), full((LANE, width)),
            full((GD_PAD, width)), full((1, width)), full((1, width)),
            full((QUAD, QUAD)),
        ],
        out_specs=[out_spec] * 7,
        out_shape=[jax.ShapeDtypeStruct((t, width), F32 if name == "lw" else BF16)
                   for name in ("r", "k", "v", "lw", "aa", "bb", "g")],
        compiler_params=_cparams(("parallel",)),
        name="rwkv_prep",
    )(u_main, u_main, u_main, u_tail, u_main, u_main, u_main, u_tail,
      mix_main, mix_tail, w0, w2, a0, a2, g2, k_k, k_a, esum)


def _rwkv_chunk_kernel(r_ref, k_ref, v_ref, lw_ref, aa_ref, bb_ref, g_ref,
                       rk_ref, gng_ref, gnb_ref, eq_ref, o_ref, st_scr):
    c = pl.program_id(0)
    nquad = RWKV_WIDTH // QUAD

    @pl.when(c == 0)
    def _():
        st_scr[...] = jnp.zeros_like(st_scr)

    eq = eq_ref[...]
    levels = int(math.log2(CHUNK))
    row_h = lax.broadcasted_iota(jnp.int32, (QUAD, QUAD), 0) // HEAD_DIM
    col_h = lax.broadcasted_iota(jnp.int32, (QUAD, QUAD), 1) // HEAD_DIM
    bdmask = row_h == col_h
    eye = (lax.broadcasted_iota(jnp.int32, (QUAD, QUAD), 0)
           == lax.broadcasted_iota(jnp.int32, (QUAD, QUAD), 1))
    t_idx = lax.broadcasted_iota(jnp.int32, (CHUNK, QUAD), 0)
    s_idx = lax.broadcasted_iota(jnp.int32, (CHUNK, QUAD), 1) % CHUNK
    strict = t_idx > s_idx
    incl = t_idx >= s_idx
    both = jnp.concatenate([strict, incl], axis=0)
    reps = QUAD // CHUNK

    def bds(x):
        xb = x.astype(BF16)
        return jnp.where(bdmask, jnp.concatenate([xb] * reps, axis=0), jnp.zeros((), BF16))

    def mm(a, b):
        return _dot(a.astype(BF16), b.astype(BF16))

    def quad_stages(bi, qd):
        sl = slice(qd * QUAD, (qd + 1) * QUAD)
        st_ref = st_scr.at[bi * nquad + qd]
        r = r_ref[bi, :, sl].astype(F32)
        k = k_ref[bi, :, sl].astype(F32)
        v = v_ref[bi, :, sl].astype(F32)
        lw = lw_ref[bi, :, sl]
        aa = aa_ref[bi, :, sl].astype(F32)
        bb = bb_ref[bi, :, sl].astype(F32)

        cum = lw
        for sh in (1 << b for b in range(levels)):
            cum = cum + jnp.where(t_idx >= sh, pltpu.roll(cum, sh, 0), 0.0)
        ctot = cum[CHUNK - 1:CHUNK, :]
        e_neg = jnp.exp(-cum)
        e_tol = jnp.exp(ctot - cum)
        a_t = aa * jnp.exp(cum - lw)
        r_t = r * jnp.exp(cum)
        b_c = bb * e_neg
        k_c = k * e_neg
        b_h = bb * e_tol
        k_h = k * e_tol
        gamma = jnp.exp(ctot)

        lhs = jnp.concatenate([a_t, r_t], axis=0).astype(BF16)
        pb = _dot_tb(lhs, bds(b_c))
        pk = _dot_tb(lhs, bds(k_c))
        bk_t = jnp.concatenate([b_h, k_h], axis=0).T.astype(BF16)
        yield
        a_rb = jnp.where(incl, pb[CHUNK:], 0.0).astype(BF16)
        akrk = mm(jnp.where(both, pk, 0.0), bds(v))
        n = jnp.where(strict, pb[:CHUNK], 0.0)
        t_inv = jnp.where(t_idx == s_idx, 1.0, 0.0) + n
        n = mm(n, bds(n))
        yield
        for lvl in range(1, levels):
            bd_n = bds(n)
            if lvl < levels - 1:
                tn = mm(jnp.concatenate([t_inv, n], axis=0), bd_n)
                t_inv = t_inv + tn[:CHUNK]
                n = tn[CHUNK:]
            else:
                t_inv = t_inv + mm(t_inv, bd_n)
            yield
        w = mm(t_inv, bds(a_t))
        u = mm(t_inv, bds(akrk[:CHUNK]))
        yield
        q_t = r_t + mm(a_rb, bds(w))
        y0 = mm(a_rb, bds(u)) + akrk[CHUNK:]
        w0 = jnp.concatenate([w, jnp.zeros_like(w)], axis=0)
        g_mat = jnp.where(bdmask, mm(bk_t, w0), 0.0) + jnp.where(eye, gamma, 0.0)
        h_mat = jnp.where(bdmask, mm(bk_t, jnp.concatenate([u, v], axis=0)), 0.0)
        yield
        gy = mm(jnp.concatenate([g_mat, q_t], axis=0), st_ref[...])
        st_ref[...] = gy[:QUAD] + h_mat
        y = gy[QUAD:] + y0
        yield

        inv_n = 1.0 / HEAD_DIM
        stats = mm(jnp.concatenate([y, r * k * rk_ref[:, sl]], axis=0), eq)
        mu = stats[:CHUNK] * inv_n
        bonus = stats[CHUNK:] * v
        yield
        d = y - mu
        var = mm(d * d, eq) * inv_n
        yield
        yn = d * lax.rsqrt(var + GN_EPS) * gng_ref[:, sl] + gnb_ref[:, sl]
        o_ref[bi, :, sl] = (yn + bonus) * g_ref[bi, :, sl].astype(F32)

    tiles = [quad_stages(bi, qd) for bi in range(r_ref.shape[0]) for qd in range(nquad)]
    for _ in itertools.zip_longest(*tiles):
        pass


def _rwkv_chunk(r, k, v, lw, aa, bb, g, r_k, gn_g, gn_b, eq):
    batch, seq, width = r.shape
    tok = pl.BlockSpec((batch, CHUNK, width), lambda c: (0, c, 0))
    full = lambda shape: pl.BlockSpec(shape, lambda c: (0,) * len(shape))
    return pl.pallas_call(
        _rwkv_chunk_kernel,
        grid=(seq // CHUNK,),
        in_specs=[tok] * 7 + [full((1, width))] * 3 + [full((QUAD, QUAD))],
        out_specs=tok,
        out_shape=jax.ShapeDtypeStruct((batch, seq, width), F32),
        scratch_shapes=[pltpu.VMEM((batch * width // QUAD, QUAD, QUAD), F32)],
        compiler_params=_cparams(("arbitrary",)),
        name="rwkv_chunk",
    )(r, k, v, lw, aa, bb, g, r_k, gn_g, gn_b, eq)


def _out_ln_kernel(h_ref, att_ref, rw_ref, wa_ref, wr_ref, g_ref, b_ref, o_ref):
    mix = _dot(att_ref[...].astype(BF16), wa_ref[...]) + _dot(rw_ref[...].astype(BF16), wr_ref[...])
    o_ref[...] = _layer_norm(ALPHA * h_ref[...] + mix, g_ref[...], b_ref[...])


def _out_ln(h, att, rw, w_out, g, b, tm=512):
    t, d = h.shape
    wa, wr = att.shape[1], rw.shape[1]
    assert wa == wr and w_out.shape[0] == wa + wr
    row = lambda n: pl.BlockSpec((tm, n), lambda i: (i, 0))
    full = lambda shape: pl.BlockSpec(shape, lambda i: (0,) * len(shape))
    return pl.pallas_call(
        _out_ln_kernel,
        grid=(t // tm,),
        in_specs=[row(d), row(wa), row(wr),
                  pl.BlockSpec((wa, d), lambda i: (0, 0)), pl.BlockSpec((wr, d), lambda i: (1, 0)),
                  full((1, d)), full((1, d))],
        out_specs=row(d),
        out_shape=jax.ShapeDtypeStruct((t, d), F32),
        compiler_params=_cparams(("parallel",)),
        name="out_ln",
    )(h, att, rw, w_out, w_out, g, b)


def _ple_ln_kernel(h_ref, p_ref, wg_ref, bg_ref, wp_ref, g_ref, b_ref, o_ref):
    h = h_ref[...]
    e = _dot(p_ref[...].astype(BF16), wp_ref[...])
    gate = jax.nn.sigmoid(_dot(h.astype(BF16), wg_ref[...]) + bg_ref[...])
    o_ref[...] = _layer_norm(ALPHA * h + gate * e, g_ref[...], b_ref[...])


def _ple_ln(h, p, w_gate, b_gate, w_up, g, b, tm=512):
    t, d = h.shape
    row = lambda n: pl.BlockSpec((tm, n), lambda i: (i, 0))
    full = lambda shape: pl.BlockSpec(shape, lambda i: (0,) * len(shape))
    return pl.pallas_call(
        _ple_ln_kernel,
        grid=(t // tm,),
        in_specs=[row(d), row(p.shape[1]), full(w_gate.shape), full((1, d)), full(w_up.shape),
                  full((1, d)), full((1, d))],
        out_specs=row(d),
        out_shape=jax.ShapeDtypeStruct((t, d), F32),
        compiler_params=_cparams(("parallel",)),
        name="ple_ln",
    )(h, p, w_gate, b_gate, w_up, g, b)


def _pad_rows(w, rows):
    return jnp.pad(w, ((0, rows - w.shape[0]), (0, 0)))


def _pack_tail(cols, axis):
    wd, ad, gd = jnp.split(cols, [DECAY_LORA, DECAY_LORA + AAA_LORA], axis=axis)

    def pad(x, n):
        widths = [(0, 0)] * x.ndim
        widths[axis] = (0, n - x.shape[axis])
        return jnp.pad(x, widths)

    return jnp.concatenate([pad(wd, LANE), pad(ad, LANE), pad(gd, GD_PAD)], axis=axis)


def kernel(x, p, ffn1_w_gate, ffn1_w_up, ffn1_w_down, ln1_g, ln1_b, w_in, rel_bias, shift_mix, decay_w0, decay_w2, a_a0, a_a2, gate_g2, k_k, k_a, r_k, gn_g, gn_b, w_out, ln2_g, ln2_b, ffn2_w_gate, ffn2_w_up, ffn2_w_down, ln3_g, ln3_b, ple_w_up, ple_w_gate, ple_b_gate, ln4_g, ln4_b):
    batch, seq, d = x.shape
    t = batch * seq
    row = lambda a: a.reshape(1, -1)
    main_cols = 3 * ATT_WIDTH + 3 * RWKV_WIDTH

    heads = jnp.arange(QUAD) // HEAD_DIM
    esum = (heads[:, None] == heads[None, :]).astype(BF16)
    tab = _bias_tab(rel_bias)

    h = x.reshape(t, d).astype(F32)
    for i in range(DEPTH):
        w_in_t = w_in[i].T
        later_weights = (ffn2_w_gate[i], ffn2_w_up[i], ffn2_w_down[i], w_out[i], ple_w_gate[i], ple_w_up[i],
                         (w_in_t, main_cols))
        h, (wg2, wu2, wd2, wo, wpg, wpu, w_main_t) = _ffn_ln(
            h, ffn1_w_gate[i].astype(BF16), ffn1_w_up[i].astype(BF16), ffn1_w_down[i].astype(BF16),
            row(ln1_g[i]), row(ln1_b[i]), to_cast=later_weights)
        u_main = _proj(h, w_main_t, main_cols)
        u_tail = _proj(h, _pack_tail(w_in_t[main_cols:], axis=0), TAIL_COLS, tn=TAIL_COLS)
        att = _moba(rel_bias, u_main.reshape(batch, seq, main_cols), tab, batch, seq)
        mix = shift_mix[i]
        prep = _rwkv_prep(
            u_main, u_tail, row(mix[:3 * RWKV_WIDTH]), row(_pack_tail(mix[3 * RWKV_WIDTH:], axis=0)),
            row(decay_w0[i]), _pad_rows(decay_w2[i], LANE).astype(BF16), row(a_a0[i]),
            _pad_rows(a_a2[i], LANE).astype(BF16), _pad_rows(gate_g2[i], GD_PAD).astype(BF16),
            row(k_k[i]), row(k_a[i]), esum, seq)
        prep = [a.reshape(batch, seq, RWKV_WIDTH) for a in prep]
        rw = _rwkv_chunk(*prep, row(r_k[i]), row(gn_g[i]), row(gn_b[i]), esum)
        h = _out_ln(h, att.reshape(t, ATT_WIDTH), rw.reshape(t, RWKV_WIDTH), wo,
                    row(ln2_g[i]), row(ln2_b[i]))
        h, _ = _ffn_ln(h, wg2, wu2, wd2, row(ln3_g[i]), row(ln3_b[i]))
        h = _ple_ln(h, p[i].reshape(t, PLE_DIM).astype(F32), wpg, row(ple_b_gate[i]), wpu,
                    row(ln4_g[i]), row(ln4_b[i]))
    return h.reshape(batch, seq, d).astype(x.dtype)
```

```python
import functools
import itertools
import math

import jax
import jax.numpy as jnp
from jax import lax
from jax.experimental import pallas as pl
from jax.experimental.pallas import tpu as pltpu

D_MODEL = 2048
DEPTH = 1
HEAD_DIM = 64
ATT_HEADS = 16
RWKV_HEADS = 16
ATT_WIDTH = ATT_HEADS * HEAD_DIM
RWKV_WIDTH = RWKV_HEADS * HEAD_DIM
MOBA_BLOCK = 256
MOBA_TOPK = 3
MOBA_UNROLL = 4
MOBA_QBLOCKS = 2
LOG2E = 1.4426950408889634
REL_BUCKETS = 32
REL_MAX_DIST = 128
DECAY_LORA = 64
AAA_LORA = 64
GATE_LORA = 160
D_FF = 5632
PLE_DIM = 256
LN_EPS = 1e-5
GN_EPS = 64e-5
NEG = -1e30
ALPHA = (2 * DEPTH) ** 0.25

LANE = 128
PAIR = LANE // HEAD_DIM
CHUNK = 64
QUAD = 256
HEADS_PER_QUAD = QUAD // HEAD_DIM
ONES_ROWS = 16
VT_ROWS = HEAD_DIM + ONES_ROWS
CAST_GROUPS = 4
TAIL_COLS = 512
TAIL_WD, TAIL_AD, TAIL_GD = 0, 128, 256
GD_PAD = 256
VMEM_LIMIT = 56 * 1024 * 1024

F32 = jnp.float32
BF16 = jnp.bfloat16
HI = lax.Precision.HIGHEST


def _cparams(sem):
    return pltpu.CompilerParams(dimension_semantics=sem, vmem_limit_bytes=VMEM_LIMIT)


def _layer_norm(y, g, b):
    mu = jnp.mean(y, axis=-1, keepdims=True)
    d = y - mu
    var = jnp.mean(d * d, axis=-1, keepdims=True)
    return d * lax.rsqrt(var + LN_EPS) * g + b


def _dot(a, b):
    return jnp.dot(a, b, preferred_element_type=F32)


def _dot_hi(a, b):
    return jnp.dot(a, b, preferred_element_type=F32, precision=HI)


def _dot_tb(a, b, precision=None):
    return lax.dot_general(a, b, (((1,), (1,)), ((), ())), preferred_element_type=F32,
                           precision=precision)


def _split3(x):
    hi = x.astype(BF16)
    r1 = x - hi.astype(F32)
    mid = r1.astype(BF16)
    lo = (r1 - mid.astype(F32)).astype(BF16)
    return hi, mid, lo


def _dot3_exact_rhs(x, w_bf16):
    hi, mid, lo = _split3(x)
    return _dot(hi, w_bf16) + _dot(mid, w_bf16) + _dot(lo, w_bf16)


def _dot3_exact_lhs(w_bf16, x):
    hi, mid, lo = _split3(x)
    return _dot(w_bf16, hi) + _dot(w_bf16, mid) + _dot(w_bf16, lo)


def _ffn_ln_kernel(*refs, n_cast):
    x_ref, wg_ref, wu_ref, wd_ref, g_ref, b_ref = refs[:6]
    cast_in = refs[6:6 + n_cast]
    o_ref = refs[6 + n_cast]
    cast_out = refs[7 + n_cast:7 + 2 * n_cast]
    (xb_ref,) = refs[7 + 2 * n_cast:]
    f = pl.program_id(1)

    @pl.when(f == 0)
    def _():
        xb_ref[...] = x_ref[...].astype(BF16)
        o_ref[...] = jnp.zeros_like(o_ref)

    xb = xb_ref[...]
    gate = _dot(xb, wg_ref[...])
    up = _dot(xb, wu_ref[...])
    act = gate * jax.nn.sigmoid(gate) * up
    o_ref[...] += _dot(act.astype(BF16), wd_ref[...])

    for src_ref, dst_ref in zip(cast_in, cast_out):
        dst_ref[...] = src_ref[...].astype(BF16)

    @pl.when(f == pl.num_programs(1) - 1)
    def _():
        y = ALPHA * x_ref[...] + 0.5 * o_ref[...]
        o_ref[...] = _layer_norm(y, g_ref[...], b_ref[...])


def _cast_spec(shape, n_i, n_f):
    rows, cols = shape
    run = -(-n_f // CAST_GROUPS)
    if rows % (16 * n_i) == 0 and cols % (LANE * CAST_GROUPS) == 0:
        return pl.BlockSpec((rows // n_i, cols // CAST_GROUPS), lambda i, f: (i, f // run))
    if rows % (16 * CAST_GROUPS) == 0 and cols % (LANE * n_i) == 0:
        return pl.BlockSpec((rows // CAST_GROUPS, cols // n_i), lambda i, f: (f // run, i))
    assert rows % (16 * n_i) == 0, shape
    return pl.BlockSpec((rows // n_i, cols), lambda i, f: (i, 0))


def _ffn_ln(x, wg, wu, wd, g, b, to_cast=(), tm=512, tf=512):
    t, d = x.shape
    ff = wg.shape[1]
    n_i, n_f = t // tm, ff // tf
    cast_specs = [_cast_spec(w.shape, n_i, n_f) for w in to_cast]
    x_mode = {"pipeline_mode": pl.Buffered(1)} if tm > 512 else {}
    outs = pl.pallas_call(
        functools.partial(_ffn_ln_kernel, n_cast=len(to_cast)),
        grid=(n_i, n_f),
        in_specs=[
            pl.BlockSpec((tm, d), lambda i, f: (i, 0), **x_mode),
            pl.BlockSpec((d, tf), lambda i, f: (0, f)),
            pl.BlockSpec((d, tf), lambda i, f: (0, f)),
            pl.BlockSpec((tf, d), lambda i, f: (f, 0)),
            pl.BlockSpec((1, d), lambda i, f: (0, 0)),
            pl.BlockSpec((1, d), lambda i, f: (0, 0)),
        ] + cast_specs,
        out_specs=[pl.BlockSpec((tm, d), lambda i, f: (i, 0))] + cast_specs,
        out_shape=[jax.ShapeDtypeStruct((t, d), F32)]
        + [jax.ShapeDtypeStruct(w.shape, BF16) for w in to_cast],
        scratch_shapes=[pltpu.VMEM((tm, d), BF16)],
        compiler_params=_cparams(("parallel", "arbitrary")),
        name="ffn_ln",
    )(x, wg, wu, wd, g, b, *to_cast)
    return outs[0], list(outs[1:])


def _proj_kernel(x_ref, w_ref, o_ref, xb_ref):
    @pl.when(pl.program_id(1) == 0)
    def _():
        xb_ref[...] = x_ref[...].astype(BF16)

    o_ref[...] = _dot_tb(xb_ref[...], w_ref[...].astype(BF16))


def _proj(x, w_t, n, tm=1024, tn=1024):
    t, d = x.shape
    return pl.pallas_call(
        _proj_kernel,
        grid=(t // tm, n // tn),
        in_specs=[
            pl.BlockSpec((tm, d), lambda i, j: (i, 0)),
            pl.BlockSpec((tn, d), lambda i, j: (j, 0)),
        ],
        out_specs=pl.BlockSpec((tm, tn), lambda i, j: (i, j)),
        out_shape=jax.ShapeDtypeStruct((t, n), F32),
        scratch_shapes=[pltpu.VMEM((tm, d), BF16)],
        compiler_params=_cparams(("parallel", "arbitrary")),
        name="proj",
    )(x, w_t)


def _rel_bucket(dist):
    max_exact = REL_BUCKETS // 2
    n = jnp.maximum(dist, 0)
    nf = jnp.maximum(n, 1).astype(F32)
    large = max_exact + (jnp.log(nf / max_exact) / math.log(REL_MAX_DIST / max_exact)
                         * (REL_BUCKETS - max_exact)).astype(jnp.int32)
    large = jnp.minimum(large, REL_BUCKETS - 1)
    return jnp.where(n < max_exact, n, large)


def _bias_tab_kernel(rb_ref, o_ref):
    p = pl.program_id(0)
    key = lax.broadcasted_iota(jnp.int32, (MOBA_BLOCK, MOBA_BLOCK), 0)
    qry = lax.broadcasted_iota(jnp.int32, (MOBA_BLOCK, MOBA_BLOCK), 1)
    for kind in range(2):
        dist = qry - key + kind * MOBA_BLOCK
        bucket = _rel_bucket(dist)
        for hh in range(PAIR):
            h = PAIR * p + hh
            acc = jnp.zeros((MOBA_BLOCK, MOBA_BLOCK), F32)
            for bkt in range(REL_BUCKETS):
                acc = jnp.where(bucket == bkt, rb_ref[bkt, h], acc)
            acc = acc * LOG2E
            if kind == 0:
                acc = jnp.where(dist >= 0, acc, NEG)
            o_ref[0, 2 * hh + kind] = acc


def _bias_tab(rel_bias):
    npair = ATT_HEADS // PAIR
    return pl.pallas_call(
        _bias_tab_kernel,
        grid=(npair,),
        in_specs=[pl.BlockSpec(memory_space=pltpu.SMEM)],
        out_specs=pl.BlockSpec((1, 2 * PAIR, MOBA_BLOCK, MOBA_BLOCK), lambda p: (p, 0, 0, 0)),
        out_shape=jax.ShapeDtypeStruct((npair, 2 * PAIR, MOBA_BLOCK, MOBA_BLOCK), F32),
        compiler_params=_cparams(("arbitrary",)),
        name="bias_tab",
    )(rel_bias)


def _moba_kernel(rb_ref, q_ref, k_ref, v_ref, tab_ref, o_ref, kb_scr, vt_scr, kmean_scr):
    p = pl.program_id(1)
    i0 = pl.program_id(2) * MOBA_QBLOCKS
    nb = kb_scr.shape[0]
    blk = MOBA_BLOCK

    @pl.when(i0 == 0)
    def _():
        lane_k = lax.broadcasted_iota(jnp.int32, (blk, LANE), 1)
        lane_m = lax.broadcasted_iota(jnp.int32, (1, LANE), 1)
        ones = jnp.ones((ONES_ROWS, blk), BF16)
        for n in range(nb):
            kblk = k_ref[0, n * blk:(n + 1) * blk, :]
            kb_scr[n, :, :LANE] = kblk.astype(BF16)
            hot = (lane_k == n) | (lane_k == nb + n) | (lane_k == 2 * nb + n)
            kb_scr[n, :, LANE:] = jnp.where(hot, 1.0, 0.0).astype(BF16)
            kmean = jnp.mean(kblk, axis=0, keepdims=True)
            v_t = v_ref[0, n * blk:(n + 1) * blk, :].T.astype(BF16)
            for hh in range(PAIR):
                in_head = (lane_m >= hh * HEAD_DIM) & (lane_m < (hh + 1) * HEAD_DIM)
                kmean_scr[hh * nb + n:hh * nb + n + 1, :] = jnp.where(in_head, kmean, 0.0)
                vt_scr[n, hh * VT_ROWS:hh * VT_ROWS + HEAD_DIM, :] = v_t[hh * HEAD_DIM:(hh + 1) * HEAD_DIM, :]
                vt_scr[n, hh * VT_ROWS + HEAD_DIM:(hh + 1) * VT_ROWS, :] = ones

    q8 = q_ref[0] * (HEAD_DIM ** -0.5)
    qs = q8 * LOG2E
    nq = q8.shape[0]
    lane = lax.broadcasted_iota(jnp.int32, q8.shape, 1)
    nidx = lax.broadcasted_iota(jnp.int32, (nb, nq), 0)
    own = i0 + lax.broadcasted_iota(jnp.int32, (1, nq), 1) // blk
    gate_all = _dot_tb(kmean_scr[...], q8, precision=HI)

    q_tail, q_far, adj_mask = [], [], []
    for hh in range(PAIR):
        gate = gate_all[hh * nb:(hh + 1) * nb]
        avail = nidx < own
        sel = jnp.zeros((nb, nq), jnp.bool_)
        for _ in range(MOBA_TOPK):
            best = jnp.max(jnp.where(avail, gate, -jnp.inf), axis=0, keepdims=True)
            first = jnp.min(jnp.where(avail & (gate == best), nidx, nb), axis=0, keepdims=True)
            pick = nidx == first
            sel = sel | pick
            avail = avail & jnp.logical_not(pick)
        b_far = rb_ref[REL_BUCKETS - 1, PAIR * p + hh] * LOG2E
        rowb = jnp.where(sel & (nidx < own - 1), b_far, NEG)
        adj_mask.append(jnp.max(jnp.where(sel & (nidx == own - 1), 0.0, NEG), axis=0, keepdims=True))
        hi, mid, lo = _split3(rowb)
        parts = jnp.concatenate([hi.astype(F32), mid.astype(F32), lo.astype(F32),
                                 jnp.zeros((LANE - 3 * nb, nq), F32)], axis=0)
        in_head = (lane >= hh * HEAD_DIM) & (lane < (hh + 1) * HEAD_DIM)
        q_h = jnp.where(in_head, qs, 0.0).astype(BF16)
        q_tail.append(q_h)
        q_far.append(jnp.concatenate([q_h, parts.T.astype(BF16)], axis=1))

    def vt(hh, n):
        return vt_scr[n, hh * VT_ROWS:(hh + 1) * VT_ROWS, :]

    def head_update(key, hh, blocks, prev, out):
        s_list = [thunk() for _, thunk in blocks]
        yield
        m_list = [jnp.max(s, axis=0, keepdims=True) for s in s_list]
        yield
        acc_list = [_dot(vt(hh, n), jnp.exp2(s - m_s).astype(BF16))
                    for s, m_s, (n, _) in zip(s_list, m_list, blocks)]
        m_new = m_list[0] if prev is None else jnp.maximum(prev[0], m_list[0])
        for m_s in m_list[1:]:
            m_new = jnp.maximum(m_new, m_s)
        yield
        acc = None if prev is None else jnp.exp2(prev[0] - m_new) * prev[1]
        for m_s, acc_s in zip(m_list, acc_list):
            term = jnp.exp2(m_s - m_new) * acc_s
            acc = term if acc is None else acc + term
        out[key] = (m_new, acc)

    def run_chains(gens):
        for _ in itertools.zip_longest(*gens):
            pass

    def tail_blocks(hh, c):
        cols = slice(c * blk, (c + 1) * blk)
        i_c = i0 + c
        n_adj = jnp.maximum(i_c - 1, 0)
        q_c = q_tail[hh][cols]

        def own_blk():
            return _dot_tb(kb_scr[i_c, :, :LANE], q_c) + tab_ref[0, 2 * hh]

        def prev_blk():
            return _dot_tb(kb_scr[n_adj, :, :LANE], q_c) + tab_ref[0, 2 * hh + 1] + adj_mask[hh][:, cols]

        return [(i_c, own_blk), (n_adj, prev_blk)]

    tails = {}
    run_chains([head_update((hh, c), hh, tail_blocks(hh, c), None, tails)
                for hh in range(PAIR) for c in range(MOBA_QBLOCKS)])
    state = [tuple(jnp.concatenate([tails[hh, c][s] for c in range(MOBA_QBLOCKS)], axis=1) for s in range(2))
             for hh in range(PAIR)]

    def far_body(j, carry):
        def blocks(hh):
            out = []
            for u in range(MOBA_UNROLL):
                nc = jnp.minimum(j * MOBA_UNROLL + u, nb - 1)
                out.append((nc, lambda nc=nc: _dot_tb(kb_scr[nc], q_far[hh])))
            return out

        out = [None] * PAIR
        run_chains([head_update(hh, hh, blocks(hh), carry[2 * hh:2 * hh + 2], out) for hh in range(PAIR)])
        return tuple(x for st in out for x in st)

    n_far = jnp.maximum(i0 + MOBA_QBLOCKS - 2, 0)
    trips = (n_far + MOBA_UNROLL - 1) // MOBA_UNROLL
    flat = lax.fori_loop(0, trips, far_body, tuple(x for st in state for x in st))
    out_t = jnp.concatenate(
        [flat[2 * hh + 1][:HEAD_DIM] / flat[2 * hh + 1][HEAD_DIM:HEAD_DIM + 1] for hh in range(PAIR)], axis=0)
    o_ref[0] = out_t.T


def _moba(rel_bias, u_main, tab, batch, seq):
    npair = ATT_HEADS // PAIR
    nb = seq // MOBA_BLOCK
    assert 3 * nb <= LANE and nb % MOBA_UNROLL == 0 and nb % MOBA_QBLOCKS == 0
    kcol = ATT_WIDTH // LANE
    qrows = MOBA_QBLOCKS * MOBA_BLOCK
    return pl.pallas_call(
        _moba_kernel,
        grid=(batch, npair, nb // MOBA_QBLOCKS),
        in_specs=[
            pl.BlockSpec(memory_space=pltpu.SMEM),
            pl.BlockSpec((1, qrows, LANE), lambda b, p, i: (b, i, p)),
            pl.BlockSpec((1, seq, LANE), lambda b, p, i: (b, 0, kcol + p)),
            pl.BlockSpec((1, seq, LANE), lambda b, p, i: (b, 0, 2 * kcol + p)),
            pl.BlockSpec((1, 2 * PAIR, MOBA_BLOCK, MOBA_BLOCK), lambda b, p, i: (p, 0, 0, 0)),
        ],
        out_specs=pl.BlockSpec((1, qrows, LANE), lambda b, p, i: (b, i, p)),
        out_shape=jax.ShapeDtypeStruct((batch, seq, ATT_WIDTH), F32),
        scratch_shapes=[
            pltpu.VMEM((nb, MOBA_BLOCK, 2 * LANE), BF16),
            pltpu.VMEM((nb, PAIR * VT_ROWS, MOBA_BLOCK), BF16),
            pltpu.VMEM((PAIR * nb, LANE), F32),
        ],
        compiler_params=_cparams(("parallel", "parallel", "arbitrary")),
        name="moba",
    )(rel_bias, u_main, u_main, u_main, tab)


def _rwkv_prep_kernel(ur_ref, uk_ref, uv_ref, ut_ref, pr_ref, pk_ref, pv_ref, pt_ref,
                      mixm_ref, mixt_ref, w0_ref, w2_ref, a0_ref, a2_ref, g2_ref, kk_ref, ka_ref,
                      esum_ref,
                      r_out, k_out, v_out, lw_out, aa_out, bb_out, g_out, *, tiles_per_seq):
    first = (pl.program_id(0) % tiles_per_seq) == 0
    width = RWKV_WIDTH

    def shifted(x_ref, p_ref, mix):
        x = x_ref[...]
        prev = pltpu.roll(x, 1, 0)
        row = lax.broadcasted_iota(jnp.int32, x.shape, 0)
        last = jnp.where(first, 0.0, p_ref[7:8, :])
        prev = jnp.where(row == 0, last, prev)
        return x + (prev - x) * mix

    r = shifted(ur_ref, pr_ref, mixm_ref[:, 0:width])
    k = shifted(uk_ref, pk_ref, mixm_ref[:, width:2 * width])
    v = shifted(uv_ref, pv_ref, mixm_ref[:, 2 * width:3 * width])
    tail = shifted(ut_ref, pt_ref, mixt_ref[...])
    wd = tail[:, TAIL_WD:TAIL_WD + LANE]
    ad = tail[:, TAIL_AD:TAIL_AD + LANE]
    gd = tail[:, TAIL_GD:TAIL_GD + GD_PAD]

    z = -(w0_ref[...] + _dot(jnp.tanh(wd).astype(BF16), w2_ref[...]))
    softplus = jnp.maximum(z, 0.0) + jnp.log(1.0 + jnp.exp(-jnp.abs(z)))
    w_log = -softplus - 0.5
    lw_out[...] = -jnp.exp(w_log)
    a = jax.nn.sigmoid(a0_ref[...] + _dot(ad.astype(BF16), a2_ref[...]))
    g_out[...] = _dot(jax.nn.sigmoid(gd).astype(BF16), g2_ref[...])
    kk = k * kk_ref[...]
    kk2 = kk * kk
    ss = jnp.concatenate([_dot3_exact_rhs(kk2[:, c:c + QUAD], esum_ref[...])
                          for c in range(0, width, QUAD)], axis=1)
    kk = kk / jnp.maximum(jnp.sqrt(ss), 1e-12)
    r_out[...] = r
    k_out[...] = k * (1.0 + (a - 1.0) * ka_ref[...])
    v_out[...] = v
    aa_out[...] = -kk
    bb_out[...] = kk * a


def _rwkv_prep(u_main, u_tail, mix_main, mix_tail, w0, w2, a0, a2, g2, k_k, k_a, esum, seq, tm=256):
    t = u_main.shape[0]
    width = RWKV_WIDTH
    col0 = 3 * ATT_WIDTH // width
    tiles_per_seq = seq // tm
    rows8 = tm // 8

    def main_spec(c):
        return pl.BlockSpec((tm, width), lambda i: (i, col0 + c))

    def prev_spec(c):
        return pl.BlockSpec((8, width), lambda i: (jnp.maximum(i * rows8 - 1, 0), col0 + c))

    full = lambda shape: pl.BlockSpec(shape, lambda i: (0,) * len(shape))
    out_spec = pl.BlockSpec((tm, width), lambda i: (i, 0))
    return pl.pallas_call(
        functools.partial(_rwkv_prep_kernel, tiles_per_seq=tiles_per_seq),
        grid=(t // tm,),
        in_specs=[
            main_spec(0), main_spec(1), main_spec(2),
            pl.BlockSpec((tm, TAIL_COLS), lambda i: (i, 0)),
            prev_spec(0), prev_spec(1), prev_spec(2),
            pl.BlockSpec((8, TAIL_COLS), lambda i: (jnp.maximum(i * rows8 - 1, 0), 0)),
            full((1, 3 * width)), full((1, TAIL_COLS)),
            full((1, width)), full((LANE, width)), full((1, width)), full((LANE, width)),
            full((GD_PAD, width)), full((1, width)), full((1, width)),
            full((QUAD, QUAD)),
        ],
        out_specs=[out_spec] * 7,
        out_shape=[jax.ShapeDtypeStruct((t, width), F32)] * 7,
        compiler_params=_cparams(("parallel",)),
        name="rwkv_prep",
    )(u_main, u_main, u_main, u_tail, u_main, u_main, u_main, u_tail,
      mix_main, mix_tail, w0, w2, a0, a2, g2, k_k, k_a, esum)


def _rwkv_chunk_kernel(r_ref, k_ref, v_ref, lw_ref, aa_ref, bb_ref, g_ref,
                       rk_ref, gng_ref, gnb_ref, eq_ref, o_ref, st_scr):
    c = pl.program_id(0)
    nquad = RWKV_WIDTH // QUAD

    @pl.when(c == 0)
    def _():
        st_scr[...] = jnp.zeros_like(st_scr)

    eq = eq_ref[...]
    levels = int(math.log2(CHUNK))
    row_h = lax.broadcasted_iota(jnp.int32, (QUAD, QUAD), 0) // HEAD_DIM
    col_h = lax.broadcasted_iota(jnp.int32, (QUAD, QUAD), 1) // HEAD_DIM
    bdmask = row_h == col_h
    eye = (lax.broadcasted_iota(jnp.int32, (QUAD, QUAD), 0)
           == lax.broadcasted_iota(jnp.int32, (QUAD, QUAD), 1))
    t_idx = lax.broadcasted_iota(jnp.int32, (CHUNK, QUAD), 0)
    s_idx = lax.broadcasted_iota(jnp.int32, (CHUNK, QUAD), 1) % CHUNK
    strict = t_idx > s_idx
    incl = t_idx >= s_idx
    both = jnp.concatenate([strict, incl], axis=0)
    reps = QUAD // CHUNK

    def bds(x):
        xb = x.astype(BF16)
        return jnp.where(bdmask, jnp.concatenate([xb] * reps, axis=0), jnp.zeros((), BF16))

    def mm(a, b):
        return _dot(a.astype(BF16), b.astype(BF16))

    def quad_stages(bi, qd):
        sl = slice(qd * QUAD, (qd + 1) * QUAD)
        st_ref = st_scr.at[bi * nquad + qd]
        r = r_ref[bi, :, sl]
        k = k_ref[bi, :, sl]
        v = v_ref[bi, :, sl]
        lw = lw_ref[bi, :, sl]
        aa = aa_ref[bi, :, sl]
        bb = bb_ref[bi, :, sl]

        cum = lw
        for sh in (1 << b for b in range(levels)):
            cum = cum + jnp.where(t_idx >= sh, pltpu.roll(cum, sh, 0), 0.0)
        ctot = cum[CHUNK - 1:CHUNK, :]
        e_neg = jnp.exp(-cum)
        e_tol = jnp.exp(ctot - cum)
        a_t = aa * jnp.exp(cum - lw)
        r_t = r * jnp.exp(cum)
        b_c = bb * e_neg
        k_c = k * e_neg
        b_h = bb * e_tol
        k_h = k * e_tol
        gamma = jnp.exp(ctot)

        lhs = jnp.concatenate([a_t, r_t], axis=0).astype(BF16)
        pb = _dot_tb(lhs, bds(b_c))
        pk = _dot_tb(lhs, bds(k_c))
        bk_t = jnp.concatenate([b_h, k_h], axis=0).T.astype(BF16)
        yield
        a_rb = jnp.where(incl, pb[CHUNK:], 0.0).astype(BF16)
        akrk = mm(jnp.where(both, pk, 0.0), bds(v))
        n = jnp.where(strict, pb[:CHUNK], 0.0)
        t_inv = jnp.where(t_idx == s_idx, 1.0, 0.0) + n
        n = mm(n, bds(n))
        yield
        for lvl in range(1, levels):
            bd_n = bds(n)
            if lvl < levels - 1:
                tn = mm(jnp.concatenate([t_inv, n], axis=0), bd_n)
                t_inv = t_inv + tn[:CHUNK]
                n = tn[CHUNK:]
            else:
                t_inv = t_inv + mm(t_inv, bd_n)
            yield
        w = mm(t_inv, bds(a_t))
        u = mm(t_inv, bds(akrk[:CHUNK]))
        yield
        q_t = r_t + mm(a_rb, bds(w))
        y0 = mm(a_rb, bds(u)) + akrk[CHUNK:]
        w0 = jnp.concatenate([w, jnp.zeros_like(w)], axis=0)
        g_mat = jnp.where(bdmask, mm(bk_t, w0), 0.0) + jnp.where(eye, gamma, 0.0)
        h_mat = jnp.where(bdmask, mm(bk_t, jnp.concatenate([u, v], axis=0)), 0.0)
        yield
        gy = mm(jnp.concatenate([g_mat, q_t], axis=0), st_ref[...])
        st_ref[...] = gy[:QUAD] + h_mat
        y = gy[QUAD:] + y0
        yield

        inv_n = 1.0 / HEAD_DIM
        stats = mm(jnp.concatenate([y, r * k * rk_ref[:, sl]], axis=0), eq)
        mu = stats[:CHUNK] * inv_n
        bonus = stats[CHUNK:] * v
        yield
        d = y - mu
        var = mm(d * d, eq) * inv_n
        yield
        yn = d * lax.rsqrt(var + GN_EPS) * gng_ref[:, sl] + gnb_ref[:, sl]
        o_ref[bi, :, sl] = (yn + bonus) * g_ref[bi, :, sl]

    tiles = [quad_stages(bi, qd) for bi in range(r_ref.shape[0]) for qd in range(nquad)]
    for _ in itertools.zip_longest(*tiles):
        pass


def _rwkv_chunk(r, k, v, lw, aa, bb, g, r_k, gn_g, gn_b, eq):
    batch, seq, width = r.shape
    tok = pl.BlockSpec((batch, CHUNK, width), lambda c: (0, c, 0))
    full = lambda shape: pl.BlockSpec(shape, lambda c: (0,) * len(shape))
    return pl.pallas_call(
        _rwkv_chunk_kernel,
        grid=(seq // CHUNK,),
        in_specs=[tok] * 7 + [full((1, width))] * 3 + [full((QUAD, QUAD))],
        out_specs=tok,
        out_shape=jax.ShapeDtypeStruct((batch, seq, width), F32),
        scratch_shapes=[pltpu.VMEM((batch * width // QUAD, QUAD, QUAD), F32)],
        compiler_params=_cparams(("arbitrary",)),
        name="rwkv_chunk",
    )(r, k, v, lw, aa, bb, g, r_k, gn_g, gn_b, eq)


def _out_ln_kernel(h_ref, att_ref, rw_ref, wa_ref, wr_ref, g_ref, b_ref, o_ref):
    mix = _dot(att_ref[...].astype(BF16), wa_ref[...]) + _dot(rw_ref[...].astype(BF16), wr_ref[...])
    o_ref[...] = _layer_norm(ALPHA * h_ref[...] + mix, g_ref[...], b_ref[...])


def _out_ln(h, att, rw, w_out, g, b, tm=512):
    t, d = h.shape
    wa, wr = att.shape[1], rw.shape[1]
    assert wa == wr and w_out.shape[0] == wa + wr
    row = lambda n: pl.BlockSpec((tm, n), lambda i: (i, 0))
    full = lambda shape: pl.BlockSpec(shape, lambda i: (0,) * len(shape))
    return pl.pallas_call(
        _out_ln_kernel,
        grid=(t // tm,),
        in_specs=[row(d), row(wa), row(wr),
                  pl.BlockSpec((wa, d), lambda i: (0, 0)), pl.BlockSpec((wr, d), lambda i: (1, 0)),
                  full((1, d)), full((1, d))],
        out_specs=row(d),
        out_shape=jax.ShapeDtypeStruct((t, d), F32),
        compiler_params=_cparams(("parallel",)),
        name="out_ln",
    )(h, att, rw, w_out, w_out, g, b)


def _ple_ln_kernel(h_ref, p_ref, wg_ref, bg_ref, wp_ref, g_ref, b_ref, o_ref):
    h = h_ref[...]
    e = _dot(p_ref[...].astype(BF16), wp_ref[...])
    gate = jax.nn.sigmoid(_dot(h.astype(BF16), wg_ref[...]) + bg_ref[...])
    o_ref[...] = _layer_norm(ALPHA * h + gate * e, g_ref[...], b_ref[...])


def _ple_ln(h, p, w_gate, b_gate, w_up, g, b, tm=512):
    t, d = h.shape
    row = lambda n: pl.BlockSpec((tm, n), lambda i: (i, 0))
    full = lambda shape: pl.BlockSpec(shape, lambda i: (0,) * len(shape))
    return pl.pallas_call(
        _ple_ln_kernel,
        grid=(t // tm,),
        in_specs=[row(d), row(p.shape[1]), full(w_gate.shape), full((1, d)), full(w_up.shape),
                  full((1, d)), full((1, d))],
        out_specs=row(d),
        out_shape=jax.ShapeDtypeStruct((t, d), F32),
        compiler_params=_cparams(("parallel",)),
        name="ple_ln",
    )(h, p, w_gate, b_gate, w_up, g, b)


def _pad_rows(w, rows):
    return jnp.pad(w, ((0, rows - w.shape[0]), (0, 0)))


def _pack_tail(cols, axis):
    wd, ad, gd = jnp.split(cols, [DECAY_LORA, DECAY_LORA + AAA_LORA], axis=axis)

    def pad(x, n):
        widths = [(0, 0)] * x.ndim
        widths[axis] = (0, n - x.shape[axis])
        return jnp.pad(x, widths)

    return jnp.concatenate([pad(wd, LANE), pad(ad, LANE), pad(gd, GD_PAD)], axis=axis)


def kernel(x, p, ffn1_w_gate, ffn1_w_up, ffn1_w_down, ln1_g, ln1_b, w_in, rel_bias, shift_mix, decay_w0, decay_w2, a_a0, a_a2, gate_g2, k_k, k_a, r_k, gn_g, gn_b, w_out, ln2_g, ln2_b, ffn2_w_gate, ffn2_w_up, ffn2_w_down, ln3_g, ln3_b, ple_w_up, ple_w_gate, ple_b_gate, ln4_g, ln4_b):
    batch, seq, d = x.shape
    t = batch * seq
    row = lambda a: a.reshape(1, -1)
    main_cols = 3 * ATT_WIDTH + 3 * RWKV_WIDTH

    heads = jnp.arange(QUAD) // HEAD_DIM
    esum = (heads[:, None] == heads[None, :]).astype(BF16)
    tab = _bias_tab(rel_bias)

    h = x.reshape(t, d).astype(F32)
    for i in range(DEPTH):
        w_in_t = w_in[i].T
        later_weights = (ffn2_w_gate[i], ffn2_w_up[i], ffn2_w_down[i], w_out[i], ple_w_gate[i], ple_w_up[i])
        h, (wg2, wu2, wd2, wo, wpg, wpu) = _ffn_ln(
            h, ffn1_w_gate[i].astype(BF16), ffn1_w_up[i].astype(BF16), ffn1_w_down[i].astype(BF16),
            row(ln1_g[i]), row(ln1_b[i]), to_cast=later_weights)
        u_main = _proj(h, w_in_t, main_cols)
        u_tail = _proj(h, _pack_tail(w_in_t[main_cols:], axis=0), TAIL_COLS, tn=TAIL_COLS)
        att = _moba(rel_bias, u_main.reshape(batch, seq, main_cols), tab, batch, seq)
        mix = shift_mix[i]
        prep = _rwkv_prep(
            u_main, u_tail, row(mix[:3 * RWKV_WIDTH]), row(_pack_tail(mix[3 * RWKV_WIDTH:], axis=0)),
            row(decay_w0[i]), _pad_rows(decay_w2[i], LANE).astype(BF16), row(a_a0[i]),
            _pad_rows(a_a2[i], LANE).astype(BF16), _pad_rows(gate_g2[i], GD_PAD).astype(BF16),
            row(k_k[i]), row(k_a[i]), esum, seq)
        prep = [a.reshape(batch, seq, RWKV_WIDTH) for a in prep]
        rw = _rwkv_chunk(*prep, row(r_k[i]), row(gn_g[i]), row(gn_b[i]), esum)
        h = _out_ln(h, att.reshape(t, ATT_WIDTH), rw.reshape(t, RWKV_WIDTH), wo,
                    row(ln2_g[i]), row(ln2_b[i]))
        h, _ = _ffn_ln(h, wg2, wu2, wd2, row(ln3_g[i]), row(ln3_b[i]), tm=1024)
        h = _ple_ln(h, p[i].reshape(t, PLE_DIM).astype(F32), wpg, row(ple_b_gate[i]), wpu,
                    row(ln4_g[i]), row(ln4_b[i]))
    return h.reshape(batch, seq, d).astype(x.dtype)
```

```python
import functools
import itertools
import math

import jax
import jax.numpy as jnp
from jax import lax
from jax.experimental import pallas as pl
from jax.experimental.pallas import tpu as pltpu

D_MODEL = 2048
DEPTH = 1
HEAD_DIM = 64
ATT_HEADS = 16
RWKV_HEADS = 16
ATT_WIDTH = ATT_HEADS * HEAD_DIM
RWKV_WIDTH = RWKV_HEADS * HEAD_DIM
MOBA_BLOCK = 256
MOBA_TOPK = 3
MOBA_UNROLL = 4
MOBA_QBLOCKS = 2
LOG2E = 1.4426950408889634
REL_BUCKETS = 32
REL_MAX_DIST = 128
DECAY_LORA = 64
AAA_LORA = 64
GATE_LORA = 160
D_FF = 5632
PLE_DIM = 256
LN_EPS = 1e-5
GN_EPS = 64e-5
NEG = -1e30
ALPHA = (2 * DEPTH) ** 0.25

LANE = 128
PAIR = LANE // HEAD_DIM
CHUNK = 64
QUAD = 256
HEADS_PER_QUAD = QUAD // HEAD_DIM
ONES_ROWS = 16
VT_ROWS = HEAD_DIM + ONES_ROWS
CAST_GROUPS = 4
TAIL_COLS = 512
TAIL_WD, TAIL_AD, TAIL_GD = 0, 128, 256
GD_PAD = 256
VMEM_LIMIT = 56 * 1024 * 1024

F32 = jnp.float32
BF16 = jnp.bfloat16
HI = lax.Precision.HIGHEST


def _cparams(sem):
    return pltpu.CompilerParams(dimension_semantics=sem, vmem_limit_bytes=VMEM_LIMIT)


def _layer_norm(y, g, b):
    mu = jnp.mean(y, axis=-1, keepdims=True)
    d = y - mu
    var = jnp.mean(d * d, axis=-1, keepdims=True)
    return d * lax.rsqrt(var + LN_EPS) * g + b


def _dot(a, b):
    return jnp.dot(a, b, preferred_element_type=F32)


def _dot_hi(a, b):
    return jnp.dot(a, b, preferred_element_type=F32, precision=HI)


def _dot_tb(a, b, precision=None):
    return lax.dot_general(a, b, (((1,), (1,)), ((), ())), preferred_element_type=F32,
                           precision=precision)


def _split3(x):
    hi = x.astype(BF16)
    r1 = x - hi.astype(F32)
    mid = r1.astype(BF16)
    lo = (r1 - mid.astype(F32)).astype(BF16)
    return hi, mid, lo


def _dot3_exact_rhs(x, w_bf16):
    hi, mid, lo = _split3(x)
    return _dot(hi, w_bf16) + _dot(mid, w_bf16) + _dot(lo, w_bf16)


def _dot3_exact_lhs(w_bf16, x):
    hi, mid, lo = _split3(x)
    return _dot(w_bf16, hi) + _dot(w_bf16, mid) + _dot(w_bf16, lo)


def _ffn_ln_kernel(*refs, n_cast):
    x_ref, wg_ref, wu_ref, wd_ref, g_ref, b_ref = refs[:6]
    cast_in = refs[6:6 + n_cast]
    o_ref = refs[6 + n_cast]
    cast_out = refs[7 + n_cast:7 + 2 * n_cast]
    (xb_ref,) = refs[7 + 2 * n_cast:]
    f = pl.program_id(1)

    @pl.when(f == 0)
    def _():
        xb_ref[...] = x_ref[...].astype(BF16)
        o_ref[...] = jnp.zeros_like(o_ref)

    xb = xb_ref[...]
    gate = _dot(xb, wg_ref[...])
    up = _dot(xb, wu_ref[...])
    act = gate * jax.nn.sigmoid(gate) * up
    o_ref[...] += _dot(act.astype(BF16), wd_ref[...])

    for src_ref, dst_ref in zip(cast_in, cast_out):
        dst_ref[...] = src_ref[...].astype(BF16)

    @pl.when(f == pl.num_programs(1) - 1)
    def _():
        y = ALPHA * x_ref[...] + 0.5 * o_ref[...]
        o_ref[...] = _layer_norm(y, g_ref[...], b_ref[...])


def _cast_spec(shape, n_i, n_f):
    rows, cols = shape
    run = -(-n_f // CAST_GROUPS)
    if rows % (16 * n_i) == 0 and cols % (LANE * CAST_GROUPS) == 0:
        return pl.BlockSpec((rows // n_i, cols // CAST_GROUPS), lambda i, f: (i, f // run))
    if rows % (16 * CAST_GROUPS) == 0 and cols % (LANE * n_i) == 0:
        return pl.BlockSpec((rows // CAST_GROUPS, cols // n_i), lambda i, f: (f // run, i))
    assert rows % (16 * n_i) == 0, shape
    return pl.BlockSpec((rows // n_i, cols), lambda i, f: (i, 0))


def _ffn_ln(x, wg, wu, wd, g, b, to_cast=(), tm=512, tf=512):
    t, d = x.shape
    ff = wg.shape[1]
    n_i, n_f = t // tm, ff // tf
    cast_specs = [_cast_spec(w.shape, n_i, n_f) for w in to_cast]
    outs = pl.pallas_call(
        functools.partial(_ffn_ln_kernel, n_cast=len(to_cast)),
        grid=(n_i, n_f),
        in_specs=[
            pl.BlockSpec((tm, d), lambda i, f: (i, 0)),
            pl.BlockSpec((d, tf), lambda i, f: (0, f)),
            pl.BlockSpec((d, tf), lambda i, f: (0, f)),
            pl.BlockSpec((tf, d), lambda i, f: (f, 0)),
            pl.BlockSpec((1, d), lambda i, f: (0, 0)),
            pl.BlockSpec((1, d), lambda i, f: (0, 0)),
        ] + cast_specs,
        out_specs=[pl.BlockSpec((tm, d), lambda i, f: (i, 0))] + cast_specs,
        out_shape=[jax.ShapeDtypeStruct((t, d), F32)]
        + [jax.ShapeDtypeStruct(w.shape, BF16) for w in to_cast],
        scratch_shapes=[pltpu.VMEM((tm, d), BF16)],
        compiler_params=_cparams(("parallel", "arbitrary")),
        name="ffn_ln",
    )(x, wg, wu, wd, g, b, *to_cast)
    return outs[0], list(outs[1:])


def _proj_kernel(x_ref, w_ref, o_ref, xb_ref):
    @pl.when(pl.program_id(1) == 0)
    def _():
        xb_ref[...] = x_ref[...].astype(BF16)

    o_ref[...] = _dot_tb(xb_ref[...], w_ref[...].astype(BF16))


def _proj(x, w_t, n, tm=1024, tn=1024):
    t, d = x.shape
    return pl.pallas_call(
        _proj_kernel,
        grid=(t // tm, n // tn),
        in_specs=[
            pl.BlockSpec((tm, d), lambda i, j: (i, 0)),
            pl.BlockSpec((tn, d), lambda i, j: (j, 0)),
        ],
        out_specs=pl.BlockSpec((tm, tn), lambda i, j: (i, j)),
        out_shape=jax.ShapeDtypeStruct((t, n), F32),
        scratch_shapes=[pltpu.VMEM((tm, d), BF16)],
        compiler_params=_cparams(("parallel", "arbitrary")),
        name="proj",
    )(x, w_t)


def _rel_bucket(dist):
    max_exact = REL_BUCKETS // 2
    n = jnp.maximum(dist, 0)
    nf = jnp.maximum(n, 1).astype(F32)
    large = max_exact + (jnp.log(nf / max_exact) / math.log(REL_MAX_DIST / max_exact)
                         * (REL_BUCKETS - max_exact)).astype(jnp.int32)
    large = jnp.minimum(large, REL_BUCKETS - 1)
    return jnp.where(n < max_exact, n, large)


def _bias_tab_kernel(rb_ref, o_ref):
    p = pl.program_id(0)
    key = lax.broadcasted_iota(jnp.int32, (MOBA_BLOCK, MOBA_BLOCK), 0)
    qry = lax.broadcasted_iota(jnp.int32, (MOBA_BLOCK, MOBA_BLOCK), 1)
    for kind in range(2):
        dist = qry - key + kind * MOBA_BLOCK
        bucket = _rel_bucket(dist)
        for hh in range(PAIR):
            h = PAIR * p + hh
            acc = jnp.zeros((MOBA_BLOCK, MOBA_BLOCK), F32)
            for bkt in range(REL_BUCKETS):
                acc = jnp.where(bucket == bkt, rb_ref[bkt, h], acc)
            acc = acc * LOG2E
            if kind == 0:
                acc = jnp.where(dist >= 0, acc, NEG)
            o_ref[0, 2 * hh + kind] = acc


def _bias_tab(rel_bias):
    npair = ATT_HEADS // PAIR
    return pl.pallas_call(
        _bias_tab_kernel,
        grid=(npair,),
        in_specs=[pl.BlockSpec(memory_space=pltpu.SMEM)],
        out_specs=pl.BlockSpec((1, 2 * PAIR, MOBA_BLOCK, MOBA_BLOCK), lambda p: (p, 0, 0, 0)),
        out_shape=jax.ShapeDtypeStruct((npair, 2 * PAIR, MOBA_BLOCK, MOBA_BLOCK), F32),
        compiler_params=_cparams(("arbitrary",)),
        name="bias_tab",
    )(rel_bias)


def _rwkv_prep_tile(tiles, prev_rows, first, mixm_ref, mixt_ref, w0_ref, w2_ref, a0_ref, a2_ref, g2_ref,
                    kk_ref, ka_ref, esum_ref, out):
    width = RWKV_WIDTH

    def shifted(x, last, mix):
        prev = pltpu.roll(x, 1, 0)
        row = lax.broadcasted_iota(jnp.int32, x.shape, 0)
        prev = jnp.where(row == 0, jnp.where(first, 0.0, last), prev)
        return x + (prev - x) * mix

    r = shifted(tiles[0], prev_rows[0], mixm_ref[:, 0:width])
    k = shifted(tiles[1], prev_rows[1], mixm_ref[:, width:2 * width])
    v = shifted(tiles[2], prev_rows[2], mixm_ref[:, 2 * width:3 * width])
    tail = shifted(tiles[3], prev_rows[3], mixt_ref[...])
    wd = tail[:, TAIL_WD:TAIL_WD + LANE]
    ad = tail[:, TAIL_AD:TAIL_AD + LANE]
    gd = tail[:, TAIL_GD:TAIL_GD + GD_PAD]

    lora_w = _dot(jnp.tanh(wd).astype(BF16), w2_ref[...])
    lora_a = _dot(ad.astype(BF16), a2_ref[...])
    g = _dot(jax.nn.sigmoid(gd).astype(BF16), g2_ref[...])
    kk = k * kk_ref[...]
    kk2 = kk * kk
    ss = jnp.concatenate([_dot3_exact_rhs(kk2[:, c:c + QUAD], esum_ref[...])
                          for c in range(0, width, QUAD)], axis=1)
    yield
    z = -(w0_ref[...] + lora_w)
    softplus = jnp.maximum(z, 0.0) + jnp.log(1.0 + jnp.exp(-jnp.abs(z)))
    w_log = -softplus - 0.5
    lw = -jnp.exp(w_log)
    a = jax.nn.sigmoid(a0_ref[...] + lora_a)
    kk = kk / jnp.maximum(jnp.sqrt(ss), 1e-12)
    out.extend((r, k * (1.0 + (a - 1.0) * ka_ref[...]), v, lw, -kk, kk * a, g))


def _moba_kernel(rb_ref, q_ref, k_ref, v_ref, tab_ref, *refs):
    prep_tiles, prep_prev, prep_params = refs[0:4], refs[4:8], refs[8:18]
    o_ref, prep_out = refs[18], refs[19:26]
    kb_scr, vt_scr, kmean_scr = refs[26:]
    p = pl.program_id(1)
    i0 = pl.program_id(2) * MOBA_QBLOCKS
    nb = kb_scr.shape[0]
    blk = MOBA_BLOCK

    @pl.when(i0 == 0)
    def _():
        lane_k = lax.broadcasted_iota(jnp.int32, (blk, LANE), 1)
        lane_m = lax.broadcasted_iota(jnp.int32, (1, LANE), 1)
        ones = jnp.ones((ONES_ROWS, blk), BF16)
        for n in range(nb):
            kblk = k_ref[0, n * blk:(n + 1) * blk, :]
            kb_scr[n, :, :LANE] = kblk.astype(BF16)
            hot = (lane_k == n) | (lane_k == nb + n) | (lane_k == 2 * nb + n)
            kb_scr[n, :, LANE:] = jnp.where(hot, 1.0, 0.0).astype(BF16)
            kmean = jnp.mean(kblk, axis=0, keepdims=True)
            v_t = v_ref[0, n * blk:(n + 1) * blk, :].T.astype(BF16)
            for hh in range(PAIR):
                in_head = (lane_m >= hh * HEAD_DIM) & (lane_m < (hh + 1) * HEAD_DIM)
                kmean_scr[hh * nb + n:hh * nb + n + 1, :] = jnp.where(in_head, kmean, 0.0)
                vt_scr[n, hh * VT_ROWS:hh * VT_ROWS + HEAD_DIM, :] = v_t[hh * HEAD_DIM:(hh + 1) * HEAD_DIM, :]
                vt_scr[n, hh * VT_ROWS + HEAD_DIM:(hh + 1) * VT_ROWS, :] = ones

    q8 = q_ref[0] * (HEAD_DIM ** -0.5)
    qs = q8 * LOG2E
    nq = q8.shape[0]
    lane = lax.broadcasted_iota(jnp.int32, q8.shape, 1)
    nidx = lax.broadcasted_iota(jnp.int32, (nb, nq), 0)
    own = i0 + lax.broadcasted_iota(jnp.int32, (1, nq), 1) // blk
    gate_all = _dot_tb(kmean_scr[...], q8, precision=HI)

    q_tail, q_far, adj_mask = [], [], []
    for hh in range(PAIR):
        gate = gate_all[hh * nb:(hh + 1) * nb]
        avail = nidx < own
        sel = jnp.zeros((nb, nq), jnp.bool_)
        for _ in range(MOBA_TOPK):
            best = jnp.max(jnp.where(avail, gate, -jnp.inf), axis=0, keepdims=True)
            first = jnp.min(jnp.where(avail & (gate == best), nidx, nb), axis=0, keepdims=True)
            pick = nidx == first
            sel = sel | pick
            avail = avail & jnp.logical_not(pick)
        b_far = rb_ref[REL_BUCKETS - 1, PAIR * p + hh] * LOG2E
        rowb = jnp.where(sel & (nidx < own - 1), b_far, NEG)
        adj_mask.append(jnp.max(jnp.where(sel & (nidx == own - 1), 0.0, NEG), axis=0, keepdims=True))
        hi, mid, lo = _split3(rowb)
        parts = jnp.concatenate([hi.astype(F32), mid.astype(F32), lo.astype(F32),
                                 jnp.zeros((LANE - 3 * nb, nq), F32)], axis=0)
        in_head = (lane >= hh * HEAD_DIM) & (lane < (hh + 1) * HEAD_DIM)
        q_h = jnp.where(in_head, qs, 0.0).astype(BF16)
        q_tail.append(q_h)
        q_far.append(jnp.concatenate([q_h, parts.T.astype(BF16)], axis=1))

    def vt(hh, n):
        return vt_scr[n, hh * VT_ROWS:(hh + 1) * VT_ROWS, :]

    def block_chain(hh, n, scores, parts):
        s = scores()
        yield
        m_s = jnp.max(s, axis=0, keepdims=True)
        yield
        parts.append((m_s, _dot(vt(hh, n), jnp.exp2(s - m_s).astype(BF16))))

    def run_chains(gens):
        for _ in itertools.zip_longest(*gens):
            pass

    def merge(prev, parts):
        m_new = parts[0][0] if prev is None else jnp.maximum(prev[0], parts[0][0])
        for m_s, _ in parts[1:]:
            m_new = jnp.maximum(m_new, m_s)
        acc = None if prev is None else jnp.exp2(prev[0] - m_new) * prev[1]
        for m_s, acc_s in parts:
            term = jnp.exp2(m_s - m_new) * acc_s
            acc = term if acc is None else acc + term
        return m_new, acc

    def tail_blocks(hh, c):
        cols = slice(c * blk, (c + 1) * blk)
        i_c = i0 + c
        n_adj = jnp.maximum(i_c - 1, 0)
        q_c = q_tail[hh][cols]

        def own_blk():
            return _dot_tb(kb_scr[i_c, :, :LANE], q_c) + tab_ref[0, 2 * hh]

        def prev_blk():
            return _dot_tb(kb_scr[n_adj, :, :LANE], q_c) + tab_ref[0, 2 * hh + 1] + adj_mask[hh][:, cols]

        return [(i_c, own_blk), (n_adj, prev_blk)]

    tile_idx = p * pl.num_programs(2) + pl.program_id(2)
    prepped = []
    prep_chain = _rwkv_prep_tile([t_ref[0] for t_ref in prep_tiles], [r_ref[0, 7:8, :] for r_ref in prep_prev],
                                 tile_idx == 0, *prep_params, prepped)

    tails = {(hh, c): [] for hh in range(PAIR) for c in range(MOBA_QBLOCKS)}
    run_chains([block_chain(hh, n, scores, tails[hh, c])
                for hh in range(PAIR) for c in range(MOBA_QBLOCKS) for n, scores in tail_blocks(hh, c)]
               + [prep_chain])
    for dst_ref, val in zip(prep_out, prepped):
        dst_ref[0] = val
    tails = {key: merge(None, parts) for key, parts in tails.items()}
    state = [tuple(jnp.concatenate([tails[hh, c][s] for c in range(MOBA_QBLOCKS)], axis=1) for s in range(2))
             for hh in range(PAIR)]

    def far_body(j, carry):
        parts = [[] for _ in range(PAIR)]
        chains = []
        for u in range(MOBA_UNROLL):
            nc = jnp.minimum(j * MOBA_UNROLL + u, nb - 1)
            for hh in range(PAIR):
                chains.append(block_chain(hh, nc, lambda nc=nc, hh=hh: _dot_tb(kb_scr[nc], q_far[hh]), parts[hh]))
        run_chains(chains)
        return tuple(x for hh in range(PAIR) for x in merge(carry[2 * hh:2 * hh + 2], parts[hh]))

    n_far = jnp.maximum(i0 + MOBA_QBLOCKS - 2, 0)
    trips = (n_far + MOBA_UNROLL - 1) // MOBA_UNROLL
    flat = lax.fori_loop(0, trips, far_body, tuple(x for st in state for x in st))
    out_t = jnp.concatenate(
        [flat[2 * hh + 1][:HEAD_DIM] / flat[2 * hh + 1][HEAD_DIM:HEAD_DIM + 1] for hh in range(PAIR)], axis=0)
    o_ref[0] = out_t.T


def _moba_prep(rel_bias, u_main, u_tail, tab, prep_params):
    batch, seq, _ = u_main.shape
    npair = ATT_HEADS // PAIR
    nb = seq // MOBA_BLOCK
    assert 3 * nb <= LANE and nb % MOBA_UNROLL == 0 and nb % MOBA_QBLOCKS == 0
    kcol = ATT_WIDTH // LANE
    qrows = MOBA_QBLOCKS * MOBA_BLOCK
    nsteps = nb // MOBA_QBLOCKS
    width = RWKV_WIDTH
    rcol = 3 * ATT_WIDTH // width
    trows = seq // (npair * nsteps)
    assert trows % 8 == 0 and trows * npair * nsteps == seq

    tile = lambda b, p, i: p * nsteps + i
    prev8 = lambda b, p, i: jnp.maximum(tile(b, p, i) * (trows // 8) - 1, 0)
    const = lambda a: pl.BlockSpec(a.shape, lambda b, p, i: (0,) * a.ndim)
    tile_spec = pl.BlockSpec((1, trows, width), lambda b, p, i: (b, tile(b, p, i), 0))
    outs = pl.pallas_call(
        _moba_kernel,
        grid=(batch, npair, nsteps),
        in_specs=[
            pl.BlockSpec(memory_space=pltpu.SMEM),
            pl.BlockSpec((1, qrows, LANE), lambda b, p, i: (b, i, p)),
            pl.BlockSpec((1, seq, LANE), lambda b, p, i: (b, 0, kcol + p)),
            pl.BlockSpec((1, seq, LANE), lambda b, p, i: (b, 0, 2 * kcol + p)),
            pl.BlockSpec((1, 2 * PAIR, MOBA_BLOCK, MOBA_BLOCK), lambda b, p, i: (p, 0, 0, 0)),
        ] + [pl.BlockSpec((1, trows, width), lambda b, p, i, c=c: (b, tile(b, p, i), rcol + c)) for c in range(3)]
        + [pl.BlockSpec((1, trows, TAIL_COLS), lambda b, p, i: (b, tile(b, p, i), 0))]
        + [pl.BlockSpec((1, 8, width), lambda b, p, i, c=c: (b, prev8(b, p, i), rcol + c)) for c in range(3)]
        + [pl.BlockSpec((1, 8, TAIL_COLS), lambda b, p, i: (b, prev8(b, p, i), 0))]
        + [const(a) for a in prep_params],
        out_specs=[pl.BlockSpec((1, qrows, LANE), lambda b, p, i: (b, i, p))] + [tile_spec] * 7,
        out_shape=[jax.ShapeDtypeStruct((batch, seq, ATT_WIDTH), F32)]
        + [jax.ShapeDtypeStruct((batch, seq, width), F32)] * 7,
        scratch_shapes=[
            pltpu.VMEM((nb, MOBA_BLOCK, 2 * LANE), BF16),
            pltpu.VMEM((nb, PAIR * VT_ROWS, MOBA_BLOCK), BF16),
            pltpu.VMEM((PAIR * nb, LANE), F32),
        ],
        compiler_params=_cparams(("parallel", "arbitrary", "arbitrary")),
        name="moba",
    )(rel_bias, u_main, u_main, u_main, tab, u_main, u_main, u_main, u_tail, u_main, u_main, u_main, u_tail,
      *prep_params)
    return outs[0], list(outs[1:])


def _rwkv_chunk_kernel(r_ref, k_ref, v_ref, lw_ref, aa_ref, bb_ref, g_ref,
                       rk_ref, gng_ref, gnb_ref, eq_ref, o_ref, st_scr):
    c = pl.program_id(0)
    nquad = RWKV_WIDTH // QUAD

    @pl.when(c == 0)
    def _():
        st_scr[...] = jnp.zeros_like(st_scr)

    eq = eq_ref[...]
    levels = int(math.log2(CHUNK))
    row_h = lax.broadcasted_iota(jnp.int32, (QUAD, QUAD), 0) // HEAD_DIM
    col_h = lax.broadcasted_iota(jnp.int32, (QUAD, QUAD), 1) // HEAD_DIM
    bdmask = row_h == col_h
    eye = (lax.broadcasted_iota(jnp.int32, (QUAD, QUAD), 0)
           == lax.broadcasted_iota(jnp.int32, (QUAD, QUAD), 1))
    t_idx = lax.broadcasted_iota(jnp.int32, (CHUNK, QUAD), 0)
    s_idx = lax.broadcasted_iota(jnp.int32, (CHUNK, QUAD), 1) % CHUNK
    strict = t_idx > s_idx
    incl = t_idx >= s_idx
    both = jnp.concatenate([strict, incl], axis=0)
    reps = QUAD // CHUNK

    def bds(x):
        xb = x.astype(BF16)
        return jnp.where(bdmask, jnp.concatenate([xb] * reps, axis=0), jnp.zeros((), BF16))

    def mm(a, b):
        return _dot(a.astype(BF16), b.astype(BF16))

    def quad_stages(bi, qd):
        sl = slice(qd * QUAD, (qd + 1) * QUAD)
        st_ref = st_scr.at[bi * nquad + qd]
        r = r_ref[bi, :, sl]
        k = k_ref[bi, :, sl]
        v = v_ref[bi, :, sl]
        lw = lw_ref[bi, :, sl]
        aa = aa_ref[bi, :, sl]
        bb = bb_ref[bi, :, sl]

        cum = lw
        for sh in (1 << b for b in range(levels)):
            cum = cum + jnp.where(t_idx >= sh, pltpu.roll(cum, sh, 0), 0.0)
        ctot = cum[CHUNK - 1:CHUNK, :]
        e_neg = jnp.exp(-cum)
        e_tol = jnp.exp(ctot - cum)
        a_t = aa * jnp.exp(cum - lw)
        r_t = r * jnp.exp(cum)
        b_c = bb * e_neg
        k_c = k * e_neg
        b_h = bb * e_tol
        k_h = k * e_tol
        gamma = jnp.exp(ctot)

        lhs = jnp.concatenate([a_t, r_t], axis=0).astype(BF16)
        pb = _dot_tb(lhs, bds(b_c))
        pk = _dot_tb(lhs, bds(k_c))
        bk_t = jnp.concatenate([b_h, k_h], axis=0).T.astype(BF16)
        yield
        a_rb = jnp.where(incl, pb[CHUNK:], 0.0).astype(BF16)
        akrk = mm(jnp.where(both, pk, 0.0), bds(v))
        n = jnp.where(strict, pb[:CHUNK], 0.0)
        t_inv = jnp.where(t_idx == s_idx, 1.0, 0.0) + n
        n = mm(n, bds(n))
        yield
        for lvl in range(1, levels):
            bd_n = bds(n)
            if lvl < levels - 1:
                tn = mm(jnp.concatenate([t_inv, n], axis=0), bd_n)
                t_inv = t_inv + tn[:CHUNK]
                n = tn[CHUNK:]
            else:
                t_inv = t_inv + mm(t_inv, bd_n)
            yield
        w = mm(t_inv, bds(a_t))
        u = mm(t_inv, bds(akrk[:CHUNK]))
        yield
        q_t = r_t + mm(a_rb, bds(w))
        y0 = mm(a_rb, bds(u)) + akrk[CHUNK:]
        w0 = jnp.concatenate([w, jnp.zeros_like(w)], axis=0)
        g_mat = jnp.where(bdmask, mm(bk_t, w0), 0.0) + jnp.where(eye, gamma, 0.0)
        h_mat = jnp.where(bdmask, mm(bk_t, jnp.concatenate([u, v], axis=0)), 0.0)
        yield
        gy = mm(jnp.concatenate([g_mat, q_t], axis=0), st_ref[...])
        st_ref[...] = gy[:QUAD] + h_mat
        y = gy[QUAD:] + y0
        yield

        inv_n = 1.0 / HEAD_DIM
        stats = mm(jnp.concatenate([y, r * k * rk_ref[:, sl]], axis=0), eq)
        mu = stats[:CHUNK] * inv_n
        bonus = stats[CHUNK:] * v
        yield
        d = y - mu
        var = mm(d * d, eq) * inv_n
        yield
        yn = d * lax.rsqrt(var + GN_EPS) * gng_ref[:, sl] + gnb_ref[:, sl]
        o_ref[bi, :, sl] = (yn + bonus) * g_ref[bi, :, sl]

    tiles = [quad_stages(bi, qd) for bi in range(r_ref.shape[0]) for qd in range(nquad)]
    for _ in itertools.zip_longest(*tiles):
        pass


def _rwkv_chunk(r, k, v, lw, aa, bb, g, r_k, gn_g, gn_b, eq):
    batch, seq, width = r.shape
    tok = pl.BlockSpec((batch, CHUNK, width), lambda c: (0, c, 0))
    full = lambda shape: pl.BlockSpec(shape, lambda c: (0,) * len(shape))
    return pl.pallas_call(
        _rwkv_chunk_kernel,
        grid=(seq // CHUNK,),
        in_specs=[tok] * 7 + [full((1, width))] * 3 + [full((QUAD, QUAD))],
        out_specs=tok,
        out_shape=jax.ShapeDtypeStruct((batch, seq, width), F32),
        scratch_shapes=[pltpu.VMEM((batch * width // QUAD, QUAD, QUAD), F32)],
        compiler_params=_cparams(("arbitrary",)),
        name="rwkv_chunk",
    )(r, k, v, lw, aa, bb, g, r_k, gn_g, gn_b, eq)


def _out_ln_kernel(h_ref, att_ref, rw_ref, wa_ref, wr_ref, g_ref, b_ref, o_ref):
    mix = _dot(att_ref[...].astype(BF16), wa_ref[...]) + _dot(rw_ref[...].astype(BF16), wr_ref[...])
    o_ref[...] = _layer_norm(ALPHA * h_ref[...] + mix, g_ref[...], b_ref[...])


def _out_ln(h, att, rw, w_out, g, b, tm=512):
    t, d = h.shape
    wa, wr = att.shape[1], rw.shape[1]
    assert wa == wr and w_out.shape[0] == wa + wr
    row = lambda n: pl.BlockSpec((tm, n), lambda i: (i, 0))
    full = lambda shape: pl.BlockSpec(shape, lambda i: (0,) * len(shape))
    return pl.pallas_call(
        _out_ln_kernel,
        grid=(t // tm,),
        in_specs=[row(d), row(wa), row(wr),
                  pl.BlockSpec((wa, d), lambda i: (0, 0)), pl.BlockSpec((wr, d), lambda i: (1, 0)),
                  full((1, d)), full((1, d))],
        out_specs=row(d),
        out_shape=jax.ShapeDtypeStruct((t, d), F32),
        compiler_params=_cparams(("parallel",)),
        name="out_ln",
    )(h, att, rw, w_out, w_out, g, b)


def _ple_ln_kernel(h_ref, p_ref, wg_ref, bg_ref, wp_ref, g_ref, b_ref, o_ref):
    h = h_ref[...]
    e = _dot(p_ref[...].astype(BF16), wp_ref[...])
    gate = jax.nn.sigmoid(_dot(h.astype(BF16), wg_ref[...]) + bg_ref[...])
    o_ref[...] = _layer_norm(ALPHA * h + gate * e, g_ref[...], b_ref[...])


def _ple_ln(h, p, w_gate, b_gate, w_up, g, b, tm=512):
    t, d = h.shape
    row = lambda n: pl.BlockSpec((tm, n), lambda i: (i, 0))
    full = lambda shape: pl.BlockSpec(shape, lambda i: (0,) * len(shape))
    return pl.pallas_call(
        _ple_ln_kernel,
        grid=(t // tm,),
        in_specs=[row(d), row(p.shape[1]), full(w_gate.shape), full((1, d)), full(w_up.shape),
                  full((1, d)), full((1, d))],
        out_specs=row(d),
        out_shape=jax.ShapeDtypeStruct((t, d), F32),
        compiler_params=_cparams(("parallel",)),
        name="ple_ln",
    )(h, p, w_gate, b_gate, w_up, g, b)


def _pad_rows(w, rows):
    return jnp.pad(w, ((0, rows - w.shape[0]), (0, 0)))


def _pack_tail(cols, axis):
    wd, ad, gd = jnp.split(cols, [DECAY_LORA, DECAY_LORA + AAA_LORA], axis=axis)

    def pad(x, n):
        widths = [(0, 0)] * x.ndim
        widths[axis] = (0, n - x.shape[axis])
        return jnp.pad(x, widths)

    return jnp.concatenate([pad(wd, LANE), pad(ad, LANE), pad(gd, GD_PAD)], axis=axis)


def kernel(x, p, ffn1_w_gate, ffn1_w_up, ffn1_w_down, ln1_g, ln1_b, w_in, rel_bias, shift_mix, decay_w0, decay_w2, a_a0, a_a2, gate_g2, k_k, k_a, r_k, gn_g, gn_b, w_out, ln2_g, ln2_b, ffn2_w_gate, ffn2_w_up, ffn2_w_down, ln3_g, ln3_b, ple_w_up, ple_w_gate, ple_b_gate, ln4_g, ln4_b):
    batch, seq, d = x.shape
    t = batch * seq
    row = lambda a: a.reshape(1, -1)
    main_cols = 3 * ATT_WIDTH + 3 * RWKV_WIDTH

    heads = jnp.arange(QUAD) // HEAD_DIM
    esum = (heads[:, None] == heads[None, :]).astype(BF16)
    tab = _bias_tab(rel_bias)

    h = x.reshape(t, d).astype(F32)
    for i in range(DEPTH):
        w_in_t = w_in[i].T
        later_weights = (ffn2_w_gate[i], ffn2_w_up[i], ffn2_w_down[i], w_out[i], ple_w_gate[i], ple_w_up[i])
        h, (wg2, wu2, wd2, wo, wpg, wpu) = _ffn_ln(
            h, ffn1_w_gate[i].astype(BF16), ffn1_w_up[i].astype(BF16), ffn1_w_down[i].astype(BF16),
            row(ln1_g[i]), row(ln1_b[i]), to_cast=later_weights)
        u_main = _proj(h, w_in_t, main_cols)
        u_tail = _proj(h, _pack_tail(w_in_t[main_cols:], axis=0), TAIL_COLS, tn=TAIL_COLS)
        mix = shift_mix[i]
        prep_params = (row(mix[:3 * RWKV_WIDTH]), row(_pack_tail(mix[3 * RWKV_WIDTH:], axis=0)),
                       row(decay_w0[i]), _pad_rows(decay_w2[i], LANE).astype(BF16), row(a_a0[i]),
                       _pad_rows(a_a2[i], LANE).astype(BF16), _pad_rows(gate_g2[i], GD_PAD).astype(BF16),
                       row(k_k[i]), row(k_a[i]), esum)
        att, prep = _moba_prep(rel_bias, u_main.reshape(batch, seq, main_cols),
                               u_tail.reshape(batch, seq, TAIL_COLS), tab, prep_params)
        rw = _rwkv_chunk(*prep, row(r_k[i]), row(gn_g[i]), row(gn_b[i]), esum)
        h = _out_ln(h, att.reshape(t, ATT_WIDTH), rw.reshape(t, RWKV_WIDTH), wo,
                    row(ln2_g[i]), row(ln2_b[i]))
        h, _ = _ffn_ln(h, wg2, wu2, wd2, row(ln3_g[i]), row(ln3_b[i]))
        h = _ple_ln(h, p[i].reshape(t, PLE_DIM).astype(F32), wpg, row(ple_b_gate[i]), wpu,
                    row(ln4_g[i]), row(ln4_b[i]))
    return h.reshape(batch, seq, d).astype(x.dtype)
```

```python
import functools
import itertools
import math

import jax
import jax.numpy as jnp
from jax import lax
from jax.experimental import pallas as pl
from jax.experimental.pallas import tpu as pltpu

D_MODEL = 2048
DEPTH = 1
HEAD_DIM = 64
ATT_HEADS = 16
RWKV_HEADS = 16
ATT_WIDTH = ATT_HEADS * HEAD_DIM
RWKV_WIDTH = RWKV_HEADS * HEAD_DIM
MOBA_BLOCK = 256
MOBA_TOPK = 3
MOBA_UNROLL = 4
MOBA_QBLOCKS = 2
LOG2E = 1.4426950408889634
REL_BUCKETS = 32
REL_MAX_DIST = 128
DECAY_LORA = 64
AAA_LORA = 64
GATE_LORA = 160
D_FF = 5632
PLE_DIM = 256
LN_EPS = 1e-5
GN_EPS = 64e-5
NEG = -1e30
ALPHA = (2 * DEPTH) ** 0.25

LANE = 128
PAIR = LANE // HEAD_DIM
CHUNK = 64
QUAD = 256
HEADS_PER_QUAD = QUAD // HEAD_DIM
ONES_ROWS = 16
VT_ROWS = HEAD_DIM + ONES_ROWS
CAST_GROUPS = 4
TAIL_COLS = 512
TAIL_WD, TAIL_AD, TAIL_GD = 0, 128, 256
GD_PAD = 256
VMEM_LIMIT = 56 * 1024 * 1024

F32 = jnp.float32
BF16 = jnp.bfloat16
HI = lax.Precision.HIGHEST


def _cparams(sem):
    return pltpu.CompilerParams(dimension_semantics=sem, vmem_limit_bytes=VMEM_LIMIT)


def _layer_norm(y, g, b):
    mu = jnp.mean(y, axis=-1, keepdims=True)
    d = y - mu
    var = jnp.mean(d * d, axis=-1, keepdims=True)
    return d * lax.rsqrt(var + LN_EPS) * g + b


def _dot(a, b):
    return jnp.dot(a, b, preferred_element_type=F32)


def _dot_hi(a, b):
    return jnp.dot(a, b, preferred_element_type=F32, precision=HI)


def _dot_tb(a, b, precision=None):
    return lax.dot_general(a, b, (((1,), (1,)), ((), ())), preferred_element_type=F32,
                           precision=precision)


def _split3(x):
    hi = x.astype(BF16)
    r1 = x - hi.astype(F32)
    mid = r1.astype(BF16)
    lo = (r1 - mid.astype(F32)).astype(BF16)
    return hi, mid, lo


def _dot3_exact_rhs(x, w_bf16):
    hi, mid, lo = _split3(x)
    return _dot(hi, w_bf16) + _dot(mid, w_bf16) + _dot(lo, w_bf16)


def _dot3_exact_lhs(w_bf16, x):
    hi, mid, lo = _split3(x)
    return _dot(w_bf16, hi) + _dot(w_bf16, mid) + _dot(w_bf16, lo)


def _ffn_ln_kernel(*refs, n_cast):
    x_ref, wg_ref, wu_ref, wd_ref, g_ref, b_ref = refs[:6]
    cast_in = refs[6:6 + n_cast]
    o_ref = refs[6 + n_cast]
    cast_out = refs[7 + n_cast:7 + 2 * n_cast]
    (xb_ref,) = refs[7 + 2 * n_cast:]
    f = pl.program_id(1)

    @pl.when(f == 0)
    def _():
        xb_ref[...] = x_ref[...].astype(BF16)
        o_ref[...] = jnp.zeros_like(o_ref)

    xb = xb_ref[...]
    gate = _dot(xb, wg_ref[...])
    up = _dot(xb, wu_ref[...])
    act = gate * jax.nn.sigmoid(gate) * up
    o_ref[...] += _dot(act.astype(BF16), wd_ref[...])

    for src_ref, dst_ref in zip(cast_in, cast_out):
        dst_ref[...] = src_ref[...].astype(BF16)

    @pl.when(f == pl.num_programs(1) - 1)
    def _():
        y = ALPHA * x_ref[...] + 0.5 * o_ref[...]
        o_ref[...] = _layer_norm(y, g_ref[...], b_ref[...])


def _cast_spec(shape, n_i, n_f):
    rows, cols = shape
    run = -(-n_f // CAST_GROUPS)
    if rows % (16 * n_i) == 0 and cols % (LANE * CAST_GROUPS) == 0:
        return pl.BlockSpec((rows // n_i, cols // CAST_GROUPS), lambda i, f: (i, f // run))
    if rows % (16 * CAST_GROUPS) == 0 and cols % (LANE * n_i) == 0:
        return pl.BlockSpec((rows // CAST_GROUPS, cols // n_i), lambda i, f: (f // run, i))
    assert rows % (16 * n_i) == 0, shape
    return pl.BlockSpec((rows // n_i, cols), lambda i, f: (i, 0))


def _ffn_ln(x, wg, wu, wd, g, b, to_cast=(), tm=512, tf=512):
    t, d = x.shape
    ff = wg.shape[1]
    n_i, n_f = t // tm, ff // tf
    cast_specs = [_cast_spec(w.shape, n_i, n_f) for w in to_cast]
    outs = pl.pallas_call(
        functools.partial(_ffn_ln_kernel, n_cast=len(to_cast)),
        grid=(n_i, n_f),
        in_specs=[
            pl.BlockSpec((tm, d), lambda i, f: (i, 0)),
            pl.BlockSpec((d, tf), lambda i, f: (0, f)),
            pl.BlockSpec((d, tf), lambda i, f: (0, f)),
            pl.BlockSpec((tf, d), lambda i, f: (f, 0)),
            pl.BlockSpec((1, d), lambda i, f: (0, 0)),
            pl.BlockSpec((1, d), lambda i, f: (0, 0)),
        ] + cast_specs,
        out_specs=[pl.BlockSpec((tm, d), lambda i, f: (i, 0))] + cast_specs,
        out_shape=[jax.ShapeDtypeStruct((t, d), F32)]
        + [jax.ShapeDtypeStruct(w.shape, BF16) for w in to_cast],
        scratch_shapes=[pltpu.VMEM((tm, d), BF16)],
        compiler_params=_cparams(("parallel", "arbitrary")),
        name="ffn_ln",
    )(x, wg, wu, wd, g, b, *to_cast)
    return outs[0], list(outs[1:])


def _proj_kernel(x_ref, w_ref, wt_ref, o_ref, ot_ref, xb_ref):
    @pl.when(pl.program_id(1) == 0)
    def _():
        xb_ref[...] = x_ref[...].astype(BF16)
        ot_ref[...] = _dot_tb(xb_ref[...], wt_ref[...])

    o_ref[...] = _dot_tb(xb_ref[...], w_ref[...].astype(BF16))


def _proj(x, w_t, n, w_tail_t, tm=1024, tn=1024):
    t, d = x.shape
    n_tail = w_tail_t.shape[0]
    return pl.pallas_call(
        _proj_kernel,
        grid=(t // tm, n // tn),
        in_specs=[
            pl.BlockSpec((tm, d), lambda i, j: (i, 0)),
            pl.BlockSpec((tn, d), lambda i, j: (j, 0)),
            pl.BlockSpec((n_tail, d), lambda i, j: (0, 0)),
        ],
        out_specs=[pl.BlockSpec((tm, tn), lambda i, j: (i, j)), pl.BlockSpec((tm, n_tail), lambda i, j: (i, 0))],
        out_shape=[jax.ShapeDtypeStruct((t, n), F32), jax.ShapeDtypeStruct((t, n_tail), F32)],
        scratch_shapes=[pltpu.VMEM((tm, d), BF16)],
        compiler_params=_cparams(("parallel", "arbitrary")),
        name="proj",
    )(x, w_t, w_tail_t)


def _rel_bucket(dist):
    max_exact = REL_BUCKETS // 2
    n = jnp.maximum(dist, 0)
    nf = jnp.maximum(n, 1).astype(F32)
    large = max_exact + (jnp.log(nf / max_exact) / math.log(REL_MAX_DIST / max_exact)
                         * (REL_BUCKETS - max_exact)).astype(jnp.int32)
    large = jnp.minimum(large, REL_BUCKETS - 1)
    return jnp.where(n < max_exact, n, large)


def _bias_tab_kernel(rb_ref, o_ref):
    p = pl.program_id(0)
    key = lax.broadcasted_iota(jnp.int32, (MOBA_BLOCK, MOBA_BLOCK), 0)
    qry = lax.broadcasted_iota(jnp.int32, (MOBA_BLOCK, MOBA_BLOCK), 1)
    for kind in range(2):
        dist = qry - key + kind * MOBA_BLOCK
        bucket = _rel_bucket(dist)
        for hh in range(PAIR):
            h = PAIR * p + hh
            acc = jnp.zeros((MOBA_BLOCK, MOBA_BLOCK), F32)
            for bkt in range(REL_BUCKETS):
                acc = jnp.where(bucket == bkt, rb_ref[bkt, h], acc)
            acc = acc * LOG2E
            if kind == 0:
                acc = jnp.where(dist >= 0, acc, NEG)
            o_ref[0, 2 * hh + kind] = acc


def _bias_tab(rel_bias):
    npair = ATT_HEADS // PAIR
    return pl.pallas_call(
        _bias_tab_kernel,
        grid=(npair,),
        in_specs=[pl.BlockSpec(memory_space=pltpu.SMEM)],
        out_specs=pl.BlockSpec((1, 2 * PAIR, MOBA_BLOCK, MOBA_BLOCK), lambda p: (p, 0, 0, 0)),
        out_shape=jax.ShapeDtypeStruct((npair, 2 * PAIR, MOBA_BLOCK, MOBA_BLOCK), F32),
        compiler_params=_cparams(("arbitrary",)),
        name="bias_tab",
    )(rel_bias)


def _rwkv_prep_tile(tiles, prev_rows, first, mixm_ref, mixt_ref, w0_ref, w2_ref, a0_ref, a2_ref, g2_ref,
                    kk_ref, ka_ref, esum_ref, out):
    width = RWKV_WIDTH

    def shifted(x, last, mix):
        prev = pltpu.roll(x, 1, 0)
        row = lax.broadcasted_iota(jnp.int32, x.shape, 0)
        prev = jnp.where(row == 0, jnp.where(first, 0.0, last), prev)
        return x + (prev - x) * mix

    r = shifted(tiles[0], prev_rows[0], mixm_ref[:, 0:width])
    k = shifted(tiles[1], prev_rows[1], mixm_ref[:, width:2 * width])
    v = shifted(tiles[2], prev_rows[2], mixm_ref[:, 2 * width:3 * width])
    tail = shifted(tiles[3], prev_rows[3], mixt_ref[...])
    wd = tail[:, TAIL_WD:TAIL_WD + LANE]
    ad = tail[:, TAIL_AD:TAIL_AD + LANE]
    gd = tail[:, TAIL_GD:TAIL_GD + GD_PAD]

    lora_w = _dot(jnp.tanh(wd).astype(BF16), w2_ref[...])
    lora_a = _dot(ad.astype(BF16), a2_ref[...])
    g = _dot(jax.nn.sigmoid(gd).astype(BF16), g2_ref[...])
    kk = k * kk_ref[...]
    kk2 = kk * kk
    ss = jnp.concatenate([_dot3_exact_rhs(kk2[:, c:c + QUAD], esum_ref[...])
                          for c in range(0, width, QUAD)], axis=1)
    yield
    z = -(w0_ref[...] + lora_w)
    softplus = jnp.maximum(z, 0.0) + jnp.log(1.0 + jnp.exp(-jnp.abs(z)))
    w_log = -softplus - 0.5
    lw = -jnp.exp(w_log)
    a = jax.nn.sigmoid(a0_ref[...] + lora_a)
    kk = kk / jnp.maximum(jnp.sqrt(ss), 1e-12)
    out.extend((r, k * (1.0 + (a - 1.0) * ka_ref[...]), v, lw, -kk, kk * a, g))


def _moba_kernel(rb_ref, q_ref, k_ref, v_ref, tab_ref, o_ref, kb_scr, vt_scr, kmean_scr):
    p = pl.program_id(1)
    i0 = pl.program_id(2) * MOBA_QBLOCKS
    nb = kb_scr.shape[0]
    blk = MOBA_BLOCK

    @pl.when(i0 == 0)
    def _():
        lane_k = lax.broadcasted_iota(jnp.int32, (blk, LANE), 1)
        lane_m = lax.broadcasted_iota(jnp.int32, (1, LANE), 1)
        ones = jnp.ones((ONES_ROWS, blk), BF16)
        for n in range(nb):
            kblk = k_ref[0, n * blk:(n + 1) * blk, :]
            kb_scr[n, :, :LANE] = kblk.astype(BF16)
            hot = (lane_k == n) | (lane_k == nb + n) | (lane_k == 2 * nb + n)
            kb_scr[n, :, LANE:] = jnp.where(hot, 1.0, 0.0).astype(BF16)
            kmean = jnp.mean(kblk, axis=0, keepdims=True)
            v_t = v_ref[0, n * blk:(n + 1) * blk, :].T.astype(BF16)
            for hh in range(PAIR):
                in_head = (lane_m >= hh * HEAD_DIM) & (lane_m < (hh + 1) * HEAD_DIM)
                kmean_scr[hh * nb + n:hh * nb + n + 1, :] = jnp.where(in_head, kmean, 0.0)
                vt_scr[n, hh * VT_ROWS:hh * VT_ROWS + HEAD_DIM, :] = v_t[hh * HEAD_DIM:(hh + 1) * HEAD_DIM, :]
                vt_scr[n, hh * VT_ROWS + HEAD_DIM:(hh + 1) * VT_ROWS, :] = ones

    q8 = q_ref[0] * (HEAD_DIM ** -0.5)
    qs = q8 * LOG2E
    nq = q8.shape[0]
    lane = lax.broadcasted_iota(jnp.int32, q8.shape, 1)
    nidx = lax.broadcasted_iota(jnp.int32, (nb, nq), 0)
    own = i0 + lax.broadcasted_iota(jnp.int32, (1, nq), 1) // blk
    gate_all = _dot_tb(kmean_scr[...], q8, precision=HI)

    q_tail, q_far, adj_mask = [], [], []
    for hh in range(PAIR):
        gate = gate_all[hh * nb:(hh + 1) * nb]
        avail = nidx < own
        sel = jnp.zeros((nb, nq), jnp.bool_)
        for _ in range(MOBA_TOPK):
            best = jnp.max(jnp.where(avail, gate, -jnp.inf), axis=0, keepdims=True)
            first = jnp.min(jnp.where(avail & (gate == best), nidx, nb), axis=0, keepdims=True)
            pick = nidx == first
            sel = sel | pick
            avail = avail & jnp.logical_not(pick)
        b_far = rb_ref[REL_BUCKETS - 1, PAIR * p + hh] * LOG2E
        rowb = jnp.where(sel & (nidx < own - 1), b_far, NEG)
        adj_mask.append(jnp.max(jnp.where(sel & (nidx == own - 1), 0.0, NEG), axis=0, keepdims=True))
        hi, mid, lo = _split3(rowb)
        parts = jnp.concatenate([hi.astype(F32), mid.astype(F32), lo.astype(F32),
                                 jnp.zeros((LANE - 3 * nb, nq), F32)], axis=0)
        in_head = (lane >= hh * HEAD_DIM) & (lane < (hh + 1) * HEAD_DIM)
        q_h = jnp.where(in_head, qs, 0.0).astype(BF16)
        q_tail.append(q_h)
        q_far.append(jnp.concatenate([q_h, parts.T.astype(BF16)], axis=1))

    def vt(hh, n):
        return vt_scr[n, hh * VT_ROWS:(hh + 1) * VT_ROWS, :]

    def block_chain(hh, n, scores, parts):
        s = scores()
        yield
        m_s = jnp.max(s, axis=0, keepdims=True)
        yield
        parts.append((m_s, _dot(vt(hh, n), jnp.exp2(s - m_s).astype(BF16))))

    def run_chains(gens):
        for _ in itertools.zip_longest(*gens):
            pass

    def merge(prev, parts):
        m_new = parts[0][0] if prev is None else jnp.maximum(prev[0], parts[0][0])
        for m_s, _ in parts[1:]:
            m_new = jnp.maximum(m_new, m_s)
        acc = None if prev is None else jnp.exp2(prev[0] - m_new) * prev[1]
        for m_s, acc_s in parts:
            term = jnp.exp2(m_s - m_new) * acc_s
            acc = term if acc is None else acc + term
        return m_new, acc

    def tail_blocks(hh, c):
        cols = slice(c * blk, (c + 1) * blk)
        i_c = i0 + c
        n_adj = jnp.maximum(i_c - 1, 0)
        q_c = q_tail[hh][cols]

        def own_blk():
            return _dot_tb(kb_scr[i_c, :, :LANE], q_c) + tab_ref[0, 2 * hh]

        def prev_blk():
            return _dot_tb(kb_scr[n_adj, :, :LANE], q_c) + tab_ref[0, 2 * hh + 1] + adj_mask[hh][:, cols]

        return [(i_c, own_blk), (n_adj, prev_blk)]

    tails = {(hh, c): [] for hh in range(PAIR) for c in range(MOBA_QBLOCKS)}
    run_chains(block_chain(hh, n, scores, tails[hh, c])
               for hh in range(PAIR) for c in range(MOBA_QBLOCKS) for n, scores in tail_blocks(hh, c))
    tails = {key: merge(None, parts) for key, parts in tails.items()}
    state = [tuple(jnp.concatenate([tails[hh, c][s] for c in range(MOBA_QBLOCKS)], axis=1) for s in range(2))
             for hh in range(PAIR)]

    def far_body(j, carry):
        parts = [[] for _ in range(PAIR)]
        chains = []
        for u in range(MOBA_UNROLL):
            nc = jnp.minimum(j * MOBA_UNROLL + u, nb - 1)
            for hh in range(PAIR):
                chains.append(block_chain(hh, nc, lambda nc=nc, hh=hh: _dot_tb(kb_scr[nc], q_far[hh]), parts[hh]))
        run_chains(chains)
        return tuple(x for hh in range(PAIR) for x in merge(carry[2 * hh:2 * hh + 2], parts[hh]))

    n_far = jnp.maximum(i0 + MOBA_QBLOCKS - 2, 0)
    trips = (n_far + MOBA_UNROLL - 1) // MOBA_UNROLL
    flat = lax.fori_loop(0, trips, far_body, tuple(x for st in state for x in st))
    out_t = jnp.concatenate(
        [flat[2 * hh + 1][:HEAD_DIM] / flat[2 * hh + 1][HEAD_DIM:HEAD_DIM + 1] for hh in range(PAIR)], axis=0)
    o_ref[0] = out_t.T


def _moba(rel_bias, u_main, tab):
    batch, seq, _ = u_main.shape
    npair = ATT_HEADS // PAIR
    nb = seq // MOBA_BLOCK
    assert 3 * nb <= LANE and nb % MOBA_UNROLL == 0 and nb % MOBA_QBLOCKS == 0
    kcol = ATT_WIDTH // LANE
    qrows = MOBA_QBLOCKS * MOBA_BLOCK
    return pl.pallas_call(
        _moba_kernel,
        grid=(batch, npair, nb // MOBA_QBLOCKS),
        in_specs=[
            pl.BlockSpec(memory_space=pltpu.SMEM),
            pl.BlockSpec((1, qrows, LANE), lambda b, p, i: (b, i, p)),
            pl.BlockSpec((1, seq, LANE), lambda b, p, i: (b, 0, kcol + p)),
            pl.BlockSpec((1, seq, LANE), lambda b, p, i: (b, 0, 2 * kcol + p)),
            pl.BlockSpec((1, 2 * PAIR, MOBA_BLOCK, MOBA_BLOCK), lambda b, p, i: (p, 0, 0, 0)),
        ],
        out_specs=pl.BlockSpec((1, qrows, LANE), lambda b, p, i: (b, i, p)),
        out_shape=jax.ShapeDtypeStruct((batch, seq, ATT_WIDTH), F32),
        scratch_shapes=[
            pltpu.VMEM((nb, MOBA_BLOCK, 2 * LANE), BF16),
            pltpu.VMEM((nb, PAIR * VT_ROWS, MOBA_BLOCK), BF16),
            pltpu.VMEM((PAIR * nb, LANE), F32),
        ],
        compiler_params=_cparams(("parallel", "parallel", "arbitrary")),
        name="moba",
    )(rel_bias, u_main, u_main, u_main, tab)


def _rwkv_chunk_kernel(*refs):
    tile_refs, prev_refs, prep_params = refs[0:4], refs[4:8], refs[8:18]
    rk_ref, gng_ref, gnb_ref, o_ref, st_scr = refs[18:]
    eq_ref = prep_params[-1]
    c = pl.program_id(0)
    nquad = RWKV_WIDTH // QUAD
    nbatch = o_ref.shape[0]

    @pl.when(c == 0)
    def _():
        st_scr[...] = jnp.zeros_like(st_scr)

    prepped = [[] for _ in range(nbatch)]
    for _ in itertools.zip_longest(*[
            _rwkv_prep_tile([t_ref[bi] for t_ref in tile_refs], [p_ref[bi, 7:8, :] for p_ref in prev_refs],
                            c == 0, *prep_params, prepped[bi]) for bi in range(nbatch)]):
        pass

    eq = eq_ref[...]
    levels = int(math.log2(CHUNK))
    row_h = lax.broadcasted_iota(jnp.int32, (QUAD, QUAD), 0) // HEAD_DIM
    col_h = lax.broadcasted_iota(jnp.int32, (QUAD, QUAD), 1) // HEAD_DIM
    bdmask = row_h == col_h
    eye = (lax.broadcasted_iota(jnp.int32, (QUAD, QUAD), 0)
           == lax.broadcasted_iota(jnp.int32, (QUAD, QUAD), 1))
    t_idx = lax.broadcasted_iota(jnp.int32, (CHUNK, QUAD), 0)
    s_idx = lax.broadcasted_iota(jnp.int32, (CHUNK, QUAD), 1) % CHUNK
    strict = t_idx > s_idx
    incl = t_idx >= s_idx
    both = jnp.concatenate([strict, incl], axis=0)
    reps = QUAD // CHUNK

    def bds(x):
        xb = x.astype(BF16)
        return jnp.where(bdmask, jnp.concatenate([xb] * reps, axis=0), jnp.zeros((), BF16))

    def mm(a, b):
        return _dot(a.astype(BF16), b.astype(BF16))

    def quad_stages(bi, qd):
        sl = slice(qd * QUAD, (qd + 1) * QUAD)
        st_ref = st_scr.at[bi * nquad + qd]
        r, k, v, lw, aa, bb, g = (x[:, sl] for x in prepped[bi])

        cum = lw
        for sh in (1 << b for b in range(levels)):
            cum = cum + jnp.where(t_idx >= sh, pltpu.roll(cum, sh, 0), 0.0)
        ctot = cum[CHUNK - 1:CHUNK, :]
        e_neg = jnp.exp(-cum)
        e_tol = jnp.exp(ctot - cum)
        a_t = aa * jnp.exp(cum - lw)
        r_t = r * jnp.exp(cum)
        b_c = bb * e_neg
        k_c = k * e_neg
        b_h = bb * e_tol
        k_h = k * e_tol
        gamma = jnp.exp(ctot)

        lhs = jnp.concatenate([a_t, r_t], axis=0).astype(BF16)
        pb = _dot_tb(lhs, bds(b_c))
        pk = _dot_tb(lhs, bds(k_c))
        bk_t = jnp.concatenate([b_h, k_h], axis=0).T.astype(BF16)
        yield
        a_rb = jnp.where(incl, pb[CHUNK:], 0.0).astype(BF16)
        akrk = mm(jnp.where(both, pk, 0.0), bds(v))
        n = jnp.where(strict, pb[:CHUNK], 0.0)
        t_inv = jnp.where(t_idx == s_idx, 1.0, 0.0) + n
        n = mm(n, bds(n))
        yield
        for lvl in range(1, levels):
            bd_n = bds(n)
            if lvl < levels - 1:
                tn = mm(jnp.concatenate([t_inv, n], axis=0), bd_n)
                t_inv = t_inv + tn[:CHUNK]
                n = tn[CHUNK:]
            else:
                t_inv = t_inv + mm(t_inv, bd_n)
            yield
        w = mm(t_inv, bds(a_t))
        u = mm(t_inv, bds(akrk[:CHUNK]))
        yield
        q_t = r_t + mm(a_rb, bds(w))
        y0 = mm(a_rb, bds(u)) + akrk[CHUNK:]
        w0 = jnp.concatenate([w, jnp.zeros_like(w)], axis=0)
        g_mat = jnp.where(bdmask, mm(bk_t, w0), 0.0) + jnp.where(eye, gamma, 0.0)
        h_mat = jnp.where(bdmask, mm(bk_t, jnp.concatenate([u, v], axis=0)), 0.0)
        yield
        gy = mm(jnp.concatenate([g_mat, q_t], axis=0), st_ref[...])
        st_ref[...] = gy[:QUAD] + h_mat
        y = gy[QUAD:] + y0
        yield

        inv_n = 1.0 / HEAD_DIM
        stats = mm(jnp.concatenate([y, r * k * rk_ref[:, sl]], axis=0), eq)
        mu = stats[:CHUNK] * inv_n
        bonus = stats[CHUNK:] * v
        yield
        d = y - mu
        var = mm(d * d, eq) * inv_n
        yield
        yn = d * lax.rsqrt(var + GN_EPS) * gng_ref[:, sl] + gnb_ref[:, sl]
        o_ref[bi, :, sl] = (yn + bonus) * g

    tiles = [quad_stages(bi, qd) for bi in range(nbatch) for qd in range(nquad)]
    for _ in itertools.zip_longest(*tiles):
        pass


def _rwkv(u_main, u_tail, prep_params, r_k, gn_g, gn_b):
    batch, seq, _ = u_main.shape
    width = RWKV_WIDTH
    rcol = 3 * ATT_WIDTH // width
    prev8 = lambda c: jnp.maximum(c * (CHUNK // 8) - 1, 0)
    const = lambda a: pl.BlockSpec(a.shape, lambda c: (0,) * a.ndim)
    return pl.pallas_call(
        _rwkv_chunk_kernel,
        grid=(seq // CHUNK,),
        in_specs=[pl.BlockSpec((batch, CHUNK, width), lambda c, g=g: (0, c, rcol + g)) for g in range(3)]
        + [pl.BlockSpec((batch, CHUNK, TAIL_COLS), lambda c: (0, c, 0))]
        + [pl.BlockSpec((batch, 8, width), lambda c, g=g: (0, prev8(c), rcol + g)) for g in range(3)]
        + [pl.BlockSpec((batch, 8, TAIL_COLS), lambda c: (0, prev8(c), 0))]
        + [const(a) for a in prep_params] + [const(r_k), const(gn_g), const(gn_b)],
        out_specs=pl.BlockSpec((batch, CHUNK, width), lambda c: (0, c, 0)),
        out_shape=jax.ShapeDtypeStruct((batch, seq, width), F32),
        scratch_shapes=[pltpu.VMEM((batch * width // QUAD, QUAD, QUAD), F32)],
        compiler_params=_cparams(("arbitrary",)),
        name="rwkv_chunk",
    )(u_main, u_main, u_main, u_tail, u_main, u_main, u_main, u_tail, *prep_params, r_k, gn_g, gn_b)


def _out_ln_kernel(h_ref, att_ref, rw_ref, wa_ref, wr_ref, g_ref, b_ref, o_ref):
    mix = _dot(att_ref[...].astype(BF16), wa_ref[...]) + _dot(rw_ref[...].astype(BF16), wr_ref[...])
    o_ref[...] = _layer_norm(ALPHA * h_ref[...] + mix, g_ref[...], b_ref[...])


def _out_ln(h, att, rw, w_out, g, b, tm=512):
    t, d = h.shape
    wa, wr = att.shape[1], rw.shape[1]
    assert wa == wr and w_out.shape[0] == wa + wr
    row = lambda n: pl.BlockSpec((tm, n), lambda i: (i, 0))
    full = lambda shape: pl.BlockSpec(shape, lambda i: (0,) * len(shape))
    return pl.pallas_call(
        _out_ln_kernel,
        grid=(t // tm,),
        in_specs=[row(d), row(wa), row(wr),
                  pl.BlockSpec((wa, d), lambda i: (0, 0)), pl.BlockSpec((wr, d), lambda i: (1, 0)),
                  full((1, d)), full((1, d))],
        out_specs=row(d),
        out_shape=jax.ShapeDtypeStruct((t, d), F32),
        compiler_params=_cparams(("parallel",)),
        name="out_ln",
    )(h, att, rw, w_out, w_out, g, b)


def _ple_ln_kernel(h_ref, p_ref, wg_ref, bg_ref, wp_ref, g_ref, b_ref, o_ref):
    h = h_ref[...]
    e = _dot(p_ref[...].astype(BF16), wp_ref[...])
    gate = jax.nn.sigmoid(_dot(h.astype(BF16), wg_ref[...]) + bg_ref[...])
    o_ref[...] = _layer_norm(ALPHA * h + gate * e, g_ref[...], b_ref[...])


def _ple_ln(h, p, w_gate, b_gate, w_up, g, b, tm=512):
    t, d = h.shape
    row = lambda n: pl.BlockSpec((tm, n), lambda i: (i, 0))
    full = lambda shape: pl.BlockSpec(shape, lambda i: (0,) * len(shape))
    return pl.pallas_call(
        _ple_ln_kernel,
        grid=(t // tm,),
        in_specs=[row(d), row(p.shape[1]), full(w_gate.shape), full((1, d)), full(w_up.shape),
                  full((1, d)), full((1, d))],
        out_specs=row(d),
        out_shape=jax.ShapeDtypeStruct((t, d), F32),
        compiler_params=_cparams(("parallel",)),
        name="ple_ln",
    )(h, p, w_gate, b_gate, w_up, g, b)


def _pad_rows(w, rows):
    return jnp.pad(w, ((0, rows - w.shape[0]), (0, 0)))


def _pack_tail(cols, axis):
    wd, ad, gd = jnp.split(cols, [DECAY_LORA, DECAY_LORA + AAA_LORA], axis=axis)

    def pad(x, n):
        widths = [(0, 0)] * x.ndim
        widths[axis] = (0, n - x.shape[axis])
        return jnp.pad(x, widths)

    return jnp.concatenate([pad(wd, LANE), pad(ad, LANE), pad(gd, GD_PAD)], axis=axis)


def kernel(x, p, ffn1_w_gate, ffn1_w_up, ffn1_w_down, ln1_g, ln1_b, w_in, rel_bias, shift_mix, decay_w0, decay_w2, a_a0, a_a2, gate_g2, k_k, k_a, r_k, gn_g, gn_b, w_out, ln2_g, ln2_b, ffn2_w_gate, ffn2_w_up, ffn2_w_down, ln3_g, ln3_b, ple_w_up, ple_w_gate, ple_b_gate, ln4_g, ln4_b):
    batch, seq, d = x.shape
    t = batch * seq
    row = lambda a: a.reshape(1, -1)
    main_cols = 3 * ATT_WIDTH + 3 * RWKV_WIDTH

    heads = jnp.arange(QUAD) // HEAD_DIM
    esum = (heads[:, None] == heads[None, :]).astype(BF16)
    tab = _bias_tab(rel_bias)

    h = x.reshape(t, d).astype(F32)
    for i in range(DEPTH):
        w_in_t = w_in[i].T
        later_weights = (ffn2_w_gate[i], ffn2_w_up[i], ffn2_w_down[i], w_out[i], ple_w_gate[i], ple_w_up[i])
        h, (wg2, wu2, wd2, wo, wpg, wpu) = _ffn_ln(
            h, ffn1_w_gate[i].astype(BF16), ffn1_w_up[i].astype(BF16), ffn1_w_down[i].astype(BF16),
            row(ln1_g[i]), row(ln1_b[i]), to_cast=later_weights)
        u_main, u_tail = _proj(h, w_in_t, main_cols, _pack_tail(w_in_t[main_cols:], axis=0).astype(BF16))
        mix = shift_mix[i]
        prep_params = (row(mix[:3 * RWKV_WIDTH]), row(_pack_tail(mix[3 * RWKV_WIDTH:], axis=0)),
                       row(decay_w0[i]), _pad_rows(decay_w2[i], LANE).astype(BF16), row(a_a0[i]),
                       _pad_rows(a_a2[i], LANE).astype(BF16), _pad_rows(gate_g2[i], GD_PAD).astype(BF16),
                       row(k_k[i]), row(k_a[i]), esum)
        u_main3 = u_main.reshape(batch, seq, main_cols)
        att = _moba(rel_bias, u_main3, tab)
        rw = _rwkv(u_main3, u_tail.reshape(batch, seq, TAIL_COLS), prep_params,
                   row(r_k[i]), row(gn_g[i]), row(gn_b[i]))
        h = _out_ln(h, att.reshape(t, ATT_WIDTH), rw.reshape(t, RWKV_WIDTH), wo,
                    row(ln2_g[i]), row(ln2_b[i]))
        h, _ = _ffn_ln(h, wg2, wu2, wd2, row(ln3_g[i]), row(ln3_b[i]))
        h = _ple_ln(h, p[i].reshape(t, PLE_DIM).astype(F32), wpg, row(ple_b_gate[i]), wpu,
                    row(ln4_g[i]), row(ln4_b[i]))
    return h.reshape(batch, seq, d).astype(x.dtype)
```

```python
import functools
import itertools
import math

import jax
import jax.numpy as jnp
from jax import lax
from jax.experimental import pallas as pl
from jax.experimental.pallas import tpu as pltpu

D_MODEL = 2048
DEPTH = 1
HEAD_DIM = 64
ATT_HEADS = 16
RWKV_HEADS = 16
ATT_WIDTH = ATT_HEADS * HEAD_DIM
RWKV_WIDTH = RWKV_HEADS * HEAD_DIM
MOBA_BLOCK = 256
MOBA_TOPK = 3
MOBA_UNROLL = 4
MOBA_QBLOCKS = 2
LOG2E = 1.4426950408889634
REL_BUCKETS = 32
REL_MAX_DIST = 128
DECAY_LORA = 64
AAA_LORA = 64
GATE_LORA = 160
D_FF = 5632
PLE_DIM = 256
LN_EPS = 1e-5
GN_EPS = 64e-5
NEG = -1e30
ALPHA = (2 * DEPTH) ** 0.25

LANE = 128
PAIR = LANE // HEAD_DIM
CHUNK = 64
QUAD = 256
HEADS_PER_QUAD = QUAD // HEAD_DIM
ONES_ROWS = 16
VT_ROWS = HEAD_DIM + ONES_ROWS
CAST_GROUPS = 4
TAIL_COLS = 512
TAIL_WD, TAIL_AD, TAIL_GD = 0, 128, 256
GD_PAD = 256
VMEM_LIMIT = 56 * 1024 * 1024

F32 = jnp.float32
BF16 = jnp.bfloat16
HI = lax.Precision.HIGHEST


def _cparams(sem):
    return pltpu.CompilerParams(dimension_semantics=sem, vmem_limit_bytes=VMEM_LIMIT)


def _layer_norm(y, g, b):
    mu = jnp.mean(y, axis=-1, keepdims=True)
    d = y - mu
    var = jnp.mean(d * d, axis=-1, keepdims=True)
    return d * lax.rsqrt(var + LN_EPS) * g + b


def _dot(a, b):
    return jnp.dot(a, b, preferred_element_type=F32)


def _dot_hi(a, b):
    return jnp.dot(a, b, preferred_element_type=F32, precision=HI)


def _dot_tb(a, b, precision=None):
    return lax.dot_general(a, b, (((1,), (1,)), ((), ())), preferred_element_type=F32,
                           precision=precision)


def _split3(x):
    hi = x.astype(BF16)
    r1 = x - hi.astype(F32)
    mid = r1.astype(BF16)
    lo = (r1 - mid.astype(F32)).astype(BF16)
    return hi, mid, lo


def _dot3_exact_rhs(x, w_bf16):
    hi, mid, lo = _split3(x)
    return _dot(hi, w_bf16) + _dot(mid, w_bf16) + _dot(lo, w_bf16)


def _dot3_exact_lhs(w_bf16, x):
    hi, mid, lo = _split3(x)
    return _dot(w_bf16, hi) + _dot(w_bf16, mid) + _dot(w_bf16, lo)


def _ffn_ln_kernel(*refs, n_cast):
    x_ref, wg_ref, wu_ref, wd_ref, g_ref, b_ref = refs[:6]
    cast_in = refs[6:6 + n_cast]
    o_ref = refs[6 + n_cast]
    cast_out = refs[7 + n_cast:7 + 2 * n_cast]
    (xb_ref,) = refs[7 + 2 * n_cast:]
    f = pl.program_id(1)

    @pl.when(f == 0)
    def _():
        xb_ref[...] = x_ref[...].astype(BF16)
        o_ref[...] = jnp.zeros_like(o_ref)

    xb = xb_ref[...]
    gate = _dot(xb, wg_ref[...])
    up = _dot(xb, wu_ref[...])
    act = gate * jax.nn.sigmoid(gate) * up
    o_ref[...] += _dot(act.astype(BF16), wd_ref[...])

    for src_ref, dst_ref in zip(cast_in, cast_out):
        dst_ref[...] = src_ref[...].astype(BF16)

    @pl.when(f == pl.num_programs(1) - 1)
    def _():
        y = ALPHA * x_ref[...] + 0.5 * o_ref[...]
        o_ref[...] = _layer_norm(y, g_ref[...], b_ref[...])


def _cast_spec(shape, n_i, n_f):
    rows, cols = shape
    run = -(-n_f // CAST_GROUPS)
    if rows % (16 * n_i) == 0 and cols % (LANE * CAST_GROUPS) == 0:
        return pl.BlockSpec((rows // n_i, cols // CAST_GROUPS), lambda i, f: (i, f // run))
    if rows % (16 * CAST_GROUPS) == 0 and cols % (LANE * n_i) == 0:
        return pl.BlockSpec((rows // CAST_GROUPS, cols // n_i), lambda i, f: (f // run, i))
    assert rows % (16 * n_i) == 0, shape
    return pl.BlockSpec((rows // n_i, cols), lambda i, f: (i, 0))


def _ffn_ln(x, wg, wu, wd, g, b, to_cast=(), tm=512, tf=512):
    t, d = x.shape
    ff = wg.shape[1]
    n_i, n_f = t // tm, ff // tf
    cast_specs = [_cast_spec(w.shape, n_i, n_f) for w in to_cast]
    outs = pl.pallas_call(
        functools.partial(_ffn_ln_kernel, n_cast=len(to_cast)),
        grid=(n_i, n_f),
        in_specs=[
            pl.BlockSpec((tm, d), lambda i, f: (i, 0)),
            pl.BlockSpec((d, tf), lambda i, f: (0, f)),
            pl.BlockSpec((d, tf), lambda i, f: (0, f)),
            pl.BlockSpec((tf, d), lambda i, f: (f, 0)),
            pl.BlockSpec((1, d), lambda i, f: (0, 0)),
            pl.BlockSpec((1, d), lambda i, f: (0, 0)),
        ] + cast_specs,
        out_specs=[pl.BlockSpec((tm, d), lambda i, f: (i, 0))] + cast_specs,
        out_shape=[jax.ShapeDtypeStruct((t, d), F32)]
        + [jax.ShapeDtypeStruct(w.shape, BF16) for w in to_cast],
        scratch_shapes=[pltpu.VMEM((tm, d), BF16)],
        compiler_params=_cparams(("parallel", "arbitrary")),
        name="ffn_ln",
    )(x, wg, wu, wd, g, b, *to_cast)
    return outs[0], list(outs[1:])


def _proj_kernel(x_ref, w_ref, wt_ref, o_ref, ot_ref, xb_ref):
    @pl.when(pl.program_id(1) == 0)
    def _():
        xb_ref[...] = x_ref[...].astype(BF16)
        ot_ref[...] = _dot_tb(xb_ref[...], wt_ref[...].astype(BF16))

    o_ref[...] = _dot_tb(xb_ref[...], w_ref[...].astype(BF16))


def _proj(x, w_t, n, w_tail_t, tm=1024, tn=1024):
    t, d = x.shape
    n_tail = w_tail_t.shape[0]
    return pl.pallas_call(
        _proj_kernel,
        grid=(t // tm, n // tn),
        in_specs=[
            pl.BlockSpec((tm, d), lambda i, j: (i, 0)),
            pl.BlockSpec((tn, d), lambda i, j: (j, 0)),
            pl.BlockSpec((n_tail, d), lambda i, j: (0, 0), pipeline_mode=pl.Buffered(1)),
        ],
        out_specs=[pl.BlockSpec((tm, tn), lambda i, j: (i, j)), pl.BlockSpec((tm, n_tail), lambda i, j: (i, 0))],
        out_shape=[jax.ShapeDtypeStruct((t, n), F32), jax.ShapeDtypeStruct((t, n_tail), F32)],
        scratch_shapes=[pltpu.VMEM((tm, d), BF16)],
        compiler_params=_cparams(("parallel", "arbitrary")),
        name="proj",
    )(x, w_t, w_tail_t)


def _rel_bucket(dist):
    max_exact = REL_BUCKETS // 2
    n = jnp.maximum(dist, 0)
    nf = jnp.maximum(n, 1).astype(F32)
    large = max_exact + (jnp.log(nf / max_exact) / math.log(REL_MAX_DIST / max_exact)
                         * (REL_BUCKETS - max_exact)).astype(jnp.int32)
    large = jnp.minimum(large, REL_BUCKETS - 1)
    return jnp.where(n < max_exact, n, large)


def _bias_tab_kernel(rb_ref, o_ref):
    p = pl.program_id(0)
    key = lax.broadcasted_iota(jnp.int32, (MOBA_BLOCK, MOBA_BLOCK), 0)
    qry = lax.broadcasted_iota(jnp.int32, (MOBA_BLOCK, MOBA_BLOCK), 1)
    for kind in range(2):
        dist = qry - key + kind * MOBA_BLOCK
        bucket = _rel_bucket(dist)
        for hh in range(PAIR):
            h = PAIR * p + hh
            acc = jnp.zeros((MOBA_BLOCK, MOBA_BLOCK), F32)
            for bkt in range(REL_BUCKETS):
                acc = jnp.where(bucket == bkt, rb_ref[bkt, h], acc)
            acc = acc * LOG2E
            if kind == 0:
                acc = jnp.where(dist >= 0, acc, NEG)
            o_ref[0, 2 * hh + kind] = acc


def _bias_tab(rel_bias):
    npair = ATT_HEADS // PAIR
    return pl.pallas_call(
        _bias_tab_kernel,
        grid=(npair,),
        in_specs=[pl.BlockSpec(memory_space=pltpu.SMEM)],
        out_specs=pl.BlockSpec((1, 2 * PAIR, MOBA_BLOCK, MOBA_BLOCK), lambda p: (p, 0, 0, 0)),
        out_shape=jax.ShapeDtypeStruct((npair, 2 * PAIR, MOBA_BLOCK, MOBA_BLOCK), F32),
        compiler_params=_cparams(("arbitrary",)),
        name="bias_tab",
    )(rel_bias)


def _rwkv_prep_tile(tiles, prev_rows, first, mixm_ref, mixt_ref, w0_ref, w2_ref, a0_ref, a2_ref, g2_ref,
                    kk_ref, ka_ref, esum_ref, out):
    width = RWKV_WIDTH

    def shifted(x, last, mix):
        prev = pltpu.roll(x, 1, 0)
        row = lax.broadcasted_iota(jnp.int32, x.shape, 0)
        prev = jnp.where(row == 0, jnp.where(first, 0.0, last), prev)
        return x + (prev - x) * mix

    r = shifted(tiles[0], prev_rows[0], mixm_ref[:, 0:width])
    k = shifted(tiles[1], prev_rows[1], mixm_ref[:, width:2 * width])
    v = shifted(tiles[2], prev_rows[2], mixm_ref[:, 2 * width:3 * width])
    tail = shifted(tiles[3], prev_rows[3], mixt_ref[...])
    wd = tail[:, TAIL_WD:TAIL_WD + LANE]
    ad = tail[:, TAIL_AD:TAIL_AD + LANE]
    gd = tail[:, TAIL_GD:TAIL_GD + GD_PAD]

    lora_w = _dot(jnp.tanh(wd).astype(BF16), w2_ref[...])
    lora_a = _dot(ad.astype(BF16), a2_ref[...])
    g = _dot(jax.nn.sigmoid(gd).astype(BF16), g2_ref[...])
    kk = k * kk_ref[...]
    kk2 = kk * kk
    ss = jnp.concatenate([_dot3_exact_rhs(kk2[:, c:c + QUAD], esum_ref[...])
                          for c in range(0, width, QUAD)], axis=1)
    yield
    z = -(w0_ref[...] + lora_w)
    softplus = jnp.maximum(z, 0.0) + jnp.log(1.0 + jnp.exp(-jnp.abs(z)))
    w_log = -softplus - 0.5
    lw = -jnp.exp(w_log)
    a = jax.nn.sigmoid(a0_ref[...] + lora_a)
    kk = kk / jnp.maximum(jnp.sqrt(ss), 1e-12)
    out.extend((r, k * (1.0 + (a - 1.0) * ka_ref[...]), v, lw, -kk, kk * a, g))


def _moba_kernel(rb_ref, q_ref, k_ref, v_ref, tab_ref, o_ref, kb_scr, vt_scr, kmean_scr):
    p = pl.program_id(1)
    i0 = pl.program_id(2) * MOBA_QBLOCKS
    nb = kb_scr.shape[0]
    blk = MOBA_BLOCK

    @pl.when(i0 == 0)
    def _():
        lane_k = lax.broadcasted_iota(jnp.int32, (blk, LANE), 1)
        lane_m = lax.broadcasted_iota(jnp.int32, (1, LANE), 1)
        ones = jnp.ones((ONES_ROWS, blk), BF16)
        for n in range(nb):
            kblk = k_ref[0, n * blk:(n + 1) * blk, :]
            kb_scr[n, :, :LANE] = kblk.astype(BF16)
            hot = (lane_k == n) | (lane_k == nb + n) | (lane_k == 2 * nb + n)
            kb_scr[n, :, LANE:] = jnp.where(hot, 1.0, 0.0).astype(BF16)
            kmean = jnp.mean(kblk, axis=0, keepdims=True)
            v_t = v_ref[0, n * blk:(n + 1) * blk, :].T.astype(BF16)
            for hh in range(PAIR):
                in_head = (lane_m >= hh * HEAD_DIM) & (lane_m < (hh + 1) * HEAD_DIM)
                kmean_scr[hh * nb + n:hh * nb + n + 1, :] = jnp.where(in_head, kmean, 0.0)
                vt_scr[n, hh * VT_ROWS:hh * VT_ROWS + HEAD_DIM, :] = v_t[hh * HEAD_DIM:(hh + 1) * HEAD_DIM, :]
                vt_scr[n, hh * VT_ROWS + HEAD_DIM:(hh + 1) * VT_ROWS, :] = ones

    q8 = q_ref[0] * (HEAD_DIM ** -0.5)
    qs = q8 * LOG2E
    nq = q8.shape[0]
    lane = lax.broadcasted_iota(jnp.int32, q8.shape, 1)
    nidx = lax.broadcasted_iota(jnp.int32, (nb, nq), 0)
    own = i0 + lax.broadcasted_iota(jnp.int32, (1, nq), 1) // blk
    gate_all = _dot_tb(kmean_scr[...], q8, precision=HI)

    q_tail, q_far, adj_mask = [], [], []
    for hh in range(PAIR):
        gate = gate_all[hh * nb:(hh + 1) * nb]
        avail = nidx < own
        sel = jnp.zeros((nb, nq), jnp.bool_)
        for _ in range(MOBA_TOPK):
            best = jnp.max(jnp.where(avail, gate, -jnp.inf), axis=0, keepdims=True)
            first = jnp.min(jnp.where(avail & (gate == best), nidx, nb), axis=0, keepdims=True)
            pick = nidx == first
            sel = sel | pick
            avail = avail & jnp.logical_not(pick)
        b_far = rb_ref[REL_BUCKETS - 1, PAIR * p + hh] * LOG2E
        rowb = jnp.where(sel & (nidx < own - 1), b_far, NEG)
        adj_mask.append(jnp.max(jnp.where(sel & (nidx == own - 1), 0.0, NEG), axis=0, keepdims=True))
        hi, mid, lo = _split3(rowb)
        parts = jnp.concatenate([hi.astype(F32), mid.astype(F32), lo.astype(F32),
                                 jnp.zeros((LANE - 3 * nb, nq), F32)], axis=0)
        in_head = (lane >= hh * HEAD_DIM) & (lane < (hh + 1) * HEAD_DIM)
        q_h = jnp.where(in_head, qs, 0.0).astype(BF16)
        q_tail.append(q_h)
        q_far.append(jnp.concatenate([q_h, parts.T.astype(BF16)], axis=1))

    def vt(hh, n):
        return vt_scr[n, hh * VT_ROWS:(hh + 1) * VT_ROWS, :]

    def block_chain(hh, n, scores, parts):
        s = scores()
        yield
        m_s = jnp.max(s, axis=0, keepdims=True)
        yield
        parts.append((m_s, _dot(vt(hh, n), jnp.exp2(s - m_s).astype(BF16))))

    def run_chains(gens):
        for _ in itertools.zip_longest(*gens):
            pass

    def merge(prev, parts):
        m_new = parts[0][0] if prev is None else jnp.maximum(prev[0], parts[0][0])
        for m_s, _ in parts[1:]:
            m_new = jnp.maximum(m_new, m_s)
        acc = None if prev is None else jnp.exp2(prev[0] - m_new) * prev[1]
        for m_s, acc_s in parts:
            term = jnp.exp2(m_s - m_new) * acc_s
            acc = term if acc is None else acc + term
        return m_new, acc

    def tail_blocks(hh, c):
        cols = slice(c * blk, (c + 1) * blk)
        i_c = i0 + c
        n_adj = jnp.maximum(i_c - 1, 0)
        q_c = q_tail[hh][cols]

        def own_blk():
            return _dot_tb(kb_scr[i_c, :, :LANE], q_c) + tab_ref[0, 2 * hh]

        def prev_blk():
            return _dot_tb(kb_scr[n_adj, :, :LANE], q_c) + tab_ref[0, 2 * hh + 1] + adj_mask[hh][:, cols]

        return [(i_c, own_blk), (n_adj, prev_blk)]

    tails = {(hh, c): [] for hh in range(PAIR) for c in range(MOBA_QBLOCKS)}
    run_chains(block_chain(hh, n, scores, tails[hh, c])
               for hh in range(PAIR) for c in range(MOBA_QBLOCKS) for n, scores in tail_blocks(hh, c))
    tails = {key: merge(None, parts) for key, parts in tails.items()}
    state = [tuple(jnp.concatenate([tails[hh, c][s] for c in range(MOBA_QBLOCKS)], axis=1) for s in range(2))
             for hh in range(PAIR)]

    def far_body(j, carry):
        parts = [[] for _ in range(PAIR)]
        chains = []
        for hh in range(PAIR):
            for u in range(MOBA_UNROLL):
                nc = jnp.minimum(j * MOBA_UNROLL + u, nb - 1)
                chains.append(block_chain(hh, nc, lambda nc=nc, hh=hh: _dot_tb(kb_scr[nc], q_far[hh]), parts[hh]))
        run_chains(chains)
        return tuple(x for hh in range(PAIR) for x in merge(carry[2 * hh:2 * hh + 2], parts[hh]))

    n_far = jnp.maximum(i0 + MOBA_QBLOCKS - 2, 0)
    trips = (n_far + MOBA_UNROLL - 1) // MOBA_UNROLL
    flat = lax.fori_loop(0, trips, far_body, tuple(x for st in state for x in st))
    out_t = jnp.concatenate(
        [flat[2 * hh + 1][:HEAD_DIM] / flat[2 * hh + 1][HEAD_DIM:HEAD_DIM + 1] for hh in range(PAIR)], axis=0)
    o_ref[0] = out_t.T


def _moba(rel_bias, u_main, tab):
    batch, seq, _ = u_main.shape
    npair = ATT_HEADS // PAIR
    nb = seq // MOBA_BLOCK
    assert 3 * nb <= LANE and nb % MOBA_UNROLL == 0 and nb % MOBA_QBLOCKS == 0
    kcol = ATT_WIDTH // LANE
    qrows = MOBA_QBLOCKS * MOBA_BLOCK
    return pl.pallas_call(
        _moba_kernel,
        grid=(batch, npair, nb // MOBA_QBLOCKS),
        in_specs=[
            pl.BlockSpec(memory_space=pltpu.SMEM),
            pl.BlockSpec((1, qrows, LANE), lambda b, p, i: (b, i, p)),
            pl.BlockSpec((1, seq, LANE), lambda b, p, i: (b, 0, kcol + p)),
            pl.BlockSpec((1, seq, LANE), lambda b, p, i: (b, 0, 2 * kcol + p)),
            pl.BlockSpec((1, 2 * PAIR, MOBA_BLOCK, MOBA_BLOCK), lambda b, p, i: (p, 0, 0, 0)),
        ],
        out_specs=pl.BlockSpec((1, qrows, LANE), lambda b, p, i: (b, i, p)),
        out_shape=jax.ShapeDtypeStruct((batch, seq, ATT_WIDTH), F32),
        scratch_shapes=[
            pltpu.VMEM((nb, MOBA_BLOCK, 2 * LANE), BF16),
            pltpu.VMEM((nb, PAIR * VT_ROWS, MOBA_BLOCK), BF16),
            pltpu.VMEM((PAIR * nb, LANE), F32),
        ],
        compiler_params=_cparams(("parallel", "parallel", "arbitrary")),
        name="moba",
    )(rel_bias, u_main, u_main, u_main, tab)


def _rwkv_chunk_kernel(*refs):
    tile_refs, prev_refs, prep_params = refs[0:4], refs[4:8], refs[8:18]
    rk_ref, gng_ref, gnb_ref, o_ref, st_scr = refs[18:]
    eq_ref = prep_params[-1]
    c = pl.program_id(0)
    nquad = RWKV_WIDTH // QUAD
    nbatch = o_ref.shape[0]

    @pl.when(c == 0)
    def _():
        st_scr[...] = jnp.zeros_like(st_scr)

    prepped = [[] for _ in range(nbatch)]
    for _ in itertools.zip_longest(*[
            _rwkv_prep_tile([t_ref[bi] for t_ref in tile_refs], [p_ref[bi, 7:8, :] for p_ref in prev_refs],
                            c == 0, *prep_params, prepped[bi]) for bi in range(nbatch)]):
        pass

    eq = eq_ref[...]
    levels = int(math.log2(CHUNK))
    row_h = lax.broadcasted_iota(jnp.int32, (QUAD, QUAD), 0) // HEAD_DIM
    col_h = lax.broadcasted_iota(jnp.int32, (QUAD, QUAD), 1) // HEAD_DIM
    bdmask = row_h == col_h
    eye = (lax.broadcasted_iota(jnp.int32, (QUAD, QUAD), 0)
           == lax.broadcasted_iota(jnp.int32, (QUAD, QUAD), 1))
    t_idx = lax.broadcasted_iota(jnp.int32, (CHUNK, QUAD), 0)
    s_idx = lax.broadcasted_iota(jnp.int32, (CHUNK, QUAD), 1) % CHUNK
    strict = t_idx > s_idx
    incl = t_idx >= s_idx
    both = jnp.concatenate([strict, incl], axis=0)
    reps = QUAD // CHUNK

    def bds(x):
        xb = x.astype(BF16)
        return jnp.where(bdmask, jnp.concatenate([xb] * reps, axis=0), jnp.zeros((), BF16))

    def mm(a, b):
        return _dot(a.astype(BF16), b.astype(BF16))

    def quad_stages(bi, qd):
        sl = slice(qd * QUAD, (qd + 1) * QUAD)
        st_ref = st_scr.at[bi * nquad + qd]
        r, k, v, lw, aa, bb, g = (x[:, sl] for x in prepped[bi])

        cum = lw
        for sh in (1 << b for b in range(levels)):
            cum = cum + jnp.where(t_idx >= sh, pltpu.roll(cum, sh, 0), 0.0)
        ctot = cum[CHUNK - 1:CHUNK, :]
        e_neg = jnp.exp(-cum)
        e_tol = jnp.exp(ctot - cum)
        a_t = aa * jnp.exp(cum - lw)
        r_t = r * jnp.exp(cum)
        b_c = bb * e_neg
        k_c = k * e_neg
        b_h = bb * e_tol
        k_h = k * e_tol
        gamma = jnp.exp(ctot)

        lhs = jnp.concatenate([a_t, r_t], axis=0).astype(BF16)
        pb = _dot_tb(lhs, bds(b_c))
        pk = _dot_tb(lhs, bds(k_c))
        bk_t = jnp.concatenate([b_h, k_h], axis=0).T.astype(BF16)
        yield
        a_rb = jnp.where(incl, pb[CHUNK:], 0.0).astype(BF16)
        akrk = mm(jnp.where(both, pk, 0.0), bds(v))
        n = jnp.where(strict, pb[:CHUNK], 0.0)
        t_inv = jnp.where(t_idx == s_idx, 1.0, 0.0) + n
        n = mm(n, bds(n))
        yield
        for lvl in range(1, levels):
            bd_n = bds(n)
            if lvl < levels - 1:
                tn = mm(jnp.concatenate([t_inv, n], axis=0), bd_n)
                t_inv = t_inv + tn[:CHUNK]
                n = tn[CHUNK:]
            else:
                t_inv = t_inv + mm(t_inv, bd_n)
            yield
        w = mm(t_inv, bds(a_t))
        u = mm(t_inv, bds(akrk[:CHUNK]))
        yield
        q_t = r_t + mm(a_rb, bds(w))
        y0 = mm(a_rb, bds(u)) + akrk[CHUNK:]
        w0 = jnp.concatenate([w, jnp.zeros_like(w)], axis=0)
        g_mat = jnp.where(bdmask, mm(bk_t, w0), 0.0) + jnp.where(eye, gamma, 0.0)
        h_mat = jnp.where(bdmask, mm(bk_t, jnp.concatenate([u, v], axis=0)), 0.0)
        yield
        gy = mm(jnp.concatenate([g_mat, q_t], axis=0), st_ref[...])
        st_ref[...] = gy[:QUAD] + h_mat
        y = gy[QUAD:] + y0
        yield

        inv_n = 1.0 / HEAD_DIM
        stats = mm(jnp.concatenate([y, r * k * rk_ref[:, sl]], axis=0), eq)
        mu = stats[:CHUNK] * inv_n
        bonus = stats[CHUNK:] * v
        yield
        d = y - mu
        var = mm(d * d, eq) * inv_n
        yield
        yn = d * lax.rsqrt(var + GN_EPS) * gng_ref[:, sl] + gnb_ref[:, sl]
        o_ref[bi, :, sl] = (yn + bonus) * g

    tiles = [quad_stages(bi, qd) for bi in range(nbatch) for qd in range(nquad)]
    for _ in itertools.zip_longest(*tiles):
        pass


def _rwkv(u_main, u_tail, prep_params, r_k, gn_g, gn_b):
    batch, seq, _ = u_main.shape
    width = RWKV_WIDTH
    rcol = 3 * ATT_WIDTH // width
    prev8 = lambda c: jnp.maximum(c * (CHUNK // 8) - 1, 0)
    const = lambda a: pl.BlockSpec(a.shape, lambda c: (0,) * a.ndim)
    return pl.pallas_call(
        _rwkv_chunk_kernel,
        grid=(seq // CHUNK,),
        in_specs=[pl.BlockSpec((batch, CHUNK, width), lambda c, g=g: (0, c, rcol + g)) for g in range(3)]
        + [pl.BlockSpec((batch, CHUNK, TAIL_COLS), lambda c: (0, c, 0))]
        + [pl.BlockSpec((batch, 8, width), lambda c, g=g: (0, prev8(c), rcol + g)) for g in range(3)]
        + [pl.BlockSpec((batch, 8, TAIL_COLS), lambda c: (0, prev8(c), 0))]
        + [const(a) for a in prep_params] + [const(r_k), const(gn_g), const(gn_b)],
        out_specs=pl.BlockSpec((batch, CHUNK, width), lambda c: (0, c, 0)),
        out_shape=jax.ShapeDtypeStruct((batch, seq, width), F32),
        scratch_shapes=[pltpu.VMEM((batch * width // QUAD, QUAD, QUAD), F32)],
        compiler_params=_cparams(("arbitrary",)),
        name="rwkv_chunk",
    )(u_main, u_main, u_main, u_tail, u_main, u_main, u_main, u_tail, *prep_params, r_k, gn_g, gn_b)


def _out_ln_kernel(h_ref, att_ref, rw_ref, wa_ref, wr_ref, g_ref, b_ref, o_ref):
    mix = _dot(att_ref[...].astype(BF16), wa_ref[...]) + _dot(rw_ref[...].astype(BF16), wr_ref[...])
    o_ref[...] = _layer_norm(ALPHA * h_ref[...] + mix, g_ref[...], b_ref[...])


def _out_ln(h, att, rw, w_out, g, b, tm=512):
    t, d = h.shape
    wa, wr = att.shape[1], rw.shape[1]
    assert wa == wr and w_out.shape[0] == wa + wr
    row = lambda n: pl.BlockSpec((tm, n), lambda i: (i, 0))
    full = lambda shape: pl.BlockSpec(shape, lambda i: (0,) * len(shape))
    return pl.pallas_call(
        _out_ln_kernel,
        grid=(t // tm,),
        in_specs=[row(d), row(wa), row(wr),
                  pl.BlockSpec((wa, d), lambda i: (0, 0)), pl.BlockSpec((wr, d), lambda i: (1, 0)),
                  full((1, d)), full((1, d))],
        out_specs=row(d),
        out_shape=jax.ShapeDtypeStruct((t, d), F32),
        compiler_params=_cparams(("parallel",)),
        name="out_ln",
    )(h, att, rw, w_out, w_out, g, b)


def _ple_ln_kernel(h_ref, p_ref, wg_ref, bg_ref, wp_ref, g_ref, b_ref, o_ref):
    h = h_ref[...]
    e = _dot(p_ref[...].astype(BF16), wp_ref[...])
    gate = jax.nn.sigmoid(_dot(h.astype(BF16), wg_ref[...]) + bg_ref[...])
    o_ref[...] = _layer_norm(ALPHA * h + gate * e, g_ref[...], b_ref[...])


def _ple_ln(h, p, w_gate, b_gate, w_up, g, b, tm=512):
    t, d = h.shape
    row = lambda n: pl.BlockSpec((tm, n), lambda i: (i, 0))
    full = lambda shape: pl.BlockSpec(shape, lambda i: (0,) * len(shape))
    return pl.pallas_call(
        _ple_ln_kernel,
        grid=(t // tm,),
        in_specs=[row(d), row(p.shape[1]), full(w_gate.shape), full((1, d)), full(w_up.shape),
                  full((1, d)), full((1, d))],
        out_specs=row(d),
        out_shape=jax.ShapeDtypeStruct((t, d), F32),
        compiler_params=_cparams(("parallel",)),
        name="ple_ln",
    )(h, p, w_gate, b_gate, w_up, g, b)


def _pad_rows(w, rows):
    return jnp.pad(w, ((0, rows - w.shape[0]), (0, 0)))


def _pack_tail(cols, axis):
    wd, ad, gd = jnp.split(cols, [DECAY_LORA, DECAY_LORA + AAA_LORA], axis=axis)

    def pad(x, n):
        widths = [(0, 0)] * x.ndim
        widths[axis] = (0, n - x.shape[axis])
        return jnp.pad(x, widths)

    return jnp.concatenate([pad(wd, LANE), pad(ad, LANE), pad(gd, GD_PAD)], axis=axis)


def kernel(x, p, ffn1_w_gate, ffn1_w_up, ffn1_w_down, ln1_g, ln1_b, w_in, rel_bias, shift_mix, decay_w0, decay_w2, a_a0, a_a2, gate_g2, k_k, k_a, r_k, gn_g, gn_b, w_out, ln2_g, ln2_b, ffn2_w_gate, ffn2_w_up, ffn2_w_down, ln3_g, ln3_b, ple_w_up, ple_w_gate, ple_b_gate, ln4_g, ln4_b):
    batch, seq, d = x.shape
    t = batch * seq
    row = lambda a: a.reshape(1, -1)
    main_cols = 3 * ATT_WIDTH + 3 * RWKV_WIDTH

    heads = jnp.arange(QUAD) // HEAD_DIM
    esum = (heads[:, None] == heads[None, :]).astype(BF16)
    tab = _bias_tab(rel_bias)

    h = x.reshape(t, d).astype(F32)
    for i in range(DEPTH):
        w_in_t = w_in[i].T
        later_weights = (ffn2_w_gate[i], ffn2_w_up[i], ffn2_w_down[i], w_out[i], ple_w_gate[i], ple_w_up[i])
        h, (wg2, wu2, wd2, wo, wpg, wpu) = _ffn_ln(
            h, ffn1_w_gate[i].astype(BF16), ffn1_w_up[i].astype(BF16), ffn1_w_down[i].astype(BF16),
            row(ln1_g[i]), row(ln1_b[i]), to_cast=later_weights)
        u_main, u_tail = _proj(h, w_in_t, main_cols, _pack_tail(w_in_t[main_cols:], axis=0))
        mix = shift_mix[i]
        prep_params = (row(mix[:3 * RWKV_WIDTH]), row(_pack_tail(mix[3 * RWKV_WIDTH:], axis=0)),
                       row(decay_w0[i]), _pad_rows(decay_w2[i], LANE).astype(BF16), row(a_a0[i]),
                       _pad_rows(a_a2[i], LANE).astype(BF16), _pad_rows(gate_g2[i], GD_PAD).astype(BF16),
                       row(k_k[i]), row(k_a[i]), esum)
        u_main3 = u_main.reshape(batch, seq, main_cols)
        att = _moba(rel_bias, u_main3, tab)
        rw = _rwkv(u_main3, u_tail.reshape(batch, seq, TAIL_COLS), prep_params,
                   row(r_k[i]), row(gn_g[i]), row(gn_b[i]))
        h = _out_ln(h, att.reshape(t, ATT_WIDTH), rw.reshape(t, RWKV_WIDTH), wo,
                    row(ln2_g[i]), row(ln2_b[i]))
        h, _ = _ffn_ln(h, wg2, wu2, wd2, row(ln3_g[i]), row(ln3_b[i]))
        h = _ple_ln(h, p[i].reshape(t, PLE_DIM).astype(F32), wpg, row(ple_b_gate[i]), wpu,
                    row(ln4_g[i]), row(ln4_b[i]))
    return h.reshape(batch, seq, d).astype(x.dtype)
```

```python
import functools
import itertools
import math

import jax
import jax.numpy as jnp
from jax import lax
from jax.experimental import pallas as pl
from jax.experimental.pallas import tpu as pltpu

D_MODEL = 2048
DEPTH = 1
HEAD_DIM = 64
ATT_HEADS = 16
RWKV_HEADS = 16
ATT_WIDTH = ATT_HEADS * HEAD_DIM
RWKV_WIDTH = RWKV_HEADS * HEAD_DIM
MOBA_BLOCK = 256
MOBA_TOPK = 3
MOBA_UNROLL = 4
MOBA_QBLOCKS = 2
LOG2E = 1.4426950408889634
REL_BUCKETS = 32
REL_MAX_DIST = 128
DECAY_LORA = 64
AAA_LORA = 64
GATE_LORA = 160
D_FF = 5632
PLE_DIM = 256
LN_EPS = 1e-5
GN_EPS = 64e-5
NEG = -1e30
ALPHA = (2 * DEPTH) ** 0.25

LANE = 128
PAIR = LANE // HEAD_DIM
CHUNK = 64
QUAD = 256
HEADS_PER_QUAD = QUAD // HEAD_DIM
ONES_ROWS = 16
VT_ROWS = HEAD_DIM + ONES_ROWS
CAST_GROUPS = 4
TAIL_COLS = 512
TAIL_WD, TAIL_AD, TAIL_GD = 0, 128, 256
GD_PAD = 256
VMEM_LIMIT = 56 * 1024 * 1024

F32 = jnp.float32
BF16 = jnp.bfloat16
HI = lax.Precision.HIGHEST


def _cparams(sem):
    return pltpu.CompilerParams(dimension_semantics=sem, vmem_limit_bytes=VMEM_LIMIT)


def _layer_norm(y, g, b):
    mu = jnp.mean(y, axis=-1, keepdims=True)
    d = y - mu
    var = jnp.mean(d * d, axis=-1, keepdims=True)
    return d * lax.rsqrt(var + LN_EPS) * g + b


def _dot(a, b):
    return jnp.dot(a, b, preferred_element_type=F32)


def _dot_hi(a, b):
    return jnp.dot(a, b, preferred_element_type=F32, precision=HI)


def _dot_tb(a, b, precision=None):
    return lax.dot_general(a, b, (((1,), (1,)), ((), ())), preferred_element_type=F32,
                           precision=precision)


def _split3(x):
    hi = x.astype(BF16)
    r1 = x - hi.astype(F32)
    mid = r1.astype(BF16)
    lo = (r1 - mid.astype(F32)).astype(BF16)
    return hi, mid, lo


def _dot3_exact_rhs(x, w_bf16):
    hi, mid, lo = _split3(x)
    return _dot(hi, w_bf16) + _dot(mid, w_bf16) + _dot(lo, w_bf16)


def _dot3_exact_lhs(w_bf16, x):
    hi, mid, lo = _split3(x)
    return _dot(w_bf16, hi) + _dot(w_bf16, mid) + _dot(w_bf16, lo)


def _ffn_ln_kernel(*refs, n_cast):
    x_ref, wg_ref, wu_ref, wd_ref, g_ref, b_ref = refs[:6]
    cast_in = refs[6:6 + n_cast]
    o_ref = refs[6 + n_cast]
    cast_out = refs[7 + n_cast:7 + 2 * n_cast]
    (xb_ref,) = refs[7 + 2 * n_cast:]
    f = pl.program_id(1)

    @pl.when(f == 0)
    def _():
        xb_ref[...] = x_ref[...].astype(BF16)
        o_ref[...] = jnp.zeros_like(o_ref)

    xb = xb_ref[...]
    gate = _dot(xb, wg_ref[...])
    up = _dot(xb, wu_ref[...])
    act = gate * jax.nn.sigmoid(gate) * up
    o_ref[...] += _dot(act.astype(BF16), wd_ref[...])

    for src_ref, dst_ref in zip(cast_in, cast_out):
        dst_ref[...] = src_ref[...].astype(BF16)

    @pl.when(f == pl.num_programs(1) - 1)
    def _():
        y = ALPHA * x_ref[...] + 0.5 * o_ref[...]
        o_ref[...] = _layer_norm(y, g_ref[...], b_ref[...])


def _cast_spec(shape, n_i, n_f):
    rows, cols = shape
    run = -(-n_f // CAST_GROUPS)
    if rows % (16 * n_i) == 0 and cols % (LANE * CAST_GROUPS) == 0:
        return pl.BlockSpec((rows // n_i, cols // CAST_GROUPS), lambda i, f: (i, f // run))
    if rows % (16 * CAST_GROUPS) == 0 and cols % (LANE * n_i) == 0:
        return pl.BlockSpec((rows // CAST_GROUPS, cols // n_i), lambda i, f: (f // run, i))
    assert rows % (16 * n_i) == 0, shape
    return pl.BlockSpec((rows // n_i, cols), lambda i, f: (i, 0))


def _ffn_ln(x, wg, wu, wd, g, b, to_cast=(), tm=512, tf=512):
    t, d = x.shape
    ff = wg.shape[1]
    n_i, n_f = t // tm, ff // tf
    cast_specs = [_cast_spec(w.shape, n_i, n_f) for w in to_cast]
    outs = pl.pallas_call(
        functools.partial(_ffn_ln_kernel, n_cast=len(to_cast)),
        grid=(n_i, n_f),
        in_specs=[
            pl.BlockSpec((tm, d), lambda i, f: (i, 0)),
            pl.BlockSpec((d, tf), lambda i, f: (0, f)),
            pl.BlockSpec((d, tf), lambda i, f: (0, f)),
            pl.BlockSpec((tf, d), lambda i, f: (f, 0)),
            pl.BlockSpec((1, d), lambda i, f: (0, 0)),
            pl.BlockSpec((1, d), lambda i, f: (0, 0)),
        ] + cast_specs,
        out_specs=[pl.BlockSpec((tm, d), lambda i, f: (i, 0))] + cast_specs,
        out_shape=[jax.ShapeDtypeStruct((t, d), F32)]
        + [jax.ShapeDtypeStruct(w.shape, BF16) for w in to_cast],
        scratch_shapes=[pltpu.VMEM((tm, d), BF16)],
        compiler_params=_cparams(("parallel", "arbitrary")),
        name="ffn_ln",
    )(x, wg, wu, wd, g, b, *to_cast)
    return outs[0], list(outs[1:])


def _proj_kernel(x_ref, w_ref, wt_ref, o_ref, ot_ref, xb_ref):
    @pl.when(pl.program_id(1) == 0)
    def _():
        xb_ref[...] = x_ref[...].astype(BF16)
        ot_ref[...] = _dot_tb(xb_ref[...], wt_ref[...].astype(BF16))

    o_ref[...] = _dot_tb(xb_ref[...], w_ref[...].astype(BF16))


def _proj(x, w_t, n, w_tail_t, tm=1024, tn=1024):
    t, d = x.shape
    n_tail = w_tail_t.shape[0]
    return pl.pallas_call(
        _proj_kernel,
        grid=(t // tm, n // tn),
        in_specs=[
            pl.BlockSpec((tm, d), lambda i, j: (i, 0)),
            pl.BlockSpec((tn, d), lambda i, j: (j, 0)),
            pl.BlockSpec((n_tail, d), lambda i, j: (0, 0), pipeline_mode=pl.Buffered(1)),
        ],
        out_specs=[pl.BlockSpec((tm, tn), lambda i, j: (i, j)), pl.BlockSpec((tm, n_tail), lambda i, j: (i, 0))],
        out_shape=[jax.ShapeDtypeStruct((t, n), F32), jax.ShapeDtypeStruct((t, n_tail), F32)],
        scratch_shapes=[pltpu.VMEM((tm, d), BF16)],
        compiler_params=_cparams(("parallel", "arbitrary")),
        name="proj",
    )(x, w_t, w_tail_t)


def _rel_bucket(dist):
    max_exact = REL_BUCKETS // 2
    n = jnp.maximum(dist, 0)
    nf = jnp.maximum(n, 1).astype(F32)
    large = max_exact + (jnp.log(nf / max_exact) / math.log(REL_MAX_DIST / max_exact)
                         * (REL_BUCKETS - max_exact)).astype(jnp.int32)
    large = jnp.minimum(large, REL_BUCKETS - 1)
    return jnp.where(n < max_exact, n, large)


def _bias_tab_kernel(rb_ref, o_ref):
    p = pl.program_id(0)
    blk = MOBA_BLOCK
    diff = lax.broadcasted_iota(jnp.int32, (8, 2 * blk), 1) - blk
    for kind in range(2):
        dist = diff + kind * blk
        bucket = _rel_bucket(dist)
        for hh in range(PAIR):
            h = PAIR * p + hh
            vec = jnp.zeros(dist.shape, F32)
            for bkt in range(REL_BUCKETS):
                vec = jnp.where(bucket == bkt, rb_ref[bkt, h], vec)
            vec = vec * LOG2E
            if kind == 0:
                vec = jnp.where(dist >= 0, vec, NEG)
            rows = jnp.broadcast_to(vec[0:1], (blk, 2 * blk))
            o_ref[0, 2 * hh + kind] = pltpu.roll(rows, 0, 1, stride=1, stride_axis=0)[:, blk:]


def _bias_tab(rel_bias):
    npair = ATT_HEADS // PAIR
    return pl.pallas_call(
        _bias_tab_kernel,
        grid=(npair,),
        in_specs=[pl.BlockSpec(memory_space=pltpu.SMEM)],
        out_specs=pl.BlockSpec((1, 2 * PAIR, MOBA_BLOCK, MOBA_BLOCK), lambda p: (p, 0, 0, 0)),
        out_shape=jax.ShapeDtypeStruct((npair, 2 * PAIR, MOBA_BLOCK, MOBA_BLOCK), F32),
        compiler_params=_cparams(("arbitrary",)),
        name="bias_tab",
    )(rel_bias)


def _rwkv_prep_tile(tiles, prev_rows, first, mixm_ref, mixt_ref, w0_ref, w2_ref, a0_ref, a2_ref, g2_ref,
                    kk_ref, ka_ref, esum_ref, out):
    width = RWKV_WIDTH

    def shifted(g, mix):
        x = tiles[g]()
        prev = pltpu.roll(x, 1, 0)
        row = lax.broadcasted_iota(jnp.int32, x.shape, 0)
        last = 0.0 if first is True else jnp.where(first, 0.0, prev_rows[g]())
        prev = jnp.where(row == 0, last, prev)
        return x + (prev - x) * mix

    tail = shifted(3, mixt_ref[...])
    wd = tail[:, TAIL_WD:TAIL_WD + LANE]
    ad = tail[:, TAIL_AD:TAIL_AD + LANE]
    gd = tail[:, TAIL_GD:TAIL_GD + GD_PAD]
    z = -(w0_ref[...] + _dot(jnp.tanh(wd).astype(BF16), w2_ref[...]))
    yield
    softplus = jnp.maximum(z, 0.0) + jnp.log(1.0 + jnp.exp(-jnp.abs(z)))
    w_log = -softplus - 0.5
    lw = -jnp.exp(w_log)
    yield
    a = jax.nn.sigmoid(a0_ref[...] + _dot(ad.astype(BF16), a2_ref[...]))
    yield
    g = _dot(jax.nn.sigmoid(gd).astype(BF16), g2_ref[...])
    yield
    k = shifted(1, mixm_ref[:, width:2 * width])
    kk = k * kk_ref[...]
    kk2 = kk * kk
    yield
    ss = []
    for c in range(0, width, QUAD):
        ss.append(_dot3_exact_rhs(kk2[:, c:c + QUAD], esum_ref[...]))
        yield
    kk = kk / jnp.maximum(jnp.sqrt(jnp.concatenate(ss, axis=1)), 1e-12)
    k = k * (1.0 + (a - 1.0) * ka_ref[...])
    yield
    r = shifted(0, mixm_ref[:, 0:width])
    yield
    v = shifted(2, mixm_ref[:, 2 * width:3 * width])
    out.extend((r, k, v, lw, -kk, kk * a, g))


def _moba_kernel(rb_ref, q_ref, k_ref, v_ref, tab_ref, o_ref, kb_scr, vt_scr, kmean_scr):
    p = pl.program_id(1)
    i0 = pl.program_id(2) * MOBA_QBLOCKS
    nb = kb_scr.shape[0]
    blk = MOBA_BLOCK

    @pl.when(i0 == 0)
    def _():
        lane_k = lax.broadcasted_iota(jnp.int32, (blk, LANE), 1)
        lane_m = lax.broadcasted_iota(jnp.int32, (1, LANE), 1)
        ones = jnp.ones((ONES_ROWS, blk), BF16)
        for n in range(nb):
            kblk = k_ref[0, n * blk:(n + 1) * blk, :]
            kb_scr[n, :, :LANE] = kblk.astype(BF16)
            hot = (lane_k == n) | (lane_k == nb + n) | (lane_k == 2 * nb + n)
            kb_scr[n, :, LANE:] = jnp.where(hot, 1.0, 0.0).astype(BF16)
            kmean = jnp.mean(kblk, axis=0, keepdims=True)
            v_t = v_ref[0, n * blk:(n + 1) * blk, :].T.astype(BF16)
            for hh in range(PAIR):
                in_head = (lane_m >= hh * HEAD_DIM) & (lane_m < (hh + 1) * HEAD_DIM)
                kmean_scr[hh * nb + n:hh * nb + n + 1, :] = jnp.where(in_head, kmean, 0.0)
                vt_scr[n, hh * VT_ROWS:hh * VT_ROWS + HEAD_DIM, :] = v_t[hh * HEAD_DIM:(hh + 1) * HEAD_DIM, :]
                vt_scr[n, hh * VT_ROWS + HEAD_DIM:(hh + 1) * VT_ROWS, :] = ones

    q8 = q_ref[0] * (HEAD_DIM ** -0.5)
    qs = q8 * LOG2E
    nq = q8.shape[0]
    lane = lax.broadcasted_iota(jnp.int32, q8.shape, 1)
    nidx = lax.broadcasted_iota(jnp.int32, (nb, nq), 0)
    own = i0 + lax.broadcasted_iota(jnp.int32, (1, nq), 1) // blk
    gate_all = _dot_tb(kmean_scr[...], q8, precision=HI)

    q_tail, q_far, adj_mask = [], [], []
    for hh in range(PAIR):
        gate = gate_all[hh * nb:(hh + 1) * nb]
        avail = nidx < own
        sel = jnp.zeros((nb, nq), jnp.bool_)
        for _ in range(MOBA_TOPK):
            best = jnp.max(jnp.where(avail, gate, -jnp.inf), axis=0, keepdims=True)
            first = jnp.min(jnp.where(avail & (gate == best), nidx, nb), axis=0, keepdims=True)
            pick = nidx == first
            sel = sel | pick
            avail = avail & jnp.logical_not(pick)
        b_far = rb_ref[REL_BUCKETS - 1, PAIR * p + hh] * LOG2E
        rowb = jnp.where(sel & (nidx < own - 1), b_far, NEG)
        adj_mask.append(jnp.max(jnp.where(sel & (nidx == own - 1), 0.0, NEG), axis=0, keepdims=True))
        hi, mid, lo = _split3(rowb)
        parts = jnp.concatenate([hi.astype(F32), mid.astype(F32), lo.astype(F32),
                                 jnp.zeros((LANE - 3 * nb, nq), F32)], axis=0)
        in_head = (lane >= hh * HEAD_DIM) & (lane < (hh + 1) * HEAD_DIM)
        q_h = jnp.where(in_head, qs, 0.0).astype(BF16)
        q_tail.append(q_h)
        q_far.append(jnp.concatenate([q_h, parts.T.astype(BF16)], axis=1))

    def vt(hh, n):
        return vt_scr[n, hh * VT_ROWS:(hh + 1) * VT_ROWS, :]

    def block_chain(hh, n, scores, parts):
        s = scores()
        yield
        m_s = jnp.max(s, axis=0, keepdims=True)
        yield
        parts.append((m_s, _dot(vt(hh, n), jnp.exp2(s - m_s).astype(BF16))))

    def run_chains(gens):
        for _ in itertools.zip_longest(*gens):
            pass

    def merge(prev, parts):
        m_new = parts[0][0] if prev is None else jnp.maximum(prev[0], parts[0][0])
        for m_s, _ in parts[1:]:
            m_new = jnp.maximum(m_new, m_s)
        acc = None if prev is None else jnp.exp2(prev[0] - m_new) * prev[1]
        for m_s, acc_s in parts:
            term = jnp.exp2(m_s - m_new) * acc_s
            acc = term if acc is None else acc + term
        return m_new, acc

    def tail_blocks(hh, c):
        cols = slice(c * blk, (c + 1) * blk)
        i_c = i0 + c
        n_adj = jnp.maximum(i_c - 1, 0)
        q_c = q_tail[hh][cols]

        def own_blk():
            return _dot_tb(kb_scr[i_c, :, :LANE], q_c) + tab_ref[0, 2 * hh]

        def prev_blk():
            return _dot_tb(kb_scr[n_adj, :, :LANE], q_c) + tab_ref[0, 2 * hh + 1] + adj_mask[hh][:, cols]

        return [(i_c, own_blk), (n_adj, prev_blk)]

    tails = {(hh, c): [] for hh in range(PAIR) for c in range(MOBA_QBLOCKS)}
    run_chains(block_chain(hh, n, scores, tails[hh, c])
               for hh in range(PAIR) for c in range(MOBA_QBLOCKS) for n, scores in tail_blocks(hh, c))
    tails = {key: merge(None, parts) for key, parts in tails.items()}
    state = [tuple(jnp.concatenate([tails[hh, c][s] for c in range(MOBA_QBLOCKS)], axis=1) for s in range(2))
             for hh in range(PAIR)]

    def far_body(j, carry):
        parts = [[] for _ in range(PAIR)]
        chains = []
        for hh in range(PAIR):
            for u in range(MOBA_UNROLL):
                nc = jnp.minimum(j * MOBA_UNROLL + u, nb - 1)
                chains.append(block_chain(hh, nc, lambda nc=nc, hh=hh: _dot_tb(kb_scr[nc], q_far[hh]), parts[hh]))
        run_chains(chains)
        return tuple(x for hh in range(PAIR) for x in merge(carry[2 * hh:2 * hh + 2], parts[hh]))

    n_far = jnp.maximum(i0 + MOBA_QBLOCKS - 2, 0)
    trips = (n_far + MOBA_UNROLL - 1) // MOBA_UNROLL
    flat = lax.fori_loop(0, trips, far_body, tuple(x for st in state for x in st))
    out_t = jnp.concatenate(
        [flat[2 * hh + 1][:HEAD_DIM] / flat[2 * hh + 1][HEAD_DIM:HEAD_DIM + 1] for hh in range(PAIR)], axis=0)
    o_ref[0] = out_t.T


def _moba(rel_bias, u_main, tab):
    batch, seq, _ = u_main.shape
    npair = ATT_HEADS // PAIR
    nb = seq // MOBA_BLOCK
    assert 3 * nb <= LANE and nb % MOBA_UNROLL == 0 and nb % MOBA_QBLOCKS == 0
    kcol = ATT_WIDTH // LANE
    qrows = MOBA_QBLOCKS * MOBA_BLOCK
    return pl.pallas_call(
        _moba_kernel,
        grid=(batch, npair, nb // MOBA_QBLOCKS),
        in_specs=[
            pl.BlockSpec(memory_space=pltpu.SMEM),
            pl.BlockSpec((1, qrows, LANE), lambda b, p, i: (b, i, p)),
            pl.BlockSpec((1, seq, LANE), lambda b, p, i: (b, 0, kcol + p)),
            pl.BlockSpec((1, seq, LANE), lambda b, p, i: (b, 0, 2 * kcol + p)),
            pl.BlockSpec((1, 2 * PAIR, MOBA_BLOCK, MOBA_BLOCK), lambda b, p, i: (p, 0, 0, 0)),
        ],
        out_specs=pl.BlockSpec((1, qrows, LANE), lambda b, p, i: (b, i, p)),
        out_shape=jax.ShapeDtypeStruct((batch, seq, ATT_WIDTH), F32),
        scratch_shapes=[
            pltpu.VMEM((nb, MOBA_BLOCK, 2 * LANE), BF16),
            pltpu.VMEM((nb, PAIR * VT_ROWS, MOBA_BLOCK), BF16),
            pltpu.VMEM((PAIR * nb, LANE), F32),
        ],
        compiler_params=_cparams(("parallel", "parallel", "arbitrary")),
        name="moba",
    )(rel_bias, u_main, u_main, u_main, tab)


def _rwkv_chunk_kernel(*refs):
    next_tiles, next_prev, first_tiles, prep_params = refs[0:4], refs[4:8], refs[8:12], refs[12:22]
    rk_ref, gng_ref, gnb_ref, o_ref, st_scr, prep_scr = refs[22:]
    eq_ref = prep_params[-1]
    c = pl.program_id(0)
    nquad = RWKV_WIDTH // QUAD
    nbatch = o_ref.shape[0]

    def prep_chain(bi, tiles, prevs, first):
        vals = []
        yield from _rwkv_prep_tile([lambda t=t: t[bi] for t in tiles], [lambda p=p: p[bi, 7:8, :] for p in prevs],
                                   first, *prep_params, vals)
        yield
        for q, val in enumerate(vals):
            prep_scr[q, bi] = val

    @pl.when(c == 0)
    def _():
        st_scr[...] = jnp.zeros_like(st_scr)
        for _ in itertools.zip_longest(*[prep_chain(bi, first_tiles, next_prev, True) for bi in range(nbatch)]):
            pass

    eq = eq_ref[...]
    levels = int(math.log2(CHUNK))
    row_h = lax.broadcasted_iota(jnp.int32, (QUAD, QUAD), 0) // HEAD_DIM
    col_h = lax.broadcasted_iota(jnp.int32, (QUAD, QUAD), 1) // HEAD_DIM
    bdmask = row_h == col_h
    eye = (lax.broadcasted_iota(jnp.int32, (QUAD, QUAD), 0)
           == lax.broadcasted_iota(jnp.int32, (QUAD, QUAD), 1))
    t_idx = lax.broadcasted_iota(jnp.int32, (CHUNK, QUAD), 0)
    s_idx = lax.broadcasted_iota(jnp.int32, (CHUNK, QUAD), 1) % CHUNK
    strict = t_idx > s_idx
    incl = t_idx >= s_idx
    both = jnp.concatenate([strict, incl], axis=0)
    reps = QUAD // CHUNK

    def bds(x):
        xb = x.astype(BF16)
        return jnp.where(bdmask, jnp.concatenate([xb] * reps, axis=0), jnp.zeros((), BF16))

    def mm(a, b):
        return _dot(a.astype(BF16), b.astype(BF16))

    def quad_stages(bi, qd):
        sl = slice(qd * QUAD, (qd + 1) * QUAD)
        st_ref = st_scr.at[bi * nquad + qd]
        r, k, v, lw, aa, bb, g = (prep_scr[q, bi, :, sl] for q in range(7))

        cum = lw
        for sh in (1 << b for b in range(levels)):
            cum = cum + jnp.where(t_idx >= sh, pltpu.roll(cum, sh, 0), 0.0)
        ctot = cum[CHUNK - 1:CHUNK, :]
        e_neg = jnp.exp(-cum)
        e_tol = jnp.exp(ctot - cum)
        a_t = aa * jnp.exp(cum - lw)
        r_t = r * jnp.exp(cum)
        b_c = bb * e_neg
        k_c = k * e_neg
        b_h = bb * e_tol
        k_h = k * e_tol
        gamma = jnp.exp(ctot)

        lhs = jnp.concatenate([a_t, r_t], axis=0).astype(BF16)
        pb = _dot_tb(lhs, bds(b_c))
        pk = _dot_tb(lhs, bds(k_c))
        bk_t = jnp.concatenate([b_h, k_h], axis=0).T.astype(BF16)
        yield
        a_rb = jnp.where(incl, pb[CHUNK:], 0.0).astype(BF16)
        akrk = mm(jnp.where(both, pk, 0.0), bds(v))
        n = jnp.where(strict, pb[:CHUNK], 0.0)
        t_inv = jnp.where(t_idx == s_idx, 1.0, 0.0) + n
        n = mm(n, bds(n))
        yield
        for lvl in range(1, levels):
            bd_n = bds(n)
            if lvl < levels - 1:
                tn = mm(jnp.concatenate([t_inv, n], axis=0), bd_n)
                t_inv = t_inv + tn[:CHUNK]
                n = tn[CHUNK:]
            else:
                t_inv = t_inv + mm(t_inv, bd_n)
            yield
        w = mm(t_inv, bds(a_t))
        u = mm(t_inv, bds(akrk[:CHUNK]))
        yield
        q_t = r_t + mm(a_rb, bds(w))
        y0 = mm(a_rb, bds(u)) + akrk[CHUNK:]
        w0 = jnp.concatenate([w, jnp.zeros_like(w)], axis=0)
        g_mat = jnp.where(bdmask, mm(bk_t, w0), 0.0) + jnp.where(eye, gamma, 0.0)
        h_mat = jnp.where(bdmask, mm(bk_t, jnp.concatenate([u, v], axis=0)), 0.0)
        yield
        gy = mm(jnp.concatenate([g_mat, q_t], axis=0), st_ref[...])
        st_ref[...] = gy[:QUAD] + h_mat
        y = gy[QUAD:] + y0
        yield

        inv_n = 1.0 / HEAD_DIM
        stats = mm(jnp.concatenate([y, r * k * rk_ref[:, sl]], axis=0), eq)
        mu = stats[:CHUNK] * inv_n
        bonus = stats[CHUNK:] * v
        yield
        d = y - mu
        var = mm(d * d, eq) * inv_n
        yield
        yn = d * lax.rsqrt(var + GN_EPS) * gng_ref[:, sl] + gnb_ref[:, sl]
        o_ref[bi, :, sl] = (yn + bonus) * g

    chains = [quad_stages(bi, qd) for bi in range(nbatch) for qd in range(nquad)]
    chains += [prep_chain(bi, next_tiles, next_prev, False) for bi in range(nbatch)]
    for _ in itertools.zip_longest(*chains):
        pass


def _rwkv(u_main, u_tail, prep_params, r_k, gn_g, gn_b):
    batch, seq, _ = u_main.shape
    width = RWKV_WIDTH
    nchunks = seq // CHUNK
    rcol = 3 * ATT_WIDTH // width
    nxt = lambda c: jnp.minimum(c + 1, nchunks - 1)
    prev8 = lambda c: nxt(c) * (CHUNK // 8) - 1
    const = lambda a: pl.BlockSpec(a.shape, lambda c: (0,) * a.ndim)

    def tiles(rows, row_index):
        return ([pl.BlockSpec((batch, rows, width), lambda c, g=g: (0, row_index(c), rcol + g)) for g in range(3)]
                + [pl.BlockSpec((batch, rows, TAIL_COLS), lambda c: (0, row_index(c), 0))])

    u_args = (u_main, u_main, u_main, u_tail)
    return pl.pallas_call(
        _rwkv_chunk_kernel,
        grid=(nchunks,),
        in_specs=tiles(CHUNK, nxt) + tiles(8, prev8) + tiles(CHUNK, lambda c: 0)
        + [const(a) for a in prep_params] + [const(r_k), const(gn_g), const(gn_b)],
        out_specs=pl.BlockSpec((batch, CHUNK, width), lambda c: (0, c, 0)),
        out_shape=jax.ShapeDtypeStruct((batch, seq, width), F32),
        scratch_shapes=[pltpu.VMEM((batch * width // QUAD, QUAD, QUAD), F32),
                        pltpu.VMEM((7, batch, CHUNK, width), F32)],
        compiler_params=_cparams(("arbitrary",)),
        name="rwkv_chunk",
    )(*u_args, *u_args, *u_args, *prep_params, r_k, gn_g, gn_b)


def _out_ln_kernel(h_ref, att_ref, rw_ref, wa_ref, wr_ref, g_ref, b_ref, o_ref):
    mix = _dot(att_ref[...].astype(BF16), wa_ref[...]) + _dot(rw_ref[...].astype(BF16), wr_ref[...])
    o_ref[...] = _layer_norm(ALPHA * h_ref[...] + mix, g_ref[...], b_ref[...])


def _out_ln(h, att, rw, w_out, g, b, tm=512):
    t, d = h.shape
    wa, wr = att.shape[1], rw.shape[1]
    assert wa == wr and w_out.shape[0] == wa + wr
    row = lambda n: pl.BlockSpec((tm, n), lambda i: (i, 0))
    full = lambda shape: pl.BlockSpec(shape, lambda i: (0,) * len(shape))
    return pl.pallas_call(
        _out_ln_kernel,
        grid=(t // tm,),
        in_specs=[row(d), row(wa), row(wr),
                  pl.BlockSpec((wa, d), lambda i: (0, 0)), pl.BlockSpec((wr, d), lambda i: (1, 0)),
                  full((1, d)), full((1, d))],
        out_specs=row(d),
        out_shape=jax.ShapeDtypeStruct((t, d), F32),
        compiler_params=_cparams(("parallel",)),
        name="out_ln",
    )(h, att, rw, w_out, w_out, g, b)


def _ple_ln_kernel(h_ref, p_ref, wg_ref, bg_ref, wp_ref, g_ref, b_ref, o_ref):
    h = h_ref[...]
    e = _dot(p_ref[...].astype(BF16), wp_ref[...])
    gate = jax.nn.sigmoid(_dot(h.astype(BF16), wg_ref[...]) + bg_ref[...])
    o_ref[...] = _layer_norm(ALPHA * h + gate * e, g_ref[...], b_ref[...])


def _ple_ln(h, p, w_gate, b_gate, w_up, g, b, tm=512):
    t, d = h.shape
    row = lambda n: pl.BlockSpec((tm, n), lambda i: (i, 0))
    full = lambda shape: pl.BlockSpec(shape, lambda i: (0,) * len(shape))
    return pl.pallas_call(
        _ple_ln_kernel,
        grid=(t // tm,),
        in_specs=[row(d), row(p.shape[1]), full(w_gate.shape), full((1, d)), full(w_up.shape),
                  full((1, d)), full((1, d))],
        out_specs=row(d),
        out_shape=jax.ShapeDtypeStruct((t, d), F32),
        compiler_params=_cparams(("parallel",)),
        name="ple_ln",
    )(h, p, w_gate, b_gate, w_up, g, b)


def _pad_rows(w, rows):
    return jnp.pad(w, ((0, rows - w.shape[0]), (0, 0)))


def _pack_tail(cols, axis):
    wd, ad, gd = jnp.split(cols, [DECAY_LORA, DECAY_LORA + AAA_LORA], axis=axis)

    def pad(x, n):
        widths = [(0, 0)] * x.ndim
        widths[axis] = (0, n - x.shape[axis])
        return jnp.pad(x, widths)

    return jnp.concatenate([pad(wd, LANE), pad(ad, LANE), pad(gd, GD_PAD)], axis=axis)


def kernel(x, p, ffn1_w_gate, ffn1_w_up, ffn1_w_down, ln1_g, ln1_b, w_in, rel_bias, shift_mix, decay_w0, decay_w2, a_a0, a_a2, gate_g2, k_k, k_a, r_k, gn_g, gn_b, w_out, ln2_g, ln2_b, ffn2_w_gate, ffn2_w_up, ffn2_w_down, ln3_g, ln3_b, ple_w_up, ple_w_gate, ple_b_gate, ln4_g, ln4_b):
    batch, seq, d = x.shape
    t = batch * seq
    row = lambda a: a.reshape(1, -1)
    main_cols = 3 * ATT_WIDTH + 3 * RWKV_WIDTH

    heads = jnp.arange(QUAD) // HEAD_DIM
    esum = (heads[:, None] == heads[None, :]).astype(BF16)
    tab = _bias_tab(rel_bias)

    h = x.reshape(t, d).astype(F32)
    for i in range(DEPTH):
        w_in_t = w_in[i].T
        later_weights = (ffn2_w_gate[i], ffn2_w_up[i], ffn2_w_down[i], w_out[i], ple_w_gate[i], ple_w_up[i])
        h, (wg2, wu2, wd2, wo, wpg, wpu) = _ffn_ln(
            h, ffn1_w_gate[i].astype(BF16), ffn1_w_up[i].astype(BF16), ffn1_w_down[i].astype(BF16),
            row(ln1_g[i]), row(ln1_b[i]), to_cast=later_weights)
        u_main, u_tail = _proj(h, w_in_t, main_cols, _pack_tail(w_in_t[main_cols:], axis=0))
        mix = shift_mix[i]
        prep_params = (row(mix[:3 * RWKV_WIDTH]), row(_pack_tail(mix[3 * RWKV_WIDTH:], axis=0)),
                       row(decay_w0[i]), _pad_rows(decay_w2[i], LANE).astype(BF16), row(a_a0[i]),
                       _pad_rows(a_a2[i], LANE).astype(BF16), _pad_rows(gate_g2[i], GD_PAD).astype(BF16),
                       row(k_k[i]), row(k_a[i]), esum)
        u_main3 = u_main.reshape(batch, seq, main_cols)
        att = _moba(rel_bias, u_main3, tab)
        rw = _rwkv(u_main3, u_tail.reshape(batch, seq, TAIL_COLS), prep_params,
                   row(r_k[i]), row(gn_g[i]), row(gn_b[i]))
        h = _out_ln(h, att.reshape(t, ATT_WIDTH), rw.reshape(t, RWKV_WIDTH), wo,
                    row(ln2_g[i]), row(ln2_b[i]))
        h, _ = _ffn_ln(h, wg2, wu2, wd2, row(ln3_g[i]), row(ln3_b[i]))
        h = _ple_ln(h, p[i].reshape(t, PLE_DIM).astype(F32), wpg, row(ple_b_gate[i]), wpu,
                    row(ln4_g[i]), row(ln4_b[i]))
    return h.reshape(batch, seq, d).astype(x.dtype)
```

```python
import functools
import itertools
import math

import jax
import jax.numpy as jnp
from jax import lax
from jax.experimental import pallas as pl
from jax.experimental.pallas import tpu as pltpu

D_MODEL = 2048
DEPTH = 1
HEAD_DIM = 64
ATT_HEADS = 16
RWKV_HEADS = 16
ATT_WIDTH = ATT_HEADS * HEAD_DIM
RWKV_WIDTH = RWKV_HEADS * HEAD_DIM
MOBA_BLOCK = 256
MOBA_TOPK = 3
MOBA_UNROLL = 4
MOBA_QBLOCKS = 2
LOG2E = 1.4426950408889634
REL_BUCKETS = 32
REL_MAX_DIST = 128
DECAY_LORA = 64
AAA_LORA = 64
GATE_LORA = 160
D_FF = 5632
PLE_DIM = 256
LN_EPS = 1e-5
GN_EPS = 64e-5
NEG = -1e30
ALPHA = (2 * DEPTH) ** 0.25

LANE = 128
PAIR = LANE // HEAD_DIM
CHUNK = 64
QUAD = 256
ONES_ROWS = 16
VT_ROWS = HEAD_DIM + ONES_ROWS
CAST_GROUPS = 4
TAIL_COLS = 512
TAIL_WD, TAIL_AD, TAIL_GD = 0, 128, 256
GD_PAD = 256
VMEM_LIMIT = 56 * 1024 * 1024

F32 = jnp.float32
BF16 = jnp.bfloat16
HI = lax.Precision.HIGHEST


def _cparams(sem):
    return pltpu.CompilerParams(dimension_semantics=sem, vmem_limit_bytes=VMEM_LIMIT)


def _layer_norm(y, g, b):
    mu = jnp.mean(y, axis=-1, keepdims=True)
    d = y - mu
    var = jnp.mean(d * d, axis=-1, keepdims=True)
    return d * lax.rsqrt(var + LN_EPS) * g + b


def _dot(a, b):
    return jnp.dot(a, b, preferred_element_type=F32)


def _dot_tb(a, b, precision=None):
    return lax.dot_general(a, b, (((1,), (1,)), ((), ())), preferred_element_type=F32,
                           precision=precision)


def _split3(x):
    hi = x.astype(BF16)
    r1 = x - hi.astype(F32)
    mid = r1.astype(BF16)
    lo = (r1 - mid.astype(F32)).astype(BF16)
    return hi, mid, lo


def _ffn_ln_kernel(*refs, n_cast):
    x_ref, wg_ref, wu_ref, wd_ref, g_ref, b_ref = refs[:6]
    cast_in = refs[6:6 + n_cast]
    o_ref = refs[6 + n_cast]
    cast_out = refs[7 + n_cast:7 + 2 * n_cast]
    (xb_ref,) = refs[7 + 2 * n_cast:]
    f = pl.program_id(1)

    @pl.when(f == 0)
    def _():
        xb_ref[...] = x_ref[...].astype(BF16)
        o_ref[...] = jnp.zeros_like(o_ref)

    xb = xb_ref[...]
    gate = _dot(xb, wg_ref[...])
    up = _dot(xb, wu_ref[...])
    act = gate * jax.nn.sigmoid(gate) * up
    o_ref[...] += _dot(act.astype(BF16), wd_ref[...])

    for src_ref, dst_ref in zip(cast_in, cast_out):
        dst_ref[...] = src_ref[...].astype(BF16)

    @pl.when(f == pl.num_programs(1) - 1)
    def _():
        y = ALPHA * x_ref[...] + 0.5 * o_ref[...]
        o_ref[...] = _layer_norm(y, g_ref[...], b_ref[...])


def _cast_spec(shape, n_i, n_f):
    rows, cols = shape
    run = -(-n_f // CAST_GROUPS)
    if rows % (16 * n_i) == 0 and cols % (LANE * CAST_GROUPS) == 0:
        return pl.BlockSpec((rows // n_i, cols // CAST_GROUPS), lambda i, f: (i, f // run))
    if rows % (16 * CAST_GROUPS) == 0 and cols % (LANE * n_i) == 0:
        return pl.BlockSpec((rows // CAST_GROUPS, cols // n_i), lambda i, f: (f // run, i))
    assert rows % (16 * n_i) == 0, shape
    return pl.BlockSpec((rows // n_i, cols), lambda i, f: (i, 0))


def _ffn_ln(x, wg, wu, wd, g, b, to_cast=(), tm=512, tf=512):
    t, d = x.shape
    ff = wg.shape[1]
    n_i, n_f = t // tm, ff // tf
    cast_specs = [_cast_spec(w.shape, n_i, n_f) for w in to_cast]
    outs = pl.pallas_call(
        functools.partial(_ffn_ln_kernel, n_cast=len(to_cast)),
        grid=(n_i, n_f),
        in_specs=[
            pl.BlockSpec((tm, d), lambda i, f: (i, 0)),
            pl.BlockSpec((d, tf), lambda i, f: (0, f)),
            pl.BlockSpec((d, tf), lambda i, f: (0, f)),
            pl.BlockSpec((tf, d), lambda i, f: (f, 0)),
            pl.BlockSpec((1, d), lambda i, f: (0, 0)),
            pl.BlockSpec((1, d), lambda i, f: (0, 0)),
        ] + cast_specs,
        out_specs=[pl.BlockSpec((tm, d), lambda i, f: (i, 0))] + cast_specs,
        out_shape=[jax.ShapeDtypeStruct((t, d), F32)]
        + [jax.ShapeDtypeStruct(w.shape, BF16) for w in to_cast],
        scratch_shapes=[pltpu.VMEM((tm, d), BF16)],
        compiler_params=_cparams(("parallel", "arbitrary")),
        name="ffn_ln",
    )(x, wg, wu, wd, g, b, *to_cast)
    return outs[0], list(outs[1:])


def _proj_kernel(x_ref, w_ref, wt_ref, o_ref, ot_ref, xb_ref):
    @pl.when(pl.program_id(1) == 0)
    def _():
        xb_ref[...] = x_ref[...].astype(BF16)
        ot_ref[...] = _dot_tb(xb_ref[...], wt_ref[...].astype(BF16))

    o_ref[...] = _dot_tb(xb_ref[...], w_ref[...].astype(BF16))


def _proj(x, w_t, n, w_tail_t, tm=1024, tn=1024):
    t, d = x.shape
    n_tail = w_tail_t.shape[0]
    return pl.pallas_call(
        _proj_kernel,
        grid=(t // tm, n // tn),
        in_specs=[
            pl.BlockSpec((tm, d), lambda i, j: (i, 0)),
            pl.BlockSpec((tn, d), lambda i, j: (j, 0)),
            pl.BlockSpec((n_tail, d), lambda i, j: (0, 0), pipeline_mode=pl.Buffered(1)),
        ],
        out_specs=[pl.BlockSpec((tm, tn), lambda i, j: (i, j)), pl.BlockSpec((tm, n_tail), lambda i, j: (i, 0))],
        out_shape=[jax.ShapeDtypeStruct((t, n), F32), jax.ShapeDtypeStruct((t, n_tail), F32)],
        scratch_shapes=[pltpu.VMEM((tm, d), BF16)],
        compiler_params=_cparams(("parallel", "arbitrary")),
        name="proj",
    )(x, w_t, w_tail_t)


def _rel_bucket(dist):
    max_exact = REL_BUCKETS // 2
    n = jnp.maximum(dist, 0)
    nf = jnp.maximum(n, 1).astype(F32)
    large = max_exact + (jnp.log(nf / max_exact) / math.log(REL_MAX_DIST / max_exact)
                         * (REL_BUCKETS - max_exact)).astype(jnp.int32)
    large = jnp.minimum(large, REL_BUCKETS - 1)
    return jnp.where(n < max_exact, n, large)


def _bias_tab_kernel(rb_ref, o_ref):
    p = pl.program_id(0)
    blk = MOBA_BLOCK
    diff = lax.broadcasted_iota(jnp.int32, (8, 2 * blk), 1) - blk
    for kind in range(2):
        dist = diff + kind * blk
        bucket = _rel_bucket(dist)
        for hh in range(PAIR):
            h = PAIR * p + hh
            vec = jnp.zeros(dist.shape, F32)
            for bkt in range(REL_BUCKETS):
                vec = jnp.where(bucket == bkt, rb_ref[bkt, h], vec)
            vec = vec * LOG2E
            if kind == 0:
                vec = jnp.where(dist >= 0, vec, NEG)
            rows = jnp.broadcast_to(vec[0:1], (blk, 2 * blk))
            o_ref[0, 2 * hh + kind] = pltpu.roll(rows, 0, 1, stride=1, stride_axis=0)[:, blk:]


def _bias_tab(rel_bias):
    npair = ATT_HEADS // PAIR
    return pl.pallas_call(
        _bias_tab_kernel,
        grid=(npair,),
        in_specs=[pl.BlockSpec(memory_space=pltpu.SMEM)],
        out_specs=pl.BlockSpec((1, 2 * PAIR, MOBA_BLOCK, MOBA_BLOCK), lambda p: (p, 0, 0, 0)),
        out_shape=jax.ShapeDtypeStruct((npair, 2 * PAIR, MOBA_BLOCK, MOBA_BLOCK), F32),
        compiler_params=_cparams(("arbitrary",)),
        name="bias_tab",
    )(rel_bias)


def _rwkv_prep_tile(tiles, prev_rows, first, mixm_ref, mixt_ref, w0_ref, w2_ref, a0_ref, a2_ref, g2_ref,
                    kk_ref, ka_ref, esum_ref, out):
    width = RWKV_WIDTH

    def shifted(g, mix):
        x = tiles[g]()
        prev = pltpu.roll(x, 1, 0)
        row = lax.broadcasted_iota(jnp.int32, x.shape, 0)
        last = 0.0 if first is True else jnp.where(first, 0.0, prev_rows[g]())
        prev = jnp.where(row == 0, last, prev)
        return x + (prev - x) * mix

    tail = shifted(3, mixt_ref[...])
    wd = tail[:, TAIL_WD:TAIL_WD + LANE]
    ad = tail[:, TAIL_AD:TAIL_AD + LANE]
    gd = tail[:, TAIL_GD:TAIL_GD + GD_PAD]
    y = w0_ref[...] + _dot(jnp.tanh(wd).astype(BF16), w2_ref[...])
    yield
    lw = -math.exp(-0.5) * jax.nn.sigmoid(y)
    yield
    a = jax.nn.sigmoid(a0_ref[...] + _dot(ad.astype(BF16), a2_ref[...]))
    yield
    g = _dot(jax.nn.sigmoid(gd).astype(BF16), g2_ref[...])
    yield
    k = shifted(1, mixm_ref[:, width:2 * width])
    kk = k * kk_ref[...]
    kk2 = kk * kk
    yield
    ss = []
    for c in range(0, width, QUAD):
        ss.append(_dot(kk2[:, c:c + QUAD].astype(BF16), esum_ref[...]))
        yield
    kk = kk * jnp.minimum(lax.rsqrt(jnp.concatenate(ss, axis=1)), 1e12)
    k = k * (1.0 + (a - 1.0) * ka_ref[...])
    yield
    r = shifted(0, mixm_ref[:, 0:width])
    yield
    v = shifted(2, mixm_ref[:, 2 * width:3 * width])
    out.extend((r, k, v, lw, -kk, kk * a, g))


def _moba_kernel(rb_ref, q_ref, k_ref, v_ref, tab_ref, o_ref, kb_scr, vt_scr, kmean_scr):
    p = pl.program_id(1)
    i0 = pl.program_id(2) * MOBA_QBLOCKS
    nb = kb_scr.shape[0]
    blk = MOBA_BLOCK

    @pl.when(i0 == 0)
    def _():
        lane_k = lax.broadcasted_iota(jnp.int32, (blk, LANE), 1)
        lane_m = lax.broadcasted_iota(jnp.int32, (1, LANE), 1)
        ones = jnp.ones((ONES_ROWS, blk), BF16)
        for n in range(nb):
            kblk = k_ref[0, n * blk:(n + 1) * blk, :]
            kb_scr[n, :, :LANE] = kblk.astype(BF16)
            hot = (lane_k == n) | (lane_k == nb + n) | (lane_k == 2 * nb + n)
            kb_scr[n, :, LANE:] = jnp.where(hot, 1.0, 0.0).astype(BF16)
            kmean = jnp.mean(kblk, axis=0, keepdims=True)
            v_t = v_ref[0, n * blk:(n + 1) * blk, :].T.astype(BF16)
            for hh in range(PAIR):
                in_head = (lane_m >= hh * HEAD_DIM) & (lane_m < (hh + 1) * HEAD_DIM)
                kmean_scr[hh * nb + n:hh * nb + n + 1, :] = jnp.where(in_head, kmean, 0.0)
                vt_scr[n, hh * VT_ROWS:hh * VT_ROWS + HEAD_DIM, :] = v_t[hh * HEAD_DIM:(hh + 1) * HEAD_DIM, :]
                vt_scr[n, hh * VT_ROWS + HEAD_DIM:(hh + 1) * VT_ROWS, :] = ones

    q8 = q_ref[0] * (HEAD_DIM ** -0.5)
    qs = q8 * LOG2E
    nq = q8.shape[0]
    lane = lax.broadcasted_iota(jnp.int32, q8.shape, 1)
    nidx = lax.broadcasted_iota(jnp.int32, (nb, nq), 0)
    own = i0 + lax.broadcasted_iota(jnp.int32, (1, nq), 1) // blk
    gate_all = _dot_tb(kmean_scr[...], q8, precision=HI)

    q_tail, q_far, adj_mask = [], [], []
    for hh in range(PAIR):
        gate = gate_all[hh * nb:(hh + 1) * nb]
        avail = nidx < own
        sel = jnp.zeros((nb, nq), jnp.bool_)
        for _ in range(MOBA_TOPK):
            best = jnp.max(jnp.where(avail, gate, -jnp.inf), axis=0, keepdims=True)
            first = jnp.min(jnp.where(avail & (gate == best), nidx, nb), axis=0, keepdims=True)
            pick = nidx == first
            sel = sel | pick
            avail = avail & jnp.logical_not(pick)
        b_far = rb_ref[REL_BUCKETS - 1, PAIR * p + hh] * LOG2E
        rowb = jnp.where(sel & (nidx < own - 1), b_far, NEG)
        adj_mask.append(jnp.max(jnp.where(sel & (nidx == own - 1), 0.0, NEG), axis=0, keepdims=True))
        hi, mid, lo = _split3(rowb)
        parts = jnp.concatenate([hi.astype(F32), mid.astype(F32), lo.astype(F32),
                                 jnp.zeros((LANE - 3 * nb, nq), F32)], axis=0)
        in_head = (lane >= hh * HEAD_DIM) & (lane < (hh + 1) * HEAD_DIM)
        q_h = jnp.where(in_head, qs, 0.0).astype(BF16)
        q_tail.append(q_h)
        q_far.append(jnp.concatenate([q_h, parts.T.astype(BF16)], axis=1))

    def vt(hh, n):
        return vt_scr[n, hh * VT_ROWS:(hh + 1) * VT_ROWS, :]

    def block_chain(hh, n, scores, parts):
        s = scores()
        yield
        m_s = jnp.max(s, axis=0, keepdims=True)
        yield
        parts.append((m_s, _dot(vt(hh, n), jnp.exp2(s - m_s).astype(BF16))))

    def run_chains(gens):
        for _ in itertools.zip_longest(*gens):
            pass

    def merge(prev, parts):
        m_new = parts[0][0] if prev is None else jnp.maximum(prev[0], parts[0][0])
        for m_s, _ in parts[1:]:
            m_new = jnp.maximum(m_new, m_s)
        acc = None if prev is None else jnp.exp2(prev[0] - m_new) * prev[1]
        for m_s, acc_s in parts:
            term = jnp.exp2(m_s - m_new) * acc_s
            acc = term if acc is None else acc + term
        return m_new, acc

    def tail_blocks(hh, c):
        cols = slice(c * blk, (c + 1) * blk)
        i_c = i0 + c
        n_adj = jnp.maximum(i_c - 1, 0)
        q_c = q_tail[hh][cols]

        def own_blk():
            return _dot_tb(kb_scr[i_c, :, :LANE], q_c) + tab_ref[0, 2 * hh]

        def prev_blk():
            return _dot_tb(kb_scr[n_adj, :, :LANE], q_c) + tab_ref[0, 2 * hh + 1] + adj_mask[hh][:, cols]

        return [(i_c, own_blk), (n_adj, prev_blk)]

    tails = {(hh, c): [] for hh in range(PAIR) for c in range(MOBA_QBLOCKS)}
    run_chains(block_chain(hh, n, scores, tails[hh, c])
               for hh in range(PAIR) for c in range(MOBA_QBLOCKS) for n, scores in tail_blocks(hh, c))
    tails = {key: merge(None, parts) for key, parts in tails.items()}
    state = [tuple(jnp.concatenate([tails[hh, c][s] for c in range(MOBA_QBLOCKS)], axis=1) for s in range(2))
             for hh in range(PAIR)]

    def far_body(j, carry):
        parts = [[] for _ in range(PAIR)]
        chains = []
        for hh in range(PAIR):
            for u in range(MOBA_UNROLL):
                nc = jnp.minimum(j * MOBA_UNROLL + u, nb - 1)
                chains.append(block_chain(hh, nc, lambda nc=nc, hh=hh: _dot_tb(kb_scr[nc], q_far[hh]), parts[hh]))
        run_chains(chains)
        return tuple(x for hh in range(PAIR) for x in merge(carry[2 * hh:2 * hh + 2], parts[hh]))

    n_far = jnp.maximum(i0 + MOBA_QBLOCKS - 2, 0)
    trips = (n_far + MOBA_UNROLL - 1) // MOBA_UNROLL
    flat = lax.fori_loop(0, trips, far_body, tuple(x for st in state for x in st))
    out_t = jnp.concatenate(
        [flat[2 * hh + 1][:HEAD_DIM] / flat[2 * hh + 1][HEAD_DIM:HEAD_DIM + 1] for hh in range(PAIR)], axis=0)
    o_ref[0] = out_t.T


def _moba(rel_bias, u_main, tab):
    batch, seq, _ = u_main.shape
    npair = ATT_HEADS // PAIR
    nb = seq // MOBA_BLOCK
    assert 3 * nb <= LANE and nb % MOBA_UNROLL == 0 and nb % MOBA_QBLOCKS == 0
    kcol = ATT_WIDTH // LANE
    qrows = MOBA_QBLOCKS * MOBA_BLOCK
    return pl.pallas_call(
        _moba_kernel,
        grid=(batch, npair, nb // MOBA_QBLOCKS),
        in_specs=[
            pl.BlockSpec(memory_space=pltpu.SMEM),
            pl.BlockSpec((1, qrows, LANE), lambda b, p, i: (b, i, p)),
            pl.BlockSpec((1, seq, LANE), lambda b, p, i: (b, 0, kcol + p)),
            pl.BlockSpec((1, seq, LANE), lambda b, p, i: (b, 0, 2 * kcol + p)),
            pl.BlockSpec((1, 2 * PAIR, MOBA_BLOCK, MOBA_BLOCK), lambda b, p, i: (p, 0, 0, 0)),
        ],
        out_specs=pl.BlockSpec((1, qrows, LANE), lambda b, p, i: (b, i, p)),
        out_shape=jax.ShapeDtypeStruct((batch, seq, ATT_WIDTH), F32),
        scratch_shapes=[
            pltpu.VMEM((nb, MOBA_BLOCK, 2 * LANE), BF16),
            pltpu.VMEM((nb, PAIR * VT_ROWS, MOBA_BLOCK), BF16),
            pltpu.VMEM((PAIR * nb, LANE), F32),
        ],
        compiler_params=_cparams(("parallel", "parallel", "arbitrary")),
        name="moba",
    )(rel_bias, u_main, u_main, u_main, tab)


def _rwkv_chunk_kernel(*refs):
    next_tiles, next_prev, first_tiles, prep_params = refs[0:4], refs[4:8], refs[8:12], refs[12:22]
    rk_ref, gng_ref, gnb_ref, o_ref, st_scr, prep_scr = refs[22:]
    eq_ref = prep_params[-1]
    c = pl.program_id(0)
    nquad = RWKV_WIDTH // QUAD
    nbatch = o_ref.shape[0]

    def prep_chain(bi, tiles, prevs, first):
        vals = []
        yield from _rwkv_prep_tile([lambda t=t: t[bi] for t in tiles], [lambda p=p: p[bi, 7:8, :] for p in prevs],
                                   first, *prep_params, vals)
        yield
        for q, val in enumerate(vals):
            prep_scr[q, bi] = val

    @pl.when(c == 0)
    def _():
        st_scr[...] = jnp.zeros_like(st_scr)
        for _ in itertools.zip_longest(*[prep_chain(bi, first_tiles, next_prev, True) for bi in range(nbatch)]):
            pass

    eq = eq_ref[...]
    levels = int(math.log2(CHUNK))
    row_h = lax.broadcasted_iota(jnp.int32, (QUAD, QUAD), 0) // HEAD_DIM
    col_h = lax.broadcasted_iota(jnp.int32, (QUAD, QUAD), 1) // HEAD_DIM
    bdmask = row_h == col_h
    eye = (lax.broadcasted_iota(jnp.int32, (QUAD, QUAD), 0)
           == lax.broadcasted_iota(jnp.int32, (QUAD, QUAD), 1))
    t_idx = lax.broadcasted_iota(jnp.int32, (CHUNK, QUAD), 0)
    s_idx = lax.broadcasted_iota(jnp.int32, (CHUNK, QUAD), 1) % CHUNK
    strict = t_idx > s_idx
    incl = t_idx >= s_idx
    both = jnp.concatenate([strict, incl], axis=0)
    reps = QUAD // CHUNK

    def bds(x):
        xb = x.astype(BF16)
        return jnp.where(bdmask, jnp.concatenate([xb] * reps, axis=0), jnp.zeros((), BF16))

    def mm(a, b):
        return _dot(a.astype(BF16), b.astype(BF16))

    def quad_stages(bi, qd):
        sl = slice(qd * QUAD, (qd + 1) * QUAD)
        st_ref = st_scr.at[bi * nquad + qd]
        r, k, v, lw, aa, bb, g = (prep_scr[q, bi, :, sl] for q in range(7))

        cum = lw
        for sh in (1 << b for b in range(levels)):
            cum = cum + jnp.where(t_idx >= sh, pltpu.roll(cum, sh, 0), 0.0)
        ctot = cum[CHUNK - 1:CHUNK, :]
        e_neg = jnp.exp(-cum)
        e_tol = jnp.exp(ctot - cum)
        a_t = aa * jnp.exp(cum - lw)
        r_t = r * jnp.exp(cum)
        b_c = bb * e_neg
        k_c = k * e_neg
        b_h = bb * e_tol
        k_h = k * e_tol
        gamma = jnp.exp(ctot)

        lhs = jnp.concatenate([a_t, r_t], axis=0).astype(BF16)
        pb = _dot_tb(lhs, bds(b_c))
        pk = _dot_tb(lhs, bds(k_c))
        bk_t = jnp.concatenate([b_h, k_h], axis=0).T.astype(BF16)
        yield
        a_rb = jnp.where(incl, pb[CHUNK:], 0.0).astype(BF16)
        akrk = mm(jnp.where(both, pk, 0.0), bds(v))
        n = jnp.where(strict, pb[:CHUNK], 0.0)
        t_inv = jnp.where(t_idx == s_idx, 1.0, 0.0) + n
        n = mm(n, bds(n))
        yield
        for lvl in range(1, levels):
            bd_n = bds(n)
            if lvl < levels - 1:
                tn = mm(jnp.concatenate([t_inv, n], axis=0), bd_n)
                t_inv = t_inv + tn[:CHUNK]
                n = tn[CHUNK:]
            else:
                t_inv = t_inv + mm(t_inv, bd_n)
            yield
        w = mm(t_inv, bds(a_t))
        u = mm(t_inv, bds(akrk[:CHUNK]))
        yield
        q_t = r_t + mm(a_rb, bds(w))
        y0 = mm(a_rb, bds(u)) + akrk[CHUNK:]
        w0 = jnp.concatenate([w, jnp.zeros_like(w)], axis=0)
        g_mat = jnp.where(bdmask, mm(bk_t, w0), 0.0) + jnp.where(eye, gamma, 0.0)
        h_mat = jnp.where(bdmask, mm(bk_t, jnp.concatenate([u, v], axis=0)), 0.0)
        yield
        gy = mm(jnp.concatenate([g_mat, q_t], axis=0), st_ref[...])
        st_ref[...] = gy[:QUAD] + h_mat
        y = gy[QUAD:] + y0
        yield

        inv_n = 1.0 / HEAD_DIM
        stats = mm(jnp.concatenate([y, r * k * rk_ref[:, sl]], axis=0), eq)
        mu = stats[:CHUNK] * inv_n
        bonus = stats[CHUNK:] * v
        yield
        d = y - mu
        var = mm(d * d, eq) * inv_n
        yield
        yn = d * lax.rsqrt(var + GN_EPS) * gng_ref[:, sl] + gnb_ref[:, sl]
        o_ref[bi, :, sl] = (yn + bonus) * g

    chains = [quad_stages(bi, qd) for bi in range(nbatch) for qd in range(nquad)]
    chains += [prep_chain(bi, next_tiles, next_prev, False) for bi in range(nbatch)]
    for _ in itertools.zip_longest(*chains):
        pass


def _rwkv(u_main, u_tail, prep_params, r_k, gn_g, gn_b):
    batch, seq, _ = u_main.shape
    width = RWKV_WIDTH
    nchunks = seq // CHUNK
    rcol = 3 * ATT_WIDTH // width
    nxt = lambda c: jnp.minimum(c + 1, nchunks - 1)
    prev8 = lambda c: nxt(c) * (CHUNK // 8) - 1
    const = lambda a: pl.BlockSpec(a.shape, lambda c: (0,) * a.ndim)

    def tiles(rows, row_index):
        return ([pl.BlockSpec((batch, rows, width), lambda c, g=g: (0, row_index(c), rcol + g)) for g in range(3)]
                + [pl.BlockSpec((batch, rows, TAIL_COLS), lambda c: (0, row_index(c), 0))])

    u_args = (u_main, u_main, u_main, u_tail)
    return pl.pallas_call(
        _rwkv_chunk_kernel,
        grid=(nchunks,),
        in_specs=tiles(CHUNK, nxt) + tiles(8, prev8) + tiles(CHUNK, lambda c: 0)
        + [const(a) for a in prep_params] + [const(r_k), const(gn_g), const(gn_b)],
        out_specs=pl.BlockSpec((batch, CHUNK, width), lambda c: (0, c, 0)),
        out_shape=jax.ShapeDtypeStruct((batch, seq, width), F32),
        scratch_shapes=[pltpu.VMEM((batch * width // QUAD, QUAD, QUAD), F32),
                        pltpu.VMEM((7, batch, CHUNK, width), F32)],
        compiler_params=_cparams(("arbitrary",)),
        name="rwkv_chunk",
    )(*u_args, *u_args, *u_args, *prep_params, r_k, gn_g, gn_b)


def _out_ln_kernel(h_ref, att_ref, rw_ref, wa_ref, wr_ref, g_ref, b_ref, o_ref):
    mix = _dot(att_ref[...].astype(BF16), wa_ref[...]) + _dot(rw_ref[...].astype(BF16), wr_ref[...])
    o_ref[...] = _layer_norm(ALPHA * h_ref[...] + mix, g_ref[...], b_ref[...])


def _out_ln(h, att, rw, w_out, g, b, tm=512):
    t, d = h.shape
    wa, wr = att.shape[1], rw.shape[1]
    assert wa == wr and w_out.shape[0] == wa + wr
    row = lambda n: pl.BlockSpec((tm, n), lambda i: (i, 0))
    full = lambda shape: pl.BlockSpec(shape, lambda i: (0,) * len(shape))
    return pl.pallas_call(
        _out_ln_kernel,
        grid=(t // tm,),
        in_specs=[row(d), row(wa), row(wr),
                  pl.BlockSpec((wa, d), lambda i: (0, 0)), pl.BlockSpec((wr, d), lambda i: (1, 0)),
                  full((1, d)), full((1, d))],
        out_specs=row(d),
        out_shape=jax.ShapeDtypeStruct((t, d), F32),
        compiler_params=_cparams(("parallel",)),
        name="out_ln",
    )(h, att, rw, w_out, w_out, g, b)


def _ple_ln_kernel(h_ref, p_ref, wg_ref, bg_ref, wp_ref, g_ref, b_ref, o_ref):
    h = h_ref[...]
    e = _dot(p_ref[...].astype(BF16), wp_ref[...])
    gate = jax.nn.sigmoid(_dot(h.astype(BF16), wg_ref[...]) + bg_ref[...])
    o_ref[...] = _layer_norm(ALPHA * h + gate * e, g_ref[...], b_ref[...])


def _ple_ln(h, p, w_gate, b_gate, w_up, g, b, tm=512):
    t, d = h.shape
    row = lambda n: pl.BlockSpec((tm, n), lambda i: (i, 0))
    full = lambda shape: pl.BlockSpec(shape, lambda i: (0,) * len(shape))
    return pl.pallas_call(
        _ple_ln_kernel,
        grid=(t // tm,),
        in_specs=[row(d), row(p.shape[1]), full(w_gate.shape), full((1, d)), full(w_up.shape),
                  full((1, d)), full((1, d))],
        out_specs=row(d),
        out_shape=jax.ShapeDtypeStruct((t, d), F32),
        compiler_params=_cparams(("parallel",)),
        name="ple_ln",
    )(h, p, w_gate, b_gate, w_up, g, b)


def _pad_rows(w, rows):
    return jnp.pad(w, ((0, rows - w.shape[0]), (0, 0)))


def _pack_tail(cols, axis):
    wd, ad, gd = jnp.split(cols, [DECAY_LORA, DECAY_LORA + AAA_LORA], axis=axis)

    def pad(x, n):
        widths = [(0, 0)] * x.ndim
        widths[axis] = (0, n - x.shape[axis])
        return jnp.pad(x, widths)

    return jnp.concatenate([pad(wd, LANE), pad(ad, LANE), pad(gd, GD_PAD)], axis=axis)


def kernel(x, p, ffn1_w_gate, ffn1_w_up, ffn1_w_down, ln1_g, ln1_b, w_in, rel_bias, shift_mix, decay_w0, decay_w2, a_a0, a_a2, gate_g2, k_k, k_a, r_k, gn_g, gn_b, w_out, ln2_g, ln2_b, ffn2_w_gate, ffn2_w_up, ffn2_w_down, ln3_g, ln3_b, ple_w_up, ple_w_gate, ple_b_gate, ln4_g, ln4_b):
    batch, seq, d = x.shape
    t = batch * seq
    row = lambda a: a.reshape(1, -1)
    main_cols = 3 * ATT_WIDTH + 3 * RWKV_WIDTH

    heads = jnp.arange(QUAD) // HEAD_DIM
    esum = (heads[:, None] == heads[None, :]).astype(BF16)
    tab = _bias_tab(rel_bias)

    h = x.reshape(t, d).astype(F32)
    for i in range(DEPTH):
        w_in_t = w_in[i].T
        later_weights = (ffn2_w_gate[i], ffn2_w_up[i], ffn2_w_down[i], w_out[i], ple_w_gate[i], ple_w_up[i])
        h, (wg2, wu2, wd2, wo, wpg, wpu) = _ffn_ln(
            h, ffn1_w_gate[i].astype(BF16), ffn1_w_up[i].astype(BF16), ffn1_w_down[i].astype(BF16),
            row(ln1_g[i]), row(ln1_b[i]), to_cast=later_weights)
        u_main, u_tail = _proj(h, w_in_t, main_cols, _pack_tail(w_in_t[main_cols:], axis=0))
        mix = shift_mix[i]
        prep_params = (row(mix[:3 * RWKV_WIDTH]), row(_pack_tail(mix[3 * RWKV_WIDTH:], axis=0)),
                       row(decay_w0[i]), _pad_rows(decay_w2[i], LANE).astype(BF16), row(a_a0[i]),
                       _pad_rows(a_a2[i], LANE).astype(BF16), _pad_rows(gate_g2[i], GD_PAD).astype(BF16),
                       row(k_k[i]), row(k_a[i]), esum)
        u_main3 = u_main.reshape(batch, seq, main_cols)
        att = _moba(rel_bias, u_main3, tab)
        rw = _rwkv(u_main3, u_tail.reshape(batch, seq, TAIL_COLS), prep_params,
                   row(r_k[i]), row(gn_g[i]), row(gn_b[i]))
        h = _out_ln(h, att.reshape(t, ATT_WIDTH), rw.reshape(t, RWKV_WIDTH), wo,
                    row(ln2_g[i]), row(ln2_b[i]))
        h, _ = _ffn_ln(h, wg2, wu2, wd2, row(ln3_g[i]), row(ln3_b[i]))
        h = _ple_ln(h, p[i].reshape(t, PLE_DIM).astype(F32), wpg, row(ple_b_gate[i]), wpu,
                    row(ln4_g[i]), row(ln4_b[i]))
    return h.reshape(batch, seq, d).astype(x.dtype)
```

```python
import functools
import itertools
import math

import jax
import jax.numpy as jnp
from jax import lax
from jax.experimental import pallas as pl
from jax.experimental.pallas import tpu as pltpu

D_MODEL = 2048
DEPTH = 1
HEAD_DIM = 64
ATT_HEADS = 16
RWKV_HEADS = 16
ATT_WIDTH = ATT_HEADS * HEAD_DIM
RWKV_WIDTH = RWKV_HEADS * HEAD_DIM
MOBA_BLOCK = 256
MOBA_TOPK = 3
MOBA_UNROLL = 4
MOBA_QBLOCKS = 2
LOG2E = 1.4426950408889634
REL_BUCKETS = 32
REL_MAX_DIST = 128
DECAY_LORA = 64
AAA_LORA = 64
GATE_LORA = 160
D_FF = 5632
PLE_DIM = 256
LN_EPS = 1e-5
GN_EPS = 64e-5
NEG = -1e30
ALPHA = (2 * DEPTH) ** 0.25

LANE = 128
PAIR = LANE // HEAD_DIM
CHUNK = 64
QUAD = 256
ONES_ROWS = 16
VT_ROWS = HEAD_DIM + ONES_ROWS
CAST_GROUPS = 4
TAIL_COLS = 512
TAIL_WD, TAIL_AD, TAIL_GD = 0, 128, 256
GD_PAD = 256
VMEM_LIMIT = 56 * 1024 * 1024

F32 = jnp.float32
BF16 = jnp.bfloat16
HI = lax.Precision.HIGHEST


def _cparams(sem):
    return pltpu.CompilerParams(dimension_semantics=sem, vmem_limit_bytes=VMEM_LIMIT)


def _layer_norm(y, g, b):
    mu = jnp.mean(y, axis=-1, keepdims=True)
    d = y - mu
    var = jnp.mean(d * d, axis=-1, keepdims=True)
    return d * lax.rsqrt(var + LN_EPS) * g + b


def _sigmoid(x):
    return 0.5 * jnp.tanh(0.5 * x) + 0.5


def _dot(a, b):
    return jnp.dot(a, b, preferred_element_type=F32)


def _dot_tb(a, b, precision=None):
    return lax.dot_general(a, b, (((1,), (1,)), ((), ())), preferred_element_type=F32,
                           precision=precision)


def _split3(x):
    hi = x.astype(BF16)
    r1 = x - hi.astype(F32)
    mid = r1.astype(BF16)
    lo = (r1 - mid.astype(F32)).astype(BF16)
    return hi, mid, lo


def _ffn_ln_kernel(*refs, n_cast, emit_bf16):
    x_ref, wg_ref, wu_ref, wd_ref, g_ref, b_ref = refs[:6]
    cast_in = refs[6:6 + n_cast]
    o_ref = refs[6 + n_cast]
    ob_ref = refs[7 + n_cast] if emit_bf16 else None
    first_cast_out = 7 + n_cast + int(emit_bf16)
    cast_out = refs[first_cast_out:first_cast_out + n_cast]
    (xb_ref,) = refs[first_cast_out + n_cast:]
    f = pl.program_id(1)

    @pl.when(f == 0)
    def _():
        xb_ref[...] = x_ref[...].astype(BF16)
        o_ref[...] = jnp.zeros_like(o_ref)

    xb = xb_ref[...]
    gate = _dot(xb, wg_ref[...])
    up = _dot(xb, wu_ref[...])
    act = gate * jax.nn.sigmoid(gate) * up
    o_ref[...] += _dot(act.astype(BF16), wd_ref[...])

    for src_ref, dst_ref in zip(cast_in, cast_out):
        dst_ref[...] = src_ref[...].astype(BF16)

    @pl.when(f == pl.num_programs(1) - 1)
    def _():
        y = _layer_norm(ALPHA * x_ref[...] + 0.5 * o_ref[...], g_ref[...], b_ref[...])
        o_ref[...] = y
        if emit_bf16:
            ob_ref[...] = y.astype(BF16)


def _cast_spec(shape, n_i, n_f):
    rows, cols = shape
    run = -(-n_f // CAST_GROUPS)
    if rows % (16 * n_i) == 0 and cols % (LANE * CAST_GROUPS) == 0:
        return pl.BlockSpec((rows // n_i, cols // CAST_GROUPS), lambda i, f: (i, f // run))
    if rows % (16 * CAST_GROUPS) == 0 and cols % (LANE * n_i) == 0:
        return pl.BlockSpec((rows // CAST_GROUPS, cols // n_i), lambda i, f: (f // run, i))
    assert rows % (16 * n_i) == 0, shape
    return pl.BlockSpec((rows // n_i, cols), lambda i, f: (i, 0))


def _ffn_ln(x, wg, wu, wd, g, b, to_cast=(), emit_bf16=False, tm=512, tf=512):
    t, d = x.shape
    ff = wg.shape[1]
    n_i, n_f = t // tm, ff // tf
    cast_specs = [_cast_spec(w.shape, n_i, n_f) for w in to_cast]
    outs = pl.pallas_call(
        functools.partial(_ffn_ln_kernel, n_cast=len(to_cast), emit_bf16=emit_bf16),
        grid=(n_i, n_f),
        in_specs=[
            pl.BlockSpec((tm, d), lambda i, f: (i, 0)),
            pl.BlockSpec((d, tf), lambda i, f: (0, f)),
            pl.BlockSpec((d, tf), lambda i, f: (0, f)),
            pl.BlockSpec((tf, d), lambda i, f: (f, 0)),
            pl.BlockSpec((1, d), lambda i, f: (0, 0)),
            pl.BlockSpec((1, d), lambda i, f: (0, 0)),
        ] + cast_specs,
        out_specs=[pl.BlockSpec((tm, d), lambda i, f: (i, 0))] * (1 + int(emit_bf16)) + cast_specs,
        out_shape=[jax.ShapeDtypeStruct((t, d), F32)] + [jax.ShapeDtypeStruct((t, d), BF16)] * int(emit_bf16)
        + [jax.ShapeDtypeStruct(w.shape, BF16) for w in to_cast],
        scratch_shapes=[pltpu.VMEM((tm, d), BF16)],
        compiler_params=_cparams(("parallel", "arbitrary")),
        name="ffn_ln",
    )(x, wg, wu, wd, g, b, *to_cast)
    n_res = 1 + int(emit_bf16)
    return outs[0], list(outs[1:n_res]), list(outs[n_res:])


def _proj_kernel(x_ref, w_ref, wt_ref, o_ref, ot_ref):
    @pl.when(pl.program_id(1) == 0)
    def _():
        ot_ref[...] = _dot_tb(x_ref[...], wt_ref[...].astype(BF16))

    o_ref[...] = _dot_tb(x_ref[...], w_ref[...].astype(BF16))


def _proj(x, w_t, n, w_tail_t, tm=2048, tn=512):
    t, d = x.shape
    n_tail = w_tail_t.shape[0]
    return pl.pallas_call(
        _proj_kernel,
        grid=(t // tm, n // tn),
        in_specs=[
            pl.BlockSpec((tm, d), lambda i, j: (i, 0)),
            pl.BlockSpec((tn, d), lambda i, j: (j, 0)),
            pl.BlockSpec((n_tail, d), lambda i, j: (0, 0), pipeline_mode=pl.Buffered(1)),
        ],
        out_specs=[pl.BlockSpec((tm, tn), lambda i, j: (i, j)), pl.BlockSpec((tm, n_tail), lambda i, j: (i, 0))],
        out_shape=[jax.ShapeDtypeStruct((t, n), F32), jax.ShapeDtypeStruct((t, n_tail), F32)],
        compiler_params=_cparams(("parallel", "arbitrary")),
        name="proj",
    )(x, w_t, w_tail_t)


def _rel_bucket(dist):
    max_exact = REL_BUCKETS // 2
    n = jnp.maximum(dist, 0)
    nf = jnp.maximum(n, 1).astype(F32)
    large = max_exact + (jnp.log(nf / max_exact) / math.log(REL_MAX_DIST / max_exact)
                         * (REL_BUCKETS - max_exact)).astype(jnp.int32)
    large = jnp.minimum(large, REL_BUCKETS - 1)
    return jnp.where(n < max_exact, n, large)


def _bias_tab_kernel(rb_ref, o_ref):
    p = pl.program_id(0)
    blk = MOBA_BLOCK
    diff = lax.broadcasted_iota(jnp.int32, (8, 2 * blk), 1) - blk
    for kind in range(2):
        dist = diff + kind * blk
        bucket = _rel_bucket(dist)
        for hh in range(PAIR):
            h = PAIR * p + hh
            vec = jnp.zeros(dist.shape, F32)
            for bkt in range(REL_BUCKETS):
                vec = jnp.where(bucket == bkt, rb_ref[bkt, h], vec)
            vec = vec * LOG2E
            if kind == 0:
                vec = jnp.where(dist >= 0, vec, NEG)
            rows = jnp.broadcast_to(vec[0:1], (blk, 2 * blk))
            o_ref[0, 2 * hh + kind] = pltpu.roll(rows, 0, 1, stride=1, stride_axis=0)[:, blk:]


def _bias_tab(rel_bias):
    npair = ATT_HEADS // PAIR
    return pl.pallas_call(
        _bias_tab_kernel,
        grid=(npair,),
        in_specs=[pl.BlockSpec(memory_space=pltpu.SMEM)],
        out_specs=pl.BlockSpec((1, 2 * PAIR, MOBA_BLOCK, MOBA_BLOCK), lambda p: (p, 0, 0, 0)),
        out_shape=jax.ShapeDtypeStruct((npair, 2 * PAIR, MOBA_BLOCK, MOBA_BLOCK), F32),
        compiler_params=_cparams(("arbitrary",)),
        name="bias_tab",
    )(rel_bias)


def _rwkv_prep_tile(tiles, prev_rows, first, mixm_ref, mixt_ref, w0_ref, w2_ref, a0_ref, a2_ref, g2_ref,
                    kk_ref, ka_ref, esum_ref, out):
    width = RWKV_WIDTH

    def shifted(g, mix):
        x = tiles[g]()
        prev = pltpu.roll(x, 1, 0)
        row = lax.broadcasted_iota(jnp.int32, x.shape, 0)
        last = 0.0 if first is True else jnp.where(first, 0.0, prev_rows[g]())
        prev = jnp.where(row == 0, last, prev)
        return x + (prev - x) * mix

    tail = shifted(3, mixt_ref[...])
    wd = tail[:, TAIL_WD:TAIL_WD + LANE]
    ad = tail[:, TAIL_AD:TAIL_AD + LANE]
    gd = tail[:, TAIL_GD:TAIL_GD + GD_PAD]
    y = w0_ref[...] + _dot(jnp.tanh(wd).astype(BF16), w2_ref[...])
    yield
    lw = -math.exp(-0.5) * _sigmoid(y)
    yield
    a = _sigmoid(a0_ref[...] + _dot(ad.astype(BF16), a2_ref[...]))
    yield
    g = _dot(_sigmoid(gd).astype(BF16), g2_ref[...])
    yield
    k = shifted(1, mixm_ref[:, width:2 * width])
    kk = k * kk_ref[...]
    kk2 = kk * kk
    yield
    ss = []
    for c in range(0, width, QUAD):
        ss.append(_dot(kk2[:, c:c + QUAD].astype(BF16), esum_ref[...]))
        yield
    kk = kk * jnp.minimum(lax.rsqrt(jnp.concatenate(ss, axis=1)), 1e12)
    k = k * (1.0 + (a - 1.0) * ka_ref[...])
    yield
    r = shifted(0, mixm_ref[:, 0:width])
    yield
    v = shifted(2, mixm_ref[:, 2 * width:3 * width])
    out.extend((r, k, v, lw, -kk, kk * a, g))


def _moba_kernel(rb_ref, q_ref, k_ref, v_ref, tab_ref, o_ref, kb_scr, vt_scr, kmean_scr):
    p = pl.program_id(1)
    i0 = pl.program_id(2) * MOBA_QBLOCKS
    nb = kb_scr.shape[0]
    blk = MOBA_BLOCK

    @pl.when(i0 == 0)
    def _():
        lane_k = lax.broadcasted_iota(jnp.int32, (blk, LANE), 1)
        lane_m = lax.broadcasted_iota(jnp.int32, (1, LANE), 1)
        ones = jnp.ones((ONES_ROWS, blk), BF16)
        for n in range(nb):
            kblk = k_ref[0, n * blk:(n + 1) * blk, :]
            kb_scr[n, :, :LANE] = kblk.astype(BF16)
            hot = (lane_k == n) | (lane_k == nb + n) | (lane_k == 2 * nb + n)
            kb_scr[n, :, LANE:] = jnp.where(hot, 1.0, 0.0).astype(BF16)
            kmean = jnp.mean(kblk, axis=0, keepdims=True)
            v_t = v_ref[0, n * blk:(n + 1) * blk, :].T.astype(BF16)
            for hh in range(PAIR):
                in_head = (lane_m >= hh * HEAD_DIM) & (lane_m < (hh + 1) * HEAD_DIM)
                kmean_scr[hh * nb + n:hh * nb + n + 1, :] = jnp.where(in_head, kmean, 0.0)
                vt_scr[n, hh * VT_ROWS:hh * VT_ROWS + HEAD_DIM, :] = v_t[hh * HEAD_DIM:(hh + 1) * HEAD_DIM, :]
                vt_scr[n, hh * VT_ROWS + HEAD_DIM:(hh + 1) * VT_ROWS, :] = ones

    q8 = q_ref[0] * (HEAD_DIM ** -0.5)
    qs = q8 * LOG2E
    nq = q8.shape[0]
    lane = lax.broadcasted_iota(jnp.int32, q8.shape, 1)
    nidx = lax.broadcasted_iota(jnp.int32, (nb, nq), 0)
    own = i0 + lax.broadcasted_iota(jnp.int32, (1, nq), 1) // blk
    gate_all = _dot_tb(kmean_scr[...], q8, precision=HI)

    q_tail, q_far, adj_mask = [], [], []
    for hh in range(PAIR):
        gate = gate_all[hh * nb:(hh + 1) * nb]
        avail = nidx < own
        sel = jnp.zeros((nb, nq), jnp.bool_)
        for _ in range(MOBA_TOPK):
            best = jnp.max(jnp.where(avail, gate, -jnp.inf), axis=0, keepdims=True)
            first = jnp.min(jnp.where(avail & (gate == best), nidx, nb), axis=0, keepdims=True)
            pick = nidx == first
            sel = sel | pick
            avail = avail & jnp.logical_not(pick)
        b_far = rb_ref[REL_BUCKETS - 1, PAIR * p + hh] * LOG2E
        rowb = jnp.where(sel & (nidx < own - 1), b_far, NEG)
        adj_mask.append(jnp.max(jnp.where(sel & (nidx == own - 1), 0.0, NEG), axis=0, keepdims=True))
        hi, mid, lo = _split3(rowb)
        parts = jnp.concatenate([hi.astype(F32), mid.astype(F32), lo.astype(F32),
                                 jnp.zeros((LANE - 3 * nb, nq), F32)], axis=0)
        in_head = (lane >= hh * HEAD_DIM) & (lane < (hh + 1) * HEAD_DIM)
        q_h = jnp.where(in_head, qs, 0.0).astype(BF16)
        q_tail.append(q_h)
        q_far.append(jnp.concatenate([q_h, parts.T.astype(BF16)], axis=1))

    def vt(hh, n):
        return vt_scr[n, hh * VT_ROWS:(hh + 1) * VT_ROWS, :]

    def block_chain(hh, n, scores, parts):
        s = scores()
        yield
        m_s = jnp.max(s, axis=0, keepdims=True)
        yield
        parts.append((m_s, _dot(vt(hh, n), jnp.exp2(s - m_s).astype(BF16))))

    def run_chains(gens):
        for _ in itertools.zip_longest(*gens):
            pass

    def merge(prev, parts):
        m_new = parts[0][0] if prev is None else jnp.maximum(prev[0], parts[0][0])
        for m_s, _ in parts[1:]:
            m_new = jnp.maximum(m_new, m_s)
        acc = None if prev is None else jnp.exp2(prev[0] - m_new) * prev[1]
        for m_s, acc_s in parts:
            term = jnp.exp2(m_s - m_new) * acc_s
            acc = term if acc is None else acc + term
        return m_new, acc

    def tail_blocks(hh, c):
        cols = slice(c * blk, (c + 1) * blk)
        i_c = i0 + c
        n_adj = jnp.maximum(i_c - 1, 0)
        q_c = q_tail[hh][cols]

        def own_blk():
            return _dot_tb(kb_scr[i_c, :, :LANE], q_c) + tab_ref[0, 2 * hh]

        def prev_blk():
            return _dot_tb(kb_scr[n_adj, :, :LANE], q_c) + tab_ref[0, 2 * hh + 1] + adj_mask[hh][:, cols]

        return [(i_c, own_blk), (n_adj, prev_blk)]

    tails = {(hh, c): [] for hh in range(PAIR) for c in range(MOBA_QBLOCKS)}
    run_chains(block_chain(hh, n, scores, tails[hh, c])
               for hh in range(PAIR) for c in range(MOBA_QBLOCKS) for n, scores in tail_blocks(hh, c))
    tails = {key: merge(None, parts) for key, parts in tails.items()}
    state = [tuple(jnp.concatenate([tails[hh, c][s] for c in range(MOBA_QBLOCKS)], axis=1) for s in range(2))
             for hh in range(PAIR)]

    def far_body(j, carry):
        parts = [[] for _ in range(PAIR)]
        chains = []
        for hh in range(PAIR):
            for u in range(MOBA_UNROLL):
                nc = jnp.minimum(j * MOBA_UNROLL + u, nb - 1)
                chains.append(block_chain(hh, nc, lambda nc=nc, hh=hh: _dot_tb(kb_scr[nc], q_far[hh]), parts[hh]))
        run_chains(chains)
        return tuple(x for hh in range(PAIR) for x in merge(carry[2 * hh:2 * hh + 2], parts[hh]))

    n_far = jnp.maximum(i0 + MOBA_QBLOCKS - 2, 0)
    trips = (n_far + MOBA_UNROLL - 1) // MOBA_UNROLL
    flat = lax.fori_loop(0, trips, far_body, tuple(x for st in state for x in st))
    out_t = jnp.concatenate(
        [flat[2 * hh + 1][:HEAD_DIM] / flat[2 * hh + 1][HEAD_DIM:HEAD_DIM + 1] for hh in range(PAIR)], axis=0)
    o_ref[0] = out_t.T


def _moba(rel_bias, u_main, tab):
    batch, seq, _ = u_main.shape
    npair = ATT_HEADS // PAIR
    nb = seq // MOBA_BLOCK
    assert 3 * nb <= LANE and nb % MOBA_UNROLL == 0 and nb % MOBA_QBLOCKS == 0
    kcol = ATT_WIDTH // LANE
    qrows = MOBA_QBLOCKS * MOBA_BLOCK
    return pl.pallas_call(
        _moba_kernel,
        grid=(batch, npair, nb // MOBA_QBLOCKS),
        in_specs=[
            pl.BlockSpec(memory_space=pltpu.SMEM),
            pl.BlockSpec((1, qrows, LANE), lambda b, p, i: (b, i, p)),
            pl.BlockSpec((1, seq, LANE), lambda b, p, i: (b, 0, kcol + p)),
            pl.BlockSpec((1, seq, LANE), lambda b, p, i: (b, 0, 2 * kcol + p)),
            pl.BlockSpec((1, 2 * PAIR, MOBA_BLOCK, MOBA_BLOCK), lambda b, p, i: (p, 0, 0, 0)),
        ],
        out_specs=pl.BlockSpec((1, qrows, LANE), lambda b, p, i: (b, i, p)),
        out_shape=jax.ShapeDtypeStruct((batch, seq, ATT_WIDTH), F32),
        scratch_shapes=[
            pltpu.VMEM((nb, MOBA_BLOCK, 2 * LANE), BF16),
            pltpu.VMEM((nb, PAIR * VT_ROWS, MOBA_BLOCK), BF16),
            pltpu.VMEM((PAIR * nb, LANE), F32),
        ],
        compiler_params=_cparams(("parallel", "parallel", "arbitrary")),
        name="moba",
    )(rel_bias, u_main, u_main, u_main, tab)


def _rwkv_chunk_kernel(*refs):
    next_tiles, next_prev, first_tiles, prep_params = refs[0:4], refs[4:8], refs[8:12], refs[12:22]
    rk_ref, gng_ref, gnb_ref, o_ref, st_scr, prep_scr = refs[22:]
    eq_ref = prep_params[-1]
    c = pl.program_id(0)
    nquad = RWKV_WIDTH // QUAD
    nbatch = o_ref.shape[0]

    def prep_chain(bi, tiles, prevs, first):
        vals = []
        yield from _rwkv_prep_tile([lambda t=t: t[bi] for t in tiles], [lambda p=p: p[bi, 7:8, :] for p in prevs],
                                   first, *prep_params, vals)
        yield
        for q, val in enumerate(vals):
            prep_scr[q, bi] = val

    @pl.when(c == 0)
    def _():
        st_scr[...] = jnp.zeros_like(st_scr)
        for _ in itertools.zip_longest(*[prep_chain(bi, first_tiles, next_prev, True) for bi in range(nbatch)]):
            pass

    eq = eq_ref[...]
    levels = int(math.log2(CHUNK))
    row_h = lax.broadcasted_iota(jnp.int32, (QUAD, QUAD), 0) // HEAD_DIM
    col_h = lax.broadcasted_iota(jnp.int32, (QUAD, QUAD), 1) // HEAD_DIM
    bdmask = row_h == col_h
    eye = (lax.broadcasted_iota(jnp.int32, (QUAD, QUAD), 0)
           == lax.broadcasted_iota(jnp.int32, (QUAD, QUAD), 1))
    t_idx = lax.broadcasted_iota(jnp.int32, (CHUNK, QUAD), 0)
    s_idx = lax.broadcasted_iota(jnp.int32, (CHUNK, QUAD), 1) % CHUNK
    strict = t_idx > s_idx
    incl = t_idx >= s_idx
    both = jnp.concatenate([strict, incl], axis=0)
    reps = QUAD // CHUNK

    def bds(x):
        xb = x.astype(BF16)
        return jnp.where(bdmask, jnp.concatenate([xb] * reps, axis=0), jnp.zeros((), BF16))

    def mm(a, b):
        return _dot(a.astype(BF16), b.astype(BF16))

    def quad_stages(bi, qd):
        sl = slice(qd * QUAD, (qd + 1) * QUAD)
        st_ref = st_scr.at[bi * nquad + qd]
        r, k, v, lw, aa, bb, g = (prep_scr[q, bi, :, sl] for q in range(7))

        cum = lw
        for sh in (1 << b for b in range(levels)):
            cum = cum + jnp.where(t_idx >= sh, pltpu.roll(cum, sh, 0), 0.0)
        e_pos = jnp.exp(cum)
        e_neg = jnp.exp(-cum)
        gamma = e_pos[CHUNK - 1:CHUNK, :]
        e_prev = jnp.where(t_idx == 0, 1.0, pltpu.roll(e_pos, 1, 0))
        a_t = aa * e_prev
        r_t = r * e_pos
        b_c = bb * e_neg
        k_c = k * e_neg
        b_h = b_c * gamma
        k_h = k_c * gamma

        lhs = jnp.concatenate([a_t, r_t], axis=0).astype(BF16)
        pb = _dot_tb(lhs, bds(b_c))
        pk = _dot_tb(lhs, bds(k_c))
        bk_t = jnp.concatenate([b_h, k_h], axis=0).T.astype(BF16)
        yield
        a_rb = jnp.where(incl, pb[CHUNK:], 0.0).astype(BF16)
        akrk = mm(jnp.where(both, pk, 0.0), bds(v))
        n = jnp.where(strict, pb[:CHUNK], 0.0)
        t_inv = jnp.where(t_idx == s_idx, 1.0, 0.0) + n
        n = mm(n, bds(n))
        yield
        for lvl in range(1, levels):
            bd_n = bds(n)
            if lvl < levels - 1:
                tn = mm(jnp.concatenate([t_inv, n], axis=0), bd_n)
                t_inv = t_inv + tn[:CHUNK]
                n = tn[CHUNK:]
            else:
                t_inv = t_inv + mm(t_inv, bd_n)
            yield
        w = mm(t_inv, bds(a_t))
        u = mm(t_inv, bds(akrk[:CHUNK]))
        yield
        q_t = r_t + mm(a_rb, bds(w))
        y0 = mm(a_rb, bds(u)) + akrk[CHUNK:]
        w0 = jnp.concatenate([w, jnp.zeros_like(w)], axis=0)
        g_mat = jnp.where(bdmask, mm(bk_t, w0), 0.0) + jnp.where(eye, gamma, 0.0)
        h_mat = jnp.where(bdmask, mm(bk_t, jnp.concatenate([u, v], axis=0)), 0.0)
        yield
        gy = mm(jnp.concatenate([g_mat, q_t], axis=0), st_ref[...])
        st_ref[...] = gy[:QUAD] + h_mat
        y = gy[QUAD:] + y0
        yield

        inv_n = 1.0 / HEAD_DIM
        stats = mm(jnp.concatenate([y, r * k * rk_ref[:, sl]], axis=0), eq)
        mu = stats[:CHUNK] * inv_n
        bonus = stats[CHUNK:] * v
        yield
        d = y - mu
        var = mm(d * d, eq) * inv_n
        yield
        yn = d * lax.rsqrt(var + GN_EPS) * gng_ref[:, sl] + gnb_ref[:, sl]
        o_ref[bi, :, sl] = (yn + bonus) * g

    chains = [quad_stages(bi, qd) for bi in range(nbatch) for qd in range(nquad)]
    chains += [prep_chain(bi, next_tiles, next_prev, False) for bi in range(nbatch)]
    for _ in itertools.zip_longest(*chains):
        pass


def _rwkv(u_main, u_tail, prep_params, r_k, gn_g, gn_b):
    batch, seq, _ = u_main.shape
    width = RWKV_WIDTH
    nchunks = seq // CHUNK
    rcol = 3 * ATT_WIDTH // width
    nxt = lambda c: jnp.minimum(c + 1, nchunks - 1)
    prev8 = lambda c: nxt(c) * (CHUNK // 8) - 1
    const = lambda a: pl.BlockSpec(a.shape, lambda c: (0,) * a.ndim)

    def tiles(rows, row_index):
        return ([pl.BlockSpec((batch, rows, width), lambda c, g=g: (0, row_index(c), rcol + g)) for g in range(3)]
                + [pl.BlockSpec((batch, rows, TAIL_COLS), lambda c: (0, row_index(c), 0))])

    u_args = (u_main, u_main, u_main, u_tail)
    return pl.pallas_call(
        _rwkv_chunk_kernel,
        grid=(nchunks,),
        in_specs=tiles(CHUNK, nxt) + tiles(8, prev8) + tiles(CHUNK, lambda c: 0)
        + [const(a) for a in prep_params] + [const(r_k), const(gn_g), const(gn_b)],
        out_specs=pl.BlockSpec((batch, CHUNK, width), lambda c: (0, c, 0)),
        out_shape=jax.ShapeDtypeStruct((batch, seq, width), F32),
        scratch_shapes=[pltpu.VMEM((batch * width // QUAD, QUAD, QUAD), F32),
                        pltpu.VMEM((7, batch, CHUNK, width), F32)],
        compiler_params=_cparams(("arbitrary",)),
        name="rwkv_chunk",
    )(*u_args, *u_args, *u_args, *prep_params, r_k, gn_g, gn_b)


def _out_ln_kernel(h_ref, att_ref, rw_ref, wa_ref, wr_ref, g_ref, b_ref, o_ref):
    mix = _dot(att_ref[...].astype(BF16), wa_ref[...]) + _dot(rw_ref[...].astype(BF16), wr_ref[...])
    o_ref[...] = _layer_norm(ALPHA * h_ref[...] + mix, g_ref[...], b_ref[...])


def _out_ln(h, att, rw, w_out, g, b, tm=512):
    t, d = h.shape
    wa, wr = att.shape[1], rw.shape[1]
    assert wa == wr and w_out.shape[0] == wa + wr
    row = lambda n: pl.BlockSpec((tm, n), lambda i: (i, 0))
    full = lambda shape: pl.BlockSpec(shape, lambda i: (0,) * len(shape))
    return pl.pallas_call(
        _out_ln_kernel,
        grid=(t // tm,),
        in_specs=[row(d), row(wa), row(wr),
                  pl.BlockSpec((wa, d), lambda i: (0, 0)), pl.BlockSpec((wr, d), lambda i: (1, 0)),
                  full((1, d)), full((1, d))],
        out_specs=row(d),
        out_shape=jax.ShapeDtypeStruct((t, d), F32),
        compiler_params=_cparams(("parallel",)),
        name="out_ln",
    )(h, att, rw, w_out, w_out, g, b)


def _ple_ln_kernel(h_ref, p_ref, wg_ref, bg_ref, wp_ref, g_ref, b_ref, o_ref):
    h = h_ref[...]
    e = _dot(p_ref[...].astype(BF16), wp_ref[...])
    gate = _sigmoid(_dot(h.astype(BF16), wg_ref[...]) + bg_ref[...])
    o_ref[...] = _layer_norm(ALPHA * h + gate * e, g_ref[...], b_ref[...])


def _ple_ln(h, p, w_gate, b_gate, w_up, g, b, tm=512):
    t, d = h.shape
    row = lambda n: pl.BlockSpec((tm, n), lambda i: (i, 0))
    full = lambda shape: pl.BlockSpec(shape, lambda i: (0,) * len(shape))
    return pl.pallas_call(
        _ple_ln_kernel,
        grid=(t // tm,),
        in_specs=[row(d), row(p.shape[1]), full(w_gate.shape), full((1, d)), full(w_up.shape),
                  full((1, d)), full((1, d))],
        out_specs=row(d),
        out_shape=jax.ShapeDtypeStruct((t, d), F32),
        compiler_params=_cparams(("parallel",)),
        name="ple_ln",
    )(h, p, w_gate, b_gate, w_up, g, b)


def _pad_rows(w, rows):
    return jnp.pad(w, ((0, rows - w.shape[0]), (0, 0)))


def _pack_tail(cols, axis):
    wd, ad, gd = jnp.split(cols, [DECAY_LORA, DECAY_LORA + AAA_LORA], axis=axis)

    def pad(x, n):
        widths = [(0, 0)] * x.ndim
        widths[axis] = (0, n - x.shape[axis])
        return jnp.pad(x, widths)

    return jnp.concatenate([pad(wd, LANE), pad(ad, LANE), pad(gd, GD_PAD)], axis=axis)


def kernel(x, p, ffn1_w_gate, ffn1_w_up, ffn1_w_down, ln1_g, ln1_b, w_in, rel_bias, shift_mix, decay_w0, decay_w2, a_a0, a_a2, gate_g2, k_k, k_a, r_k, gn_g, gn_b, w_out, ln2_g, ln2_b, ffn2_w_gate, ffn2_w_up, ffn2_w_down, ln3_g, ln3_b, ple_w_up, ple_w_gate, ple_b_gate, ln4_g, ln4_b):
    batch, seq, d = x.shape
    t = batch * seq
    row = lambda a: a.reshape(1, -1)
    main_cols = 3 * ATT_WIDTH + 3 * RWKV_WIDTH

    heads = jnp.arange(QUAD) // HEAD_DIM
    esum = (heads[:, None] == heads[None, :]).astype(BF16)
    tab = _bias_tab(rel_bias)

    h = x.reshape(t, d).astype(F32)
    for i in range(DEPTH):
        w_in_t = w_in[i].T
        later_weights = (ffn2_w_gate[i], ffn2_w_up[i], ffn2_w_down[i], w_out[i], ple_w_gate[i], ple_w_up[i])
        h, (h_bf16,), (wg2, wu2, wd2, wo, wpg, wpu) = _ffn_ln(
            h, ffn1_w_gate[i].astype(BF16), ffn1_w_up[i].astype(BF16), ffn1_w_down[i].astype(BF16),
            row(ln1_g[i]), row(ln1_b[i]), to_cast=later_weights, emit_bf16=True)
        u_main, u_tail = _proj(h_bf16, w_in_t, main_cols, _pack_tail(w_in_t[main_cols:], axis=0))
        mix = shift_mix[i]
        prep_params = (row(mix[:3 * RWKV_WIDTH]), row(_pack_tail(mix[3 * RWKV_WIDTH:], axis=0)),
                       row(decay_w0[i]), _pad_rows(decay_w2[i], LANE).astype(BF16), row(a_a0[i]),
                       _pad_rows(a_a2[i], LANE).astype(BF16), _pad_rows(gate_g2[i], GD_PAD).astype(BF16),
                       row(k_k[i]), row(k_a[i]), esum)
        u_main3 = u_main.reshape(batch, seq, main_cols)
        att = _moba(rel_bias, u_main3, tab)
        rw = _rwkv(u_main3, u_tail.reshape(batch, seq, TAIL_COLS), prep_params,
                   row(r_k[i]), row(gn_g[i]), row(gn_b[i]))
        h = _out_ln(h, att.reshape(t, ATT_WIDTH), rw.reshape(t, RWKV_WIDTH), wo,
                    row(ln2_g[i]), row(ln2_b[i]))
        h, _, _ = _ffn_ln(h, wg2, wu2, wd2, row(ln3_g[i]), row(ln3_b[i]))
        h = _ple_ln(h, p[i].reshape(t, PLE_DIM).astype(F32), wpg, row(ple_b_gate[i]), wpu,
                    row(ln4_g[i]), row(ln4_b[i]))
    return h.reshape(batch, seq, d).astype(x.dtype)
```

```python
import functools
import itertools
import math

import jax
import jax.numpy as jnp
from jax import lax
from jax.experimental import pallas as pl
from jax.experimental.pallas import tpu as pltpu

D_MODEL = 2048
DEPTH = 1
HEAD_DIM = 64
ATT_HEADS = 16
RWKV_HEADS = 16
ATT_WIDTH = ATT_HEADS * HEAD_DIM
RWKV_WIDTH = RWKV_HEADS * HEAD_DIM
MOBA_BLOCK = 256
MOBA_TOPK = 3
MOBA_UNROLL = 4
MOBA_QBLOCKS = 2
LOG2E = 1.4426950408889634
REL_BUCKETS = 32
REL_MAX_DIST = 128
DECAY_LORA = 64
AAA_LORA = 64
GATE_LORA = 160
D_FF = 5632
PLE_DIM = 256
LN_EPS = 1e-5
GN_EPS = 64e-5
NEG = -1e30
ALPHA = (2 * DEPTH) ** 0.25

LANE = 128
PAIR = LANE // HEAD_DIM
CHUNK = 64
QUAD = 256
ONES_ROWS = 16
VT_ROWS = HEAD_DIM + ONES_ROWS
CAST_GROUPS = 4
SUB_ROWS = 128
TAIL_COLS = 512
TAIL_WD, TAIL_AD, TAIL_GD = 0, 128, 256
GD_PAD = 256
VMEM_LIMIT = 56 * 1024 * 1024

F32 = jnp.float32
BF16 = jnp.bfloat16
HI = lax.Precision.HIGHEST


def _cparams(sem):
    return pltpu.CompilerParams(dimension_semantics=sem, vmem_limit_bytes=VMEM_LIMIT)


def _layer_norm(y, g, b):
    mu = jnp.mean(y, axis=-1, keepdims=True)
    d = y - mu
    var = jnp.mean(d * d, axis=-1, keepdims=True)
    return d * lax.rsqrt(var + LN_EPS) * g + b


def _sigmoid(x):
    return 0.5 * jnp.tanh(0.5 * x) + 0.5


def _dot(a, b):
    return jnp.dot(a, b, preferred_element_type=F32)


def _dot_tb(a, b, precision=None):
    return lax.dot_general(a, b, (((1,), (1,)), ((), ())), preferred_element_type=F32,
                           precision=precision)


def _split3(x):
    hi = x.astype(BF16)
    r1 = x - hi.astype(F32)
    mid = r1.astype(BF16)
    lo = (r1 - mid.astype(F32)).astype(BF16)
    return hi, mid, lo


def _ffn_ln_kernel(*refs, n_cast, emit_bf16):
    x_ref, wg_ref, wu_ref, wd_ref, g_ref, b_ref = refs[:6]
    cast_in = refs[6:6 + n_cast]
    o_ref = refs[6 + n_cast]
    ob_ref = refs[7 + n_cast] if emit_bf16 else None
    first_cast_out = 7 + n_cast + int(emit_bf16)
    cast_out = refs[first_cast_out:first_cast_out + n_cast]
    (xb_ref,) = refs[first_cast_out + n_cast:]
    f = pl.program_id(1)

    @pl.when(f == 0)
    def _():
        xb_ref[...] = x_ref[...].astype(BF16)
        o_ref[...] = jnp.zeros_like(o_ref)

    xb = xb_ref[...]
    gate = _dot(xb, wg_ref[...])
    up = _dot(xb, wu_ref[...])
    act = gate * jax.nn.sigmoid(gate) * up
    o_ref[...] += _dot(act.astype(BF16), wd_ref[...])

    for src_ref, dst_ref in zip(cast_in, cast_out):
        dst_ref[...] = src_ref[...].astype(BF16)

    @pl.when(f == pl.num_programs(1) - 1)
    def _():
        y = _layer_norm(ALPHA * x_ref[...] + 0.5 * o_ref[...], g_ref[...], b_ref[...])
        o_ref[...] = y
        if emit_bf16:
            ob_ref[...] = y.astype(BF16)


def _cast_spec(shape, n_i, n_f):
    rows, cols = shape
    run = -(-n_f // CAST_GROUPS)
    if rows % (16 * n_i) == 0 and cols % (LANE * CAST_GROUPS) == 0:
        return pl.BlockSpec((rows // n_i, cols // CAST_GROUPS), lambda i, f: (i, f // run))
    if rows % (16 * CAST_GROUPS) == 0 and cols % (LANE * n_i) == 0:
        return pl.BlockSpec((rows // CAST_GROUPS, cols // n_i), lambda i, f: (f // run, i))
    assert rows % (16 * n_i) == 0, shape
    return pl.BlockSpec((rows // n_i, cols), lambda i, f: (i, 0))


def _ffn_ln(x, wg, wu, wd, g, b, to_cast=(), emit_bf16=False, tm=512, tf=512):
    t, d = x.shape
    ff = wg.shape[1]
    n_i, n_f = t // tm, ff // tf
    cast_specs = [_cast_spec(w.shape, n_i, n_f) for w in to_cast]
    outs = pl.pallas_call(
        functools.partial(_ffn_ln_kernel, n_cast=len(to_cast), emit_bf16=emit_bf16),
        grid=(n_i, n_f),
        in_specs=[
            pl.BlockSpec((tm, d), lambda i, f: (i, 0)),
            pl.BlockSpec((d, tf), lambda i, f: (0, f)),
            pl.BlockSpec((d, tf), lambda i, f: (0, f)),
            pl.BlockSpec((tf, d), lambda i, f: (f, 0)),
            pl.BlockSpec((1, d), lambda i, f: (0, 0)),
            pl.BlockSpec((1, d), lambda i, f: (0, 0)),
        ] + cast_specs,
        out_specs=[pl.BlockSpec((tm, d), lambda i, f: (i, 0))] * (1 + int(emit_bf16)) + cast_specs,
        out_shape=[jax.ShapeDtypeStruct((t, d), F32)] + [jax.ShapeDtypeStruct((t, d), BF16)] * int(emit_bf16)
        + [jax.ShapeDtypeStruct(w.shape, BF16) for w in to_cast],
        scratch_shapes=[pltpu.VMEM((tm, d), BF16)],
        compiler_params=_cparams(("parallel", "arbitrary")),
        name="ffn_ln",
    )(x, wg, wu, wd, g, b, *to_cast)
    n_res = 1 + int(emit_bf16)
    return outs[0], list(outs[1:n_res]), list(outs[n_res:])


def _proj_kernel(x_ref, w_ref, wt_ref, o_ref, ot_ref):
    @pl.when(pl.program_id(1) == 0)
    def _():
        ot_ref[...] = _dot_tb(x_ref[...], wt_ref[...].astype(BF16))

    o_ref[...] = _dot_tb(x_ref[...], w_ref[...].astype(BF16))


def _proj(x, w_t, n, w_tail_t, tm=2048, tn=512):
    t, d = x.shape
    n_tail = w_tail_t.shape[0]
    return pl.pallas_call(
        _proj_kernel,
        grid=(t // tm, n // tn),
        in_specs=[
            pl.BlockSpec((tm, d), lambda i, j: (i, 0)),
            pl.BlockSpec((tn, d), lambda i, j: (j, 0)),
            pl.BlockSpec((n_tail, d), lambda i, j: (0, 0), pipeline_mode=pl.Buffered(1)),
        ],
        out_specs=[pl.BlockSpec((tm, tn), lambda i, j: (i, j)), pl.BlockSpec((tm, n_tail), lambda i, j: (i, 0))],
        out_shape=[jax.ShapeDtypeStruct((t, n), F32), jax.ShapeDtypeStruct((t, n_tail), F32)],
        compiler_params=_cparams(("parallel", "arbitrary")),
        name="proj",
    )(x, w_t, w_tail_t)


def _rel_bucket(dist):
    max_exact = REL_BUCKETS // 2
    n = jnp.maximum(dist, 0)
    nf = jnp.maximum(n, 1).astype(F32)
    large = max_exact + (jnp.log(nf / max_exact) / math.log(REL_MAX_DIST / max_exact)
                         * (REL_BUCKETS - max_exact)).astype(jnp.int32)
    large = jnp.minimum(large, REL_BUCKETS - 1)
    return jnp.where(n < max_exact, n, large)


def _bias_tab_kernel(rb_ref, o_ref):
    p = pl.program_id(0)
    blk = MOBA_BLOCK
    diff = lax.broadcasted_iota(jnp.int32, (8, 2 * blk), 1) - blk
    for kind in range(2):
        dist = diff + kind * blk
        bucket = _rel_bucket(dist)
        for hh in range(PAIR):
            h = PAIR * p + hh
            vec = jnp.zeros(dist.shape, F32)
            for bkt in range(REL_BUCKETS):
                vec = jnp.where(bucket == bkt, rb_ref[bkt, h], vec)
            vec = vec * LOG2E
            if kind == 0:
                vec = jnp.where(dist >= 0, vec, NEG)
            rows = jnp.broadcast_to(vec[0:1], (blk, 2 * blk))
            o_ref[0, 2 * hh + kind] = pltpu.roll(rows, 0, 1, stride=1, stride_axis=0)[:, blk:]


def _bias_tab(rel_bias):
    npair = ATT_HEADS // PAIR
    return pl.pallas_call(
        _bias_tab_kernel,
        grid=(npair,),
        in_specs=[pl.BlockSpec(memory_space=pltpu.SMEM)],
        out_specs=pl.BlockSpec((1, 2 * PAIR, MOBA_BLOCK, MOBA_BLOCK), lambda p: (p, 0, 0, 0)),
        out_shape=jax.ShapeDtypeStruct((npair, 2 * PAIR, MOBA_BLOCK, MOBA_BLOCK), F32),
        compiler_params=_cparams(("arbitrary",)),
        name="bias_tab",
    )(rel_bias)


def _rwkv_prep_tile(tiles, prev_rows, first, mixm_ref, mixt_ref, w0_ref, w2_ref, a0_ref, a2_ref, g2_ref,
                    kk_ref, ka_ref, esum_ref, out):
    width = RWKV_WIDTH

    def shifted(g, mix):
        x = tiles[g]()
        prev = pltpu.roll(x, 1, 0)
        row = lax.broadcasted_iota(jnp.int32, x.shape, 0)
        last = 0.0 if first is True else jnp.where(first, 0.0, prev_rows[g]())
        prev = jnp.where(row == 0, last, prev)
        return x + (prev - x) * mix

    tail = shifted(3, mixt_ref[...])
    wd = tail[:, TAIL_WD:TAIL_WD + LANE]
    ad = tail[:, TAIL_AD:TAIL_AD + LANE]
    gd = tail[:, TAIL_GD:TAIL_GD + GD_PAD]
    y = w0_ref[...] + _dot(jnp.tanh(wd).astype(BF16), w2_ref[...])
    yield
    lw = -math.exp(-0.5) * _sigmoid(y)
    yield
    a = _sigmoid(a0_ref[...] + _dot(ad.astype(BF16), a2_ref[...]))
    yield
    g = _dot(_sigmoid(gd).astype(BF16), g2_ref[...])
    yield
    k = shifted(1, mixm_ref[:, width:2 * width])
    kk = k * kk_ref[...]
    kk2 = kk * kk
    yield
    ss = []
    for c in range(0, width, QUAD):
        ss.append(_dot(kk2[:, c:c + QUAD].astype(BF16), esum_ref[...]))
        yield
    kk = kk * jnp.minimum(lax.rsqrt(jnp.concatenate(ss, axis=1)), 1e12)
    k = k * (1.0 + (a - 1.0) * ka_ref[...])
    yield
    r = shifted(0, mixm_ref[:, 0:width])
    yield
    v = shifted(2, mixm_ref[:, 2 * width:3 * width])
    out.extend((r, k, v, lw, -kk, kk * a, g))


def _moba_kernel(rb_ref, q_ref, k_ref, v_ref, tab_ref, o_ref, kb_scr, vt_scr, kmean_scr):
    p = pl.program_id(1)
    i0 = pl.program_id(2) * MOBA_QBLOCKS
    nb = kb_scr.shape[0]
    blk = MOBA_BLOCK

    @pl.when(i0 == 0)
    def _():
        lane_k = lax.broadcasted_iota(jnp.int32, (blk, LANE), 1)
        lane_m = lax.broadcasted_iota(jnp.int32, (1, LANE), 1)
        ones = jnp.ones((ONES_ROWS, blk), BF16)
        for n in range(nb):
            kblk = k_ref[0, n * blk:(n + 1) * blk, :]
            kb_scr[n, :, :LANE] = kblk.astype(BF16)
            hot = (lane_k == n) | (lane_k == nb + n) | (lane_k == 2 * nb + n)
            kb_scr[n, :, LANE:] = jnp.where(hot, 1.0, 0.0).astype(BF16)
            kmean = jnp.mean(kblk, axis=0, keepdims=True)
            v_t = v_ref[0, n * blk:(n + 1) * blk, :].T.astype(BF16)
            for hh in range(PAIR):
                in_head = (lane_m >= hh * HEAD_DIM) & (lane_m < (hh + 1) * HEAD_DIM)
                kmean_scr[hh * nb + n:hh * nb + n + 1, :] = jnp.where(in_head, kmean, 0.0)
                vt_scr[n, hh * VT_ROWS:hh * VT_ROWS + HEAD_DIM, :] = v_t[hh * HEAD_DIM:(hh + 1) * HEAD_DIM, :]
                vt_scr[n, hh * VT_ROWS + HEAD_DIM:(hh + 1) * VT_ROWS, :] = ones

    q8 = q_ref[0] * (HEAD_DIM ** -0.5)
    qs = q8 * LOG2E
    nq = q8.shape[0]
    lane = lax.broadcasted_iota(jnp.int32, q8.shape, 1)
    nidx = lax.broadcasted_iota(jnp.int32, (nb, nq), 0)
    own = i0 + lax.broadcasted_iota(jnp.int32, (1, nq), 1) // blk
    gate_all = _dot_tb(kmean_scr[...], q8, precision=HI)

    q_tail, q_far, adj_mask = [], [], []
    for hh in range(PAIR):
        gate = gate_all[hh * nb:(hh + 1) * nb]
        avail = nidx < own
        sel = jnp.zeros((nb, nq), jnp.bool_)
        for _ in range(MOBA_TOPK):
            best = jnp.max(jnp.where(avail, gate, -jnp.inf), axis=0, keepdims=True)
            first = jnp.min(jnp.where(avail & (gate == best), nidx, nb), axis=0, keepdims=True)
            pick = nidx == first
            sel = sel | pick
            avail = avail & jnp.logical_not(pick)
        b_far = rb_ref[REL_BUCKETS - 1, PAIR * p + hh] * LOG2E
        rowb = jnp.where(sel & (nidx < own - 1), b_far, NEG)
        adj_mask.append(jnp.max(jnp.where(sel & (nidx == own - 1), 0.0, NEG), axis=0, keepdims=True))
        hi, mid, lo = _split3(rowb)
        parts = jnp.concatenate([hi.astype(F32), mid.astype(F32), lo.astype(F32),
                                 jnp.zeros((LANE - 3 * nb, nq), F32)], axis=0)
        in_head = (lane >= hh * HEAD_DIM) & (lane < (hh + 1) * HEAD_DIM)
        q_h = jnp.where(in_head, qs, 0.0).astype(BF16)
        q_tail.append(q_h)
        q_far.append(jnp.concatenate([q_h, parts.T.astype(BF16)], axis=1))

    def vt(hh, n):
        return vt_scr[n, hh * VT_ROWS:(hh + 1) * VT_ROWS, :]

    def block_chain(hh, n, scores, parts):
        s = scores()
        yield
        m_s = jnp.max(s, axis=0, keepdims=True)
        yield
        parts.append((m_s, _dot(vt(hh, n), jnp.exp2(s - m_s).astype(BF16))))

    def run_chains(gens):
        for _ in itertools.zip_longest(*gens):
            pass

    def merge(prev, parts):
        m_new = parts[0][0] if prev is None else jnp.maximum(prev[0], parts[0][0])
        for m_s, _ in parts[1:]:
            m_new = jnp.maximum(m_new, m_s)
        acc = None if prev is None else jnp.exp2(prev[0] - m_new) * prev[1]
        for m_s, acc_s in parts:
            term = jnp.exp2(m_s - m_new) * acc_s
            acc = term if acc is None else acc + term
        return m_new, acc

    def tail_blocks(hh, c):
        cols = slice(c * blk, (c + 1) * blk)
        i_c = i0 + c
        n_adj = jnp.maximum(i_c - 1, 0)
        q_c = q_tail[hh][cols]

        def own_blk():
            return _dot_tb(kb_scr[i_c, :, :LANE], q_c) + tab_ref[0, 2 * hh]

        def prev_blk():
            return _dot_tb(kb_scr[n_adj, :, :LANE], q_c) + tab_ref[0, 2 * hh + 1] + adj_mask[hh][:, cols]

        return [(i_c, own_blk), (n_adj, prev_blk)]

    tails = {(hh, c): [] for hh in range(PAIR) for c in range(MOBA_QBLOCKS)}
    run_chains(block_chain(hh, n, scores, tails[hh, c])
               for hh in range(PAIR) for c in range(MOBA_QBLOCKS) for n, scores in tail_blocks(hh, c))
    tails = {key: merge(None, parts) for key, parts in tails.items()}
    state = [tuple(jnp.concatenate([tails[hh, c][s] for c in range(MOBA_QBLOCKS)], axis=1) for s in range(2))
             for hh in range(PAIR)]

    def far_trip(first_block, n_blocks, carry):
        parts = [[] for _ in range(PAIR)]
        chains = []
        for hh in range(PAIR):
            for u in range(n_blocks):
                nc = first_block + u
                chains.append(block_chain(hh, nc, lambda nc=nc, hh=hh: _dot_tb(kb_scr[nc], q_far[hh]), parts[hh]))
        run_chains(chains)
        return tuple(x for hh in range(PAIR) for x in merge(carry[2 * hh:2 * hh + 2], parts[hh]))

    n_far = jnp.maximum(i0 + MOBA_QBLOCKS - 2, 0)
    full = n_far // MOBA_UNROLL
    flat = lax.fori_loop(0, full, lambda j, c: far_trip(j * MOBA_UNROLL, MOBA_UNROLL, c),
                         tuple(x for st in state for x in st))
    flat = lax.cond(n_far > full * MOBA_UNROLL,
                    lambda c: far_trip(full * MOBA_UNROLL, MOBA_UNROLL // 2, c), lambda c: c, flat)
    out_t = jnp.concatenate(
        [flat[2 * hh + 1][:HEAD_DIM] / flat[2 * hh + 1][HEAD_DIM:HEAD_DIM + 1] for hh in range(PAIR)], axis=0)
    o_ref[0] = out_t.T


def _moba(rel_bias, u_main, tab):
    batch, seq, _ = u_main.shape
    npair = ATT_HEADS // PAIR
    nb = seq // MOBA_BLOCK
    assert 3 * nb <= LANE and nb % MOBA_QBLOCKS == 0
    assert MOBA_QBLOCKS == 2 and MOBA_UNROLL == 2 * MOBA_QBLOCKS
    kcol = ATT_WIDTH // LANE
    qrows = MOBA_QBLOCKS * MOBA_BLOCK
    return pl.pallas_call(
        _moba_kernel,
        grid=(batch, npair, nb // MOBA_QBLOCKS),
        in_specs=[
            pl.BlockSpec(memory_space=pltpu.SMEM),
            pl.BlockSpec((1, qrows, LANE), lambda b, p, i: (b, i, p)),
            pl.BlockSpec((1, seq, LANE), lambda b, p, i: (b, 0, kcol + p)),
            pl.BlockSpec((1, seq, LANE), lambda b, p, i: (b, 0, 2 * kcol + p)),
            pl.BlockSpec((1, 2 * PAIR, MOBA_BLOCK, MOBA_BLOCK), lambda b, p, i: (p, 0, 0, 0)),
        ],
        out_specs=pl.BlockSpec((1, qrows, LANE), lambda b, p, i: (b, i, p)),
        out_shape=jax.ShapeDtypeStruct((batch, seq, ATT_WIDTH), F32),
        scratch_shapes=[
            pltpu.VMEM((nb, MOBA_BLOCK, 2 * LANE), BF16),
            pltpu.VMEM((nb, PAIR * VT_ROWS, MOBA_BLOCK), BF16),
            pltpu.VMEM((PAIR * nb, LANE), F32),
        ],
        compiler_params=_cparams(("parallel", "parallel", "arbitrary")),
        name="moba",
    )(rel_bias, u_main, u_main, u_main, tab)


def _rwkv_chunk_kernel(*refs):
    next_tiles, next_prev, first_tiles, prep_params = refs[0:4], refs[4:8], refs[8:12], refs[12:22]
    rk_ref, gng_ref, gnb_ref, o_ref, st_scr, prep_scr = refs[22:]
    eq_ref = prep_params[-1]
    c = pl.program_id(0)
    nquad = RWKV_WIDTH // QUAD
    nbatch = o_ref.shape[0]

    def prep_chain(bi, tiles, prevs, first):
        vals = []
        yield from _rwkv_prep_tile([lambda t=t: t[bi] for t in tiles], [lambda p=p: p[bi, 7:8, :] for p in prevs],
                                   first, *prep_params, vals)
        yield
        for q, val in enumerate(vals):
            prep_scr[q, bi] = val

    @pl.when(c == 0)
    def _():
        st_scr[...] = jnp.zeros_like(st_scr)
        for _ in itertools.zip_longest(*[prep_chain(bi, first_tiles, next_prev, True) for bi in range(nbatch)]):
            pass

    eq = eq_ref[...]
    levels = int(math.log2(CHUNK))
    row_h = lax.broadcasted_iota(jnp.int32, (QUAD, QUAD), 0) // HEAD_DIM
    col_h = lax.broadcasted_iota(jnp.int32, (QUAD, QUAD), 1) // HEAD_DIM
    bdmask = row_h == col_h
    eye = (lax.broadcasted_iota(jnp.int32, (QUAD, QUAD), 0)
           == lax.broadcasted_iota(jnp.int32, (QUAD, QUAD), 1))
    t_idx = lax.broadcasted_iota(jnp.int32, (CHUNK, QUAD), 0)
    s_idx = lax.broadcasted_iota(jnp.int32, (CHUNK, QUAD), 1) % CHUNK
    strict = t_idx > s_idx
    incl = t_idx >= s_idx
    both = jnp.concatenate([strict, incl], axis=0)
    reps = QUAD // CHUNK

    def bds(x):
        xb = x.astype(BF16)
        return jnp.where(bdmask, jnp.concatenate([xb] * reps, axis=0), jnp.zeros((), BF16))

    def mm(a, b):
        return _dot(a.astype(BF16), b.astype(BF16))

    def quad_stages(bi, qd):
        sl = slice(qd * QUAD, (qd + 1) * QUAD)
        st_ref = st_scr.at[bi * nquad + qd]
        r, k, v, lw, aa, bb, g = (prep_scr[q, bi, :, sl] for q in range(7))

        cum = lw
        for sh in (1 << b for b in range(levels)):
            cum = cum + jnp.where(t_idx >= sh, pltpu.roll(cum, sh, 0), 0.0)
        e_pos = jnp.exp(cum)
        e_neg = jnp.exp(-cum)
        gamma = e_pos[CHUNK - 1:CHUNK, :]
        e_prev = jnp.where(t_idx == 0, 1.0, pltpu.roll(e_pos, 1, 0))
        a_t = aa * e_prev
        r_t = r * e_pos
        b_c = bb * e_neg
        k_c = k * e_neg
        b_h = b_c * gamma
        k_h = k_c * gamma

        lhs = jnp.concatenate([a_t, r_t], axis=0).astype(BF16)
        pb = _dot_tb(lhs, bds(b_c))
        pk = _dot_tb(lhs, bds(k_c))
        bk_t = jnp.concatenate([b_h, k_h], axis=0).T.astype(BF16)
        yield
        a_rb = jnp.where(incl, pb[CHUNK:], 0.0).astype(BF16)
        akrk = mm(jnp.where(both, pk, 0.0), bds(v))
        n = jnp.where(strict, pb[:CHUNK], 0.0)
        t_inv = jnp.where(t_idx == s_idx, 1.0, 0.0) + n
        n = mm(n, bds(n))
        yield
        for lvl in range(1, levels):
            bd_n = bds(n)
            if lvl < levels - 1:
                tn = mm(jnp.concatenate([t_inv, n], axis=0), bd_n)
                t_inv = t_inv + tn[:CHUNK]
                n = tn[CHUNK:]
            else:
                t_inv = t_inv + mm(t_inv, bd_n)
            yield
        w = mm(t_inv, bds(a_t))
        u = mm(t_inv, bds(akrk[:CHUNK]))
        yield
        q_t = r_t + mm(a_rb, bds(w))
        y0 = mm(a_rb, bds(u)) + akrk[CHUNK:]
        w0 = jnp.concatenate([w, jnp.zeros_like(w)], axis=0)
        g_mat = jnp.where(bdmask, mm(bk_t, w0), 0.0) + jnp.where(eye, gamma, 0.0)
        h_mat = jnp.where(bdmask, mm(bk_t, jnp.concatenate([u, v], axis=0)), 0.0)
        yield
        gy = mm(jnp.concatenate([g_mat, q_t], axis=0), st_ref[...])
        st_ref[...] = gy[:QUAD] + h_mat
        y = gy[QUAD:] + y0
        yield

        inv_n = 1.0 / HEAD_DIM
        stats = mm(jnp.concatenate([y, r * k * rk_ref[:, sl]], axis=0), eq)
        mu = stats[:CHUNK] * inv_n
        bonus = stats[CHUNK:] * v
        yield
        d = y - mu
        var = mm(d * d, eq) * inv_n
        yield
        yn = d * lax.rsqrt(var + GN_EPS) * gng_ref[:, sl] + gnb_ref[:, sl]
        o_ref[bi, :, sl] = (yn + bonus) * g

    chains = [quad_stages(bi, qd) for bi in range(nbatch) for qd in range(nquad)]
    chains += [prep_chain(bi, next_tiles, next_prev, False) for bi in range(nbatch)]
    for _ in itertools.zip_longest(*chains):
        pass


def _rwkv(u_main, u_tail, prep_params, r_k, gn_g, gn_b):
    batch, seq, _ = u_main.shape
    width = RWKV_WIDTH
    nchunks = seq // CHUNK
    rcol = 3 * ATT_WIDTH // width
    nxt = lambda c: jnp.minimum(c + 1, nchunks - 1)
    prev8 = lambda c: nxt(c) * (CHUNK // 8) - 1
    const = lambda a: pl.BlockSpec(a.shape, lambda c: (0,) * a.ndim)

    def tiles(rows, row_index):
        return ([pl.BlockSpec((batch, rows, width), lambda c, g=g: (0, row_index(c), rcol + g)) for g in range(3)]
                + [pl.BlockSpec((batch, rows, TAIL_COLS), lambda c: (0, row_index(c), 0))])

    u_args = (u_main, u_main, u_main, u_tail)
    return pl.pallas_call(
        _rwkv_chunk_kernel,
        grid=(nchunks,),
        in_specs=tiles(CHUNK, nxt) + tiles(8, prev8) + tiles(CHUNK, lambda c: 0)
        + [const(a) for a in prep_params] + [const(r_k), const(gn_g), const(gn_b)],
        out_specs=pl.BlockSpec((batch, CHUNK, width), lambda c: (0, c, 0)),
        out_shape=jax.ShapeDtypeStruct((batch, seq, width), F32),
        scratch_shapes=[pltpu.VMEM((batch * width // QUAD, QUAD, QUAD), F32),
                        pltpu.VMEM((7, batch, CHUNK, width), F32)],
        compiler_params=_cparams(("arbitrary",)),
        name="rwkv_chunk",
    )(*u_args, *u_args, *u_args, *prep_params, r_k, gn_g, gn_b)


def _row_pipelined(n_rows, matmul, finish):
    pending = None
    for r0 in range(0, n_rows, SUB_ROWS):
        rows = slice(r0, r0 + SUB_ROWS)
        acc = matmul(rows)
        if pending is not None:
            finish(*pending)
        pending = (rows, acc)
    finish(*pending)


def _out_ln_kernel(h_ref, att_ref, rw_ref, wa_ref, wr_ref, g_ref, b_ref, o_ref):
    def matmul(rows):
        return (_dot(att_ref[rows, :].astype(BF16), wa_ref[...])
                + _dot(rw_ref[rows, :].astype(BF16), wr_ref[...]))

    def finish(rows, mix):
        o_ref[rows, :] = _layer_norm(ALPHA * h_ref[rows, :] + mix, g_ref[...], b_ref[...])

    _row_pipelined(h_ref.shape[0], matmul, finish)


def _out_ln(h, att, rw, w_out, g, b, tm=512):
    t, d = h.shape
    wa, wr = att.shape[1], rw.shape[1]
    assert wa == wr and w_out.shape[0] == wa + wr
    row = lambda n: pl.BlockSpec((tm, n), lambda i: (i, 0))
    full = lambda shape: pl.BlockSpec(shape, lambda i: (0,) * len(shape))
    return pl.pallas_call(
        _out_ln_kernel,
        grid=(t // tm,),
        in_specs=[row(d), row(wa), row(wr),
                  pl.BlockSpec((wa, d), lambda i: (0, 0)), pl.BlockSpec((wr, d), lambda i: (1, 0)),
                  full((1, d)), full((1, d))],
        out_specs=row(d),
        out_shape=jax.ShapeDtypeStruct((t, d), F32),
        compiler_params=_cparams(("parallel",)),
        name="out_ln",
    )(h, att, rw, w_out, w_out, g, b)


def _ple_ln_kernel(h_ref, p_ref, wg_ref, bg_ref, wp_ref, g_ref, b_ref, o_ref):
    def matmul(rows):
        return (_dot(h_ref[rows, :].astype(BF16), wg_ref[...]), _dot(p_ref[rows, :].astype(BF16), wp_ref[...]))

    def finish(rows, acc):
        pre_gate, e = acc
        y = ALPHA * h_ref[rows, :] + _sigmoid(pre_gate + bg_ref[...]) * e
        o_ref[rows, :] = _layer_norm(y, g_ref[...], b_ref[...])

    _row_pipelined(h_ref.shape[0], matmul, finish)


def _ple_ln(h, p, w_gate, b_gate, w_up, g, b, tm=512):
    t, d = h.shape
    row = lambda n: pl.BlockSpec((tm, n), lambda i: (i, 0))
    full = lambda shape: pl.BlockSpec(shape, lambda i: (0,) * len(shape))
    return pl.pallas_call(
        _ple_ln_kernel,
        grid=(t // tm,),
        in_specs=[row(d), row(p.shape[1]), full(w_gate.shape), full((1, d)), full(w_up.shape),
                  full((1, d)), full((1, d))],
        out_specs=row(d),
        out_shape=jax.ShapeDtypeStruct((t, d), F32),
        compiler_params=_cparams(("parallel",)),
        name="ple_ln",
    )(h, p, w_gate, b_gate, w_up, g, b)


def _pad_rows(w, rows):
    return jnp.pad(w, ((0, rows - w.shape[0]), (0, 0)))


def _pack_tail(cols, axis):
    wd, ad, gd = jnp.split(cols, [DECAY_LORA, DECAY_LORA + AAA_LORA], axis=axis)

    def pad(x, n):
        widths = [(0, 0)] * x.ndim
        widths[axis] = (0, n - x.shape[axis])
        return jnp.pad(x, widths)

    return jnp.concatenate([pad(wd, LANE), pad(ad, LANE), pad(gd, GD_PAD)], axis=axis)


def kernel(x, p, ffn1_w_gate, ffn1_w_up, ffn1_w_down, ln1_g, ln1_b, w_in, rel_bias, shift_mix, decay_w0, decay_w2, a_a0, a_a2, gate_g2, k_k, k_a, r_k, gn_g, gn_b, w_out, ln2_g, ln2_b, ffn2_w_gate, ffn2_w_up, ffn2_w_down, ln3_g, ln3_b, ple_w_up, ple_w_gate, ple_b_gate, ln4_g, ln4_b):
    batch, seq, d = x.shape
    t = batch * seq
    row = lambda a: a.reshape(1, -1)
    main_cols = 3 * ATT_WIDTH + 3 * RWKV_WIDTH

    heads = jnp.arange(QUAD) // HEAD_DIM
    esum = (heads[:, None] == heads[None, :]).astype(BF16)
    tab = _bias_tab(rel_bias)

    h = x.reshape(t, d).astype(F32)
    for i in range(DEPTH):
        w_in_t = w_in[i].T
        later_weights = (ffn2_w_gate[i], ffn2_w_up[i], ffn2_w_down[i], w_out[i], ple_w_gate[i], ple_w_up[i])
        h, (h_bf16,), (wg2, wu2, wd2, wo, wpg, wpu) = _ffn_ln(
            h, ffn1_w_gate[i].astype(BF16), ffn1_w_up[i].astype(BF16), ffn1_w_down[i].astype(BF16),
            row(ln1_g[i]), row(ln1_b[i]), to_cast=later_weights, emit_bf16=True)
        u_main, u_tail = _proj(h_bf16, w_in_t, main_cols, _pack_tail(w_in_t[main_cols:], axis=0))
        mix = shift_mix[i]
        prep_params = (row(mix[:3 * RWKV_WIDTH]), row(_pack_tail(mix[3 * RWKV_WIDTH:], axis=0)),
                       row(decay_w0[i]), _pad_rows(decay_w2[i], LANE).astype(BF16), row(a_a0[i]),
                       _pad_rows(a_a2[i], LANE).astype(BF16), _pad_rows(gate_g2[i], GD_PAD).astype(BF16),
                       row(k_k[i]), row(k_a[i]), esum)
        u_main3 = u_main.reshape(batch, seq, main_cols)
        att = _moba(rel_bias, u_main3, tab)
        rw = _rwkv(u_main3, u_tail.reshape(batch, seq, TAIL_COLS), prep_params,
                   row(r_k[i]), row(gn_g[i]), row(gn_b[i]))
        h = _out_ln(h, att.reshape(t, ATT_WIDTH), rw.reshape(t, RWKV_WIDTH), wo,
                    row(ln2_g[i]), row(ln2_b[i]))
        h, _, _ = _ffn_ln(h, wg2, wu2, wd2, row(ln3_g[i]), row(ln3_b[i]))
        h = _ple_ln(h, p[i].reshape(t, PLE_DIM).astype(F32), wpg, row(ple_b_gate[i]), wpu,
                    row(ln4_g[i]), row(ln4_b[i]))
    return h.reshape(batch, seq, d).astype(x.dtype)
```

```python
import functools
import itertools
import math

import jax
import jax.numpy as jnp
from jax import lax
from jax.experimental import pallas as pl
from jax.experimental.pallas import tpu as pltpu

DEPTH = 1
HEAD_DIM = 64
ATT_HEADS = 16
RWKV_HEADS = 16
ATT_WIDTH = ATT_HEADS * HEAD_DIM
RWKV_WIDTH = RWKV_HEADS * HEAD_DIM
MOBA_BLOCK = 256
MOBA_TOPK = 3
MOBA_UNROLL = 4
MOBA_QBLOCKS = 2
LOG2E = 1.4426950408889634
REL_BUCKETS = 32
REL_MAX_DIST = 128
DECAY_LORA = 64
AAA_LORA = 64
PLE_DIM = 256
LN_EPS = 1e-5
GN_EPS = 64e-5
NEG = -1e30
ALPHA = (2 * DEPTH) ** 0.25

LANE = 128
SUBLANE = 8
SUBLANE_BF16 = 16
PAIR = LANE // HEAD_DIM
CHUNK = 64
QUAD = 256
ONES_ROWS = SUBLANE_BF16
VT_ROWS = HEAD_DIM + ONES_ROWS
CAST_GROUPS = 4
SUB_ROWS = 128
TAIL_COLS = 512
TAIL_WD, TAIL_AD, TAIL_GD = 0, 128, 256
GD_PAD = 256
VMEM_LIMIT = 56 * 1024 * 1024

F32 = jnp.float32
BF16 = jnp.bfloat16
HI = lax.Precision.HIGHEST


def _cparams(sem):
    return pltpu.CompilerParams(dimension_semantics=sem, vmem_limit_bytes=VMEM_LIMIT)


def _layer_norm(y, g, b):
    mu = jnp.mean(y, axis=-1, keepdims=True)
    d = y - mu
    var = jnp.mean(d * d, axis=-1, keepdims=True)
    return d * lax.rsqrt(var + LN_EPS) * g + b


def _sigmoid(x):
    return 0.5 * jnp.tanh(0.5 * x) + 0.5


def _dot(a, b):
    return jnp.dot(a, b, preferred_element_type=F32)


def _dot_tb(a, b, precision=None):
    return lax.dot_general(a, b, (((1,), (1,)), ((), ())), preferred_element_type=F32,
                           precision=precision)


def _split3(x):
    hi = x.astype(BF16)
    r1 = x - hi.astype(F32)
    mid = r1.astype(BF16)
    lo = (r1 - mid.astype(F32)).astype(BF16)
    return hi, mid, lo


def _ffn_ln_kernel(*refs, n_cast, emit_bf16):
    x_ref, wg_ref, wu_ref, wd_ref, g_ref, b_ref = refs[:6]
    cast_in = refs[6:6 + n_cast]
    o_ref = refs[6 + n_cast]
    ob_ref = refs[7 + n_cast] if emit_bf16 else None
    first_cast_out = 7 + n_cast + int(emit_bf16)
    cast_out = refs[first_cast_out:first_cast_out + n_cast]
    (xb_ref,) = refs[first_cast_out + n_cast:]
    f = pl.program_id(1)

    @pl.when(f == 0)
    def _():
        xb_ref[...] = x_ref[...].astype(BF16)
        o_ref[...] = jnp.zeros_like(o_ref)

    xb = xb_ref[...]
    gate = _dot(xb, wg_ref[...])
    up = _dot(xb, wu_ref[...])
    act = gate * jax.nn.sigmoid(gate) * up
    o_ref[...] += _dot(act.astype(BF16), wd_ref[...])

    for src_ref, dst_ref in zip(cast_in, cast_out):
        dst_ref[...] = src_ref[...].astype(BF16)

    @pl.when(f == pl.num_programs(1) - 1)
    def _():
        y = _layer_norm(ALPHA * x_ref[...] + 0.5 * o_ref[...], g_ref[...], b_ref[...])
        o_ref[...] = y
        if emit_bf16:
            ob_ref[...] = y.astype(BF16)


def _cast_spec(shape, n_i, n_f):
    rows, cols = shape
    run = -(-n_f // CAST_GROUPS)
    if rows % (SUBLANE_BF16 * n_i) == 0 and cols % (LANE * CAST_GROUPS) == 0:
        return pl.BlockSpec((rows // n_i, cols // CAST_GROUPS), lambda i, f: (i, f // run))
    if rows % (SUBLANE_BF16 * CAST_GROUPS) == 0 and cols % (LANE * n_i) == 0:
        return pl.BlockSpec((rows // CAST_GROUPS, cols // n_i), lambda i, f: (f // run, i))
    assert rows % (SUBLANE_BF16 * n_i) == 0, shape
    return pl.BlockSpec((rows // n_i, cols), lambda i, f: (i, 0))


def _ffn_ln(x, wg, wu, wd, g, b, to_cast=(), emit_bf16=False, tm=512, tf=512):
    t, d = x.shape
    ff = wg.shape[1]
    n_i, n_f = t // tm, ff // tf
    cast_specs = [_cast_spec(w.shape, n_i, n_f) for w in to_cast]
    outs = pl.pallas_call(
        functools.partial(_ffn_ln_kernel, n_cast=len(to_cast), emit_bf16=emit_bf16),
        grid=(n_i, n_f),
        in_specs=[
            pl.BlockSpec((tm, d), lambda i, f: (i, 0)),
            pl.BlockSpec((d, tf), lambda i, f: (0, f)),
            pl.BlockSpec((d, tf), lambda i, f: (0, f)),
            pl.BlockSpec((tf, d), lambda i, f: (f, 0)),
            pl.BlockSpec((1, d), lambda i, f: (0, 0)),
            pl.BlockSpec((1, d), lambda i, f: (0, 0)),
        ] + cast_specs,
        out_specs=[pl.BlockSpec((tm, d), lambda i, f: (i, 0))] * (1 + int(emit_bf16)) + cast_specs,
        out_shape=[jax.ShapeDtypeStruct((t, d), F32)] + [jax.ShapeDtypeStruct((t, d), BF16)] * int(emit_bf16)
        + [jax.ShapeDtypeStruct(w.shape, BF16) for w in to_cast],
        scratch_shapes=[pltpu.VMEM((tm, d), BF16)],
        compiler_params=_cparams(("parallel", "arbitrary")),
        name="ffn_ln",
    )(x, wg, wu, wd, g, b, *to_cast)
    n_res = 1 + int(emit_bf16)
    return outs[0], list(outs[1:n_res]), list(outs[n_res:])


def _proj_kernel(x_ref, w_ref, wt_ref, o_ref, ot_ref):
    @pl.when(pl.program_id(1) == 0)
    def _():
        ot_ref[...] = _dot_tb(x_ref[...], wt_ref[...].astype(BF16))

    o_ref[...] = _dot_tb(x_ref[...], w_ref[...].astype(BF16))


def _proj(x, w_t, n, w_tail_t, tm=2048, tn=512):
    t, d = x.shape
    n_tail = w_tail_t.shape[0]
    return pl.pallas_call(
        _proj_kernel,
        grid=(t // tm, n // tn),
        in_specs=[
            pl.BlockSpec((tm, d), lambda i, j: (i, 0)),
            pl.BlockSpec((tn, d), lambda i, j: (j, 0)),
            pl.BlockSpec((n_tail, d), lambda i, j: (0, 0), pipeline_mode=pl.Buffered(1)),
        ],
        out_specs=[pl.BlockSpec((tm, tn), lambda i, j: (i, j)), pl.BlockSpec((tm, n_tail), lambda i, j: (i, 0))],
        out_shape=[jax.ShapeDtypeStruct((t, n), F32), jax.ShapeDtypeStruct((t, n_tail), F32)],
        compiler_params=_cparams(("parallel", "arbitrary")),
        name="proj",
    )(x, w_t, w_tail_t)


def _rel_bucket(dist):
    max_exact = REL_BUCKETS // 2
    n = jnp.maximum(dist, 0)
    nf = jnp.maximum(n, 1).astype(F32)
    large = max_exact + (jnp.log(nf / max_exact) / math.log(REL_MAX_DIST / max_exact)
                         * (REL_BUCKETS - max_exact)).astype(jnp.int32)
    large = jnp.minimum(large, REL_BUCKETS - 1)
    return jnp.where(n < max_exact, n, large)


def _bias_tab_kernel(rb_ref, o_ref):
    p = pl.program_id(0)
    blk = MOBA_BLOCK
    diff = lax.broadcasted_iota(jnp.int32, (SUBLANE, 2 * blk), 1) - blk
    for kind in range(2):
        dist = diff + kind * blk
        bucket = _rel_bucket(dist)
        for hh in range(PAIR):
            h = PAIR * p + hh
            vec = jnp.zeros(dist.shape, F32)
            for bkt in range(REL_BUCKETS):
                vec = jnp.where(bucket == bkt, rb_ref[bkt, h], vec)
            vec = vec * LOG2E
            if kind == 0:
                vec = jnp.where(dist >= 0, vec, NEG)
            rows = jnp.broadcast_to(vec[0:1], (blk, 2 * blk))
            o_ref[0, 2 * hh + kind] = pltpu.roll(rows, 0, 1, stride=1, stride_axis=0)[:, blk:]


def _bias_tab(rel_bias):
    npair = ATT_HEADS // PAIR
    return pl.pallas_call(
        _bias_tab_kernel,
        grid=(npair,),
        in_specs=[pl.BlockSpec(memory_space=pltpu.SMEM)],
        out_specs=pl.BlockSpec((1, 2 * PAIR, MOBA_BLOCK, MOBA_BLOCK), lambda p: (p, 0, 0, 0)),
        out_shape=jax.ShapeDtypeStruct((npair, 2 * PAIR, MOBA_BLOCK, MOBA_BLOCK), F32),
        compiler_params=_cparams(("arbitrary",)),
        name="bias_tab",
    )(rel_bias)


def _rwkv_prep_tile(tiles, prev_rows, first, mixm_ref, mixt_ref, w0_ref, w2_ref, a0_ref, a2_ref, g2_ref,
                    kk_ref, ka_ref, esum_ref, out):
    width = RWKV_WIDTH

    def shifted(g, mix):
        x = tiles[g]()
        prev = pltpu.roll(x, 1, 0)
        row = lax.broadcasted_iota(jnp.int32, x.shape, 0)
        last = 0.0 if first is True else jnp.where(first, 0.0, prev_rows[g]())
        prev = jnp.where(row == 0, last, prev)
        return x + (prev - x) * mix

    tail = shifted(3, mixt_ref[...])
    wd = tail[:, TAIL_WD:TAIL_WD + LANE]
    ad = tail[:, TAIL_AD:TAIL_AD + LANE]
    gd = tail[:, TAIL_GD:TAIL_GD + GD_PAD]
    y = w0_ref[...] + _dot(jnp.tanh(wd).astype(BF16), w2_ref[...])
    yield
    lw = -math.exp(-0.5) * _sigmoid(y)
    yield
    a = _sigmoid(a0_ref[...] + _dot(ad.astype(BF16), a2_ref[...]))
    yield
    g = _dot(_sigmoid(gd).astype(BF16), g2_ref[...])
    yield
    k = shifted(1, mixm_ref[:, width:2 * width])
    kk = k * kk_ref[...]
    kk2 = kk * kk
    yield
    ss = []
    for c in range(0, width, QUAD):
        ss.append(_dot(kk2[:, c:c + QUAD].astype(BF16), esum_ref[...]))
        yield
    kk = kk * jnp.minimum(lax.rsqrt(jnp.concatenate(ss, axis=1)), 1e12)
    k = k * (1.0 + (a - 1.0) * ka_ref[...])
    yield
    r = shifted(0, mixm_ref[:, 0:width])
    yield
    v = shifted(2, mixm_ref[:, 2 * width:3 * width])
    out.extend((r, k, v, lw, -kk, kk * a, g))


def _moba_kernel(rb_ref, q_ref, k_ref, v_ref, tab_ref, o_ref, kb_scr, vt_scr, kmean_scr):
    p = pl.program_id(1)
    i0 = pl.program_id(2) * MOBA_QBLOCKS
    nb = kb_scr.shape[0]
    blk = MOBA_BLOCK

    @pl.when(i0 == 0)
    def _():
        lane_k = lax.broadcasted_iota(jnp.int32, (blk, LANE), 1)
        lane_m = lax.broadcasted_iota(jnp.int32, (1, LANE), 1)
        ones = jnp.ones((ONES_ROWS, blk), BF16)
        for n in range(nb):
            kblk = k_ref[0, n * blk:(n + 1) * blk, :]
            kb_scr[n, :, :LANE] = kblk.astype(BF16)
            hot = (lane_k == n) | (lane_k == nb + n) | (lane_k == 2 * nb + n)
            kb_scr[n, :, LANE:] = jnp.where(hot, 1.0, 0.0).astype(BF16)
            kmean = jnp.mean(kblk, axis=0, keepdims=True)
            v_t = v_ref[0, n * blk:(n + 1) * blk, :].T.astype(BF16)
            for hh in range(PAIR):
                in_head = (lane_m >= hh * HEAD_DIM) & (lane_m < (hh + 1) * HEAD_DIM)
                kmean_scr[hh * nb + n:hh * nb + n + 1, :] = jnp.where(in_head, kmean, 0.0)
                vt_scr[n, hh * VT_ROWS:hh * VT_ROWS + HEAD_DIM, :] = v_t[hh * HEAD_DIM:(hh + 1) * HEAD_DIM, :]
                vt_scr[n, hh * VT_ROWS + HEAD_DIM:(hh + 1) * VT_ROWS, :] = ones

    q8 = q_ref[0] * (HEAD_DIM ** -0.5)
    qs = q8 * LOG2E
    nq = q8.shape[0]
    lane = lax.broadcasted_iota(jnp.int32, q8.shape, 1)
    nidx = lax.broadcasted_iota(jnp.int32, (nb, nq), 0)
    own = i0 + lax.broadcasted_iota(jnp.int32, (1, nq), 1) // blk
    gate_all = _dot_tb(kmean_scr[...], q8, precision=HI)

    q_tail, q_far, adj_mask = [], [], []
    for hh in range(PAIR):
        gate = gate_all[hh * nb:(hh + 1) * nb]
        avail = nidx < own
        sel = jnp.zeros((nb, nq), jnp.bool_)
        for _ in range(MOBA_TOPK):
            best = jnp.max(jnp.where(avail, gate, -jnp.inf), axis=0, keepdims=True)
            first = jnp.min(jnp.where(avail & (gate == best), nidx, nb), axis=0, keepdims=True)
            pick = nidx == first
            sel = sel | pick
            avail = avail & jnp.logical_not(pick)
        b_far = rb_ref[REL_BUCKETS - 1, PAIR * p + hh] * LOG2E
        rowb = jnp.where(sel & (nidx < own - 1), b_far, NEG)
        adj_mask.append(jnp.max(jnp.where(sel & (nidx == own - 1), 0.0, NEG), axis=0, keepdims=True))
        hi, mid, lo = _split3(rowb)
        parts = jnp.concatenate([hi.astype(F32), mid.astype(F32), lo.astype(F32),
                                 jnp.zeros((LANE - 3 * nb, nq), F32)], axis=0)
        in_head = (lane >= hh * HEAD_DIM) & (lane < (hh + 1) * HEAD_DIM)
        q_h = jnp.where(in_head, qs, 0.0).astype(BF16)
        q_tail.append(q_h)
        q_far.append(jnp.concatenate([q_h, parts.T.astype(BF16)], axis=1))

    def vt(hh, n):
        return vt_scr[n, hh * VT_ROWS:(hh + 1) * VT_ROWS, :]

    def block_chain(hh, n, scores, parts):
        s = scores()
        yield
        m_s = jnp.max(s, axis=0, keepdims=True)
        yield
        parts.append((m_s, _dot(vt(hh, n), jnp.exp2(s - m_s).astype(BF16))))

    def run_chains(gens):
        for _ in itertools.zip_longest(*gens):
            pass

    def merge(prev, parts):
        m_new = parts[0][0] if prev is None else jnp.maximum(prev[0], parts[0][0])
        for m_s, _ in parts[1:]:
            m_new = jnp.maximum(m_new, m_s)
        acc = None if prev is None else jnp.exp2(prev[0] - m_new) * prev[1]
        for m_s, acc_s in parts:
            term = jnp.exp2(m_s - m_new) * acc_s
            acc = term if acc is None else acc + term
        return m_new, acc

    def tail_blocks(hh, c):
        cols = slice(c * blk, (c + 1) * blk)
        i_c = i0 + c
        n_adj = jnp.maximum(i_c - 1, 0)
        q_c = q_tail[hh][cols]

        def own_blk():
            return _dot_tb(kb_scr[i_c, :, :LANE], q_c) + tab_ref[0, 2 * hh]

        def prev_blk():
            return _dot_tb(kb_scr[n_adj, :, :LANE], q_c) + tab_ref[0, 2 * hh + 1] + adj_mask[hh][:, cols]

        return [(i_c, own_blk), (n_adj, prev_blk)]

    tails = {(hh, c): [] for hh in range(PAIR) for c in range(MOBA_QBLOCKS)}
    run_chains(block_chain(hh, n, scores, tails[hh, c])
               for hh in range(PAIR) for c in range(MOBA_QBLOCKS) for n, scores in tail_blocks(hh, c))
    tails = {key: merge(None, parts) for key, parts in tails.items()}
    state = [tuple(jnp.concatenate([tails[hh, c][s] for c in range(MOBA_QBLOCKS)], axis=1) for s in range(2))
             for hh in range(PAIR)]

    def far_trip(first_block, n_blocks, carry):
        parts = [[] for _ in range(PAIR)]
        chains = []
        for hh in range(PAIR):
            for u in range(n_blocks):
                nc = first_block + u
                chains.append(block_chain(hh, nc, lambda nc=nc, hh=hh: _dot_tb(kb_scr[nc], q_far[hh]), parts[hh]))
        run_chains(chains)
        return tuple(x for hh in range(PAIR) for x in merge(carry[2 * hh:2 * hh + 2], parts[hh]))

    n_far = jnp.maximum(i0 + MOBA_QBLOCKS - 2, 0)
    full = n_far // MOBA_UNROLL
    flat = lax.fori_loop(0, full, lambda j, c: far_trip(j * MOBA_UNROLL, MOBA_UNROLL, c),
                         tuple(x for st in state for x in st))
    flat = lax.cond(n_far > full * MOBA_UNROLL,
                    lambda c: far_trip(full * MOBA_UNROLL, MOBA_UNROLL // 2, c), lambda c: c, flat)
    out_t = jnp.concatenate(
        [flat[2 * hh + 1][:HEAD_DIM] / flat[2 * hh + 1][HEAD_DIM:HEAD_DIM + 1] for hh in range(PAIR)], axis=0)
    o_ref[0] = out_t.T


def _moba(rel_bias, u_main, tab):
    batch, seq, _ = u_main.shape
    npair = ATT_HEADS // PAIR
    nb = seq // MOBA_BLOCK
    assert 3 * nb <= LANE and nb % MOBA_QBLOCKS == 0
    assert MOBA_QBLOCKS == 2 and MOBA_UNROLL == 2 * MOBA_QBLOCKS
    kcol = ATT_WIDTH // LANE
    qrows = MOBA_QBLOCKS * MOBA_BLOCK
    return pl.pallas_call(
        _moba_kernel,
        grid=(batch, npair, nb // MOBA_QBLOCKS),
        in_specs=[
            pl.BlockSpec(memory_space=pltpu.SMEM),
            pl.BlockSpec((1, qrows, LANE), lambda b, p, i: (b, i, p)),
            pl.BlockSpec((1, seq, LANE), lambda b, p, i: (b, 0, kcol + p)),
            pl.BlockSpec((1, seq, LANE), lambda b, p, i: (b, 0, 2 * kcol + p)),
            pl.BlockSpec((1, 2 * PAIR, MOBA_BLOCK, MOBA_BLOCK), lambda b, p, i: (p, 0, 0, 0)),
        ],
        out_specs=pl.BlockSpec((1, qrows, LANE), lambda b, p, i: (b, i, p)),
        out_shape=jax.ShapeDtypeStruct((batch, seq, ATT_WIDTH), F32),
        scratch_shapes=[
            pltpu.VMEM((nb, MOBA_BLOCK, 2 * LANE), BF16),
            pltpu.VMEM((nb, PAIR * VT_ROWS, MOBA_BLOCK), BF16),
            pltpu.VMEM((PAIR * nb, LANE), F32),
        ],
        compiler_params=_cparams(("parallel", "parallel", "arbitrary")),
        name="moba",
    )(rel_bias, u_main, u_main, u_main, tab)


def _rwkv_chunk_kernel(*refs):
    next_tiles, next_prev, first_tiles, prep_params = refs[0:4], refs[4:8], refs[8:12], refs[12:22]
    rk_ref, gng_ref, gnb_ref, o_ref, st_scr, prep_scr = refs[22:]
    eq_ref = prep_params[-1]
    c = pl.program_id(0)
    nquad = RWKV_WIDTH // QUAD
    nbatch = o_ref.shape[0]

    def prep_chain(bi, tiles, prevs, first):
        vals = []
        yield from _rwkv_prep_tile([lambda t=t: t[bi] for t in tiles], [lambda p=p: p[bi, SUBLANE - 1:SUBLANE, :] for p in prevs],
                                   first, *prep_params, vals)
        yield
        for q, val in enumerate(vals):
            prep_scr[q, bi] = val

    @pl.when(c == 0)
    def _():
        st_scr[...] = jnp.zeros_like(st_scr)
        for _ in itertools.zip_longest(*[prep_chain(bi, first_tiles, next_prev, True) for bi in range(nbatch)]):
            pass

    eq = eq_ref[...]
    levels = int(math.log2(CHUNK))
    row_h = lax.broadcasted_iota(jnp.int32, (QUAD, QUAD), 0) // HEAD_DIM
    col_h = lax.broadcasted_iota(jnp.int32, (QUAD, QUAD), 1) // HEAD_DIM
    bdmask = row_h == col_h
    eye = (lax.broadcasted_iota(jnp.int32, (QUAD, QUAD), 0)
           == lax.broadcasted_iota(jnp.int32, (QUAD, QUAD), 1))
    t_idx = lax.broadcasted_iota(jnp.int32, (CHUNK, QUAD), 0)
    s_idx = lax.broadcasted_iota(jnp.int32, (CHUNK, QUAD), 1) % CHUNK
    strict = t_idx > s_idx
    incl = t_idx >= s_idx
    both = jnp.concatenate([strict, incl], axis=0)
    reps = QUAD // CHUNK

    def bds(x):
        xb = x.astype(BF16)
        return jnp.where(bdmask, jnp.concatenate([xb] * reps, axis=0), jnp.zeros((), BF16))

    def mm(a, b):
        return _dot(a.astype(BF16), b.astype(BF16))

    def quad_stages(bi, qd):
        sl = slice(qd * QUAD, (qd + 1) * QUAD)
        st_ref = st_scr.at[bi * nquad + qd]
        r, k, v, lw, aa, bb, g = (prep_scr[q, bi, :, sl] for q in range(7))

        cum = lw
        for sh in (1 << b for b in range(levels)):
            cum = cum + jnp.where(t_idx >= sh, pltpu.roll(cum, sh, 0), 0.0)
        e_pos = jnp.exp(cum)
        e_neg = jnp.exp(-cum)
        gamma = e_pos[CHUNK - 1:CHUNK, :]
        e_prev = jnp.where(t_idx == 0, 1.0, pltpu.roll(e_pos, 1, 0))
        a_t = aa * e_prev
        r_t = r * e_pos
        b_c = bb * e_neg
        k_c = k * e_neg
        b_h = b_c * gamma
        k_h = k_c * gamma

        lhs = jnp.concatenate([a_t, r_t], axis=0).astype(BF16)
        pb = _dot_tb(lhs, bds(b_c))
        pk = _dot_tb(lhs, bds(k_c))
        bk_t = jnp.concatenate([b_h, k_h], axis=0).T.astype(BF16)
        yield
        a_rb = jnp.where(incl, pb[CHUNK:], 0.0).astype(BF16)
        akrk = mm(jnp.where(both, pk, 0.0), bds(v))
        n = jnp.where(strict, pb[:CHUNK], 0.0)
        t_inv = jnp.where(t_idx == s_idx, 1.0, 0.0) + n
        n = mm(n, bds(n))
        yield
        for lvl in range(1, levels):
            bd_n = bds(n)
            if lvl < levels - 1:
                tn = mm(jnp.concatenate([t_inv, n], axis=0), bd_n)
                t_inv = t_inv + tn[:CHUNK]
                n = tn[CHUNK:]
            else:
                t_inv = t_inv + mm(t_inv, bd_n)
            yield
        w = mm(t_inv, bds(a_t))
        u = mm(t_inv, bds(akrk[:CHUNK]))
        yield
        q_t = r_t + mm(a_rb, bds(w))
        y0 = mm(a_rb, bds(u)) + akrk[CHUNK:]
        w0 = jnp.concatenate([w, jnp.zeros_like(w)], axis=0)
        g_mat = jnp.where(bdmask, mm(bk_t, w0), 0.0) + jnp.where(eye, gamma, 0.0)
        h_mat = jnp.where(bdmask, mm(bk_t, jnp.concatenate([u, v], axis=0)), 0.0)
        yield
        gy = mm(jnp.concatenate([g_mat, q_t], axis=0), st_ref[...])
        st_ref[...] = gy[:QUAD] + h_mat
        y = gy[QUAD:] + y0
        yield

        inv_n = 1.0 / HEAD_DIM
        stats = mm(jnp.concatenate([y, r * k * rk_ref[:, sl]], axis=0), eq)
        mu = stats[:CHUNK] * inv_n
        bonus = stats[CHUNK:] * v
        yield
        d = y - mu
        var = mm(d * d, eq) * inv_n
        yield
        yn = d * lax.rsqrt(var + GN_EPS) * gng_ref[:, sl] + gnb_ref[:, sl]
        o_ref[bi, :, sl] = (yn + bonus) * g

    chains = [quad_stages(bi, qd) for bi in range(nbatch) for qd in range(nquad)]
    chains += [prep_chain(bi, next_tiles, next_prev, False) for bi in range(nbatch)]
    for _ in itertools.zip_longest(*chains):
        pass


def _rwkv(u_main, u_tail, prep_params, r_k, gn_g, gn_b):
    batch, seq, _ = u_main.shape
    width = RWKV_WIDTH
    nchunks = seq // CHUNK
    rcol = 3 * ATT_WIDTH // width
    nxt = lambda c: jnp.minimum(c + 1, nchunks - 1)
    prev_tile = lambda c: nxt(c) * (CHUNK // SUBLANE) - 1
    const = lambda a: pl.BlockSpec(a.shape, lambda c: (0,) * a.ndim)

    def tiles(rows, row_index):
        return ([pl.BlockSpec((batch, rows, width), lambda c, g=g: (0, row_index(c), rcol + g)) for g in range(3)]
                + [pl.BlockSpec((batch, rows, TAIL_COLS), lambda c: (0, row_index(c), 0))])

    u_args = (u_main, u_main, u_main, u_tail)
    return pl.pallas_call(
        _rwkv_chunk_kernel,
        grid=(nchunks,),
        in_specs=tiles(CHUNK, nxt) + tiles(SUBLANE, prev_tile) + tiles(CHUNK, lambda c: 0)
        + [const(a) for a in prep_params] + [const(r_k), const(gn_g), const(gn_b)],
        out_specs=pl.BlockSpec((batch, CHUNK, width), lambda c: (0, c, 0)),
        out_shape=jax.ShapeDtypeStruct((batch, seq, width), F32),
        scratch_shapes=[pltpu.VMEM((batch * width // QUAD, QUAD, QUAD), F32),
                        pltpu.VMEM((7, batch, CHUNK, width), F32)],
        compiler_params=_cparams(("arbitrary",)),
        name="rwkv_chunk",
    )(*u_args, *u_args, *u_args, *prep_params, r_k, gn_g, gn_b)


def _row_pipelined(n_rows, matmul, finish):
    pending = None
    for r0 in range(0, n_rows, SUB_ROWS):
        rows = slice(r0, r0 + SUB_ROWS)
        acc = matmul(rows)
        if pending is not None:
            finish(*pending)
        pending = (rows, acc)
    finish(*pending)


def _out_ln_kernel(h_ref, att_ref, rw_ref, wa_ref, wr_ref, g_ref, b_ref, o_ref):
    def matmul(rows):
        return (_dot(att_ref[rows, :].astype(BF16), wa_ref[...])
                + _dot(rw_ref[rows, :].astype(BF16), wr_ref[...]))

    def finish(rows, mix):
        o_ref[rows, :] = _layer_norm(ALPHA * h_ref[rows, :] + mix, g_ref[...], b_ref[...])

    _row_pipelined(h_ref.shape[0], matmul, finish)


def _out_ln(h, att, rw, w_out, g, b, tm=512):
    t, d = h.shape
    wa, wr = att.shape[1], rw.shape[1]
    assert wa == wr and w_out.shape[0] == wa + wr
    row = lambda n: pl.BlockSpec((tm, n), lambda i: (i, 0))
    full = lambda shape: pl.BlockSpec(shape, lambda i: (0,) * len(shape))
    return pl.pallas_call(
        _out_ln_kernel,
        grid=(t // tm,),
        in_specs=[row(d), row(wa), row(wr),
                  pl.BlockSpec((wa, d), lambda i: (0, 0)), pl.BlockSpec((wr, d), lambda i: (1, 0)),
                  full((1, d)), full((1, d))],
        out_specs=row(d),
        out_shape=jax.ShapeDtypeStruct((t, d), F32),
        compiler_params=_cparams(("parallel",)),
        name="out_ln",
    )(h, att, rw, w_out, w_out, g, b)


def _ple_ln_kernel(h_ref, p_ref, wg_ref, bg_ref, wp_ref, g_ref, b_ref, o_ref):
    def matmul(rows):
        return (_dot(h_ref[rows, :].astype(BF16), wg_ref[...]), _dot(p_ref[rows, :].astype(BF16), wp_ref[...]))

    def finish(rows, acc):
        pre_gate, e = acc
        y = ALPHA * h_ref[rows, :] + _sigmoid(pre_gate + bg_ref[...]) * e
        o_ref[rows, :] = _layer_norm(y, g_ref[...], b_ref[...])

    _row_pipelined(h_ref.shape[0], matmul, finish)


def _ple_ln(h, p, w_gate, b_gate, w_up, g, b, tm=512):
    t, d = h.shape
    row = lambda n: pl.BlockSpec((tm, n), lambda i: (i, 0))
    full = lambda shape: pl.BlockSpec(shape, lambda i: (0,) * len(shape))
    return pl.pallas_call(
        _ple_ln_kernel,
        grid=(t // tm,),
        in_specs=[row(d), row(p.shape[1]), full(w_gate.shape), full((1, d)), full(w_up.shape),
                  full((1, d)), full((1, d))],
        out_specs=row(d),
        out_shape=jax.ShapeDtypeStruct((t, d), F32),
        compiler_params=_cparams(("parallel",)),
        name="ple_ln",
    )(h, p, w_gate, b_gate, w_up, g, b)


def _pad_rows(w, rows):
    return jnp.pad(w, ((0, rows - w.shape[0]), (0, 0)))


def _pack_tail(cols, axis):
    wd, ad, gd = jnp.split(cols, [DECAY_LORA, DECAY_LORA + AAA_LORA], axis=axis)

    def pad(x, n):
        widths = [(0, 0)] * x.ndim
        widths[axis] = (0, n - x.shape[axis])
        return jnp.pad(x, widths)

    return jnp.concatenate([pad(wd, LANE), pad(ad, LANE), pad(gd, GD_PAD)], axis=axis)


def kernel(x, p, ffn1_w_gate, ffn1_w_up, ffn1_w_down, ln1_g, ln1_b, w_in, rel_bias, shift_mix, decay_w0, decay_w2, a_a0, a_a2, gate_g2, k_k, k_a, r_k, gn_g, gn_b, w_out, ln2_g, ln2_b, ffn2_w_gate, ffn2_w_up, ffn2_w_down, ln3_g, ln3_b, ple_w_up, ple_w_gate, ple_b_gate, ln4_g, ln4_b):
    batch, seq, d = x.shape
    t = batch * seq
    row = lambda a: a.reshape(1, -1)
    main_cols = 3 * ATT_WIDTH + 3 * RWKV_WIDTH

    heads = jnp.arange(QUAD) // HEAD_DIM
    esum = (heads[:, None] == heads[None, :]).astype(BF16)
    tab = _bias_tab(rel_bias)

    h = x.reshape(t, d).astype(F32)
    for i in range(DEPTH):
        w_in_t = w_in[i].T
        later_weights = (ffn2_w_gate[i], ffn2_w_up[i], ffn2_w_down[i], w_out[i], ple_w_gate[i], ple_w_up[i])
        h, (h_bf16,), (wg2, wu2, wd2, wo, wpg, wpu) = _ffn_ln(
            h, ffn1_w_gate[i].astype(BF16), ffn1_w_up[i].astype(BF16), ffn1_w_down[i].astype(BF16),
            row(ln1_g[i]), row(ln1_b[i]), to_cast=later_weights, emit_bf16=True)
        u_main, u_tail = _proj(h_bf16, w_in_t, main_cols, _pack_tail(w_in_t[main_cols:], axis=0))
        mix = shift_mix[i]
        prep_params = (row(mix[:3 * RWKV_WIDTH]), row(_pack_tail(mix[3 * RWKV_WIDTH:], axis=0)),
                       row(decay_w0[i]), _pad_rows(decay_w2[i], LANE).astype(BF16), row(a_a0[i]),
                       _pad_rows(a_a2[i], LANE).astype(BF16), _pad_rows(gate_g2[i], GD_PAD).astype(BF16),
                       row(k_k[i]), row(k_a[i]), esum)
        u_main3 = u_main.reshape(batch, seq, main_cols)
        att = _moba(rel_bias, u_main3, tab)
        rw = _rwkv(u_main3, u_tail.reshape(batch, seq, TAIL_COLS), prep_params,
                   row(r_k[i]), row(gn_g[i]), row(gn_b[i]))
        h = _out_ln(h, att.reshape(t, ATT_WIDTH), rw.reshape(t, RWKV_WIDTH), wo,
                    row(ln2_g[i]), row(ln2_b[i]))
        h, _, _ = _ffn_ln(h, wg2, wu2, wd2, row(ln3_g[i]), row(ln3_b[i]))
        h = _ple_ln(h, p[i].reshape(t, PLE_DIM).astype(F32), wpg, row(ple_b_gate[i]), wpu,
                    row(ln4_g[i]), row(ln4_b[i]))
    return h.reshape(batch, seq, d).astype(x.dtype)
```

```python
import functools
import itertools
import math

import jax
import jax.numpy as jnp
from jax import lax
from jax.experimental import pallas as pl
from jax.experimental.pallas import tpu as pltpu

DEPTH = 1
HEAD_DIM = 64
ATT_HEADS = 16
RWKV_HEADS = 16
ATT_WIDTH = ATT_HEADS * HEAD_DIM
RWKV_WIDTH = RWKV_HEADS * HEAD_DIM
MOBA_BLOCK = 256
MOBA_TOPK = 3
MOBA_UNROLL = 4
MOBA_QBLOCKS = 2
LOG2E = 1.4426950408889634
REL_BUCKETS = 32
REL_MAX_DIST = 128
DECAY_LORA = 64
AAA_LORA = 64
PLE_DIM = 256
LN_EPS = 1e-5
GN_EPS = 64e-5
NEG = -1e30
ALPHA = (2 * DEPTH) ** 0.25

LANE = 128
SUBLANE = 8
SUBLANE_BF16 = 16
PAIR = LANE // HEAD_DIM
CHUNK = 64
QUAD = 256
ONES_ROWS = SUBLANE_BF16
VT_ROWS = HEAD_DIM + ONES_ROWS
CAST_GROUPS = 4
SUB_ROWS = 128
TAIL_COLS = 512
TAIL_WD, TAIL_AD, TAIL_GD = 0, 128, 256
GD_PAD = 256
VMEM_LIMIT = 56 * 1024 * 1024

F32 = jnp.float32
BF16 = jnp.bfloat16
HI = lax.Precision.HIGHEST


def _cparams(sem):
    return pltpu.CompilerParams(dimension_semantics=sem, vmem_limit_bytes=VMEM_LIMIT)


def _layer_norm(y, g, b):
    mu = jnp.mean(y, axis=-1, keepdims=True)
    d = y - mu
    var = jnp.mean(d * d, axis=-1, keepdims=True)
    return d * lax.rsqrt(var + LN_EPS) * g + b


def _sigmoid(x):
    return 0.5 * jnp.tanh(0.5 * x) + 0.5


def _dot(a, b):
    return jnp.dot(a, b, preferred_element_type=F32)


def _dot_tb(a, b, precision=None):
    return lax.dot_general(a, b, (((1,), (1,)), ((), ())), preferred_element_type=F32,
                           precision=precision)


def _split3(x):
    hi = x.astype(BF16)
    r1 = x - hi.astype(F32)
    mid = r1.astype(BF16)
    lo = (r1 - mid.astype(F32)).astype(BF16)
    return hi, mid, lo


def _ffn_ln_kernel(*refs, n_cast, emit_bf16):
    x_ref, wg_ref, wu_ref, wd_ref, g_ref, b_ref = refs[:6]
    cast_in = refs[6:6 + n_cast]
    o_ref = refs[6 + n_cast]
    ob_ref = refs[7 + n_cast] if emit_bf16 else None
    first_cast_out = 7 + n_cast + int(emit_bf16)
    cast_out = refs[first_cast_out:first_cast_out + n_cast]
    (xb_ref,) = refs[first_cast_out + n_cast:]
    f = pl.program_id(1)

    @pl.when(f == 0)
    def _():
        xb_ref[...] = x_ref[...].astype(BF16)
        o_ref[...] = jnp.zeros_like(o_ref)

    xb = xb_ref[...]
    gate = _dot(xb, wg_ref[...].reshape(wg_ref.shape[-2:]))
    up = _dot(xb, wu_ref[...].reshape(wu_ref.shape[-2:]))
    act = gate * jax.nn.sigmoid(gate) * up
    o_ref[...] += _dot(act.astype(BF16), wd_ref[...])

    for src_ref, dst_ref in zip(cast_in, cast_out):
        dst_ref[...] = src_ref[...].astype(BF16)

    @pl.when(f == pl.num_programs(1) - 1)
    def _():
        y = _layer_norm(ALPHA * x_ref[...] + 0.5 * o_ref[...], g_ref[...], b_ref[...])
        o_ref[...] = y
        if emit_bf16:
            ob_ref[...] = y.astype(BF16)


def _cast_spec(shape, n_i, n_f):
    rows, cols = shape
    run = -(-n_f // CAST_GROUPS)
    if rows % (SUBLANE_BF16 * n_i) == 0 and cols % (LANE * CAST_GROUPS) == 0:
        return pl.BlockSpec((rows // n_i, cols // CAST_GROUPS), lambda i, f: (i, f // run))
    if rows % (SUBLANE_BF16 * CAST_GROUPS) == 0 and cols % (LANE * n_i) == 0:
        return pl.BlockSpec((rows // CAST_GROUPS, cols // n_i), lambda i, f: (f // run, i))
    assert rows % (SUBLANE_BF16 * n_i) == 0, shape
    return pl.BlockSpec((rows // n_i, cols), lambda i, f: (i, 0))


def _ffn_ln(x, wg, wu, wd, g, b, to_cast=(), emit_bf16=False, tm=512, tf=512):
    t, d = x.shape
    blocked = wg.ndim == 3
    ff = wg.shape[0] * wg.shape[2] if blocked else wg.shape[1]
    n_i, n_f = t // tm, ff // tf
    gu_spec = (pl.BlockSpec((1, d, tf), lambda i, f: (f, 0, 0)) if blocked
               else pl.BlockSpec((d, tf), lambda i, f: (0, f)))
    cast_specs = [_cast_spec(w.shape, n_i, n_f) for w in to_cast]
    outs = pl.pallas_call(
        functools.partial(_ffn_ln_kernel, n_cast=len(to_cast), emit_bf16=emit_bf16),
        grid=(n_i, n_f),
        in_specs=[
            pl.BlockSpec((tm, d), lambda i, f: (i, 0)),
            gu_spec,
            gu_spec,
            pl.BlockSpec((tf, d), lambda i, f: (f, 0)),
            pl.BlockSpec((1, d), lambda i, f: (0, 0)),
            pl.BlockSpec((1, d), lambda i, f: (0, 0)),
        ] + cast_specs,
        out_specs=[pl.BlockSpec((tm, d), lambda i, f: (i, 0))] * (1 + int(emit_bf16)) + cast_specs,
        out_shape=[jax.ShapeDtypeStruct((t, d), F32)] + [jax.ShapeDtypeStruct((t, d), BF16)] * int(emit_bf16)
        + [jax.ShapeDtypeStruct(w.shape, BF16) for w in to_cast],
        scratch_shapes=[pltpu.VMEM((tm, d), BF16)],
        compiler_params=_cparams(("parallel", "arbitrary")),
        name="ffn_ln",
    )(x, wg, wu, wd, g, b, *to_cast)
    n_res = 1 + int(emit_bf16)
    return outs[0], list(outs[1:n_res]), list(outs[n_res:])


def _proj_kernel(x_ref, w_ref, wt_ref, o_ref, ot_ref):
    @pl.when(pl.program_id(1) == 0)
    def _():
        ot_ref[...] = _dot_tb(x_ref[...], wt_ref[...].astype(BF16))

    o_ref[...] = _dot_tb(x_ref[...], w_ref[...].astype(BF16))


def _proj(x, w_t, n, w_tail_t, tm=2048, tn=512):
    t, d = x.shape
    n_tail = w_tail_t.shape[0]
    return pl.pallas_call(
        _proj_kernel,
        grid=(t // tm, n // tn),
        in_specs=[
            pl.BlockSpec((tm, d), lambda i, j: (i, 0)),
            pl.BlockSpec((tn, d), lambda i, j: (j, 0)),
            pl.BlockSpec((n_tail, d), lambda i, j: (0, 0), pipeline_mode=pl.Buffered(1)),
        ],
        out_specs=[pl.BlockSpec((tm, tn), lambda i, j: (i, j)), pl.BlockSpec((tm, n_tail), lambda i, j: (i, 0))],
        out_shape=[jax.ShapeDtypeStruct((t, n), F32), jax.ShapeDtypeStruct((t, n_tail), F32)],
        compiler_params=_cparams(("parallel", "arbitrary")),
        name="proj",
    )(x, w_t, w_tail_t)


def _rel_bucket(dist):
    max_exact = REL_BUCKETS // 2
    n = jnp.maximum(dist, 0)
    nf = jnp.maximum(n, 1).astype(F32)
    large = max_exact + (jnp.log(nf / max_exact) / math.log(REL_MAX_DIST / max_exact)
                         * (REL_BUCKETS - max_exact)).astype(jnp.int32)
    large = jnp.minimum(large, REL_BUCKETS - 1)
    return jnp.where(n < max_exact, n, large)


def _bias_tab_kernel(rb_ref, o_ref):
    p = pl.program_id(0)
    blk = MOBA_BLOCK
    diff = lax.broadcasted_iota(jnp.int32, (SUBLANE, 2 * blk), 1) - blk
    for kind in range(2):
        dist = diff + kind * blk
        bucket = _rel_bucket(dist)
        for hh in range(PAIR):
            h = PAIR * p + hh
            vec = jnp.zeros(dist.shape, F32)
            for bkt in range(REL_BUCKETS):
                vec = jnp.where(bucket == bkt, rb_ref[bkt, h], vec)
            vec = vec * LOG2E
            if kind == 0:
                vec = jnp.where(dist >= 0, vec, NEG)
            rows = jnp.broadcast_to(vec[0:1], (blk, 2 * blk))
            o_ref[0, 2 * hh + kind] = pltpu.roll(rows, 0, 1, stride=1, stride_axis=0)[:, blk:]


def _bias_tab(rel_bias):
    npair = ATT_HEADS // PAIR
    return pl.pallas_call(
        _bias_tab_kernel,
        grid=(npair,),
        in_specs=[pl.BlockSpec(memory_space=pltpu.SMEM)],
        out_specs=pl.BlockSpec((1, 2 * PAIR, MOBA_BLOCK, MOBA_BLOCK), lambda p: (p, 0, 0, 0)),
        out_shape=jax.ShapeDtypeStruct((npair, 2 * PAIR, MOBA_BLOCK, MOBA_BLOCK), F32),
        compiler_params=_cparams(("arbitrary",)),
        name="bias_tab",
    )(rel_bias)


def _rwkv_prep_tile(tiles, prev_rows, first, mixm_ref, mixt_ref, w0_ref, w2_ref, a0_ref, a2_ref, g2_ref,
                    kk_ref, ka_ref, esum_ref, out):
    width = RWKV_WIDTH

    def shifted(g, mix):
        x = tiles[g]()
        prev = pltpu.roll(x, 1, 0)
        row = lax.broadcasted_iota(jnp.int32, x.shape, 0)
        last = 0.0 if first is True else jnp.where(first, 0.0, prev_rows[g]())
        prev = jnp.where(row == 0, last, prev)
        return x + (prev - x) * mix

    tail = shifted(3, mixt_ref[...])
    wd = tail[:, TAIL_WD:TAIL_WD + LANE]
    ad = tail[:, TAIL_AD:TAIL_AD + LANE]
    gd = tail[:, TAIL_GD:TAIL_GD + GD_PAD]
    y = w0_ref[...] + _dot(jnp.tanh(wd).astype(BF16), w2_ref[...])
    yield
    lw = -math.exp(-0.5) * _sigmoid(y)
    yield
    a = _sigmoid(a0_ref[...] + _dot(ad.astype(BF16), a2_ref[...]))
    yield
    g = _dot(_sigmoid(gd).astype(BF16), g2_ref[...])
    yield
    k = shifted(1, mixm_ref[:, width:2 * width])
    kk = k * kk_ref[...]
    kk2 = kk * kk
    yield
    ss = []
    for c in range(0, width, QUAD):
        ss.append(_dot(kk2[:, c:c + QUAD].astype(BF16), esum_ref[...]))
        yield
    kk = kk * jnp.minimum(lax.rsqrt(jnp.concatenate(ss, axis=1)), 1e12)
    k = k * (1.0 + (a - 1.0) * ka_ref[...])
    yield
    r = shifted(0, mixm_ref[:, 0:width])
    yield
    v = shifted(2, mixm_ref[:, 2 * width:3 * width])
    out.extend((r, k, v, lw, -kk, kk * a, g))


def _moba_kernel(rb_ref, q_ref, k_ref, v_ref, tab_ref, o_ref, kb_scr, vt_scr, kmean_scr):
    p = pl.program_id(1)
    i0 = pl.program_id(2) * MOBA_QBLOCKS
    nb = kb_scr.shape[0]
    blk = MOBA_BLOCK

    @pl.when(i0 == 0)
    def _():
        lane_k = lax.broadcasted_iota(jnp.int32, (blk, LANE), 1)
        lane_m = lax.broadcasted_iota(jnp.int32, (1, LANE), 1)
        ones = jnp.ones((ONES_ROWS, blk), BF16)
        for n in range(nb):
            kblk = k_ref[0, n * blk:(n + 1) * blk, :]
            kb_scr[n, :, :LANE] = kblk.astype(BF16)
            hot = (lane_k == n) | (lane_k == nb + n) | (lane_k == 2 * nb + n)
            kb_scr[n, :, LANE:] = jnp.where(hot, 1.0, 0.0).astype(BF16)
            kmean = jnp.mean(kblk, axis=0, keepdims=True)
            v_t = v_ref[0, n * blk:(n + 1) * blk, :].T.astype(BF16)
            for hh in range(PAIR):
                in_head = (lane_m >= hh * HEAD_DIM) & (lane_m < (hh + 1) * HEAD_DIM)
                kmean_scr[hh * nb + n:hh * nb + n + 1, :] = jnp.where(in_head, kmean, 0.0)
                vt_scr[n, hh * VT_ROWS:hh * VT_ROWS + HEAD_DIM, :] = v_t[hh * HEAD_DIM:(hh + 1) * HEAD_DIM, :]
                vt_scr[n, hh * VT_ROWS + HEAD_DIM:(hh + 1) * VT_ROWS, :] = ones

    q8 = q_ref[0] * (HEAD_DIM ** -0.5)
    qs = q8 * LOG2E
    nq = q8.shape[0]
    lane = lax.broadcasted_iota(jnp.int32, q8.shape, 1)
    nidx = lax.broadcasted_iota(jnp.int32, (nb, nq), 0)
    own = i0 + lax.broadcasted_iota(jnp.int32, (1, nq), 1) // blk
    gate_all = _dot_tb(kmean_scr[...], q8, precision=HI)

    q_tail, q_far, adj_mask = [], [], []
    for hh in range(PAIR):
        gate = gate_all[hh * nb:(hh + 1) * nb]
        avail = nidx < own
        sel = jnp.zeros((nb, nq), jnp.bool_)
        for _ in range(MOBA_TOPK):
            best = jnp.max(jnp.where(avail, gate, -jnp.inf), axis=0, keepdims=True)
            first = jnp.min(jnp.where(avail & (gate == best), nidx, nb), axis=0, keepdims=True)
            pick = nidx == first
            sel = sel | pick
            avail = avail & jnp.logical_not(pick)
        b_far = rb_ref[REL_BUCKETS - 1, PAIR * p + hh] * LOG2E
        rowb = jnp.where(sel & (nidx < own - 1), b_far, NEG)
        adj_mask.append(jnp.max(jnp.where(sel & (nidx == own - 1), 0.0, NEG), axis=0, keepdims=True))
        hi, mid, lo = _split3(rowb)
        parts = jnp.concatenate([hi.astype(F32), mid.astype(F32), lo.astype(F32),
                                 jnp.zeros((LANE - 3 * nb, nq), F32)], axis=0)
        in_head = (lane >= hh * HEAD_DIM) & (lane < (hh + 1) * HEAD_DIM)
        q_h = jnp.where(in_head, qs, 0.0).astype(BF16)
        q_tail.append(q_h)
        q_far.append(jnp.concatenate([q_h, parts.T.astype(BF16)], axis=1))

    def vt(hh, n):
        return vt_scr[n, hh * VT_ROWS:(hh + 1) * VT_ROWS, :]

    def block_chain(hh, n, scores, parts):
        s = scores()
        yield
        m_s = jnp.max(s, axis=0, keepdims=True)
        yield
        parts.append((m_s, _dot(vt(hh, n), jnp.exp2(s - m_s).astype(BF16))))

    def run_chains(gens):
        for _ in itertools.zip_longest(*gens):
            pass

    def merge(prev, parts):
        m_new = parts[0][0] if prev is None else jnp.maximum(prev[0], parts[0][0])
        for m_s, _ in parts[1:]:
            m_new = jnp.maximum(m_new, m_s)
        acc = None if prev is None else jnp.exp2(prev[0] - m_new) * prev[1]
        for m_s, acc_s in parts:
            term = jnp.exp2(m_s - m_new) * acc_s
            acc = term if acc is None else acc + term
        return m_new, acc

    def tail_blocks(hh, c):
        cols = slice(c * blk, (c + 1) * blk)
        i_c = i0 + c
        n_adj = jnp.maximum(i_c - 1, 0)
        q_c = q_tail[hh][cols]

        def own_blk():
            return _dot_tb(kb_scr[i_c, :, :LANE], q_c) + tab_ref[0, 2 * hh]

        def prev_blk():
            return _dot_tb(kb_scr[n_adj, :, :LANE], q_c) + tab_ref[0, 2 * hh + 1] + adj_mask[hh][:, cols]

        return [(i_c, own_blk), (n_adj, prev_blk)]

    tails = {(hh, c): [] for hh in range(PAIR) for c in range(MOBA_QBLOCKS)}
    run_chains(block_chain(hh, n, scores, tails[hh, c])
               for hh in range(PAIR) for c in range(MOBA_QBLOCKS) for n, scores in tail_blocks(hh, c))
    tails = {key: merge(None, parts) for key, parts in tails.items()}
    state = [tuple(jnp.concatenate([tails[hh, c][s] for c in range(MOBA_QBLOCKS)], axis=1) for s in range(2))
             for hh in range(PAIR)]

    def far_trip(first_block, n_blocks, carry):
        parts = [[] for _ in range(PAIR)]
        chains = []
        for hh in range(PAIR):
            for u in range(n_blocks):
                nc = first_block + u
                chains.append(block_chain(hh, nc, lambda nc=nc, hh=hh: _dot_tb(kb_scr[nc], q_far[hh]), parts[hh]))
        run_chains(chains)
        return tuple(x for hh in range(PAIR) for x in merge(carry[2 * hh:2 * hh + 2], parts[hh]))

    n_far = jnp.maximum(i0 + MOBA_QBLOCKS - 2, 0)
    full = n_far // MOBA_UNROLL
    flat = lax.fori_loop(0, full, lambda j, c: far_trip(j * MOBA_UNROLL, MOBA_UNROLL, c),
                         tuple(x for st in state for x in st))
    flat = lax.cond(n_far > full * MOBA_UNROLL,
                    lambda c: far_trip(full * MOBA_UNROLL, MOBA_UNROLL // 2, c), lambda c: c, flat)
    out_t = jnp.concatenate(
        [flat[2 * hh + 1][:HEAD_DIM] / flat[2 * hh + 1][HEAD_DIM:HEAD_DIM + 1] for hh in range(PAIR)], axis=0)
    o_ref[0] = out_t.T


def _moba(rel_bias, u_main, tab):
    batch, seq, _ = u_main.shape
    npair = ATT_HEADS // PAIR
    nb = seq // MOBA_BLOCK
    assert 3 * nb <= LANE and nb % MOBA_QBLOCKS == 0
    assert MOBA_QBLOCKS == 2 and MOBA_UNROLL == 2 * MOBA_QBLOCKS
    kcol = ATT_WIDTH // LANE
    qrows = MOBA_QBLOCKS * MOBA_BLOCK
    return pl.pallas_call(
        _moba_kernel,
        grid=(batch, npair, nb // MOBA_QBLOCKS),
        in_specs=[
            pl.BlockSpec(memory_space=pltpu.SMEM),
            pl.BlockSpec((1, qrows, LANE), lambda b, p, i: (b, i, p)),
            pl.BlockSpec((1, seq, LANE), lambda b, p, i: (b, 0, kcol + p)),
            pl.BlockSpec((1, seq, LANE), lambda b, p, i: (b, 0, 2 * kcol + p)),
            pl.BlockSpec((1, 2 * PAIR, MOBA_BLOCK, MOBA_BLOCK), lambda b, p, i: (p, 0, 0, 0)),
        ],
        out_specs=pl.BlockSpec((1, qrows, LANE), lambda b, p, i: (b, i, p)),
        out_shape=jax.ShapeDtypeStruct((batch, seq, ATT_WIDTH), F32),
        scratch_shapes=[
            pltpu.VMEM((nb, MOBA_BLOCK, 2 * LANE), BF16),
            pltpu.VMEM((nb, PAIR * VT_ROWS, MOBA_BLOCK), BF16),
            pltpu.VMEM((PAIR * nb, LANE), F32),
        ],
        compiler_params=_cparams(("parallel", "parallel", "arbitrary")),
        name="moba",
    )(rel_bias, u_main, u_main, u_main, tab)


def _rwkv_chunk_kernel(*refs):
    next_tiles, next_prev, first_tiles, prep_params = refs[0:4], refs[4:8], refs[8:12], refs[12:22]
    rk_ref, gng_ref, gnb_ref, o_ref, st_scr, prep_scr = refs[22:]
    eq_ref = prep_params[-1]
    c = pl.program_id(0)
    nquad = RWKV_WIDTH // QUAD
    nbatch = o_ref.shape[0]

    def prep_chain(bi, tiles, prevs, first):
        vals = []
        yield from _rwkv_prep_tile([lambda t=t: t[bi] for t in tiles], [lambda p=p: p[bi, SUBLANE - 1:SUBLANE, :] for p in prevs],
                                   first, *prep_params, vals)
        yield
        for q, val in enumerate(vals):
            prep_scr[q, bi] = val

    @pl.when(c == 0)
    def _():
        st_scr[...] = jnp.zeros_like(st_scr)
        for _ in itertools.zip_longest(*[prep_chain(bi, first_tiles, next_prev, True) for bi in range(nbatch)]):
            pass

    eq = eq_ref[...]
    levels = int(math.log2(CHUNK))
    row_h = lax.broadcasted_iota(jnp.int32, (QUAD, QUAD), 0) // HEAD_DIM
    col_h = lax.broadcasted_iota(jnp.int32, (QUAD, QUAD), 1) // HEAD_DIM
    bdmask = row_h == col_h
    eye = (lax.broadcasted_iota(jnp.int32, (QUAD, QUAD), 0)
           == lax.broadcasted_iota(jnp.int32, (QUAD, QUAD), 1))
    t_idx = lax.broadcasted_iota(jnp.int32, (CHUNK, QUAD), 0)
    s_idx = lax.broadcasted_iota(jnp.int32, (CHUNK, QUAD), 1) % CHUNK
    strict = t_idx > s_idx
    incl = t_idx >= s_idx
    both = jnp.concatenate([strict, incl], axis=0)
    reps = QUAD // CHUNK

    def bds(x):
        xb = x.astype(BF16)
        return jnp.where(bdmask, jnp.concatenate([xb] * reps, axis=0), jnp.zeros((), BF16))

    def mm(a, b):
        return _dot(a.astype(BF16), b.astype(BF16))

    def quad_stages(bi, qd):
        sl = slice(qd * QUAD, (qd + 1) * QUAD)
        st_ref = st_scr.at[bi * nquad + qd]
        r, k, v, lw, aa, bb, g = (prep_scr[q, bi, :, sl] for q in range(7))

        cum = lw
        for sh in (1 << b for b in range(levels)):
            cum = cum + jnp.where(t_idx >= sh, pltpu.roll(cum, sh, 0), 0.0)
        e_pos = jnp.exp(cum)
        e_neg = jnp.exp(-cum)
        gamma = e_pos[CHUNK - 1:CHUNK, :]
        e_prev = jnp.where(t_idx == 0, 1.0, pltpu.roll(e_pos, 1, 0))
        a_t = aa * e_prev
        r_t = r * e_pos
        b_c = bb * e_neg
        k_c = k * e_neg
        b_h = b_c * gamma
        k_h = k_c * gamma

        lhs = jnp.concatenate([a_t, r_t], axis=0).astype(BF16)
        pb = _dot_tb(lhs, bds(b_c))
        pk = _dot_tb(lhs, bds(k_c))
        bk_t = jnp.concatenate([b_h, k_h], axis=0).T.astype(BF16)
        yield
        a_rb = jnp.where(incl, pb[CHUNK:], 0.0).astype(BF16)
        akrk = mm(jnp.where(both, pk, 0.0), bds(v))
        n = jnp.where(strict, pb[:CHUNK], 0.0)
        t_inv = jnp.where(t_idx == s_idx, 1.0, 0.0) + n
        n = mm(n, bds(n))
        yield
        for lvl in range(1, levels):
            bd_n = bds(n)
            if lvl < levels - 1:
                tn = mm(jnp.concatenate([t_inv, n], axis=0), bd_n)
                t_inv = t_inv + tn[:CHUNK]
                n = tn[CHUNK:]
            else:
                t_inv = t_inv + mm(t_inv, bd_n)
            yield
        w = mm(t_inv, bds(a_t))
        u = mm(t_inv, bds(akrk[:CHUNK]))
        yield
        q_t = r_t + mm(a_rb, bds(w))
        y0 = mm(a_rb, bds(u)) + akrk[CHUNK:]
        w0 = jnp.concatenate([w, jnp.zeros_like(w)], axis=0)
        g_mat = jnp.where(bdmask, mm(bk_t, w0), 0.0) + jnp.where(eye, gamma, 0.0)
        h_mat = jnp.where(bdmask, mm(bk_t, jnp.concatenate([u, v], axis=0)), 0.0)
        yield
        gy = mm(jnp.concatenate([g_mat, q_t], axis=0), st_ref[...])
        st_ref[...] = gy[:QUAD] + h_mat
        y = gy[QUAD:] + y0
        yield

        inv_n = 1.0 / HEAD_DIM
        stats = mm(jnp.concatenate([y, r * k * rk_ref[:, sl]], axis=0), eq)
        mu = stats[:CHUNK] * inv_n
        bonus = stats[CHUNK:] * v
        yield
        d = y - mu
        var = mm(d * d, eq) * inv_n
        yield
        yn = d * lax.rsqrt(var + GN_EPS) * gng_ref[:, sl] + gnb_ref[:, sl]
        o_ref[bi, :, sl] = (yn + bonus) * g

    chains = [quad_stages(bi, qd) for bi in range(nbatch) for qd in range(nquad)]
    chains += [prep_chain(bi, next_tiles, next_prev, False) for bi in range(nbatch)]
    for _ in itertools.zip_longest(*chains):
        pass


def _rwkv(u_main, u_tail, prep_params, r_k, gn_g, gn_b):
    batch, seq, _ = u_main.shape
    width = RWKV_WIDTH
    nchunks = seq // CHUNK
    rcol = 3 * ATT_WIDTH // width
    nxt = lambda c: jnp.minimum(c + 1, nchunks - 1)
    prev_tile = lambda c: nxt(c) * (CHUNK // SUBLANE) - 1
    const = lambda a: pl.BlockSpec(a.shape, lambda c: (0,) * a.ndim)

    def tiles(rows, row_index):
        return ([pl.BlockSpec((batch, rows, width), lambda c, g=g: (0, row_index(c), rcol + g)) for g in range(3)]
                + [pl.BlockSpec((batch, rows, TAIL_COLS), lambda c: (0, row_index(c), 0))])

    u_args = (u_main, u_main, u_main, u_tail)
    return pl.pallas_call(
        _rwkv_chunk_kernel,
        grid=(nchunks,),
        in_specs=tiles(CHUNK, nxt) + tiles(SUBLANE, prev_tile) + tiles(CHUNK, lambda c: 0)
        + [const(a) for a in prep_params] + [const(r_k), const(gn_g), const(gn_b)],
        out_specs=pl.BlockSpec((batch, CHUNK, width), lambda c: (0, c, 0)),
        out_shape=jax.ShapeDtypeStruct((batch, seq, width), F32),
        scratch_shapes=[pltpu.VMEM((batch * width // QUAD, QUAD, QUAD), F32),
                        pltpu.VMEM((7, batch, CHUNK, width), F32)],
        compiler_params=_cparams(("arbitrary",)),
        name="rwkv_chunk",
    )(*u_args, *u_args, *u_args, *prep_params, r_k, gn_g, gn_b)


def _row_pipelined(n_rows, matmul, finish):
    pending = None
    for r0 in range(0, n_rows, SUB_ROWS):
        rows = slice(r0, r0 + SUB_ROWS)
        acc = matmul(rows)
        if pending is not None:
            finish(*pending)
        pending = (rows, acc)
    finish(*pending)


def _out_ln_kernel(h_ref, att_ref, rw_ref, wa_ref, wr_ref, g_ref, b_ref, o_ref):
    def matmul(rows):
        return (_dot(att_ref[rows, :].astype(BF16), wa_ref[...])
                + _dot(rw_ref[rows, :].astype(BF16), wr_ref[...]))

    def finish(rows, mix):
        o_ref[rows, :] = _layer_norm(ALPHA * h_ref[rows, :] + mix, g_ref[...], b_ref[...])

    _row_pipelined(h_ref.shape[0], matmul, finish)


def _out_ln(h, att, rw, w_out, g, b, tm=512):
    t, d = h.shape
    wa, wr = att.shape[1], rw.shape[1]
    assert wa == wr and w_out.shape[0] == wa + wr
    row = lambda n: pl.BlockSpec((tm, n), lambda i: (i, 0))
    full = lambda shape: pl.BlockSpec(shape, lambda i: (0,) * len(shape))
    return pl.pallas_call(
        _out_ln_kernel,
        grid=(t // tm,),
        in_specs=[row(d), row(wa), row(wr),
                  pl.BlockSpec((wa, d), lambda i: (0, 0)), pl.BlockSpec((wr, d), lambda i: (1, 0)),
                  full((1, d)), full((1, d))],
        out_specs=row(d),
        out_shape=jax.ShapeDtypeStruct((t, d), F32),
        compiler_params=_cparams(("parallel",)),
        name="out_ln",
    )(h, att, rw, w_out, w_out, g, b)


def _ple_ln_kernel(h_ref, p_ref, wg_ref, bg_ref, wp_ref, g_ref, b_ref, o_ref):
    def matmul(rows):
        return (_dot(h_ref[rows, :].astype(BF16), wg_ref[...]), _dot(p_ref[rows, :].astype(BF16), wp_ref[...]))

    def finish(rows, acc):
        pre_gate, e = acc
        y = ALPHA * h_ref[rows, :] + _sigmoid(pre_gate + bg_ref[...]) * e
        o_ref[rows, :] = _layer_norm(y, g_ref[...], b_ref[...])

    _row_pipelined(h_ref.shape[0], matmul, finish)


def _ple_ln(h, p, w_gate, b_gate, w_up, g, b, tm=512):
    t, d = h.shape
    row = lambda n: pl.BlockSpec((tm, n), lambda i: (i, 0))
    full = lambda shape: pl.BlockSpec(shape, lambda i: (0,) * len(shape))
    return pl.pallas_call(
        _ple_ln_kernel,
        grid=(t // tm,),
        in_specs=[row(d), row(p.shape[1]), full(w_gate.shape), full((1, d)), full(w_up.shape),
                  full((1, d)), full((1, d))],
        out_specs=row(d),
        out_shape=jax.ShapeDtypeStruct((t, d), F32),
        compiler_params=_cparams(("parallel",)),
        name="ple_ln",
    )(h, p, w_gate, b_gate, w_up, g, b)


def _pad_rows(w, rows):
    return jnp.pad(w, ((0, rows - w.shape[0]), (0, 0)))


def _pack_tail(cols, axis):
    wd, ad, gd = jnp.split(cols, [DECAY_LORA, DECAY_LORA + AAA_LORA], axis=axis)

    def pad(x, n):
        widths = [(0, 0)] * x.ndim
        widths[axis] = (0, n - x.shape[axis])
        return jnp.pad(x, widths)

    return jnp.concatenate([pad(wd, LANE), pad(ad, LANE), pad(gd, GD_PAD)], axis=axis)


def kernel(x, p, ffn1_w_gate, ffn1_w_up, ffn1_w_down, ln1_g, ln1_b, w_in, rel_bias, shift_mix, decay_w0, decay_w2, a_a0, a_a2, gate_g2, k_k, k_a, r_k, gn_g, gn_b, w_out, ln2_g, ln2_b, ffn2_w_gate, ffn2_w_up, ffn2_w_down, ln3_g, ln3_b, ple_w_up, ple_w_gate, ple_b_gate, ln4_g, ln4_b):
    batch, seq, d = x.shape
    t = batch * seq
    row = lambda a: a.reshape(1, -1)
    main_cols = 3 * ATT_WIDTH + 3 * RWKV_WIDTH

    heads = jnp.arange(QUAD) // HEAD_DIM
    esum = (heads[:, None] == heads[None, :]).astype(BF16)
    tab = _bias_tab(rel_bias)

    h = x.reshape(t, d).astype(F32)
    for i in range(DEPTH):
        w_in_t = w_in[i].T
        later_weights = (ffn2_w_gate[i], ffn2_w_up[i], ffn2_w_down[i], w_out[i], ple_w_gate[i], ple_w_up[i])
        h, (h_bf16,), (wg2, wu2, wd2, wo, wpg, wpu) = _ffn_ln(
            h, ffn1_w_gate[i].astype(BF16), ffn1_w_up[i].astype(BF16), ffn1_w_down[i].astype(BF16),
            row(ln1_g[i]), row(ln1_b[i]), to_cast=later_weights, emit_bf16=True)
        u_main, u_tail = _proj(h_bf16, w_in_t, main_cols, _pack_tail(w_in_t[main_cols:], axis=0))
        mix = shift_mix[i]
        prep_params = (row(mix[:3 * RWKV_WIDTH]), row(_pack_tail(mix[3 * RWKV_WIDTH:], axis=0)),
                       row(decay_w0[i]), _pad_rows(decay_w2[i], LANE).astype(BF16), row(a_a0[i]),
                       _pad_rows(a_a2[i], LANE).astype(BF16), _pad_rows(gate_g2[i], GD_PAD).astype(BF16),
                       row(k_k[i]), row(k_a[i]), esum)
        u_main3 = u_main.reshape(batch, seq, main_cols)
        att = _moba(rel_bias, u_main3, tab)
        rw = _rwkv(u_main3, u_tail.reshape(batch, seq, TAIL_COLS), prep_params,
                   row(r_k[i]), row(gn_g[i]), row(gn_b[i]))
        h = _out_ln(h, att.reshape(t, ATT_WIDTH), rw.reshape(t, RWKV_WIDTH), wo,
                    row(ln2_g[i]), row(ln2_b[i]))
        blk3 = lambda w: w.reshape(w.shape[0], -1, 512).transpose(1, 0, 2)
        h, _, _ = _ffn_ln(h, blk3(wg2), blk3(wu2), wd2, row(ln3_g[i]), row(ln3_b[i]))
        h = _ple_ln(h, p[i].reshape(t, PLE_DIM).astype(F32), wpg, row(ple_b_gate[i]), wpu,
                    row(ln4_g[i]), row(ln4_b[i]))
    return h.reshape(batch, seq, d).astype(x.dtype)
```

```python
import functools
import itertools
import math

import jax
import jax.numpy as jnp
from jax import lax
from jax.experimental import pallas as pl
from jax.experimental.pallas import tpu as pltpu

DEPTH = 1
HEAD_DIM = 64
ATT_HEADS = 16
RWKV_HEADS = 16
ATT_WIDTH = ATT_HEADS * HEAD_DIM
RWKV_WIDTH = RWKV_HEADS * HEAD_DIM
MOBA_BLOCK = 256
MOBA_TOPK = 3
MOBA_UNROLL = 4
MOBA_QBLOCKS = 2
MOBA_CAST_RUN = 4
LOG2E = 1.4426950408889634
REL_BUCKETS = 32
REL_MAX_DIST = 128
DECAY_LORA = 64
AAA_LORA = 64
PLE_DIM = 256
LN_EPS = 1e-5
GN_EPS = 64e-5
NEG = -1e30
ALPHA = (2 * DEPTH) ** 0.25

LANE = 128
SUBLANE = 8
SUBLANE_BF16 = 16
PAIR = LANE // HEAD_DIM
CHUNK = 64
QUAD = 256
ONES_ROWS = SUBLANE_BF16
VT_ROWS = HEAD_DIM + ONES_ROWS
CAST_GROUPS = 4
SUB_ROWS = 128
TAIL_COLS = 512
TAIL_WD, TAIL_AD, TAIL_GD = 0, 128, 256
GD_PAD = 256
VMEM_LIMIT = 56 * 1024 * 1024

F32 = jnp.float32
BF16 = jnp.bfloat16
HI = lax.Precision.HIGHEST


def _cparams(sem):
    return pltpu.CompilerParams(dimension_semantics=sem, vmem_limit_bytes=VMEM_LIMIT)


def _layer_norm(y, g, b):
    mu = jnp.mean(y, axis=-1, keepdims=True)
    d = y - mu
    var = jnp.mean(d * d, axis=-1, keepdims=True)
    return d * lax.rsqrt(var + LN_EPS) * g + b


def _sigmoid(x):
    return 0.5 * jnp.tanh(0.5 * x) + 0.5


def _dot(a, b):
    return jnp.dot(a, b, preferred_element_type=F32)


def _dot_tb(a, b, precision=None):
    return lax.dot_general(a, b, (((1,), (1,)), ((), ())), preferred_element_type=F32,
                           precision=precision)


def _split3(x):
    hi = x.astype(BF16)
    r1 = x - hi.astype(F32)
    mid = r1.astype(BF16)
    lo = (r1 - mid.astype(F32)).astype(BF16)
    return hi, mid, lo


def _ffn_ln_kernel(*refs, n_cast, emit_bf16):
    x_ref, wg_ref, wu_ref, wd_ref, g_ref, b_ref = refs[:6]
    cast_in = refs[6:6 + n_cast]
    o_ref = refs[6 + n_cast]
    ob_ref = refs[7 + n_cast] if emit_bf16 else None
    first_cast_out = 7 + n_cast + int(emit_bf16)
    cast_out = refs[first_cast_out:first_cast_out + n_cast]
    (xb_ref,) = refs[first_cast_out + n_cast:]
    f = pl.program_id(1)

    @pl.when(f == 0)
    def _():
        xb_ref[...] = x_ref[...].astype(BF16)
        o_ref[...] = jnp.zeros_like(o_ref)

    xb = xb_ref[...]
    gate = _dot(xb, wg_ref[...])
    up = _dot(xb, wu_ref[...])
    act = gate * jax.nn.sigmoid(gate) * up
    o_ref[...] += _dot(act.astype(BF16), wd_ref[...])

    for src_ref, dst_ref in zip(cast_in, cast_out):
        dst_ref[...] = src_ref[...].astype(BF16)

    @pl.when(f == pl.num_programs(1) - 1)
    def _():
        y = _layer_norm(ALPHA * x_ref[...] + 0.5 * o_ref[...], g_ref[...], b_ref[...])
        o_ref[...] = y
        if emit_bf16:
            ob_ref[...] = y.astype(BF16)


def _cast_spec(shape, n_i, n_f):
    rows, cols = shape
    run = -(-n_f // CAST_GROUPS)
    if rows % (SUBLANE_BF16 * n_i) == 0 and cols % (LANE * CAST_GROUPS) == 0:
        return pl.BlockSpec((rows // n_i, cols // CAST_GROUPS), lambda i, f: (i, f // run))
    if rows % (SUBLANE_BF16 * CAST_GROUPS) == 0 and cols % (LANE * n_i) == 0:
        return pl.BlockSpec((rows // CAST_GROUPS, cols // n_i), lambda i, f: (f // run, i))
    assert rows % (SUBLANE_BF16 * n_i) == 0, shape
    return pl.BlockSpec((rows // n_i, cols), lambda i, f: (i, 0))


def _ffn_ln(x, wg, wu, wd, g, b, to_cast=(), emit_bf16=False, tm=512, tf=512):
    t, d = x.shape
    ff = wg.shape[1]
    n_i, n_f = t // tm, ff // tf
    cast_specs = [_cast_spec(w.shape, n_i, n_f) for w in to_cast]
    outs = pl.pallas_call(
        functools.partial(_ffn_ln_kernel, n_cast=len(to_cast), emit_bf16=emit_bf16),
        grid=(n_i, n_f),
        in_specs=[
            pl.BlockSpec((tm, d), lambda i, f: (i, 0)),
            pl.BlockSpec((d, tf), lambda i, f: (0, f)),
            pl.BlockSpec((d, tf), lambda i, f: (0, f)),
            pl.BlockSpec((tf, d), lambda i, f: (f, 0)),
            pl.BlockSpec((1, d), lambda i, f: (0, 0)),
            pl.BlockSpec((1, d), lambda i, f: (0, 0)),
        ] + cast_specs,
        out_specs=[pl.BlockSpec((tm, d), lambda i, f: (i, 0))] * (1 + int(emit_bf16)) + cast_specs,
        out_shape=[jax.ShapeDtypeStruct((t, d), F32)] + [jax.ShapeDtypeStruct((t, d), BF16)] * int(emit_bf16)
        + [jax.ShapeDtypeStruct(w.shape, BF16) for w in to_cast],
        scratch_shapes=[pltpu.VMEM((tm, d), BF16)],
        compiler_params=_cparams(("parallel", "arbitrary")),
        name="ffn_ln",
    )(x, wg, wu, wd, g, b, *to_cast)
    n_res = 1 + int(emit_bf16)
    return outs[0], list(outs[1:n_res]), list(outs[n_res:])


def _proj_kernel(x_ref, w_ref, wt_ref, o_ref, ot_ref):
    @pl.when(pl.program_id(1) == 0)
    def _():
        ot_ref[...] = _dot_tb(x_ref[...], wt_ref[...].astype(BF16))

    o_ref[...] = _dot_tb(x_ref[...], w_ref[...].astype(BF16))


def _proj(x, w_t, n, w_tail_t, tm=2048, tn=512):
    t, d = x.shape
    n_tail = w_tail_t.shape[0]
    return pl.pallas_call(
        _proj_kernel,
        grid=(t // tm, n // tn),
        in_specs=[
            pl.BlockSpec((tm, d), lambda i, j: (i, 0)),
            pl.BlockSpec((tn, d), lambda i, j: (j, 0)),
            pl.BlockSpec((n_tail, d), lambda i, j: (0, 0), pipeline_mode=pl.Buffered(1)),
        ],
        out_specs=[pl.BlockSpec((tm, tn), lambda i, j: (i, j)), pl.BlockSpec((tm, n_tail), lambda i, j: (i, 0))],
        out_shape=[jax.ShapeDtypeStruct((t, n), F32), jax.ShapeDtypeStruct((t, n_tail), F32)],
        compiler_params=_cparams(("parallel", "arbitrary")),
        name="proj",
    )(x, w_t, w_tail_t)


def _rel_bucket(dist):
    max_exact = REL_BUCKETS // 2
    n = jnp.maximum(dist, 0)
    nf = jnp.maximum(n, 1).astype(F32)
    large = max_exact + (jnp.log(nf / max_exact) / math.log(REL_MAX_DIST / max_exact)
                         * (REL_BUCKETS - max_exact)).astype(jnp.int32)
    large = jnp.minimum(large, REL_BUCKETS - 1)
    return jnp.where(n < max_exact, n, large)


def _bias_tab_kernel(rb_ref, o_ref):
    p = pl.program_id(0)
    blk = MOBA_BLOCK
    diff = lax.broadcasted_iota(jnp.int32, (SUBLANE, 2 * blk), 1) - blk
    for kind in range(2):
        dist = diff + kind * blk
        bucket = _rel_bucket(dist)
        for hh in range(PAIR):
            h = PAIR * p + hh
            vec = jnp.zeros(dist.shape, F32)
            for bkt in range(REL_BUCKETS):
                vec = jnp.where(bucket == bkt, rb_ref[bkt, h], vec)
            vec = vec * LOG2E
            if kind == 0:
                vec = jnp.where(dist >= 0, vec, NEG)
            rows = jnp.broadcast_to(vec[0:1], (blk, 2 * blk))
            o_ref[0, 2 * hh + kind] = pltpu.roll(rows, 0, 1, stride=1, stride_axis=0)[:, blk:]


def _bias_tab(rel_bias):
    npair = ATT_HEADS // PAIR
    return pl.pallas_call(
        _bias_tab_kernel,
        grid=(npair,),
        in_specs=[pl.BlockSpec(memory_space=pltpu.SMEM)],
        out_specs=pl.BlockSpec((1, 2 * PAIR, MOBA_BLOCK, MOBA_BLOCK), lambda p: (p, 0, 0, 0)),
        out_shape=jax.ShapeDtypeStruct((npair, 2 * PAIR, MOBA_BLOCK, MOBA_BLOCK), F32),
        compiler_params=_cparams(("arbitrary",)),
        name="bias_tab",
    )(rel_bias)


def _rwkv_prep_tile(tiles, prev_rows, first, mixm_ref, mixt_ref, w0_ref, w2_ref, a0_ref, a2_ref, g2_ref,
                    kk_ref, ka_ref, esum_ref, out):
    width = RWKV_WIDTH

    def shifted(g, mix):
        x = tiles[g]()
        prev = pltpu.roll(x, 1, 0)
        row = lax.broadcasted_iota(jnp.int32, x.shape, 0)
        last = 0.0 if first is True else jnp.where(first, 0.0, prev_rows[g]())
        prev = jnp.where(row == 0, last, prev)
        return x + (prev - x) * mix

    tail = shifted(3, mixt_ref[...])
    wd = tail[:, TAIL_WD:TAIL_WD + LANE]
    ad = tail[:, TAIL_AD:TAIL_AD + LANE]
    gd = tail[:, TAIL_GD:TAIL_GD + GD_PAD]
    y = w0_ref[...] + _dot(jnp.tanh(wd).astype(BF16), w2_ref[...])
    yield
    lw = -math.exp(-0.5) * _sigmoid(y)
    yield
    a = _sigmoid(a0_ref[...] + _dot(ad.astype(BF16), a2_ref[...]))
    yield
    g = _dot(_sigmoid(gd).astype(BF16), g2_ref[...])
    yield
    k = shifted(1, mixm_ref[:, width:2 * width])
    kk = k * kk_ref[...]
    kk2 = kk * kk
    yield
    ss = []
    for c in range(0, width, QUAD):
        ss.append(_dot(kk2[:, c:c + QUAD].astype(BF16), esum_ref[...]))
        yield
    kk = kk * jnp.minimum(lax.rsqrt(jnp.concatenate(ss, axis=1)), 1e12)
    k = k * (1.0 + (a - 1.0) * ka_ref[...])
    yield
    r = shifted(0, mixm_ref[:, 0:width])
    yield
    v = shifted(2, mixm_ref[:, 2 * width:3 * width])
    out.extend((r, k, v, lw, -kk, kk * a, g))


def _moba_kernel(rb_ref, q_ref, k_ref, v_ref, tab_ref, *refs, n_cast):
    cast_in, o_ref, cast_out = refs[:n_cast], refs[n_cast], refs[n_cast + 1:2 * n_cast + 1]
    kb_scr, vt_scr, kmean_scr = refs[2 * n_cast + 1:]
    p = pl.program_id(1)
    i0 = pl.program_id(2) * MOBA_QBLOCKS
    nb = kb_scr.shape[0]
    blk = MOBA_BLOCK

    @pl.when(pl.program_id(2) % MOBA_CAST_RUN == 0)
    def _():
        for src_ref, dst_ref in zip(cast_in, cast_out):
            dst_ref[...] = src_ref[...].astype(BF16)

    @pl.when(i0 == 0)
    def _():
        lane_k = lax.broadcasted_iota(jnp.int32, (blk, LANE), 1)
        lane_m = lax.broadcasted_iota(jnp.int32, (1, LANE), 1)
        ones = jnp.ones((ONES_ROWS, blk), BF16)
        for n in range(nb):
            kblk = k_ref[0, n * blk:(n + 1) * blk, :]
            kb_scr[n, :, :LANE] = kblk.astype(BF16)
            hot = (lane_k == n) | (lane_k == nb + n) | (lane_k == 2 * nb + n)
            kb_scr[n, :, LANE:] = jnp.where(hot, 1.0, 0.0).astype(BF16)
            kmean = jnp.mean(kblk, axis=0, keepdims=True)
            v_t = v_ref[0, n * blk:(n + 1) * blk, :].T.astype(BF16)
            for hh in range(PAIR):
                in_head = (lane_m >= hh * HEAD_DIM) & (lane_m < (hh + 1) * HEAD_DIM)
                kmean_scr[hh * nb + n:hh * nb + n + 1, :] = jnp.where(in_head, kmean, 0.0)
                vt_scr[n, hh * VT_ROWS:hh * VT_ROWS + HEAD_DIM, :] = v_t[hh * HEAD_DIM:(hh + 1) * HEAD_DIM, :]
                vt_scr[n, hh * VT_ROWS + HEAD_DIM:(hh + 1) * VT_ROWS, :] = ones

    q8 = q_ref[0] * (HEAD_DIM ** -0.5)
    qs = q8 * LOG2E
    nq = q8.shape[0]
    lane = lax.broadcasted_iota(jnp.int32, q8.shape, 1)
    nidx = lax.broadcasted_iota(jnp.int32, (nb, nq), 0)
    own = i0 + lax.broadcasted_iota(jnp.int32, (1, nq), 1) // blk
    gate_all = _dot_tb(kmean_scr[...], q8, precision=HI)

    q_tail, q_far, adj_mask = [], [], []
    for hh in range(PAIR):
        gate = gate_all[hh * nb:(hh + 1) * nb]
        avail = nidx < own
        sel = jnp.zeros((nb, nq), jnp.bool_)
        for _ in range(MOBA_TOPK):
            best = jnp.max(jnp.where(avail, gate, -jnp.inf), axis=0, keepdims=True)
            first = jnp.min(jnp.where(avail & (gate == best), nidx, nb), axis=0, keepdims=True)
            pick = nidx == first
            sel = sel | pick
            avail = avail & jnp.logical_not(pick)
        b_far = rb_ref[REL_BUCKETS - 1, PAIR * p + hh] * LOG2E
        rowb = jnp.where(sel & (nidx < own - 1), b_far, NEG)
        adj_mask.append(jnp.max(jnp.where(sel & (nidx == own - 1), 0.0, NEG), axis=0, keepdims=True))
        hi, mid, lo = _split3(rowb)
        parts = jnp.concatenate([hi.astype(F32), mid.astype(F32), lo.astype(F32),
                                 jnp.zeros((LANE - 3 * nb, nq), F32)], axis=0)
        in_head = (lane >= hh * HEAD_DIM) & (lane < (hh + 1) * HEAD_DIM)
        q_h = jnp.where(in_head, qs, 0.0).astype(BF16)
        q_tail.append(q_h)
        q_far.append(jnp.concatenate([q_h, parts.T.astype(BF16)], axis=1))

    def vt(hh, n):
        return vt_scr[n, hh * VT_ROWS:(hh + 1) * VT_ROWS, :]

    def block_chain(hh, n, scores, parts):
        s = scores()
        yield
        m_s = jnp.max(s, axis=0, keepdims=True)
        yield
        parts.append((m_s, _dot(vt(hh, n), jnp.exp2(s - m_s).astype(BF16))))

    def run_chains(gens):
        for _ in itertools.zip_longest(*gens):
            pass

    def merge(prev, parts):
        m_new = parts[0][0] if prev is None else jnp.maximum(prev[0], parts[0][0])
        for m_s, _ in parts[1:]:
            m_new = jnp.maximum(m_new, m_s)
        acc = None if prev is None else jnp.exp2(prev[0] - m_new) * prev[1]
        for m_s, acc_s in parts:
            term = jnp.exp2(m_s - m_new) * acc_s
            acc = term if acc is None else acc + term
        return m_new, acc

    def tail_blocks(hh, c):
        cols = slice(c * blk, (c + 1) * blk)
        i_c = i0 + c
        n_adj = jnp.maximum(i_c - 1, 0)
        q_c = q_tail[hh][cols]

        def own_blk():
            return _dot_tb(kb_scr[i_c, :, :LANE], q_c) + tab_ref[0, 2 * hh]

        def prev_blk():
            return _dot_tb(kb_scr[n_adj, :, :LANE], q_c) + tab_ref[0, 2 * hh + 1] + adj_mask[hh][:, cols]

        return [(i_c, own_blk), (n_adj, prev_blk)]

    tails = {(hh, c): [] for hh in range(PAIR) for c in range(MOBA_QBLOCKS)}
    run_chains(block_chain(hh, n, scores, tails[hh, c])
               for hh in range(PAIR) for c in range(MOBA_QBLOCKS) for n, scores in tail_blocks(hh, c))
    tails = {key: merge(None, parts) for key, parts in tails.items()}
    state = [tuple(jnp.concatenate([tails[hh, c][s] for c in range(MOBA_QBLOCKS)], axis=1) for s in range(2))
             for hh in range(PAIR)]

    def far_trip(first_block, n_blocks, carry):
        parts = [[] for _ in range(PAIR)]
        chains = []
        for hh in range(PAIR):
            for u in range(n_blocks):
                nc = first_block + u
                chains.append(block_chain(hh, nc, lambda nc=nc, hh=hh: _dot_tb(kb_scr[nc], q_far[hh]), parts[hh]))
        run_chains(chains)
        return tuple(x for hh in range(PAIR) for x in merge(carry[2 * hh:2 * hh + 2], parts[hh]))

    n_far = jnp.maximum(i0 + MOBA_QBLOCKS - 2, 0)
    full = n_far // MOBA_UNROLL
    flat = lax.fori_loop(0, full, lambda j, c: far_trip(j * MOBA_UNROLL, MOBA_UNROLL, c),
                         tuple(x for st in state for x in st))
    flat = lax.cond(n_far > full * MOBA_UNROLL,
                    lambda c: far_trip(full * MOBA_UNROLL, MOBA_UNROLL // 2, c), lambda c: c, flat)
    out_t = jnp.concatenate(
        [flat[2 * hh + 1][:HEAD_DIM] / flat[2 * hh + 1][HEAD_DIM:HEAD_DIM + 1] for hh in range(PAIR)], axis=0)
    o_ref[0] = out_t.T


def _moba(rel_bias, u_main, tab, to_cast=()):
    batch, seq, _ = u_main.shape
    npair = ATT_HEADS // PAIR
    nb = seq // MOBA_BLOCK
    assert 3 * nb <= LANE and nb % MOBA_QBLOCKS == 0
    assert MOBA_QBLOCKS == 2 and MOBA_UNROLL == 2 * MOBA_QBLOCKS
    kcol = ATT_WIDTH // LANE
    qrows = MOBA_QBLOCKS * MOBA_BLOCK
    nsteps = nb // MOBA_QBLOCKS
    assert nsteps % MOBA_CAST_RUN == 0
    total_steps = batch * npair * nsteps

    def cast_spec(w):
        rows, cols = w.shape
        n_blocks = min(total_steps // MOBA_CAST_RUN, rows // SUBLANE_BF16)
        assert rows % (n_blocks * SUBLANE_BF16) == 0 and total_steps % n_blocks == 0, w.shape
        run = total_steps // n_blocks
        return pl.BlockSpec((rows // n_blocks, cols), lambda b, p, i: (((b * npair + p) * nsteps + i) // run, 0))

    cast_specs = [cast_spec(w) for w in to_cast]
    outs = pl.pallas_call(
        functools.partial(_moba_kernel, n_cast=len(to_cast)),
        grid=(batch, npair, nsteps),
        in_specs=[
            pl.BlockSpec(memory_space=pltpu.SMEM),
            pl.BlockSpec((1, qrows, LANE), lambda b, p, i: (b, i, p)),
            pl.BlockSpec((1, seq, LANE), lambda b, p, i: (b, 0, kcol + p)),
            pl.BlockSpec((1, seq, LANE), lambda b, p, i: (b, 0, 2 * kcol + p)),
            pl.BlockSpec((1, 2 * PAIR, MOBA_BLOCK, MOBA_BLOCK), lambda b, p, i: (p, 0, 0, 0)),
        ] + cast_specs,
        out_specs=[pl.BlockSpec((1, qrows, LANE), lambda b, p, i: (b, i, p))] + cast_specs,
        out_shape=[jax.ShapeDtypeStruct((batch, seq, ATT_WIDTH), F32)]
        + [jax.ShapeDtypeStruct(w.shape, BF16) for w in to_cast],
        scratch_shapes=[
            pltpu.VMEM((nb, MOBA_BLOCK, 2 * LANE), BF16),
            pltpu.VMEM((nb, PAIR * VT_ROWS, MOBA_BLOCK), BF16),
            pltpu.VMEM((PAIR * nb, LANE), F32),
        ],
        compiler_params=_cparams(("arbitrary", "arbitrary", "arbitrary")),
        name="moba",
    )(rel_bias, u_main, u_main, u_main, tab, *to_cast)
    return outs[0], list(outs[1:])


def _rwkv_chunk_kernel(*refs):
    next_tiles, next_prev, first_tiles, prep_params = refs[0:4], refs[4:8], refs[8:12], refs[12:22]
    rk_ref, gng_ref, gnb_ref, o_ref, st_scr, prep_scr = refs[22:]
    eq_ref = prep_params[-1]
    c = pl.program_id(0)
    nquad = RWKV_WIDTH // QUAD
    nbatch = o_ref.shape[0]

    def prep_chain(bi, tiles, prevs, first):
        vals = []
        yield from _rwkv_prep_tile([lambda t=t: t[bi] for t in tiles], [lambda p=p: p[bi, SUBLANE - 1:SUBLANE, :] for p in prevs],
                                   first, *prep_params, vals)
        yield
        for q, val in enumerate(vals):
            prep_scr[q, bi] = val

    @pl.when(c == 0)
    def _():
        st_scr[...] = jnp.zeros_like(st_scr)
        for _ in itertools.zip_longest(*[prep_chain(bi, first_tiles, next_prev, True) for bi in range(nbatch)]):
            pass

    eq = eq_ref[...]
    levels = int(math.log2(CHUNK))
    row_h = lax.broadcasted_iota(jnp.int32, (QUAD, QUAD), 0) // HEAD_DIM
    col_h = lax.broadcasted_iota(jnp.int32, (QUAD, QUAD), 1) // HEAD_DIM
    bdmask = row_h == col_h
    eye = (lax.broadcasted_iota(jnp.int32, (QUAD, QUAD), 0)
           == lax.broadcasted_iota(jnp.int32, (QUAD, QUAD), 1))
    t_idx = lax.broadcasted_iota(jnp.int32, (CHUNK, QUAD), 0)
    s_idx = lax.broadcasted_iota(jnp.int32, (CHUNK, QUAD), 1) % CHUNK
    strict = t_idx > s_idx
    incl = t_idx >= s_idx
    both = jnp.concatenate([strict, incl], axis=0)
    reps = QUAD // CHUNK

    def bds(x):
        xb = x.astype(BF16)
        return jnp.where(bdmask, jnp.concatenate([xb] * reps, axis=0), jnp.zeros((), BF16))

    def mm(a, b):
        return _dot(a.astype(BF16), b.astype(BF16))

    def quad_stages(bi, qd):
        sl = slice(qd * QUAD, (qd + 1) * QUAD)
        st_ref = st_scr.at[bi * nquad + qd]
        r, k, v, lw, aa, bb, g = (prep_scr[q, bi, :, sl] for q in range(7))

        cum = lw
        for sh in (1 << b for b in range(levels)):
            cum = cum + jnp.where(t_idx >= sh, pltpu.roll(cum, sh, 0), 0.0)
        e_pos = jnp.exp(cum)
        e_neg = jnp.exp(-cum)
        gamma = e_pos[CHUNK - 1:CHUNK, :]
        e_prev = jnp.where(t_idx == 0, 1.0, pltpu.roll(e_pos, 1, 0))
        a_t = aa * e_prev
        r_t = r * e_pos
        b_c = bb * e_neg
        k_c = k * e_neg
        b_h = b_c * gamma
        k_h = k_c * gamma

        lhs = jnp.concatenate([a_t, r_t], axis=0).astype(BF16)
        pb = _dot_tb(lhs, bds(b_c))
        pk = _dot_tb(lhs, bds(k_c))
        bk_t = jnp.concatenate([b_h, k_h], axis=0).T.astype(BF16)
        yield
        a_rb = jnp.where(incl, pb[CHUNK:], 0.0).astype(BF16)
        akrk = mm(jnp.where(both, pk, 0.0), bds(v))
        n = jnp.where(strict, pb[:CHUNK], 0.0)
        t_inv = jnp.where(t_idx == s_idx, 1.0, 0.0) + n
        n = mm(n, bds(n))
        yield
        for lvl in range(1, levels):
            bd_n = bds(n)
            if lvl < levels - 1:
                tn = mm(jnp.concatenate([t_inv, n], axis=0), bd_n)
                t_inv = t_inv + tn[:CHUNK]
                n = tn[CHUNK:]
            else:
                t_inv = t_inv + mm(t_inv, bd_n)
            yield
        w = mm(t_inv, bds(a_t))
        u = mm(t_inv, bds(akrk[:CHUNK]))
        yield
        q_t = r_t + mm(a_rb, bds(w))
        y0 = mm(a_rb, bds(u)) + akrk[CHUNK:]
        w0 = jnp.concatenate([w, jnp.zeros_like(w)], axis=0)
        g_mat = jnp.where(bdmask, mm(bk_t, w0), 0.0) + jnp.where(eye, gamma, 0.0)
        h_mat = jnp.where(bdmask, mm(bk_t, jnp.concatenate([u, v], axis=0)), 0.0)
        yield
        gy = mm(jnp.concatenate([g_mat, q_t], axis=0), st_ref[...])
        st_ref[...] = gy[:QUAD] + h_mat
        y = gy[QUAD:] + y0
        yield

        inv_n = 1.0 / HEAD_DIM
        stats = mm(jnp.concatenate([y, r * k * rk_ref[:, sl]], axis=0), eq)
        mu = stats[:CHUNK] * inv_n
        bonus = stats[CHUNK:] * v
        yield
        d = y - mu
        var = mm(d * d, eq) * inv_n
        yield
        yn = d * lax.rsqrt(var + GN_EPS) * gng_ref[:, sl] + gnb_ref[:, sl]
        o_ref[bi, :, sl] = (yn + bonus) * g

    chains = [quad_stages(bi, qd) for bi in range(nbatch) for qd in range(nquad)]
    chains += [prep_chain(bi, next_tiles, next_prev, False) for bi in range(nbatch)]
    for _ in itertools.zip_longest(*chains):
        pass


def _rwkv(u_main, u_tail, prep_params, r_k, gn_g, gn_b):
    batch, seq, _ = u_main.shape
    width = RWKV_WIDTH
    nchunks = seq // CHUNK
    rcol = 3 * ATT_WIDTH // width
    nxt = lambda c: jnp.minimum(c + 1, nchunks - 1)
    prev_tile = lambda c: nxt(c) * (CHUNK // SUBLANE) - 1
    const = lambda a: pl.BlockSpec(a.shape, lambda c: (0,) * a.ndim)

    def tiles(rows, row_index):
        return ([pl.BlockSpec((batch, rows, width), lambda c, g=g: (0, row_index(c), rcol + g)) for g in range(3)]
                + [pl.BlockSpec((batch, rows, TAIL_COLS), lambda c: (0, row_index(c), 0))])

    u_args = (u_main, u_main, u_main, u_tail)
    return pl.pallas_call(
        _rwkv_chunk_kernel,
        grid=(nchunks,),
        in_specs=tiles(CHUNK, nxt) + tiles(SUBLANE, prev_tile) + tiles(CHUNK, lambda c: 0)
        + [const(a) for a in prep_params] + [const(r_k), const(gn_g), const(gn_b)],
        out_specs=pl.BlockSpec((batch, CHUNK, width), lambda c: (0, c, 0)),
        out_shape=jax.ShapeDtypeStruct((batch, seq, width), F32),
        scratch_shapes=[pltpu.VMEM((batch * width // QUAD, QUAD, QUAD), F32),
                        pltpu.VMEM((7, batch, CHUNK, width), F32)],
        compiler_params=_cparams(("arbitrary",)),
        name="rwkv_chunk",
    )(*u_args, *u_args, *u_args, *prep_params, r_k, gn_g, gn_b)


def _row_pipelined(n_rows, matmul, finish):
    pending = None
    for r0 in range(0, n_rows, SUB_ROWS):
        rows = slice(r0, r0 + SUB_ROWS)
        acc = matmul(rows)
        if pending is not None:
            finish(*pending)
        pending = (rows, acc)
    finish(*pending)


def _out_ln_kernel(h_ref, att_ref, rw_ref, wa_ref, wr_ref, g_ref, b_ref, o_ref):
    def matmul(rows):
        return (_dot(att_ref[rows, :].astype(BF16), wa_ref[...])
                + _dot(rw_ref[rows, :].astype(BF16), wr_ref[...]))

    def finish(rows, mix):
        o_ref[rows, :] = _layer_norm(ALPHA * h_ref[rows, :] + mix, g_ref[...], b_ref[...])

    _row_pipelined(h_ref.shape[0], matmul, finish)


def _out_ln(h, att, rw, w_out, g, b, tm=512):
    t, d = h.shape
    wa, wr = att.shape[1], rw.shape[1]
    assert wa == wr and w_out.shape[0] == wa + wr
    row = lambda n: pl.BlockSpec((tm, n), lambda i: (i, 0))
    full = lambda shape: pl.BlockSpec(shape, lambda i: (0,) * len(shape))
    return pl.pallas_call(
        _out_ln_kernel,
        grid=(t // tm,),
        in_specs=[row(d), row(wa), row(wr),
                  pl.BlockSpec((wa, d), lambda i: (0, 0)), pl.BlockSpec((wr, d), lambda i: (1, 0)),
                  full((1, d)), full((1, d))],
        out_specs=row(d),
        out_shape=jax.ShapeDtypeStruct((t, d), F32),
        compiler_params=_cparams(("parallel",)),
        name="out_ln",
    )(h, att, rw, w_out, w_out, g, b)


def _ple_ln_kernel(h_ref, p_ref, wg_ref, bg_ref, wp_ref, g_ref, b_ref, o_ref):
    def matmul(rows):
        return (_dot(h_ref[rows, :].astype(BF16), wg_ref[...]), _dot(p_ref[rows, :].astype(BF16), wp_ref[...]))

    def finish(rows, acc):
        pre_gate, e = acc
        y = ALPHA * h_ref[rows, :] + _sigmoid(pre_gate + bg_ref[...]) * e
        o_ref[rows, :] = _layer_norm(y, g_ref[...], b_ref[...])

    _row_pipelined(h_ref.shape[0], matmul, finish)


def _ple_ln(h, p, w_gate, b_gate, w_up, g, b, tm=512):
    t, d = h.shape
    row = lambda n: pl.BlockSpec((tm, n), lambda i: (i, 0))
    full = lambda shape: pl.BlockSpec(shape, lambda i: (0,) * len(shape))
    return pl.pallas_call(
        _ple_ln_kernel,
        grid=(t // tm,),
        in_specs=[row(d), row(p.shape[1]), full(w_gate.shape), full((1, d)), full(w_up.shape),
                  full((1, d)), full((1, d))],
        out_specs=row(d),
        out_shape=jax.ShapeDtypeStruct((t, d), F32),
        compiler_params=_cparams(("parallel",)),
        name="ple_ln",
    )(h, p, w_gate, b_gate, w_up, g, b)


def _pad_rows(w, rows):
    return jnp.pad(w, ((0, rows - w.shape[0]), (0, 0)))


def _pack_tail(cols, axis):
    wd, ad, gd = jnp.split(cols, [DECAY_LORA, DECAY_LORA + AAA_LORA], axis=axis)

    def pad(x, n):
        widths = [(0, 0)] * x.ndim
        widths[axis] = (0, n - x.shape[axis])
        return jnp.pad(x, widths)

    return jnp.concatenate([pad(wd, LANE), pad(ad, LANE), pad(gd, GD_PAD)], axis=axis)


def kernel(x, p, ffn1_w_gate, ffn1_w_up, ffn1_w_down, ln1_g, ln1_b, w_in, rel_bias, shift_mix, decay_w0, decay_w2, a_a0, a_a2, gate_g2, k_k, k_a, r_k, gn_g, gn_b, w_out, ln2_g, ln2_b, ffn2_w_gate, ffn2_w_up, ffn2_w_down, ln3_g, ln3_b, ple_w_up, ple_w_gate, ple_b_gate, ln4_g, ln4_b):
    batch, seq, d = x.shape
    t = batch * seq
    row = lambda a: a.reshape(1, -1)
    main_cols = 3 * ATT_WIDTH + 3 * RWKV_WIDTH

    heads = jnp.arange(QUAD) // HEAD_DIM
    esum = (heads[:, None] == heads[None, :]).astype(BF16)
    tab = _bias_tab(rel_bias)

    h = x.reshape(t, d).astype(F32)
    for i in range(DEPTH):
        w_in_t = w_in[i].T
        later_weights = (ffn2_w_gate[i], ffn2_w_up[i], ffn2_w_down[i], w_out[i], ple_w_gate[i], ple_w_up[i])
        h, (h_bf16,), _ = _ffn_ln(
            h, ffn1_w_gate[i].astype(BF16), ffn1_w_up[i].astype(BF16), ffn1_w_down[i].astype(BF16),
            row(ln1_g[i]), row(ln1_b[i]), emit_bf16=True)
        u_main, u_tail = _proj(h_bf16, w_in_t, main_cols, _pack_tail(w_in_t[main_cols:], axis=0))
        mix = shift_mix[i]
        prep_params = (row(mix[:3 * RWKV_WIDTH]), row(_pack_tail(mix[3 * RWKV_WIDTH:], axis=0)),
                       row(decay_w0[i]), _pad_rows(decay_w2[i], LANE).astype(BF16), row(a_a0[i]),
                       _pad_rows(a_a2[i], LANE).astype(BF16), _pad_rows(gate_g2[i], GD_PAD).astype(BF16),
                       row(k_k[i]), row(k_a[i]), esum)
        u_main3 = u_main.reshape(batch, seq, main_cols)
        att, (wg2, wu2, wd2, wo, wpg, wpu) = _moba(rel_bias, u_main3, tab, to_cast=later_weights)
        rw = _rwkv(u_main3, u_tail.reshape(batch, seq, TAIL_COLS), prep_params,
                   row(r_k[i]), row(gn_g[i]), row(gn_b[i]))
        h = _out_ln(h, att.reshape(t, ATT_WIDTH), rw.reshape(t, RWKV_WIDTH), wo,
                    row(ln2_g[i]), row(ln2_b[i]))
        h, _, _ = _ffn_ln(h, wg2, wu2, wd2, row(ln3_g[i]), row(ln3_b[i]))
        h = _ple_ln(h, p[i].reshape(t, PLE_DIM).astype(F32), wpg, row(ple_b_gate[i]), wpu,
                    row(ln4_g[i]), row(ln4_b[i]))
    return h.reshape(batch, seq, d).astype(x.dtype)
```

```python
import functools
import itertools
import math

import jax
import jax.numpy as jnp
from jax import lax
from jax.experimental import pallas as pl
from jax.experimental.pallas import tpu as pltpu

DEPTH = 1
HEAD_DIM = 64
ATT_HEADS = 16
RWKV_HEADS = 16
ATT_WIDTH = ATT_HEADS * HEAD_DIM
RWKV_WIDTH = RWKV_HEADS * HEAD_DIM
MOBA_BLOCK = 256
MOBA_TOPK = 3
MOBA_UNROLL = 4
MOBA_QBLOCKS = 2
MOBA_CAST_RUN = 4
LOG2E = 1.4426950408889634
REL_BUCKETS = 32
REL_MAX_DIST = 128
DECAY_LORA = 64
AAA_LORA = 64
PLE_DIM = 256
LN_EPS = 1e-5
GN_EPS = 64e-5
NEG = -1e30
ALPHA = (2 * DEPTH) ** 0.25

LANE = 128
SUBLANE = 8
SUBLANE_BF16 = 16
PAIR = LANE // HEAD_DIM
CHUNK = 64
QUAD = 256
ONES_ROWS = SUBLANE_BF16
VT_ROWS = HEAD_DIM + ONES_ROWS
CAST_GROUPS = 4
SUB_ROWS = 128
TAIL_COLS = 512
TAIL_WD, TAIL_AD, TAIL_GD = 0, 128, 256
GD_PAD = 256
VMEM_LIMIT = 56 * 1024 * 1024

F32 = jnp.float32
BF16 = jnp.bfloat16
HI = lax.Precision.HIGHEST


def _cparams(sem):
    return pltpu.CompilerParams(dimension_semantics=sem, vmem_limit_bytes=VMEM_LIMIT)


def _layer_norm(y, g, b):
    mu = jnp.mean(y, axis=-1, keepdims=True)
    d = y - mu
    var = jnp.mean(d * d, axis=-1, keepdims=True)
    return d * lax.rsqrt(var + LN_EPS) * g + b


def _sigmoid(x):
    return 0.5 * jnp.tanh(0.5 * x) + 0.5


def _dot(a, b):
    return jnp.dot(a, b, preferred_element_type=F32)


def _dot_tb(a, b, precision=None):
    return lax.dot_general(a, b, (((1,), (1,)), ((), ())), preferred_element_type=F32,
                           precision=precision)


def _split3(x):
    hi = x.astype(BF16)
    r1 = x - hi.astype(F32)
    mid = r1.astype(BF16)
    lo = (r1 - mid.astype(F32)).astype(BF16)
    return hi, mid, lo


def _ffn_ln_kernel(*refs, n_cast, emit_bf16):
    x_ref, wg_ref, wu_ref, wd_ref, g_ref, b_ref = refs[:6]
    cast_in = refs[6:6 + n_cast]
    o_ref = refs[6 + n_cast]
    ob_ref = refs[7 + n_cast] if emit_bf16 else None
    first_cast_out = 7 + n_cast + int(emit_bf16)
    cast_out = refs[first_cast_out:first_cast_out + n_cast]
    (xb_ref,) = refs[first_cast_out + n_cast:]
    f = pl.program_id(1)

    @pl.when(f == 0)
    def _():
        xb_ref[...] = x_ref[...].astype(BF16)
        o_ref[...] = jnp.zeros_like(o_ref)

    xb = xb_ref[...]
    gate = _dot(xb, wg_ref[...])
    up = _dot(xb, wu_ref[...])
    act = gate * jax.nn.sigmoid(gate) * up
    o_ref[...] += _dot(act.astype(BF16), wd_ref[...].astype(BF16))

    for src_ref, dst_ref in zip(cast_in, cast_out):
        dst_ref[...] = src_ref[...].astype(BF16)

    @pl.when(f == pl.num_programs(1) - 1)
    def _():
        y = _layer_norm(ALPHA * x_ref[...] + 0.5 * o_ref[...], g_ref[...], b_ref[...])
        o_ref[...] = y
        if emit_bf16:
            ob_ref[...] = y.astype(BF16)


def _cast_spec(shape, n_i, n_f):
    rows, cols = shape
    run = -(-n_f // CAST_GROUPS)
    if rows % (SUBLANE_BF16 * n_i) == 0 and cols % (LANE * CAST_GROUPS) == 0:
        return pl.BlockSpec((rows // n_i, cols // CAST_GROUPS), lambda i, f: (i, f // run))
    if rows % (SUBLANE_BF16 * CAST_GROUPS) == 0 and cols % (LANE * n_i) == 0:
        return pl.BlockSpec((rows // CAST_GROUPS, cols // n_i), lambda i, f: (f // run, i))
    assert rows % (SUBLANE_BF16 * n_i) == 0, shape
    return pl.BlockSpec((rows // n_i, cols), lambda i, f: (i, 0))


def _ffn_ln(x, wg, wu, wd, g, b, to_cast=(), emit_bf16=False, tm=512, tf=512):
    t, d = x.shape
    ff = wg.shape[1]
    n_i, n_f = t // tm, ff // tf
    cast_specs = [_cast_spec(w.shape, n_i, n_f) for w in to_cast]
    outs = pl.pallas_call(
        functools.partial(_ffn_ln_kernel, n_cast=len(to_cast), emit_bf16=emit_bf16),
        grid=(n_i, n_f),
        in_specs=[
            pl.BlockSpec((tm, d), lambda i, f: (i, 0)),
            pl.BlockSpec((d, tf), lambda i, f: (0, f)),
            pl.BlockSpec((d, tf), lambda i, f: (0, f)),
            pl.BlockSpec((tf, d), lambda i, f: (f, 0)),
            pl.BlockSpec((1, d), lambda i, f: (0, 0)),
            pl.BlockSpec((1, d), lambda i, f: (0, 0)),
        ] + cast_specs,
        out_specs=[pl.BlockSpec((tm, d), lambda i, f: (i, 0))] * (1 + int(emit_bf16)) + cast_specs,
        out_shape=[jax.ShapeDtypeStruct((t, d), F32)] + [jax.ShapeDtypeStruct((t, d), BF16)] * int(emit_bf16)
        + [jax.ShapeDtypeStruct(w.shape, BF16) for w in to_cast],
        scratch_shapes=[pltpu.VMEM((tm, d), BF16)],
        compiler_params=_cparams(("parallel", "arbitrary")),
        name="ffn_ln",
    )(x, wg, wu, wd, g, b, *to_cast)
    n_res = 1 + int(emit_bf16)
    return outs[0], list(outs[1:n_res]), list(outs[n_res:])


def _proj_kernel(x_ref, w_ref, wt_ref, o_ref, ot_ref):
    @pl.when(pl.program_id(1) == 0)
    def _():
        ot_ref[...] = _dot_tb(x_ref[...], wt_ref[...].astype(BF16))

    o_ref[...] = _dot_tb(x_ref[...], w_ref[...].astype(BF16))


def _proj(x, w_t, n, w_tail_t, tm=2048, tn=512):
    t, d = x.shape
    n_tail = w_tail_t.shape[0]
    return pl.pallas_call(
        _proj_kernel,
        grid=(t // tm, n // tn),
        in_specs=[
            pl.BlockSpec((tm, d), lambda i, j: (i, 0)),
            pl.BlockSpec((tn, d), lambda i, j: (j, 0)),
            pl.BlockSpec((n_tail, d), lambda i, j: (0, 0), pipeline_mode=pl.Buffered(1)),
        ],
        out_specs=[pl.BlockSpec((tm, tn), lambda i, j: (i, j)), pl.BlockSpec((tm, n_tail), lambda i, j: (i, 0))],
        out_shape=[jax.ShapeDtypeStruct((t, n), F32), jax.ShapeDtypeStruct((t, n_tail), F32)],
        compiler_params=_cparams(("parallel", "arbitrary")),
        name="proj",
    )(x, w_t, w_tail_t)


def _rel_bucket(dist):
    max_exact = REL_BUCKETS // 2
    n = jnp.maximum(dist, 0)
    nf = jnp.maximum(n, 1).astype(F32)
    large = max_exact + (jnp.log(nf / max_exact) / math.log(REL_MAX_DIST / max_exact)
                         * (REL_BUCKETS - max_exact)).astype(jnp.int32)
    large = jnp.minimum(large, REL_BUCKETS - 1)
    return jnp.where(n < max_exact, n, large)


def _bias_tab_kernel(rb_ref, o_ref):
    p = pl.program_id(0)
    blk = MOBA_BLOCK
    diff = lax.broadcasted_iota(jnp.int32, (SUBLANE, 2 * blk), 1) - blk
    for kind in range(2):
        dist = diff + kind * blk
        bucket = _rel_bucket(dist)
        for hh in range(PAIR):
            h = PAIR * p + hh
            vec = jnp.zeros(dist.shape, F32)
            for bkt in range(REL_BUCKETS):
                vec = jnp.where(bucket == bkt, rb_ref[bkt, h], vec)
            vec = vec * LOG2E
            if kind == 0:
                vec = jnp.where(dist >= 0, vec, NEG)
            rows = jnp.broadcast_to(vec[0:1], (blk, 2 * blk))
            o_ref[0, 2 * hh + kind] = pltpu.roll(rows, 0, 1, stride=1, stride_axis=0)[:, blk:]


def _bias_tab(rel_bias):
    npair = ATT_HEADS // PAIR
    return pl.pallas_call(
        _bias_tab_kernel,
        grid=(npair,),
        in_specs=[pl.BlockSpec(memory_space=pltpu.SMEM)],
        out_specs=pl.BlockSpec((1, 2 * PAIR, MOBA_BLOCK, MOBA_BLOCK), lambda p: (p, 0, 0, 0)),
        out_shape=jax.ShapeDtypeStruct((npair, 2 * PAIR, MOBA_BLOCK, MOBA_BLOCK), F32),
        compiler_params=_cparams(("arbitrary",)),
        name="bias_tab",
    )(rel_bias)


def _rwkv_prep_tile(tiles, prev_rows, first, mixm_ref, mixt_ref, w0_ref, w2_ref, a0_ref, a2_ref, g2_ref,
                    kk_ref, ka_ref, esum_ref, out):
    width = RWKV_WIDTH

    def shifted(g, mix):
        x = tiles[g]()
        prev = pltpu.roll(x, 1, 0)
        row = lax.broadcasted_iota(jnp.int32, x.shape, 0)
        last = 0.0 if first is True else jnp.where(first, 0.0, prev_rows[g]())
        prev = jnp.where(row == 0, last, prev)
        return x + (prev - x) * mix

    tail = shifted(3, mixt_ref[...])
    wd = tail[:, TAIL_WD:TAIL_WD + LANE]
    ad = tail[:, TAIL_AD:TAIL_AD + LANE]
    gd = tail[:, TAIL_GD:TAIL_GD + GD_PAD]
    y = w0_ref[...] + _dot(jnp.tanh(wd).astype(BF16), w2_ref[...])
    yield
    lw = -math.exp(-0.5) * _sigmoid(y)
    yield
    a = _sigmoid(a0_ref[...] + _dot(ad.astype(BF16), a2_ref[...]))
    yield
    g = _dot(_sigmoid(gd).astype(BF16), g2_ref[...])
    yield
    k = shifted(1, mixm_ref[:, width:2 * width])
    kk = k * kk_ref[...]
    kk2 = kk * kk
    yield
    ss = []
    for c in range(0, width, QUAD):
        ss.append(_dot(kk2[:, c:c + QUAD].astype(BF16), esum_ref[...]))
        yield
    kk = kk * jnp.minimum(lax.rsqrt(jnp.concatenate(ss, axis=1)), 1e12)
    k = k * (1.0 + (a - 1.0) * ka_ref[...])
    yield
    r = shifted(0, mixm_ref[:, 0:width])
    yield
    v = shifted(2, mixm_ref[:, 2 * width:3 * width])
    out.extend((r, k, v, lw, -kk, kk * a, g))


def _moba_kernel(rb_ref, q_ref, k_ref, v_ref, tab_ref, *refs, n_cast):
    cast_in, o_ref, cast_out = refs[:n_cast], refs[n_cast], refs[n_cast + 1:2 * n_cast + 1]
    kb_scr, vt_scr, kmean_scr = refs[2 * n_cast + 1:]
    p = pl.program_id(1)
    i0 = pl.program_id(2) * MOBA_QBLOCKS
    nb = kb_scr.shape[0]
    blk = MOBA_BLOCK

    @pl.when(pl.program_id(2) % MOBA_CAST_RUN == 0)
    def _():
        for src_ref, dst_ref in zip(cast_in, cast_out):
            dst_ref[...] = src_ref[...].astype(BF16)

    @pl.when(i0 == 0)
    def _():
        lane_k = lax.broadcasted_iota(jnp.int32, (blk, LANE), 1)
        lane_m = lax.broadcasted_iota(jnp.int32, (1, LANE), 1)
        ones = jnp.ones((ONES_ROWS, blk), BF16)
        for n in range(nb):
            kblk = k_ref[0, n * blk:(n + 1) * blk, :]
            kb_scr[n, :, :LANE] = kblk.astype(BF16)
            hot = (lane_k == n) | (lane_k == nb + n) | (lane_k == 2 * nb + n)
            kb_scr[n, :, LANE:] = jnp.where(hot, 1.0, 0.0).astype(BF16)
            kmean = jnp.mean(kblk, axis=0, keepdims=True)
            v_t = v_ref[0, n * blk:(n + 1) * blk, :].T.astype(BF16)
            for hh in range(PAIR):
                in_head = (lane_m >= hh * HEAD_DIM) & (lane_m < (hh + 1) * HEAD_DIM)
                kmean_scr[hh * nb + n:hh * nb + n + 1, :] = jnp.where(in_head, kmean, 0.0)
                vt_scr[n, hh * VT_ROWS:hh * VT_ROWS + HEAD_DIM, :] = v_t[hh * HEAD_DIM:(hh + 1) * HEAD_DIM, :]
                vt_scr[n, hh * VT_ROWS + HEAD_DIM:(hh + 1) * VT_ROWS, :] = ones

    q8 = q_ref[0] * (HEAD_DIM ** -0.5)
    qs = q8 * LOG2E
    nq = q8.shape[0]
    lane = lax.broadcasted_iota(jnp.int32, q8.shape, 1)
    nidx = lax.broadcasted_iota(jnp.int32, (nb, nq), 0)
    own = i0 + lax.broadcasted_iota(jnp.int32, (1, nq), 1) // blk
    gate_all = _dot_tb(kmean_scr[...], q8, precision=HI)

    q_tail, q_far, adj_mask = [], [], []
    for hh in range(PAIR):
        gate = gate_all[hh * nb:(hh + 1) * nb]
        avail = nidx < own
        sel = jnp.zeros((nb, nq), jnp.bool_)
        for _ in range(MOBA_TOPK):
            best = jnp.max(jnp.where(avail, gate, -jnp.inf), axis=0, keepdims=True)
            first = jnp.min(jnp.where(avail & (gate == best), nidx, nb), axis=0, keepdims=True)
            pick = nidx == first
            sel = sel | pick
            avail = avail & jnp.logical_not(pick)
        b_far = rb_ref[REL_BUCKETS - 1, PAIR * p + hh] * LOG2E
        rowb = jnp.where(sel & (nidx < own - 1), b_far, NEG)
        adj_mask.append(jnp.max(jnp.where(sel & (nidx == own - 1), 0.0, NEG), axis=0, keepdims=True))
        hi, mid, lo = _split3(rowb)
        parts = jnp.concatenate([hi.astype(F32), mid.astype(F32), lo.astype(F32),
                                 jnp.zeros((LANE - 3 * nb, nq), F32)], axis=0)
        in_head = (lane >= hh * HEAD_DIM) & (lane < (hh + 1) * HEAD_DIM)
        q_h = jnp.where(in_head, qs, 0.0).astype(BF16)
        q_tail.append(q_h)
        q_far.append(jnp.concatenate([q_h, parts.T.astype(BF16)], axis=1))

    def vt(hh, n):
        return vt_scr[n, hh * VT_ROWS:(hh + 1) * VT_ROWS, :]

    def block_chain(hh, n, scores, parts):
        s = scores()
        yield
        m_s = jnp.max(s, axis=0, keepdims=True)
        yield
        parts.append((m_s, _dot(vt(hh, n), jnp.exp2(s - m_s).astype(BF16))))

    def run_chains(gens):
        for _ in itertools.zip_longest(*gens):
            pass

    def merge(prev, parts):
        m_new = parts[0][0] if prev is None else jnp.maximum(prev[0], parts[0][0])
        for m_s, _ in parts[1:]:
            m_new = jnp.maximum(m_new, m_s)
        acc = None if prev is None else jnp.exp2(prev[0] - m_new) * prev[1]
        for m_s, acc_s in parts:
            term = jnp.exp2(m_s - m_new) * acc_s
            acc = term if acc is None else acc + term
        return m_new, acc

    def tail_blocks(hh, c):
        cols = slice(c * blk, (c + 1) * blk)
        i_c = i0 + c
        n_adj = jnp.maximum(i_c - 1, 0)
        q_c = q_tail[hh][cols]

        def own_blk():
            return _dot_tb(kb_scr[i_c, :, :LANE], q_c) + tab_ref[0, 2 * hh]

        def prev_blk():
            return _dot_tb(kb_scr[n_adj, :, :LANE], q_c) + tab_ref[0, 2 * hh + 1] + adj_mask[hh][:, cols]

        return [(i_c, own_blk), (n_adj, prev_blk)]

    tails = {(hh, c): [] for hh in range(PAIR) for c in range(MOBA_QBLOCKS)}
    run_chains(block_chain(hh, n, scores, tails[hh, c])
               for hh in range(PAIR) for c in range(MOBA_QBLOCKS) for n, scores in tail_blocks(hh, c))
    tails = {key: merge(None, parts) for key, parts in tails.items()}
    state = [tuple(jnp.concatenate([tails[hh, c][s] for c in range(MOBA_QBLOCKS)], axis=1) for s in range(2))
             for hh in range(PAIR)]

    def far_trip(first_block, n_blocks, carry):
        parts = [[] for _ in range(PAIR)]
        chains = []
        for hh in range(PAIR):
            for u in range(n_blocks):
                nc = first_block + u
                chains.append(block_chain(hh, nc, lambda nc=nc, hh=hh: _dot_tb(kb_scr[nc], q_far[hh]), parts[hh]))
        run_chains(chains)
        return tuple(x for hh in range(PAIR) for x in merge(carry[2 * hh:2 * hh + 2], parts[hh]))

    n_far = jnp.maximum(i0 + MOBA_QBLOCKS - 2, 0)
    full = n_far // MOBA_UNROLL
    flat = lax.fori_loop(0, full, lambda j, c: far_trip(j * MOBA_UNROLL, MOBA_UNROLL, c),
                         tuple(x for st in state for x in st))
    flat = lax.cond(n_far > full * MOBA_UNROLL,
                    lambda c: far_trip(full * MOBA_UNROLL, MOBA_UNROLL // 2, c), lambda c: c, flat)
    out_t = jnp.concatenate(
        [flat[2 * hh + 1][:HEAD_DIM] / flat[2 * hh + 1][HEAD_DIM:HEAD_DIM + 1] for hh in range(PAIR)], axis=0)
    o_ref[0] = out_t.T


def _moba(rel_bias, u_main, tab, to_cast=()):
    batch, seq, _ = u_main.shape
    npair = ATT_HEADS // PAIR
    nb = seq // MOBA_BLOCK
    assert 3 * nb <= LANE and nb % MOBA_QBLOCKS == 0
    assert MOBA_QBLOCKS == 2 and MOBA_UNROLL == 2 * MOBA_QBLOCKS
    kcol = ATT_WIDTH // LANE
    qrows = MOBA_QBLOCKS * MOBA_BLOCK
    nsteps = nb // MOBA_QBLOCKS
    assert nsteps % MOBA_CAST_RUN == 0
    total_steps = batch * npair * nsteps

    def cast_spec(w):
        rows, cols = w.shape
        n_blocks = min(total_steps // MOBA_CAST_RUN, rows // SUBLANE_BF16)
        assert rows % (n_blocks * SUBLANE_BF16) == 0 and total_steps % n_blocks == 0, w.shape
        run = total_steps // n_blocks
        return pl.BlockSpec((rows // n_blocks, cols), lambda b, p, i: (((b * npair + p) * nsteps + i) // run, 0))

    cast_specs = [cast_spec(w) for w in to_cast]
    outs = pl.pallas_call(
        functools.partial(_moba_kernel, n_cast=len(to_cast)),
        grid=(batch, npair, nsteps),
        in_specs=[
            pl.BlockSpec(memory_space=pltpu.SMEM),
            pl.BlockSpec((1, qrows, LANE), lambda b, p, i: (b, i, p)),
            pl.BlockSpec((1, seq, LANE), lambda b, p, i: (b, 0, kcol + p)),
            pl.BlockSpec((1, seq, LANE), lambda b, p, i: (b, 0, 2 * kcol + p)),
            pl.BlockSpec((1, 2 * PAIR, MOBA_BLOCK, MOBA_BLOCK), lambda b, p, i: (p, 0, 0, 0)),
        ] + cast_specs,
        out_specs=[pl.BlockSpec((1, qrows, LANE), lambda b, p, i: (b, i, p))] + cast_specs,
        out_shape=[jax.ShapeDtypeStruct((batch, seq, ATT_WIDTH), F32)]
        + [jax.ShapeDtypeStruct(w.shape, BF16) for w in to_cast],
        scratch_shapes=[
            pltpu.VMEM((nb, MOBA_BLOCK, 2 * LANE), BF16),
            pltpu.VMEM((nb, PAIR * VT_ROWS, MOBA_BLOCK), BF16),
            pltpu.VMEM((PAIR * nb, LANE), F32),
        ],
        compiler_params=_cparams(("arbitrary", "arbitrary", "arbitrary")),
        name="moba",
    )(rel_bias, u_main, u_main, u_main, tab, *to_cast)
    return outs[0], list(outs[1:])


def _rwkv_chunk_kernel(*refs):
    next_tiles, next_prev, first_tiles, prep_params = refs[0:4], refs[4:8], refs[8:12], refs[12:22]
    rk_ref, gng_ref, gnb_ref, o_ref, st_scr, prep_scr = refs[22:]
    eq_ref = prep_params[-1]
    c = pl.program_id(0)
    nquad = RWKV_WIDTH // QUAD
    nbatch = o_ref.shape[0]

    def prep_chain(bi, tiles, prevs, first):
        vals = []
        yield from _rwkv_prep_tile([lambda t=t: t[bi] for t in tiles], [lambda p=p: p[bi, SUBLANE - 1:SUBLANE, :] for p in prevs],
                                   first, *prep_params, vals)
        yield
        for q, val in enumerate(vals):
            prep_scr[q, bi] = val

    @pl.when(c == 0)
    def _():
        st_scr[...] = jnp.zeros_like(st_scr)
        for _ in itertools.zip_longest(*[prep_chain(bi, first_tiles, next_prev, True) for bi in range(nbatch)]):
            pass

    eq = eq_ref[...]
    levels = int(math.log2(CHUNK))
    row_h = lax.broadcasted_iota(jnp.int32, (QUAD, QUAD), 0) // HEAD_DIM
    col_h = lax.broadcasted_iota(jnp.int32, (QUAD, QUAD), 1) // HEAD_DIM
    bdmask = row_h == col_h
    eye = (lax.broadcasted_iota(jnp.int32, (QUAD, QUAD), 0)
           == lax.broadcasted_iota(jnp.int32, (QUAD, QUAD), 1))
    t_idx = lax.broadcasted_iota(jnp.int32, (CHUNK, QUAD), 0)
    s_idx = lax.broadcasted_iota(jnp.int32, (CHUNK, QUAD), 1) % CHUNK
    strict = t_idx > s_idx
    incl = t_idx >= s_idx
    both = jnp.concatenate([strict, incl], axis=0)
    reps = QUAD // CHUNK

    def bds(x):
        xb = x.astype(BF16)
        return jnp.where(bdmask, jnp.concatenate([xb] * reps, axis=0), jnp.zeros((), BF16))

    def mm(a, b):
        return _dot(a.astype(BF16), b.astype(BF16))

    def quad_stages(bi, qd):
        sl = slice(qd * QUAD, (qd + 1) * QUAD)
        st_ref = st_scr.at[bi * nquad + qd]
        r, k, v, lw, aa, bb, g = (prep_scr[q, bi, :, sl] for q in range(7))

        cum = lw
        for sh in (1 << b for b in range(levels)):
            cum = cum + jnp.where(t_idx >= sh, pltpu.roll(cum, sh, 0), 0.0)
        e_pos = jnp.exp(cum)
        e_neg = jnp.exp(-cum)
        gamma = e_pos[CHUNK - 1:CHUNK, :]
        e_prev = jnp.where(t_idx == 0, 1.0, pltpu.roll(e_pos, 1, 0))
        a_t = aa * e_prev
        r_t = r * e_pos
        b_c = bb * e_neg
        k_c = k * e_neg
        b_h = b_c * gamma
        k_h = k_c * gamma

        lhs = jnp.concatenate([a_t, r_t], axis=0).astype(BF16)
        pb = _dot_tb(lhs, bds(b_c))
        pk = _dot_tb(lhs, bds(k_c))
        bk_t = jnp.concatenate([b_h, k_h], axis=0).T.astype(BF16)
        yield
        a_rb = jnp.where(incl, pb[CHUNK:], 0.0).astype(BF16)
        akrk = mm(jnp.where(both, pk, 0.0), bds(v))
        n = jnp.where(strict, pb[:CHUNK], 0.0)
        t_inv = jnp.where(t_idx == s_idx, 1.0, 0.0) + n
        n = mm(n, bds(n))
        yield
        for lvl in range(1, levels):
            bd_n = bds(n)
            if lvl < levels - 1:
                tn = mm(jnp.concatenate([t_inv, n], axis=0), bd_n)
                t_inv = t_inv + tn[:CHUNK]
                n = tn[CHUNK:]
            else:
                t_inv = t_inv + mm(t_inv, bd_n)
            yield
        w = mm(t_inv, bds(a_t))
        u = mm(t_inv, bds(akrk[:CHUNK]))
        yield
        q_t = r_t + mm(a_rb, bds(w))
        y0 = mm(a_rb, bds(u)) + akrk[CHUNK:]
        w0 = jnp.concatenate([w, jnp.zeros_like(w)], axis=0)
        g_mat = jnp.where(bdmask, mm(bk_t, w0), 0.0) + jnp.where(eye, gamma, 0.0)
        h_mat = jnp.where(bdmask, mm(bk_t, jnp.concatenate([u, v], axis=0)), 0.0)
        yield
        gy = mm(jnp.concatenate([g_mat, q_t], axis=0), st_ref[...])
        st_ref[...] = gy[:QUAD] + h_mat
        y = gy[QUAD:] + y0
        yield

        inv_n = 1.0 / HEAD_DIM
        stats = mm(jnp.concatenate([y, r * k * rk_ref[:, sl]], axis=0), eq)
        mu = stats[:CHUNK] * inv_n
        bonus = stats[CHUNK:] * v
        yield
        d = y - mu
        var = mm(d * d, eq) * inv_n
        yield
        yn = d * lax.rsqrt(var + GN_EPS) * gng_ref[:, sl] + gnb_ref[:, sl]
        o_ref[bi, :, sl] = (yn + bonus) * g

    chains = [quad_stages(bi, qd) for bi in range(nbatch) for qd in range(nquad)]
    chains += [prep_chain(bi, next_tiles, next_prev, False) for bi in range(nbatch)]
    for _ in itertools.zip_longest(*chains):
        pass


def _rwkv(u_main, u_tail, prep_params, r_k, gn_g, gn_b):
    batch, seq, _ = u_main.shape
    width = RWKV_WIDTH
    nchunks = seq // CHUNK
    rcol = 3 * ATT_WIDTH // width
    nxt = lambda c: jnp.minimum(c + 1, nchunks - 1)
    prev_tile = lambda c: nxt(c) * (CHUNK // SUBLANE) - 1
    const = lambda a: pl.BlockSpec(a.shape, lambda c: (0,) * a.ndim)

    def tiles(rows, row_index):
        return ([pl.BlockSpec((batch, rows, width), lambda c, g=g: (0, row_index(c), rcol + g)) for g in range(3)]
                + [pl.BlockSpec((batch, rows, TAIL_COLS), lambda c: (0, row_index(c), 0))])

    u_args = (u_main, u_main, u_main, u_tail)
    return pl.pallas_call(
        _rwkv_chunk_kernel,
        grid=(nchunks,),
        in_specs=tiles(CHUNK, nxt) + tiles(SUBLANE, prev_tile) + tiles(CHUNK, lambda c: 0)
        + [const(a) for a in prep_params] + [const(r_k), const(gn_g), const(gn_b)],
        out_specs=pl.BlockSpec((batch, CHUNK, width), lambda c: (0, c, 0)),
        out_shape=jax.ShapeDtypeStruct((batch, seq, width), F32),
        scratch_shapes=[pltpu.VMEM((batch * width // QUAD, QUAD, QUAD), F32),
                        pltpu.VMEM((7, batch, CHUNK, width), F32)],
        compiler_params=_cparams(("arbitrary",)),
        name="rwkv_chunk",
    )(*u_args, *u_args, *u_args, *prep_params, r_k, gn_g, gn_b)


def _row_pipelined(n_rows, matmul, finish):
    pending = None
    for r0 in range(0, n_rows, SUB_ROWS):
        rows = slice(r0, r0 + SUB_ROWS)
        acc = matmul(rows)
        if pending is not None:
            finish(*pending)
        pending = (rows, acc)
    finish(*pending)


def _out_ln_kernel(h_ref, att_ref, rw_ref, wa_ref, wr_ref, g_ref, b_ref, o_ref):
    def matmul(rows):
        return (_dot(att_ref[rows, :].astype(BF16), wa_ref[...])
                + _dot(rw_ref[rows, :].astype(BF16), wr_ref[...]))

    def finish(rows, mix):
        o_ref[rows, :] = _layer_norm(ALPHA * h_ref[rows, :] + mix, g_ref[...], b_ref[...])

    _row_pipelined(h_ref.shape[0], matmul, finish)


def _out_ln(h, att, rw, w_out, g, b, tm=512):
    t, d = h.shape
    wa, wr = att.shape[1], rw.shape[1]
    assert wa == wr and w_out.shape[0] == wa + wr
    row = lambda n: pl.BlockSpec((tm, n), lambda i: (i, 0))
    full = lambda shape: pl.BlockSpec(shape, lambda i: (0,) * len(shape))
    return pl.pallas_call(
        _out_ln_kernel,
        grid=(t // tm,),
        in_specs=[row(d), row(wa), row(wr),
                  pl.BlockSpec((wa, d), lambda i: (0, 0)), pl.BlockSpec((wr, d), lambda i: (1, 0)),
                  full((1, d)), full((1, d))],
        out_specs=row(d),
        out_shape=jax.ShapeDtypeStruct((t, d), F32),
        compiler_params=_cparams(("parallel",)),
        name="out_ln",
    )(h, att, rw, w_out, w_out, g, b)


def _ple_ln_kernel(h_ref, p_ref, wg_ref, bg_ref, wp_ref, g_ref, b_ref, o_ref):
    def matmul(rows):
        return (_dot(h_ref[rows, :].astype(BF16), wg_ref[...]), _dot(p_ref[rows, :].astype(BF16), wp_ref[...]))

    def finish(rows, acc):
        pre_gate, e = acc
        y = ALPHA * h_ref[rows, :] + _sigmoid(pre_gate + bg_ref[...]) * e
        o_ref[rows, :] = _layer_norm(y, g_ref[...], b_ref[...])

    _row_pipelined(h_ref.shape[0], matmul, finish)


def _ple_ln(h, p, w_gate, b_gate, w_up, g, b, tm=512):
    t, d = h.shape
    row = lambda n: pl.BlockSpec((tm, n), lambda i: (i, 0))
    full = lambda shape: pl.BlockSpec(shape, lambda i: (0,) * len(shape))
    return pl.pallas_call(
        _ple_ln_kernel,
        grid=(t // tm,),
        in_specs=[row(d), row(p.shape[1]), full(w_gate.shape), full((1, d)), full(w_up.shape),
                  full((1, d)), full((1, d))],
        out_specs=row(d),
        out_shape=jax.ShapeDtypeStruct((t, d), F32),
        compiler_params=_cparams(("parallel",)),
        name="ple_ln",
    )(h, p, w_gate, b_gate, w_up, g, b)


def _pad_rows(w, rows):
    return jnp.pad(w, ((0, rows - w.shape[0]), (0, 0)))


def _pack_tail(cols, axis):
    wd, ad, gd = jnp.split(cols, [DECAY_LORA, DECAY_LORA + AAA_LORA], axis=axis)

    def pad(x, n):
        widths = [(0, 0)] * x.ndim
        widths[axis] = (0, n - x.shape[axis])
        return jnp.pad(x, widths)

    return jnp.concatenate([pad(wd, LANE), pad(ad, LANE), pad(gd, GD_PAD)], axis=axis)


def kernel(x, p, ffn1_w_gate, ffn1_w_up, ffn1_w_down, ln1_g, ln1_b, w_in, rel_bias, shift_mix, decay_w0, decay_w2, a_a0, a_a2, gate_g2, k_k, k_a, r_k, gn_g, gn_b, w_out, ln2_g, ln2_b, ffn2_w_gate, ffn2_w_up, ffn2_w_down, ln3_g, ln3_b, ple_w_up, ple_w_gate, ple_b_gate, ln4_g, ln4_b):
    batch, seq, d = x.shape
    t = batch * seq
    row = lambda a: a.reshape(1, -1)
    main_cols = 3 * ATT_WIDTH + 3 * RWKV_WIDTH

    heads = jnp.arange(QUAD) // HEAD_DIM
    esum = (heads[:, None] == heads[None, :]).astype(BF16)
    tab = _bias_tab(rel_bias)

    h = x.reshape(t, d).astype(F32)
    for i in range(DEPTH):
        w_in_t = w_in[i].T
        later_weights = (ffn2_w_gate[i], ffn2_w_up[i], ffn2_w_down[i], w_out[i], ple_w_gate[i], ple_w_up[i])
        h, (h_bf16,), _ = _ffn_ln(
            h, ffn1_w_gate[i].astype(BF16), ffn1_w_up[i].astype(BF16), ffn1_w_down[i],
            row(ln1_g[i]), row(ln1_b[i]), emit_bf16=True)
        u_main, u_tail = _proj(h_bf16, w_in_t, main_cols, _pack_tail(w_in_t[main_cols:], axis=0))
        mix = shift_mix[i]
        prep_params = (row(mix[:3 * RWKV_WIDTH]), row(_pack_tail(mix[3 * RWKV_WIDTH:], axis=0)),
                       row(decay_w0[i]), _pad_rows(decay_w2[i], LANE).astype(BF16), row(a_a0[i]),
                       _pad_rows(a_a2[i], LANE).astype(BF16), _pad_rows(gate_g2[i], GD_PAD).astype(BF16),
                       row(k_k[i]), row(k_a[i]), esum)
        u_main3 = u_main.reshape(batch, seq, main_cols)
        att, (wg2, wu2, wd2, wo, wpg, wpu) = _moba(rel_bias, u_main3, tab, to_cast=later_weights)
        rw = _rwkv(u_main3, u_tail.reshape(batch, seq, TAIL_COLS), prep_params,
                   row(r_k[i]), row(gn_g[i]), row(gn_b[i]))
        h = _out_ln(h, att.reshape(t, ATT_WIDTH), rw.reshape(t, RWKV_WIDTH), wo,
                    row(ln2_g[i]), row(ln2_b[i]))
        h, _, _ = _ffn_ln(h, wg2, wu2, wd2, row(ln3_g[i]), row(ln3_b[i]))
        h = _ple_ln(h, p[i].reshape(t, PLE_DIM).astype(F32), wpg, row(ple_b_gate[i]), wpu,
                    row(ln4_g[i]), row(ln4_b[i]))
    return h.reshape(batch, seq, d).astype(x.dtype)
```

```python
import functools
import itertools
import math

import jax
import jax.numpy as jnp
from jax import lax
from jax.experimental import pallas as pl
from jax.experimental.pallas import tpu as pltpu

DEPTH = 1
HEAD_DIM = 64
ATT_HEADS = 16
RWKV_HEADS = 16
ATT_WIDTH = ATT_HEADS * HEAD_DIM
RWKV_WIDTH = RWKV_HEADS * HEAD_DIM
MOBA_BLOCK = 256
MOBA_TOPK = 3
MOBA_UNROLL = 4
MOBA_QBLOCKS = 2
MOBA_CAST_RUN = 4
LOG2E = 1.4426950408889634
REL_BUCKETS = 32
REL_MAX_DIST = 128
DECAY_LORA = 64
AAA_LORA = 64
PLE_DIM = 256
LN_EPS = 1e-5
GN_EPS = 64e-5
NEG = -1e30
ALPHA = (2 * DEPTH) ** 0.25

LANE = 128
SUBLANE = 8
SUBLANE_BF16 = 16
PAIR = LANE // HEAD_DIM
CHUNK = 64
QUAD = 256
ONES_ROWS = SUBLANE_BF16
VT_ROWS = HEAD_DIM + ONES_ROWS
FFN_HEAD_TF = 256
SUB_ROWS = 128
TAIL_COLS = 512
TAIL_WD, TAIL_AD, TAIL_GD = 0, 128, 256
GD_PAD = 256
VMEM_LIMIT = 56 * 1024 * 1024

F32 = jnp.float32
BF16 = jnp.bfloat16
HI = lax.Precision.HIGHEST


def _cparams(sem):
    return pltpu.CompilerParams(dimension_semantics=sem, vmem_limit_bytes=VMEM_LIMIT)


def _layer_norm(y, g, b):
    mu = jnp.mean(y, axis=-1, keepdims=True)
    d = y - mu
    var = jnp.mean(d * d, axis=-1, keepdims=True)
    return d * lax.rsqrt(var + LN_EPS) * g + b


def _sigmoid(x):
    return 0.5 * jnp.tanh(0.5 * x) + 0.5


def _dot(a, b):
    return jnp.dot(a, b, preferred_element_type=F32)


def _dot_tb(a, b, precision=None):
    return lax.dot_general(a, b, (((1,), (1,)), ((), ())), preferred_element_type=F32,
                           precision=precision)


def _split3(x):
    hi = x.astype(BF16)
    r1 = x - hi.astype(F32)
    mid = r1.astype(BF16)
    lo = (r1 - mid.astype(F32)).astype(BF16)
    return hi, mid, lo


def _ffn_ln_kernel(*refs, emit_bf16, copy_weights, has_head):
    x_ref, wg_ref, wu_ref, wd_ref, g_ref, b_ref = refs[:6]
    rest = list(refs[6:])
    head_ref, head_bf16_ref = (rest.pop(0), rest.pop(0)) if has_head else (None, None)
    o_ref = rest.pop(0)
    ob_ref = rest.pop(0) if emit_bf16 else None
    w_copies = [rest.pop(0) for _ in range(3)] if copy_weights else []
    (xb_ref,) = rest
    i, f = pl.program_id(0), pl.program_id(1)
    last = pl.num_programs(1) - 1

    def compute():
        @pl.when(f == 0)
        def _():
            xb_ref[...] = x_ref[...].astype(BF16)
            o_ref[...] = jnp.zeros_like(o_ref)

        weights = [w_ref[...] for w_ref in (wg_ref, wu_ref, wd_ref)]
        if copy_weights:
            weights = [w.astype(BF16) for w in weights]
            for dst_ref, w in zip(w_copies, weights):
                dst_ref[...] = w
        wg, wu, wd = weights
        xb = xb_ref[...]
        gate = _dot(xb, wg)
        up = _dot(xb, wu)
        act = gate * jax.nn.sigmoid(gate) * up
        o_ref[...] += _dot(act.astype(BF16), wd)

        @pl.when(f == last)
        def _():
            y = _layer_norm(ALPHA * x_ref[...] + 0.5 * o_ref[...], g_ref[...], b_ref[...])
            o_ref[...] = y
            if emit_bf16:
                ob_ref[...] = y.astype(BF16)

    if not has_head:
        compute()
        return
    pl.when(i > 0)(compute)

    @pl.when((i == 0) & (f == last))
    def _():
        o_ref[...] = head_ref[...]
        ob_ref[...] = head_bf16_ref[...]


def _ffn_ln(x, wg, wu, wd, g, b, emit_bf16=False, copy_weights=False, head=None, tm=512, tf=512):
    t, d = x.shape
    ff = wg.shape[1]
    n_i, n_f = (1 if copy_weights else t // tm), ff // tf
    has_head = head is not None
    assert emit_bf16 or not has_head
    if has_head:
        x_row = lambda i: jnp.maximum(i, 1)
        w_blk = lambda i, f: jnp.where(i == 0, 0, f)
    else:
        x_row = lambda i: i
        w_blk = lambda i, f: f
    vec = pl.BlockSpec((1, d), lambda i, f: (0, 0))
    in_specs = [
        pl.BlockSpec((tm, d), lambda i, f: (x_row(i), 0)),
        pl.BlockSpec((d, tf), lambda i, f: (0, w_blk(i, f))),
        pl.BlockSpec((d, tf), lambda i, f: (0, w_blk(i, f))),
        pl.BlockSpec((tf, d), lambda i, f: (w_blk(i, f), 0)),
        vec, vec,
    ] + [pl.BlockSpec((tm, d), lambda i, f: (0, 0), pipeline_mode=pl.Buffered(1))] * (2 * int(has_head))
    n_res = 1 + int(emit_bf16)
    out_specs = [pl.BlockSpec((tm, d), lambda i, f: (i, 0))] * n_res
    out_shape = [jax.ShapeDtypeStruct((n_i * tm, d), dt) for dt in (F32, BF16)[:n_res]]
    if copy_weights:
        out_specs += [pl.BlockSpec((d, tf), lambda i, f: (0, f)), pl.BlockSpec((d, tf), lambda i, f: (0, f)),
                      pl.BlockSpec((tf, d), lambda i, f: (f, 0))]
        out_shape += [jax.ShapeDtypeStruct(w.shape, BF16) for w in (wg, wu, wd)]
    outs = pl.pallas_call(
        functools.partial(_ffn_ln_kernel, emit_bf16=emit_bf16, copy_weights=copy_weights, has_head=has_head),
        grid=(n_i, n_f),
        in_specs=in_specs,
        out_specs=out_specs,
        out_shape=out_shape,
        scratch_shapes=[pltpu.VMEM((tm, d), BF16)],
        compiler_params=_cparams(("parallel", "arbitrary")),
        name="ffn_ln",
    )(x, wg, wu, wd, g, b, *(head or ()))
    return outs[0], list(outs[1:n_res]), list(outs[n_res:])


def _proj_kernel(x_ref, w_ref, wt_ref, o_ref, ot_ref):
    @pl.when(pl.program_id(1) == 0)
    def _():
        ot_ref[...] = _dot_tb(x_ref[...], wt_ref[...].astype(BF16))

    o_ref[...] = _dot_tb(x_ref[...], w_ref[...].astype(BF16))


def _proj(x, w_t, n, w_tail_t, tm=2048, tn=512):
    t, d = x.shape
    n_tail = w_tail_t.shape[0]
    return pl.pallas_call(
        _proj_kernel,
        grid=(t // tm, n // tn),
        in_specs=[
            pl.BlockSpec((tm, d), lambda i, j: (i, 0)),
            pl.BlockSpec((tn, d), lambda i, j: (j, 0)),
            pl.BlockSpec((n_tail, d), lambda i, j: (0, 0), pipeline_mode=pl.Buffered(1)),
        ],
        out_specs=[pl.BlockSpec((tm, tn), lambda i, j: (i, j)), pl.BlockSpec((tm, n_tail), lambda i, j: (i, 0))],
        out_shape=[jax.ShapeDtypeStruct((t, n), F32), jax.ShapeDtypeStruct((t, n_tail), F32)],
        compiler_params=_cparams(("parallel", "arbitrary")),
        name="proj",
    )(x, w_t, w_tail_t)


def _rel_bucket(dist):
    max_exact = REL_BUCKETS // 2
    n = jnp.maximum(dist, 0)
    nf = jnp.maximum(n, 1).astype(F32)
    large = max_exact + (jnp.log(nf / max_exact) / math.log(REL_MAX_DIST / max_exact)
                         * (REL_BUCKETS - max_exact)).astype(jnp.int32)
    large = jnp.minimum(large, REL_BUCKETS - 1)
    return jnp.where(n < max_exact, n, large)


def _bias_tab_kernel(rb_ref, o_ref):
    p = pl.program_id(0)
    blk = MOBA_BLOCK
    diff = lax.broadcasted_iota(jnp.int32, (SUBLANE, 2 * blk), 1) - blk
    for kind in range(2):
        dist = diff + kind * blk
        bucket = _rel_bucket(dist)
        for hh in range(PAIR):
            h = PAIR * p + hh
            vec = jnp.zeros(dist.shape, F32)
            for bkt in range(REL_BUCKETS):
                vec = jnp.where(bucket == bkt, rb_ref[bkt, h], vec)
            vec = vec * LOG2E
            if kind == 0:
                vec = jnp.where(dist >= 0, vec, NEG)
            rows = jnp.broadcast_to(vec[0:1], (blk, 2 * blk))
            o_ref[0, 2 * hh + kind] = pltpu.roll(rows, 0, 1, stride=1, stride_axis=0)[:, blk:]


def _bias_tab(rel_bias):
    npair = ATT_HEADS // PAIR
    return pl.pallas_call(
        _bias_tab_kernel,
        grid=(npair,),
        in_specs=[pl.BlockSpec(memory_space=pltpu.SMEM)],
        out_specs=pl.BlockSpec((1, 2 * PAIR, MOBA_BLOCK, MOBA_BLOCK), lambda p: (p, 0, 0, 0)),
        out_shape=jax.ShapeDtypeStruct((npair, 2 * PAIR, MOBA_BLOCK, MOBA_BLOCK), F32),
        compiler_params=_cparams(("arbitrary",)),
        name="bias_tab",
    )(rel_bias)


def _rwkv_prep_tile(tiles, prev_rows, first, mixm_ref, mixt_ref, w0_ref, w2_ref, a0_ref, a2_ref, g2_ref,
                    kk_ref, ka_ref, esum_ref, out):
    width = RWKV_WIDTH

    def shifted(g, mix):
        x = tiles[g]()
        prev = pltpu.roll(x, 1, 0)
        row = lax.broadcasted_iota(jnp.int32, x.shape, 0)
        last = 0.0 if first is True else jnp.where(first, 0.0, prev_rows[g]())
        prev = jnp.where(row == 0, last, prev)
        return x + (prev - x) * mix

    tail = shifted(3, mixt_ref[...])
    wd = tail[:, TAIL_WD:TAIL_WD + LANE]
    ad = tail[:, TAIL_AD:TAIL_AD + LANE]
    gd = tail[:, TAIL_GD:TAIL_GD + GD_PAD]
    y = w0_ref[...] + _dot(jnp.tanh(wd).astype(BF16), w2_ref[...])
    yield
    lw = -math.exp(-0.5) * _sigmoid(y)
    yield
    a = _sigmoid(a0_ref[...] + _dot(ad.astype(BF16), a2_ref[...]))
    yield
    g = _dot(_sigmoid(gd).astype(BF16), g2_ref[...])
    yield
    k = shifted(1, mixm_ref[:, width:2 * width])
    kk = k * kk_ref[...]
    kk2 = kk * kk
    yield
    ss = []
    for c in range(0, width, QUAD):
        ss.append(_dot(kk2[:, c:c + QUAD].astype(BF16), esum_ref[...]))
        yield
    kk = kk * jnp.minimum(lax.rsqrt(jnp.concatenate(ss, axis=1)), 1e12)
    k = k * (1.0 + (a - 1.0) * ka_ref[...])
    yield
    r = shifted(0, mixm_ref[:, 0:width])
    yield
    v = shifted(2, mixm_ref[:, 2 * width:3 * width])
    out.extend((r, k, v, lw, -kk, kk * a, g))


def _moba_kernel(rb_ref, q_ref, k_ref, v_ref, tab_ref, *refs, n_cast):
    cast_in, o_ref, cast_out = refs[:n_cast], refs[n_cast], refs[n_cast + 1:2 * n_cast + 1]
    kb_scr, vt_scr, kmean_scr = refs[2 * n_cast + 1:]
    p = pl.program_id(1)
    i0 = pl.program_id(2) * MOBA_QBLOCKS
    nb = kb_scr.shape[0]
    blk = MOBA_BLOCK

    @pl.when(pl.program_id(2) % MOBA_CAST_RUN == 0)
    def _():
        for src_ref, dst_ref in zip(cast_in, cast_out):
            dst_ref[...] = src_ref[...].astype(BF16)

    @pl.when(i0 == 0)
    def _():
        lane_k = lax.broadcasted_iota(jnp.int32, (blk, LANE), 1)
        lane_m = lax.broadcasted_iota(jnp.int32, (1, LANE), 1)
        ones = jnp.ones((ONES_ROWS, blk), BF16)
        for n in range(nb):
            kblk = k_ref[0, n * blk:(n + 1) * blk, :]
            kb_scr[n, :, :LANE] = kblk.astype(BF16)
            hot = (lane_k == n) | (lane_k == nb + n) | (lane_k == 2 * nb + n)
            kb_scr[n, :, LANE:] = jnp.where(hot, 1.0, 0.0).astype(BF16)
            kmean = jnp.mean(kblk, axis=0, keepdims=True)
            v_t = v_ref[0, n * blk:(n + 1) * blk, :].T.astype(BF16)
            for hh in range(PAIR):
                in_head = (lane_m >= hh * HEAD_DIM) & (lane_m < (hh + 1) * HEAD_DIM)
                kmean_scr[hh * nb + n:hh * nb + n + 1, :] = jnp.where(in_head, kmean, 0.0)
                vt_scr[n, hh * VT_ROWS:hh * VT_ROWS + HEAD_DIM, :] = v_t[hh * HEAD_DIM:(hh + 1) * HEAD_DIM, :]
                vt_scr[n, hh * VT_ROWS + HEAD_DIM:(hh + 1) * VT_ROWS, :] = ones

    q8 = q_ref[0] * (HEAD_DIM ** -0.5)
    qs = q8 * LOG2E
    nq = q8.shape[0]
    lane = lax.broadcasted_iota(jnp.int32, q8.shape, 1)
    nidx = lax.broadcasted_iota(jnp.int32, (nb, nq), 0)
    own = i0 + lax.broadcasted_iota(jnp.int32, (1, nq), 1) // blk
    gate_all = _dot_tb(kmean_scr[...], q8, precision=HI)

    q_tail, q_far, adj_mask = [], [], []
    for hh in range(PAIR):
        gate = gate_all[hh * nb:(hh + 1) * nb]
        avail = nidx < own
        sel = jnp.zeros((nb, nq), jnp.bool_)
        for _ in range(MOBA_TOPK):
            best = jnp.max(jnp.where(avail, gate, -jnp.inf), axis=0, keepdims=True)
            first = jnp.min(jnp.where(avail & (gate == best), nidx, nb), axis=0, keepdims=True)
            pick = nidx == first
            sel = sel | pick
            avail = avail & jnp.logical_not(pick)
        b_far = rb_ref[REL_BUCKETS - 1, PAIR * p + hh] * LOG2E
        rowb = jnp.where(sel & (nidx < own - 1), b_far, NEG)
        adj_mask.append(jnp.max(jnp.where(sel & (nidx == own - 1), 0.0, NEG), axis=0, keepdims=True))
        hi, mid, lo = _split3(rowb)
        parts = jnp.concatenate([hi.astype(F32), mid.astype(F32), lo.astype(F32),
                                 jnp.zeros((LANE - 3 * nb, nq), F32)], axis=0)
        in_head = (lane >= hh * HEAD_DIM) & (lane < (hh + 1) * HEAD_DIM)
        q_h = jnp.where(in_head, qs, 0.0).astype(BF16)
        q_tail.append(q_h)
        q_far.append(jnp.concatenate([q_h, parts.T.astype(BF16)], axis=1))

    def vt(hh, n):
        return vt_scr[n, hh * VT_ROWS:(hh + 1) * VT_ROWS, :]

    def block_chain(hh, n, scores, parts):
        s = scores()
        yield
        m_s = jnp.max(s, axis=0, keepdims=True)
        yield
        parts.append((m_s, _dot(vt(hh, n), jnp.exp2(s - m_s).astype(BF16))))

    def run_chains(gens):
        for _ in itertools.zip_longest(*gens):
            pass

    def merge(prev, parts):
        m_new = parts[0][0] if prev is None else jnp.maximum(prev[0], parts[0][0])
        for m_s, _ in parts[1:]:
            m_new = jnp.maximum(m_new, m_s)
        acc = None if prev is None else jnp.exp2(prev[0] - m_new) * prev[1]
        for m_s, acc_s in parts:
            term = jnp.exp2(m_s - m_new) * acc_s
            acc = term if acc is None else acc + term
        return m_new, acc

    def tail_blocks(hh, c):
        cols = slice(c * blk, (c + 1) * blk)
        i_c = i0 + c
        n_adj = jnp.maximum(i_c - 1, 0)
        q_c = q_tail[hh][cols]

        def own_blk():
            return _dot_tb(kb_scr[i_c, :, :LANE], q_c) + tab_ref[0, 2 * hh]

        def prev_blk():
            return _dot_tb(kb_scr[n_adj, :, :LANE], q_c) + tab_ref[0, 2 * hh + 1] + adj_mask[hh][:, cols]

        return [(i_c, own_blk), (n_adj, prev_blk)]

    tails = {(hh, c): [] for hh in range(PAIR) for c in range(MOBA_QBLOCKS)}
    run_chains(block_chain(hh, n, scores, tails[hh, c])
               for hh in range(PAIR) for c in range(MOBA_QBLOCKS) for n, scores in tail_blocks(hh, c))
    tails = {key: merge(None, parts) for key, parts in tails.items()}
    state = [tuple(jnp.concatenate([tails[hh, c][s] for c in range(MOBA_QBLOCKS)], axis=1) for s in range(2))
             for hh in range(PAIR)]

    def far_trip(first_block, n_blocks, carry):
        parts = [[] for _ in range(PAIR)]
        chains = []
        for hh in range(PAIR):
            for u in range(n_blocks):
                nc = first_block + u
                chains.append(block_chain(hh, nc, lambda nc=nc, hh=hh: _dot_tb(kb_scr[nc], q_far[hh]), parts[hh]))
        run_chains(chains)
        return tuple(x for hh in range(PAIR) for x in merge(carry[2 * hh:2 * hh + 2], parts[hh]))

    n_far = jnp.maximum(i0 + MOBA_QBLOCKS - 2, 0)
    full = n_far // MOBA_UNROLL
    flat = lax.fori_loop(0, full, lambda j, c: far_trip(j * MOBA_UNROLL, MOBA_UNROLL, c),
                         tuple(x for st in state for x in st))
    flat = lax.cond(n_far > full * MOBA_UNROLL,
                    lambda c: far_trip(full * MOBA_UNROLL, MOBA_UNROLL // 2, c), lambda c: c, flat)
    out_t = jnp.concatenate(
        [flat[2 * hh + 1][:HEAD_DIM] / flat[2 * hh + 1][HEAD_DIM:HEAD_DIM + 1] for hh in range(PAIR)], axis=0)
    o_ref[0] = out_t.T


def _moba(rel_bias, u_main, tab, to_cast=()):
    batch, seq, _ = u_main.shape
    npair = ATT_HEADS // PAIR
    nb = seq // MOBA_BLOCK
    assert 3 * nb <= LANE and nb % MOBA_QBLOCKS == 0
    assert MOBA_QBLOCKS == 2 and MOBA_UNROLL == 2 * MOBA_QBLOCKS
    kcol = ATT_WIDTH // LANE
    qrows = MOBA_QBLOCKS * MOBA_BLOCK
    nsteps = nb // MOBA_QBLOCKS
    assert nsteps % MOBA_CAST_RUN == 0
    total_steps = batch * npair * nsteps

    def cast_spec(w):
        rows, cols = w.shape
        n_blocks = min(total_steps // MOBA_CAST_RUN, rows // SUBLANE_BF16)
        assert rows % (n_blocks * SUBLANE_BF16) == 0 and total_steps % n_blocks == 0, w.shape
        run = total_steps // n_blocks
        return pl.BlockSpec((rows // n_blocks, cols), lambda b, p, i: (((b * npair + p) * nsteps + i) // run, 0))

    cast_specs = [cast_spec(w) for w in to_cast]
    outs = pl.pallas_call(
        functools.partial(_moba_kernel, n_cast=len(to_cast)),
        grid=(batch, npair, nsteps),
        in_specs=[
            pl.BlockSpec(memory_space=pltpu.SMEM),
            pl.BlockSpec((1, qrows, LANE), lambda b, p, i: (b, i, p)),
            pl.BlockSpec((1, seq, LANE), lambda b, p, i: (b, 0, kcol + p)),
            pl.BlockSpec((1, seq, LANE), lambda b, p, i: (b, 0, 2 * kcol + p)),
            pl.BlockSpec((1, 2 * PAIR, MOBA_BLOCK, MOBA_BLOCK), lambda b, p, i: (p, 0, 0, 0)),
        ] + cast_specs,
        out_specs=[pl.BlockSpec((1, qrows, LANE), lambda b, p, i: (b, i, p))] + cast_specs,
        out_shape=[jax.ShapeDtypeStruct((batch, seq, ATT_WIDTH), F32)]
        + [jax.ShapeDtypeStruct(w.shape, BF16) for w in to_cast],
        scratch_shapes=[
            pltpu.VMEM((nb, MOBA_BLOCK, 2 * LANE), BF16),
            pltpu.VMEM((nb, PAIR * VT_ROWS, MOBA_BLOCK), BF16),
            pltpu.VMEM((PAIR * nb, LANE), F32),
        ],
        compiler_params=_cparams(("arbitrary", "arbitrary", "arbitrary")),
        name="moba",
    )(rel_bias, u_main, u_main, u_main, tab, *to_cast)
    return outs[0], list(outs[1:])


def _rwkv_chunk_kernel(*refs):
    next_tiles, next_prev, first_tiles, prep_params = refs[0:4], refs[4:8], refs[8:12], refs[12:22]
    rk_ref, gng_ref, gnb_ref, o_ref, st_scr, prep_scr = refs[22:]
    eq_ref = prep_params[-1]
    c = pl.program_id(0)
    nquad = RWKV_WIDTH // QUAD
    nbatch = o_ref.shape[0]

    def prep_chain(bi, tiles, prevs, first):
        vals = []
        yield from _rwkv_prep_tile([lambda t=t: t[bi] for t in tiles], [lambda p=p: p[bi, SUBLANE - 1:SUBLANE, :] for p in prevs],
                                   first, *prep_params, vals)
        yield
        for q, val in enumerate(vals):
            prep_scr[q, bi] = val

    @pl.when(c == 0)
    def _():
        st_scr[...] = jnp.zeros_like(st_scr)
        for _ in itertools.zip_longest(*[prep_chain(bi, first_tiles, next_prev, True) for bi in range(nbatch)]):
            pass

    eq = eq_ref[...]
    levels = int(math.log2(CHUNK))
    row_h = lax.broadcasted_iota(jnp.int32, (QUAD, QUAD), 0) // HEAD_DIM
    col_h = lax.broadcasted_iota(jnp.int32, (QUAD, QUAD), 1) // HEAD_DIM
    bdmask = row_h == col_h
    eye = (lax.broadcasted_iota(jnp.int32, (QUAD, QUAD), 0)
           == lax.broadcasted_iota(jnp.int32, (QUAD, QUAD), 1))
    t_idx = lax.broadcasted_iota(jnp.int32, (CHUNK, QUAD), 0)
    s_idx = lax.broadcasted_iota(jnp.int32, (CHUNK, QUAD), 1) % CHUNK
    strict = t_idx > s_idx
    incl = t_idx >= s_idx
    both = jnp.concatenate([strict, incl], axis=0)
    reps = QUAD // CHUNK

    def bds(x):
        xb = x.astype(BF16)
        return jnp.where(bdmask, jnp.concatenate([xb] * reps, axis=0), jnp.zeros((), BF16))

    def mm(a, b):
        return _dot(a.astype(BF16), b.astype(BF16))

    def quad_stages(bi, qd):
        sl = slice(qd * QUAD, (qd + 1) * QUAD)
        st_ref = st_scr.at[bi * nquad + qd]
        r, k, v, lw, aa, bb, g = (prep_scr[q, bi, :, sl] for q in range(7))

        cum = lw
        for sh in (1 << b for b in range(levels)):
            cum = cum + jnp.where(t_idx >= sh, pltpu.roll(cum, sh, 0), 0.0)
        e_pos = jnp.exp(cum)
        e_neg = jnp.exp(-cum)
        gamma = e_pos[CHUNK - 1:CHUNK, :]
        e_prev = jnp.where(t_idx == 0, 1.0, pltpu.roll(e_pos, 1, 0))
        a_t = aa * e_prev
        r_t = r * e_pos
        b_c = bb * e_neg
        k_c = k * e_neg
        b_h = b_c * gamma
        k_h = k_c * gamma

        lhs = jnp.concatenate([a_t, r_t], axis=0).astype(BF16)
        pb = _dot_tb(lhs, bds(b_c))
        pk = _dot_tb(lhs, bds(k_c))
        bk_t = jnp.concatenate([b_h, k_h], axis=0).T.astype(BF16)
        yield
        a_rb = jnp.where(incl, pb[CHUNK:], 0.0).astype(BF16)
        akrk = mm(jnp.where(both, pk, 0.0), bds(v))
        n = jnp.where(strict, pb[:CHUNK], 0.0)
        t_inv = jnp.where(t_idx == s_idx, 1.0, 0.0) + n
        n = mm(n, bds(n))
        yield
        for lvl in range(1, levels):
            bd_n = bds(n)
            if lvl < levels - 1:
                tn = mm(jnp.concatenate([t_inv, n], axis=0), bd_n)
                t_inv = t_inv + tn[:CHUNK]
                n = tn[CHUNK:]
            else:
                t_inv = t_inv + mm(t_inv, bd_n)
            yield
        w = mm(t_inv, bds(a_t))
        u = mm(t_inv, bds(akrk[:CHUNK]))
        yield
        q_t = r_t + mm(a_rb, bds(w))
        y0 = mm(a_rb, bds(u)) + akrk[CHUNK:]
        w0 = jnp.concatenate([w, jnp.zeros_like(w)], axis=0)
        g_mat = jnp.where(bdmask, mm(bk_t, w0), 0.0) + jnp.where(eye, gamma, 0.0)
        h_mat = jnp.where(bdmask, mm(bk_t, jnp.concatenate([u, v], axis=0)), 0.0)
        yield
        gy = mm(jnp.concatenate([g_mat, q_t], axis=0), st_ref[...])
        st_ref[...] = gy[:QUAD] + h_mat
        y = gy[QUAD:] + y0
        yield

        inv_n = 1.0 / HEAD_DIM
        stats = mm(jnp.concatenate([y, r * k * rk_ref[:, sl]], axis=0), eq)
        mu = stats[:CHUNK] * inv_n
        bonus = stats[CHUNK:] * v
        yield
        d = y - mu
        var = mm(d * d, eq) * inv_n
        yield
        yn = d * lax.rsqrt(var + GN_EPS) * gng_ref[:, sl] + gnb_ref[:, sl]
        o_ref[bi, :, sl] = (yn + bonus) * g

    chains = [quad_stages(bi, qd) for bi in range(nbatch) for qd in range(nquad)]
    chains += [prep_chain(bi, next_tiles, next_prev, False) for bi in range(nbatch)]
    for _ in itertools.zip_longest(*chains):
        pass


def _rwkv(u_main, u_tail, prep_params, r_k, gn_g, gn_b):
    batch, seq, _ = u_main.shape
    width = RWKV_WIDTH
    nchunks = seq // CHUNK
    rcol = 3 * ATT_WIDTH // width
    nxt = lambda c: jnp.minimum(c + 1, nchunks - 1)
    prev_tile = lambda c: nxt(c) * (CHUNK // SUBLANE) - 1
    const = lambda a: pl.BlockSpec(a.shape, lambda c: (0,) * a.ndim)

    def tiles(rows, row_index):
        return ([pl.BlockSpec((batch, rows, width), lambda c, g=g: (0, row_index(c), rcol + g)) for g in range(3)]
                + [pl.BlockSpec((batch, rows, TAIL_COLS), lambda c: (0, row_index(c), 0))])

    u_args = (u_main, u_main, u_main, u_tail)
    return pl.pallas_call(
        _rwkv_chunk_kernel,
        grid=(nchunks,),
        in_specs=tiles(CHUNK, nxt) + tiles(SUBLANE, prev_tile) + tiles(CHUNK, lambda c: 0)
        + [const(a) for a in prep_params] + [const(r_k), const(gn_g), const(gn_b)],
        out_specs=pl.BlockSpec((batch, CHUNK, width), lambda c: (0, c, 0)),
        out_shape=jax.ShapeDtypeStruct((batch, seq, width), F32),
        scratch_shapes=[pltpu.VMEM((batch * width // QUAD, QUAD, QUAD), F32),
                        pltpu.VMEM((7, batch, CHUNK, width), F32)],
        compiler_params=_cparams(("arbitrary",)),
        name="rwkv_chunk",
    )(*u_args, *u_args, *u_args, *prep_params, r_k, gn_g, gn_b)


def _row_pipelined(n_rows, matmul, finish):
    pending = None
    for r0 in range(0, n_rows, SUB_ROWS):
        rows = slice(r0, r0 + SUB_ROWS)
        acc = matmul(rows)
        if pending is not None:
            finish(*pending)
        pending = (rows, acc)
    finish(*pending)


def _out_ln_kernel(h_ref, att_ref, rw_ref, wa_ref, wr_ref, g_ref, b_ref, o_ref):
    def matmul(rows):
        return (_dot(att_ref[rows, :].astype(BF16), wa_ref[...])
                + _dot(rw_ref[rows, :].astype(BF16), wr_ref[...]))

    def finish(rows, mix):
        o_ref[rows, :] = _layer_norm(ALPHA * h_ref[rows, :] + mix, g_ref[...], b_ref[...])

    _row_pipelined(h_ref.shape[0], matmul, finish)


def _out_ln(h, att, rw, w_out, g, b, tm=512):
    t, d = h.shape
    wa, wr = att.shape[1], rw.shape[1]
    assert wa == wr and w_out.shape[0] == wa + wr
    row = lambda n: pl.BlockSpec((tm, n), lambda i: (i, 0))
    full = lambda shape: pl.BlockSpec(shape, lambda i: (0,) * len(shape))
    return pl.pallas_call(
        _out_ln_kernel,
        grid=(t // tm,),
        in_specs=[row(d), row(wa), row(wr),
                  pl.BlockSpec((wa, d), lambda i: (0, 0)), pl.BlockSpec((wr, d), lambda i: (1, 0)),
                  full((1, d)), full((1, d))],
        out_specs=row(d),
        out_shape=jax.ShapeDtypeStruct((t, d), F32),
        compiler_params=_cparams(("parallel",)),
        name="out_ln",
    )(h, att, rw, w_out, w_out, g, b)


def _ple_ln_kernel(h_ref, p_ref, wg_ref, bg_ref, wp_ref, g_ref, b_ref, o_ref):
    def matmul(rows):
        return (_dot(h_ref[rows, :].astype(BF16), wg_ref[...]), _dot(p_ref[rows, :].astype(BF16), wp_ref[...]))

    def finish(rows, acc):
        pre_gate, e = acc
        y = ALPHA * h_ref[rows, :] + _sigmoid(pre_gate + bg_ref[...]) * e
        o_ref[rows, :] = _layer_norm(y, g_ref[...], b_ref[...])

    _row_pipelined(h_ref.shape[0], matmul, finish)


def _ple_ln(h, p, w_gate, b_gate, w_up, g, b, tm=512):
    t, d = h.shape
    row = lambda n: pl.BlockSpec((tm, n), lambda i: (i, 0))
    full = lambda shape: pl.BlockSpec(shape, lambda i: (0,) * len(shape))
    return pl.pallas_call(
        _ple_ln_kernel,
        grid=(t // tm,),
        in_specs=[row(d), row(p.shape[1]), full(w_gate.shape), full((1, d)), full(w_up.shape),
                  full((1, d)), full((1, d))],
        out_specs=row(d),
        out_shape=jax.ShapeDtypeStruct((t, d), F32),
        compiler_params=_cparams(("parallel",)),
        name="ple_ln",
    )(h, p, w_gate, b_gate, w_up, g, b)


def _pad_rows(w, rows):
    return jnp.pad(w, ((0, rows - w.shape[0]), (0, 0)))


def _pack_tail(cols, axis):
    wd, ad, gd = jnp.split(cols, [DECAY_LORA, DECAY_LORA + AAA_LORA], axis=axis)

    def pad(x, n):
        widths = [(0, 0)] * x.ndim
        widths[axis] = (0, n - x.shape[axis])
        return jnp.pad(x, widths)

    return jnp.concatenate([pad(wd, LANE), pad(ad, LANE), pad(gd, GD_PAD)], axis=axis)


def kernel(x, p, ffn1_w_gate, ffn1_w_up, ffn1_w_down, ln1_g, ln1_b, w_in, rel_bias, shift_mix, decay_w0, decay_w2, a_a0, a_a2, gate_g2, k_k, k_a, r_k, gn_g, gn_b, w_out, ln2_g, ln2_b, ffn2_w_gate, ffn2_w_up, ffn2_w_down, ln3_g, ln3_b, ple_w_up, ple_w_gate, ple_b_gate, ln4_g, ln4_b):
    batch, seq, d = x.shape
    t = batch * seq
    row = lambda a: a.reshape(1, -1)
    main_cols = 3 * ATT_WIDTH + 3 * RWKV_WIDTH

    heads = jnp.arange(QUAD) // HEAD_DIM
    esum = (heads[:, None] == heads[None, :]).astype(BF16)
    tab = _bias_tab(rel_bias)

    h = x.reshape(t, d).astype(F32)
    for i in range(DEPTH):
        w_in_t = w_in[i].T
        later_weights = (ffn2_w_gate[i], ffn2_w_up[i], ffn2_w_down[i], w_out[i], ple_w_gate[i], ple_w_up[i])
        ln1 = (row(ln1_g[i]), row(ln1_b[i]))
        head, (head_bf16,), ffn1_bf16 = _ffn_ln(h, ffn1_w_gate[i], ffn1_w_up[i], ffn1_w_down[i], *ln1,
                                                emit_bf16=True, copy_weights=True, tf=FFN_HEAD_TF)
        h, (h_bf16,), _ = _ffn_ln(h, *ffn1_bf16, *ln1, emit_bf16=True, head=(head, head_bf16))
        u_main, u_tail = _proj(h_bf16, w_in_t, main_cols, _pack_tail(w_in_t[main_cols:], axis=0))
        mix = shift_mix[i]
        prep_params = (row(mix[:3 * RWKV_WIDTH]), row(_pack_tail(mix[3 * RWKV_WIDTH:], axis=0)),
                       row(decay_w0[i]), _pad_rows(decay_w2[i], LANE).astype(BF16), row(a_a0[i]),
                       _pad_rows(a_a2[i], LANE).astype(BF16), _pad_rows(gate_g2[i], GD_PAD).astype(BF16),
                       row(k_k[i]), row(k_a[i]), esum)
        u_main3 = u_main.reshape(batch, seq, main_cols)
        att, (wg2, wu2, wd2, wo, wpg, wpu) = _moba(rel_bias, u_main3, tab, to_cast=later_weights)
        rw = _rwkv(u_main3, u_tail.reshape(batch, seq, TAIL_COLS), prep_params,
                   row(r_k[i]), row(gn_g[i]), row(gn_b[i]))
        h = _out_ln(h, att.reshape(t, ATT_WIDTH), rw.reshape(t, RWKV_WIDTH), wo,
                    row(ln2_g[i]), row(ln2_b[i]))
        h, _, _ = _ffn_ln(h, wg2, wu2, wd2, row(ln3_g[i]), row(ln3_b[i]))
        h = _ple_ln(h, p[i].reshape(t, PLE_DIM).astype(F32), wpg, row(ple_b_gate[i]), wpu,
                    row(ln4_g[i]), row(ln4_b[i]))
    return h.reshape(batch, seq, d).astype(x.dtype)
```

```python
import functools
import itertools
import math

import jax
import jax.numpy as jnp
from jax import lax
from jax.experimental import pallas as pl
from jax.experimental.pallas import tpu as pltpu

DEPTH = 1
HEAD_DIM = 64
ATT_HEADS = 16
RWKV_HEADS = 16
ATT_WIDTH = ATT_HEADS * HEAD_DIM
RWKV_WIDTH = RWKV_HEADS * HEAD_DIM
MOBA_BLOCK = 256
MOBA_TOPK = 3
MOBA_UNROLL = 4
MOBA_QBLOCKS = 2
MOBA_CAST_RUN = 4
LOG2E = 1.4426950408889634
REL_BUCKETS = 32
REL_MAX_DIST = 128
DECAY_LORA = 64
AAA_LORA = 64
PLE_DIM = 256
LN_EPS = 1e-5
GN_EPS = 64e-5
NEG = -1e30
ALPHA = (2 * DEPTH) ** 0.25

LANE = 128
SUBLANE = 8
SUBLANE_BF16 = 16
PAIR = LANE // HEAD_DIM
CHUNK = 64
QUAD = 256
ONES_ROWS = SUBLANE_BF16
VT_ROWS = HEAD_DIM + ONES_ROWS
FFN_HEAD_TM = 1024
FFN_HEAD_TF = 256
SUB_ROWS = 128
TAIL_COLS = 512
TAIL_WD, TAIL_AD, TAIL_GD = 0, 128, 256
GD_PAD = 256
VMEM_LIMIT = 56 * 1024 * 1024

F32 = jnp.float32
BF16 = jnp.bfloat16
HI = lax.Precision.HIGHEST


def _cparams(sem):
    return pltpu.CompilerParams(dimension_semantics=sem, vmem_limit_bytes=VMEM_LIMIT)


def _layer_norm(y, g, b):
    mu = jnp.mean(y, axis=-1, keepdims=True)
    d = y - mu
    var = jnp.mean(d * d, axis=-1, keepdims=True)
    return d * lax.rsqrt(var + LN_EPS) * g + b


def _sigmoid(x):
    return 0.5 * jnp.tanh(0.5 * x) + 0.5


def _dot(a, b):
    return jnp.dot(a, b, preferred_element_type=F32)


def _dot_tb(a, b, precision=None):
    return lax.dot_general(a, b, (((1,), (1,)), ((), ())), preferred_element_type=F32,
                           precision=precision)


def _split3(x):
    hi = x.astype(BF16)
    r1 = x - hi.astype(F32)
    mid = r1.astype(BF16)
    lo = (r1 - mid.astype(F32)).astype(BF16)
    return hi, mid, lo


def _ffn_ln_kernel(*refs, n_f, n_pass, emit_bf16, copy_weights):
    x_ref, wg_ref, wu_ref, wd_ref, g_ref, b_ref = refs[:6]
    rest = list(refs[6:])
    head_ref, head_bf16_ref = (rest.pop(0), rest.pop(0)) if n_pass else (None, None)
    o_ref = rest.pop(0)
    ob_ref = rest.pop(0) if emit_bf16 else None
    w_copies = [rest.pop(0) for _ in range(3)] if copy_weights else []
    (xb_ref,) = rest
    step = pl.program_id(0) - n_pass
    f = lax.rem(step, n_f)
    last = n_f - 1

    def compute():
        @pl.when(f == 0)
        def _():
            xb_ref[...] = x_ref[...].astype(BF16)
            o_ref[...] = jnp.zeros_like(o_ref)

        weights = [w_ref[...] for w_ref in (wg_ref, wu_ref, wd_ref)]
        if copy_weights:
            weights = [w.astype(BF16) for w in weights]
            for dst_ref, w in zip(w_copies, weights):
                dst_ref[...] = w
        wg, wu, wd = weights
        xb = xb_ref[...]
        gate = _dot(xb, wg)
        up = _dot(xb, wu)
        act = gate * jax.nn.sigmoid(gate) * up
        o_ref[...] += _dot(act.astype(BF16), wd)

        @pl.when(f == last)
        def _():
            y = _layer_norm(ALPHA * x_ref[...] + 0.5 * o_ref[...], g_ref[...], b_ref[...])
            o_ref[...] = y
            if emit_bf16:
                ob_ref[...] = y.astype(BF16)

    if n_pass == 0:
        compute()
        return
    pl.when(step >= 0)(compute)

    @pl.when(step < 0)
    def _():
        o_ref[...] = head_ref[...]
        ob_ref[...] = head_bf16_ref[...]


def _ffn_ln(x, wg, wu, wd, g, b, emit_bf16=False, copy_weights=False, head=None, tm=512, tf=512):
    t, d = x.shape
    ff = wg.shape[1]
    n_i, n_f = (1 if copy_weights else t // tm), ff // tf
    n_pass = head[0].shape[0] // tm if head else 0
    assert emit_bf16 or not head

    def row_f(s):
        c = jnp.maximum(s - n_pass, 0)
        return n_pass + c // n_f, c % n_f

    resident = dict(pipeline_mode=pl.Buffered(1)) if n_i == 1 else {}
    vec = pl.BlockSpec((1, d), lambda s: (0, 0))
    in_specs = [
        pl.BlockSpec((tm, d), lambda s: (row_f(s)[0], 0), **resident),
        pl.BlockSpec((d, tf), lambda s: (0, row_f(s)[1])),
        pl.BlockSpec((d, tf), lambda s: (0, row_f(s)[1])),
        pl.BlockSpec((tf, d), lambda s: (row_f(s)[1], 0)),
        vec, vec,
    ] + [pl.BlockSpec((tm, d), lambda s: (jnp.minimum(s, n_pass - 1), 0))] * (2 * int(n_pass > 0))
    n_res = 1 + int(emit_bf16)
    out_specs = [pl.BlockSpec((tm, d), lambda s: (jnp.where(s < n_pass, s, row_f(s)[0]), 0), **resident)] * n_res
    out_shape = [jax.ShapeDtypeStruct((n_i * tm, d), dt) for dt in (F32, BF16)[:n_res]]
    if copy_weights:
        out_specs += [pl.BlockSpec(spec.block_shape, spec.index_map) for spec in in_specs[1:4]]
        out_shape += [jax.ShapeDtypeStruct(w.shape, BF16) for w in (wg, wu, wd)]
    outs = pl.pallas_call(
        functools.partial(_ffn_ln_kernel, n_f=n_f, n_pass=n_pass, emit_bf16=emit_bf16, copy_weights=copy_weights),
        grid=(n_pass + (n_i - n_pass) * n_f,),
        in_specs=in_specs,
        out_specs=out_specs,
        out_shape=out_shape,
        scratch_shapes=[pltpu.VMEM((tm, d), BF16)],
        compiler_params=_cparams(("arbitrary",)),
        name="ffn_ln",
    )(x, wg, wu, wd, g, b, *(head or ()))
    return outs[0], list(outs[1:n_res]), list(outs[n_res:])


def _proj_kernel(x_ref, w_ref, wt_ref, o_ref, ot_ref):
    @pl.when(pl.program_id(1) == 0)
    def _():
        ot_ref[...] = _dot_tb(x_ref[...], wt_ref[...].astype(BF16))

    o_ref[...] = _dot_tb(x_ref[...], w_ref[...].astype(BF16))


def _proj(x, w_t, n, w_tail_t, tm=2048, tn=512):
    t, d = x.shape
    n_tail = w_tail_t.shape[0]
    return pl.pallas_call(
        _proj_kernel,
        grid=(t // tm, n // tn),
        in_specs=[
            pl.BlockSpec((tm, d), lambda i, j: (i, 0)),
            pl.BlockSpec((tn, d), lambda i, j: (j, 0)),
            pl.BlockSpec((n_tail, d), lambda i, j: (0, 0), pipeline_mode=pl.Buffered(1)),
        ],
        out_specs=[pl.BlockSpec((tm, tn), lambda i, j: (i, j)), pl.BlockSpec((tm, n_tail), lambda i, j: (i, 0))],
        out_shape=[jax.ShapeDtypeStruct((t, n), F32), jax.ShapeDtypeStruct((t, n_tail), F32)],
        compiler_params=_cparams(("parallel", "arbitrary")),
        name="proj",
    )(x, w_t, w_tail_t)


def _rel_bucket(dist):
    max_exact = REL_BUCKETS // 2
    n = jnp.maximum(dist, 0)
    nf = jnp.maximum(n, 1).astype(F32)
    large = max_exact + (jnp.log(nf / max_exact) / math.log(REL_MAX_DIST / max_exact)
                         * (REL_BUCKETS - max_exact)).astype(jnp.int32)
    large = jnp.minimum(large, REL_BUCKETS - 1)
    return jnp.where(n < max_exact, n, large)


def _bias_tab_kernel(rb_ref, o_ref):
    p = pl.program_id(0)
    blk = MOBA_BLOCK
    diff = lax.broadcasted_iota(jnp.int32, (SUBLANE, 2 * blk), 1) - blk
    for kind in range(2):
        dist = diff + kind * blk
        bucket = _rel_bucket(dist)
        for hh in range(PAIR):
            h = PAIR * p + hh
            vec = jnp.zeros(dist.shape, F32)
            for bkt in range(REL_BUCKETS):
                vec = jnp.where(bucket == bkt, rb_ref[bkt, h], vec)
            vec = vec * LOG2E
            if kind == 0:
                vec = jnp.where(dist >= 0, vec, NEG)
            rows = jnp.broadcast_to(vec[0:1], (blk, 2 * blk))
            o_ref[0, 2 * hh + kind] = pltpu.roll(rows, 0, 1, stride=1, stride_axis=0)[:, blk:]


def _bias_tab(rel_bias):
    npair = ATT_HEADS // PAIR
    return pl.pallas_call(
        _bias_tab_kernel,
        grid=(npair,),
        in_specs=[pl.BlockSpec(memory_space=pltpu.SMEM)],
        out_specs=pl.BlockSpec((1, 2 * PAIR, MOBA_BLOCK, MOBA_BLOCK), lambda p: (p, 0, 0, 0)),
        out_shape=jax.ShapeDtypeStruct((npair, 2 * PAIR, MOBA_BLOCK, MOBA_BLOCK), F32),
        compiler_params=_cparams(("arbitrary",)),
        name="bias_tab",
    )(rel_bias)


def _rwkv_prep_tile(tiles, prev_rows, first, mixm_ref, mixt_ref, w0_ref, w2_ref, a0_ref, a2_ref, g2_ref,
                    kk_ref, ka_ref, esum_ref, out):
    width = RWKV_WIDTH

    def shifted(g, mix):
        x = tiles[g]()
        prev = pltpu.roll(x, 1, 0)
        row = lax.broadcasted_iota(jnp.int32, x.shape, 0)
        last = 0.0 if first is True else jnp.where(first, 0.0, prev_rows[g]())
        prev = jnp.where(row == 0, last, prev)
        return x + (prev - x) * mix

    tail = shifted(3, mixt_ref[...])
    wd = tail[:, TAIL_WD:TAIL_WD + LANE]
    ad = tail[:, TAIL_AD:TAIL_AD + LANE]
    gd = tail[:, TAIL_GD:TAIL_GD + GD_PAD]
    y = w0_ref[...] + _dot(jnp.tanh(wd).astype(BF16), w2_ref[...])
    yield
    lw = -math.exp(-0.5) * _sigmoid(y)
    yield
    a = _sigmoid(a0_ref[...] + _dot(ad.astype(BF16), a2_ref[...]))
    yield
    g = _dot(_sigmoid(gd).astype(BF16), g2_ref[...])
    yield
    k = shifted(1, mixm_ref[:, width:2 * width])
    kk = k * kk_ref[...]
    kk2 = kk * kk
    yield
    ss = []
    for c in range(0, width, QUAD):
        ss.append(_dot(kk2[:, c:c + QUAD].astype(BF16), esum_ref[...]))
        yield
    kk = kk * jnp.minimum(lax.rsqrt(jnp.concatenate(ss, axis=1)), 1e12)
    k = k * (1.0 + (a - 1.0) * ka_ref[...])
    yield
    r = shifted(0, mixm_ref[:, 0:width])
    yield
    v = shifted(2, mixm_ref[:, 2 * width:3 * width])
    out.extend((r, k, v, lw, -kk, kk * a, g))


def _moba_kernel(rb_ref, q_ref, k_ref, v_ref, tab_ref, *refs, n_cast):
    cast_in, o_ref, cast_out = refs[:n_cast], refs[n_cast], refs[n_cast + 1:2 * n_cast + 1]
    kb_scr, vt_scr, kmean_scr = refs[2 * n_cast + 1:]
    p = pl.program_id(1)
    i0 = pl.program_id(2) * MOBA_QBLOCKS
    nb = kb_scr.shape[0]
    blk = MOBA_BLOCK

    @pl.when(pl.program_id(2) % MOBA_CAST_RUN == 0)
    def _():
        for src_ref, dst_ref in zip(cast_in, cast_out):
            dst_ref[...] = src_ref[...].astype(BF16)

    @pl.when(i0 == 0)
    def _():
        lane_k = lax.broadcasted_iota(jnp.int32, (blk, LANE), 1)
        lane_m = lax.broadcasted_iota(jnp.int32, (1, LANE), 1)
        ones = jnp.ones((ONES_ROWS, blk), BF16)
        for n in range(nb):
            kblk = k_ref[0, n * blk:(n + 1) * blk, :]
            kb_scr[n, :, :LANE] = kblk.astype(BF16)
            hot = (lane_k == n) | (lane_k == nb + n) | (lane_k == 2 * nb + n)
            kb_scr[n, :, LANE:] = jnp.where(hot, 1.0, 0.0).astype(BF16)
            kmean = jnp.mean(kblk, axis=0, keepdims=True)
            v_t = v_ref[0, n * blk:(n + 1) * blk, :].T.astype(BF16)
            for hh in range(PAIR):
                in_head = (lane_m >= hh * HEAD_DIM) & (lane_m < (hh + 1) * HEAD_DIM)
                kmean_scr[hh * nb + n:hh * nb + n + 1, :] = jnp.where(in_head, kmean, 0.0)
                vt_scr[n, hh * VT_ROWS:hh * VT_ROWS + HEAD_DIM, :] = v_t[hh * HEAD_DIM:(hh + 1) * HEAD_DIM, :]
                vt_scr[n, hh * VT_ROWS + HEAD_DIM:(hh + 1) * VT_ROWS, :] = ones

    q8 = q_ref[0] * (HEAD_DIM ** -0.5)
    qs = q8 * LOG2E
    nq = q8.shape[0]
    lane = lax.broadcasted_iota(jnp.int32, q8.shape, 1)
    nidx = lax.broadcasted_iota(jnp.int32, (nb, nq), 0)
    own = i0 + lax.broadcasted_iota(jnp.int32, (1, nq), 1) // blk
    gate_all = _dot_tb(kmean_scr[...], q8, precision=HI)

    q_tail, q_far, adj_mask = [], [], []
    for hh in range(PAIR):
        gate = gate_all[hh * nb:(hh + 1) * nb]
        avail = nidx < own
        sel = jnp.zeros((nb, nq), jnp.bool_)
        for _ in range(MOBA_TOPK):
            best = jnp.max(jnp.where(avail, gate, -jnp.inf), axis=0, keepdims=True)
            first = jnp.min(jnp.where(avail & (gate == best), nidx, nb), axis=0, keepdims=True)
            pick = nidx == first
            sel = sel | pick
            avail = avail & jnp.logical_not(pick)
        b_far = rb_ref[REL_BUCKETS - 1, PAIR * p + hh] * LOG2E
        rowb = jnp.where(sel & (nidx < own - 1), b_far, NEG)
        adj_mask.append(jnp.max(jnp.where(sel & (nidx == own - 1), 0.0, NEG), axis=0, keepdims=True))
        hi, mid, lo = _split3(rowb)
        parts = jnp.concatenate([hi.astype(F32), mid.astype(F32), lo.astype(F32),
                                 jnp.zeros((LANE - 3 * nb, nq), F32)], axis=0)
        in_head = (lane >= hh * HEAD_DIM) & (lane < (hh + 1) * HEAD_DIM)
        q_h = jnp.where(in_head, qs, 0.0).astype(BF16)
        q_tail.append(q_h)
        q_far.append(jnp.concatenate([q_h, parts.T.astype(BF16)], axis=1))

    def vt(hh, n):
        return vt_scr[n, hh * VT_ROWS:(hh + 1) * VT_ROWS, :]

    def block_chain(hh, n, scores, parts):
        s = scores()
        yield
        m_s = jnp.max(s, axis=0, keepdims=True)
        yield
        parts.append((m_s, _dot(vt(hh, n), jnp.exp2(s - m_s).astype(BF16))))

    def run_chains(gens):
        for _ in itertools.zip_longest(*gens):
            pass

    def merge(prev, parts):
        m_new = parts[0][0] if prev is None else jnp.maximum(prev[0], parts[0][0])
        for m_s, _ in parts[1:]:
            m_new = jnp.maximum(m_new, m_s)
        acc = None if prev is None else jnp.exp2(prev[0] - m_new) * prev[1]
        for m_s, acc_s in parts:
            term = jnp.exp2(m_s - m_new) * acc_s
            acc = term if acc is None else acc + term
        return m_new, acc

    def tail_blocks(hh, c):
        cols = slice(c * blk, (c + 1) * blk)
        i_c = i0 + c
        n_adj = jnp.maximum(i_c - 1, 0)
        q_c = q_tail[hh][cols]

        def own_blk():
            return _dot_tb(kb_scr[i_c, :, :LANE], q_c) + tab_ref[0, 2 * hh]

        def prev_blk():
            return _dot_tb(kb_scr[n_adj, :, :LANE], q_c) + tab_ref[0, 2 * hh + 1] + adj_mask[hh][:, cols]

        return [(i_c, own_blk), (n_adj, prev_blk)]

    tails = {(hh, c): [] for hh in range(PAIR) for c in range(MOBA_QBLOCKS)}
    run_chains(block_chain(hh, n, scores, tails[hh, c])
               for hh in range(PAIR) for c in range(MOBA_QBLOCKS) for n, scores in tail_blocks(hh, c))
    tails = {key: merge(None, parts) for key, parts in tails.items()}
    state = [tuple(jnp.concatenate([tails[hh, c][s] for c in range(MOBA_QBLOCKS)], axis=1) for s in range(2))
             for hh in range(PAIR)]

    def far_trip(first_block, n_blocks, carry):
        parts = [[] for _ in range(PAIR)]
        chains = []
        for hh in range(PAIR):
            for u in range(n_blocks):
                nc = first_block + u
                chains.append(block_chain(hh, nc, lambda nc=nc, hh=hh: _dot_tb(kb_scr[nc], q_far[hh]), parts[hh]))
        run_chains(chains)
        return tuple(x for hh in range(PAIR) for x in merge(carry[2 * hh:2 * hh + 2], parts[hh]))

    n_far = jnp.maximum(i0 + MOBA_QBLOCKS - 2, 0)
    full = n_far // MOBA_UNROLL
    flat = lax.fori_loop(0, full, lambda j, c: far_trip(j * MOBA_UNROLL, MOBA_UNROLL, c),
                         tuple(x for st in state for x in st))
    flat = lax.cond(n_far > full * MOBA_UNROLL,
                    lambda c: far_trip(full * MOBA_UNROLL, MOBA_UNROLL // 2, c), lambda c: c, flat)
    out_t = jnp.concatenate(
        [flat[2 * hh + 1][:HEAD_DIM] / flat[2 * hh + 1][HEAD_DIM:HEAD_DIM + 1] for hh in range(PAIR)], axis=0)
    o_ref[0] = out_t.T


def _moba(rel_bias, u_main, tab, to_cast=()):
    batch, seq, _ = u_main.shape
    npair = ATT_HEADS // PAIR
    nb = seq // MOBA_BLOCK
    assert 3 * nb <= LANE and nb % MOBA_QBLOCKS == 0
    assert MOBA_QBLOCKS == 2 and MOBA_UNROLL == 2 * MOBA_QBLOCKS
    kcol = ATT_WIDTH // LANE
    qrows = MOBA_QBLOCKS * MOBA_BLOCK
    nsteps = nb // MOBA_QBLOCKS
    assert nsteps % MOBA_CAST_RUN == 0
    total_steps = batch * npair * nsteps

    def cast_spec(w):
        rows, cols = w.shape
        n_blocks = min(total_steps // MOBA_CAST_RUN, rows // SUBLANE_BF16)
        assert rows % (n_blocks * SUBLANE_BF16) == 0 and total_steps % n_blocks == 0, w.shape
        run = total_steps // n_blocks
        return pl.BlockSpec((rows // n_blocks, cols), lambda b, p, i: (((b * npair + p) * nsteps + i) // run, 0))

    cast_specs = [cast_spec(w) for w in to_cast]
    outs = pl.pallas_call(
        functools.partial(_moba_kernel, n_cast=len(to_cast)),
        grid=(batch, npair, nsteps),
        in_specs=[
            pl.BlockSpec(memory_space=pltpu.SMEM),
            pl.BlockSpec((1, qrows, LANE), lambda b, p, i: (b, i, p)),
            pl.BlockSpec((1, seq, LANE), lambda b, p, i: (b, 0, kcol + p)),
            pl.BlockSpec((1, seq, LANE), lambda b, p, i: (b, 0, 2 * kcol + p)),
            pl.BlockSpec((1, 2 * PAIR, MOBA_BLOCK, MOBA_BLOCK), lambda b, p, i: (p, 0, 0, 0)),
        ] + cast_specs,
        out_specs=[pl.BlockSpec((1, qrows, LANE), lambda b, p, i: (b, i, p))] + cast_specs,
        out_shape=[jax.ShapeDtypeStruct((batch, seq, ATT_WIDTH), F32)]
        + [jax.ShapeDtypeStruct(w.shape, BF16) for w in to_cast],
        scratch_shapes=[
            pltpu.VMEM((nb, MOBA_BLOCK, 2 * LANE), BF16),
            pltpu.VMEM((nb, PAIR * VT_ROWS, MOBA_BLOCK), BF16),
            pltpu.VMEM((PAIR * nb, LANE), F32),
        ],
        compiler_params=_cparams(("arbitrary", "arbitrary", "arbitrary")),
        name="moba",
    )(rel_bias, u_main, u_main, u_main, tab, *to_cast)
    return outs[0], list(outs[1:])


def _rwkv_chunk_kernel(*refs):
    next_tiles, next_prev, first_tiles, prep_params = refs[0:4], refs[4:8], refs[8:12], refs[12:22]
    rk_ref, gng_ref, gnb_ref, o_ref, st_scr, prep_scr = refs[22:]
    eq_ref = prep_params[-1]
    c = pl.program_id(0)
    nquad = RWKV_WIDTH // QUAD
    nbatch = o_ref.shape[0]

    def prep_chain(bi, tiles, prevs, first):
        vals = []
        yield from _rwkv_prep_tile([lambda t=t: t[bi] for t in tiles], [lambda p=p: p[bi, SUBLANE - 1:SUBLANE, :] for p in prevs],
                                   first, *prep_params, vals)
        yield
        for q, val in enumerate(vals):
            prep_scr[q, bi] = val

    @pl.when(c == 0)
    def _():
        st_scr[...] = jnp.zeros_like(st_scr)
        for _ in itertools.zip_longest(*[prep_chain(bi, first_tiles, next_prev, True) for bi in range(nbatch)]):
            pass

    eq = eq_ref[...]
    levels = int(math.log2(CHUNK))
    row_h = lax.broadcasted_iota(jnp.int32, (QUAD, QUAD), 0) // HEAD_DIM
    col_h = lax.broadcasted_iota(jnp.int32, (QUAD, QUAD), 1) // HEAD_DIM
    bdmask = row_h == col_h
    eye = (lax.broadcasted_iota(jnp.int32, (QUAD, QUAD), 0)
           == lax.broadcasted_iota(jnp.int32, (QUAD, QUAD), 1))
    t_idx = lax.broadcasted_iota(jnp.int32, (CHUNK, QUAD), 0)
    s_idx = lax.broadcasted_iota(jnp.int32, (CHUNK, QUAD), 1) % CHUNK
    strict = t_idx > s_idx
    incl = t_idx >= s_idx
    both = jnp.concatenate([strict, incl], axis=0)
    reps = QUAD // CHUNK

    def bds(x):
        xb = x.astype(BF16)
        return jnp.where(bdmask, jnp.concatenate([xb] * reps, axis=0), jnp.zeros((), BF16))

    def mm(a, b):
        return _dot(a.astype(BF16), b.astype(BF16))

    def quad_stages(bi, qd):
        sl = slice(qd * QUAD, (qd + 1) * QUAD)
        st_ref = st_scr.at[bi * nquad + qd]
        r, k, v, lw, aa, bb, g = (prep_scr[q, bi, :, sl] for q in range(7))

        cum = lw
        for sh in (1 << b for b in range(levels)):
            cum = cum + jnp.where(t_idx >= sh, pltpu.roll(cum, sh, 0), 0.0)
        e_pos = jnp.exp(cum)
        e_neg = jnp.exp(-cum)
        gamma = e_pos[CHUNK - 1:CHUNK, :]
        e_prev = jnp.where(t_idx == 0, 1.0, pltpu.roll(e_pos, 1, 0))
        a_t = aa * e_prev
        r_t = r * e_pos
        b_c = bb * e_neg
        k_c = k * e_neg
        b_h = b_c * gamma
        k_h = k_c * gamma

        lhs = jnp.concatenate([a_t, r_t], axis=0).astype(BF16)
        pb = _dot_tb(lhs, bds(b_c))
        pk = _dot_tb(lhs, bds(k_c))
        bk_t = jnp.concatenate([b_h, k_h], axis=0).T.astype(BF16)
        yield
        a_rb = jnp.where(incl, pb[CHUNK:], 0.0).astype(BF16)
        akrk = mm(jnp.where(both, pk, 0.0), bds(v))
        n = jnp.where(strict, pb[:CHUNK], 0.0)
        t_inv = jnp.where(t_idx == s_idx, 1.0, 0.0) + n
        n = mm(n, bds(n))
        yield
        for lvl in range(1, levels):
            bd_n = bds(n)
            if lvl < levels - 1:
                tn = mm(jnp.concatenate([t_inv, n], axis=0), bd_n)
                t_inv = t_inv + tn[:CHUNK]
                n = tn[CHUNK:]
            else:
                t_inv = t_inv + mm(t_inv, bd_n)
            yield
        w = mm(t_inv, bds(a_t))
        u = mm(t_inv, bds(akrk[:CHUNK]))
        yield
        q_t = r_t + mm(a_rb, bds(w))
        y0 = mm(a_rb, bds(u)) + akrk[CHUNK:]
        w0 = jnp.concatenate([w, jnp.zeros_like(w)], axis=0)
        g_mat = jnp.where(bdmask, mm(bk_t, w0), 0.0) + jnp.where(eye, gamma, 0.0)
        h_mat = jnp.where(bdmask, mm(bk_t, jnp.concatenate([u, v], axis=0)), 0.0)
        yield
        gy = mm(jnp.concatenate([g_mat, q_t], axis=0), st_ref[...])
        st_ref[...] = gy[:QUAD] + h_mat
        y = gy[QUAD:] + y0
        yield

        inv_n = 1.0 / HEAD_DIM
        stats = mm(jnp.concatenate([y, r * k * rk_ref[:, sl]], axis=0), eq)
        mu = stats[:CHUNK] * inv_n
        bonus = stats[CHUNK:] * v
        yield
        d = y - mu
        var = mm(d * d, eq) * inv_n
        yield
        yn = d * lax.rsqrt(var + GN_EPS) * gng_ref[:, sl] + gnb_ref[:, sl]
        o_ref[bi, :, sl] = (yn + bonus) * g

    chains = [quad_stages(bi, qd) for bi in range(nbatch) for qd in range(nquad)]
    chains += [prep_chain(bi, next_tiles, next_prev, False) for bi in range(nbatch)]
    for _ in itertools.zip_longest(*chains):
        pass


def _rwkv(u_main, u_tail, prep_params, r_k, gn_g, gn_b):
    batch, seq, _ = u_main.shape
    width = RWKV_WIDTH
    nchunks = seq // CHUNK
    rcol = 3 * ATT_WIDTH // width
    nxt = lambda c: jnp.minimum(c + 1, nchunks - 1)
    prev_tile = lambda c: nxt(c) * (CHUNK // SUBLANE) - 1
    const = lambda a: pl.BlockSpec(a.shape, lambda c: (0,) * a.ndim)

    def tiles(rows, row_index):
        return ([pl.BlockSpec((batch, rows, width), lambda c, g=g: (0, row_index(c), rcol + g)) for g in range(3)]
                + [pl.BlockSpec((batch, rows, TAIL_COLS), lambda c: (0, row_index(c), 0))])

    u_args = (u_main, u_main, u_main, u_tail)
    return pl.pallas_call(
        _rwkv_chunk_kernel,
        grid=(nchunks,),
        in_specs=tiles(CHUNK, nxt) + tiles(SUBLANE, prev_tile) + tiles(CHUNK, lambda c: 0)
        + [const(a) for a in prep_params] + [const(r_k), const(gn_g), const(gn_b)],
        out_specs=pl.BlockSpec((batch, CHUNK, width), lambda c: (0, c, 0)),
        out_shape=jax.ShapeDtypeStruct((batch, seq, width), F32),
        scratch_shapes=[pltpu.VMEM((batch * width // QUAD, QUAD, QUAD), F32),
                        pltpu.VMEM((7, batch, CHUNK, width), F32)],
        compiler_params=_cparams(("arbitrary",)),
        name="rwkv_chunk",
    )(*u_args, *u_args, *u_args, *prep_params, r_k, gn_g, gn_b)


def _row_pipelined(n_rows, matmul, finish):
    pending = None
    for r0 in range(0, n_rows, SUB_ROWS):
        rows = slice(r0, r0 + SUB_ROWS)
        acc = matmul(rows)
        if pending is not None:
            finish(*pending)
        pending = (rows, acc)
    finish(*pending)


def _out_ln_kernel(h_ref, att_ref, rw_ref, wa_ref, wr_ref, g_ref, b_ref, o_ref):
    def matmul(rows):
        return (_dot(att_ref[rows, :].astype(BF16), wa_ref[...])
                + _dot(rw_ref[rows, :].astype(BF16), wr_ref[...]))

    def finish(rows, mix):
        o_ref[rows, :] = _layer_norm(ALPHA * h_ref[rows, :] + mix, g_ref[...], b_ref[...])

    _row_pipelined(h_ref.shape[0], matmul, finish)


def _out_ln(h, att, rw, w_out, g, b, tm=512):
    t, d = h.shape
    wa, wr = att.shape[1], rw.shape[1]
    assert wa == wr and w_out.shape[0] == wa + wr
    row = lambda n: pl.BlockSpec((tm, n), lambda i: (i, 0))
    full = lambda shape: pl.BlockSpec(shape, lambda i: (0,) * len(shape))
    return pl.pallas_call(
        _out_ln_kernel,
        grid=(t // tm,),
        in_specs=[row(d), row(wa), row(wr),
                  pl.BlockSpec((wa, d), lambda i: (0, 0)), pl.BlockSpec((wr, d), lambda i: (1, 0)),
                  full((1, d)), full((1, d))],
        out_specs=row(d),
        out_shape=jax.ShapeDtypeStruct((t, d), F32),
        compiler_params=_cparams(("parallel",)),
        name="out_ln",
    )(h, att, rw, w_out, w_out, g, b)


def _ple_ln_kernel(h_ref, p_ref, wg_ref, bg_ref, wp_ref, g_ref, b_ref, o_ref):
    def matmul(rows):
        return (_dot(h_ref[rows, :].astype(BF16), wg_ref[...]), _dot(p_ref[rows, :].astype(BF16), wp_ref[...]))

    def finish(rows, acc):
        pre_gate, e = acc
        y = ALPHA * h_ref[rows, :] + _sigmoid(pre_gate + bg_ref[...]) * e
        o_ref[rows, :] = _layer_norm(y, g_ref[...], b_ref[...])

    _row_pipelined(h_ref.shape[0], matmul, finish)


def _ple_ln(h, p, w_gate, b_gate, w_up, g, b, tm=512):
    t, d = h.shape
    row = lambda n: pl.BlockSpec((tm, n), lambda i: (i, 0))
    full = lambda shape: pl.BlockSpec(shape, lambda i: (0,) * len(shape))
    return pl.pallas_call(
        _ple_ln_kernel,
        grid=(t // tm,),
        in_specs=[row(d), row(p.shape[1]), full(w_gate.shape), full((1, d)), full(w_up.shape),
                  full((1, d)), full((1, d))],
        out_specs=row(d),
        out_shape=jax.ShapeDtypeStruct((t, d), F32),
        compiler_params=_cparams(("parallel",)),
        name="ple_ln",
    )(h, p, w_gate, b_gate, w_up, g, b)


def _pad_rows(w, rows):
    return jnp.pad(w, ((0, rows - w.shape[0]), (0, 0)))


def _pack_tail(cols, axis):
    wd, ad, gd = jnp.split(cols, [DECAY_LORA, DECAY_LORA + AAA_LORA], axis=axis)

    def pad(x, n):
        widths = [(0, 0)] * x.ndim
        widths[axis] = (0, n - x.shape[axis])
        return jnp.pad(x, widths)

    return jnp.concatenate([pad(wd, LANE), pad(ad, LANE), pad(gd, GD_PAD)], axis=axis)


def kernel(x, p, ffn1_w_gate, ffn1_w_up, ffn1_w_down, ln1_g, ln1_b, w_in, rel_bias, shift_mix, decay_w0, decay_w2, a_a0, a_a2, gate_g2, k_k, k_a, r_k, gn_g, gn_b, w_out, ln2_g, ln2_b, ffn2_w_gate, ffn2_w_up, ffn2_w_down, ln3_g, ln3_b, ple_w_up, ple_w_gate, ple_b_gate, ln4_g, ln4_b):
    batch, seq, d = x.shape
    t = batch * seq
    row = lambda a: a.reshape(1, -1)
    main_cols = 3 * ATT_WIDTH + 3 * RWKV_WIDTH

    heads = jnp.arange(QUAD) // HEAD_DIM
    esum = (heads[:, None] == heads[None, :]).astype(BF16)
    tab = _bias_tab(rel_bias)

    h = x.reshape(t, d).astype(F32)
    for i in range(DEPTH):
        w_in_t = w_in[i].T
        later_weights = (ffn2_w_gate[i], ffn2_w_up[i], ffn2_w_down[i], w_out[i], ple_w_gate[i], ple_w_up[i])
        ln1 = (row(ln1_g[i]), row(ln1_b[i]))
        head, (head_bf16,), ffn1_bf16 = _ffn_ln(h, ffn1_w_gate[i], ffn1_w_up[i], ffn1_w_down[i], *ln1,
                                                emit_bf16=True, copy_weights=True, tm=FFN_HEAD_TM, tf=FFN_HEAD_TF)
        h, (h_bf16,), _ = _ffn_ln(h, *ffn1_bf16, *ln1, emit_bf16=True, head=(head, head_bf16))
        u_main, u_tail = _proj(h_bf16, w_in_t, main_cols, _pack_tail(w_in_t[main_cols:], axis=0))
        mix = shift_mix[i]
        prep_params = (row(mix[:3 * RWKV_WIDTH]), row(_pack_tail(mix[3 * RWKV_WIDTH:], axis=0)),
                       row(decay_w0[i]), _pad_rows(decay_w2[i], LANE).astype(BF16), row(a_a0[i]),
                       _pad_rows(a_a2[i], LANE).astype(BF16), _pad_rows(gate_g2[i], GD_PAD).astype(BF16),
                       row(k_k[i]), row(k_a[i]), esum)
        u_main3 = u_main.reshape(batch, seq, main_cols)
        att, (wg2, wu2, wd2, wo, wpg, wpu) = _moba(rel_bias, u_main3, tab, to_cast=later_weights)
        rw = _rwkv(u_main3, u_tail.reshape(batch, seq, TAIL_COLS), prep_params,
                   row(r_k[i]), row(gn_g[i]), row(gn_b[i]))
        h = _out_ln(h, att.reshape(t, ATT_WIDTH), rw.reshape(t, RWKV_WIDTH), wo,
                    row(ln2_g[i]), row(ln2_b[i]))
        h, _, _ = _ffn_ln(h, wg2, wu2, wd2, row(ln3_g[i]), row(ln3_b[i]))
        h = _ple_ln(h, p[i].reshape(t, PLE_DIM).astype(F32), wpg, row(ple_b_gate[i]), wpu,
                    row(ln4_g[i]), row(ln4_b[i]))
    return h.reshape(batch, seq, d).astype(x.dtype)
```

```python
import functools
import itertools
import math

import jax
import jax.numpy as jnp
from jax import lax
from jax.experimental import pallas as pl
from jax.experimental.pallas import tpu as pltpu

DEPTH = 1
HEAD_DIM = 64
ATT_HEADS = 16
RWKV_HEADS = 16
ATT_WIDTH = ATT_HEADS * HEAD_DIM
RWKV_WIDTH = RWKV_HEADS * HEAD_DIM
MOBA_BLOCK = 256
MOBA_TOPK = 3
MOBA_UNROLL = 4
MOBA_QBLOCKS = 2
MOBA_CAST_RUN = 4
LOG2E = 1.4426950408889634
REL_BUCKETS = 32
REL_MAX_DIST = 128
DECAY_LORA = 64
AAA_LORA = 64
PLE_DIM = 256
LN_EPS = 1e-5
GN_EPS = 64e-5
NEG = -1e30
ALPHA = (2 * DEPTH) ** 0.25

LANE = 128
SUBLANE = 8
SUBLANE_BF16 = 16
PAIR = LANE // HEAD_DIM
CHUNK = 64
QUAD = 256
ONES_ROWS = SUBLANE_BF16
VT_ROWS = HEAD_DIM + ONES_ROWS
FFN_HEAD_TM = 1024
FFN_HEAD_TF = 256
SUB_ROWS = 128
TAIL_COLS = 512
TAIL_WD, TAIL_AD, TAIL_GD = 0, 128, 256
GD_PAD = 256
VMEM_LIMIT = 56 * 1024 * 1024

F32 = jnp.float32
BF16 = jnp.bfloat16
HI = lax.Precision.HIGHEST


def _cparams(sem):
    return pltpu.CompilerParams(dimension_semantics=sem, vmem_limit_bytes=VMEM_LIMIT)


def _layer_norm(y, g, b):
    mu = jnp.mean(y, axis=-1, keepdims=True)
    d = y - mu
    var = jnp.mean(d * d, axis=-1, keepdims=True)
    return d * lax.rsqrt(var + LN_EPS) * g + b


def _sigmoid(x):
    return 0.5 * jnp.tanh(0.5 * x) + 0.5


def _dot(a, b):
    return jnp.dot(a, b, preferred_element_type=F32)


def _dot_tb(a, b, precision=None):
    return lax.dot_general(a, b, (((1,), (1,)), ((), ())), preferred_element_type=F32,
                           precision=precision)


def _split3(x):
    hi = x.astype(BF16)
    r1 = x - hi.astype(F32)
    mid = r1.astype(BF16)
    lo = (r1 - mid.astype(F32)).astype(BF16)
    return hi, mid, lo


def _ffn_ln_kernel(*refs, n_f, n_rows, n_pass, emit_bf16, copy_weights):
    x_ref, wg_ref, wu_ref, wd_ref, g_ref, b_ref = refs[:6]
    rest = list(refs[6:])
    head_ref, head_bf16_ref = (rest.pop(0), rest.pop(0)) if n_pass else (None, None)
    o_ref = rest.pop(0)
    ob_ref = rest.pop(0) if emit_bf16 else None
    w_copies = [rest.pop(0) for _ in range(3)] if copy_weights else []
    xb_ref = rest.pop(0)
    lagged = n_rows > 1
    acc_ref, pre_ref = rest if lagged else (o_ref, None)
    tm = o_ref.shape[0]
    step = pl.program_id(0) - n_pass
    f = lax.rem(step, n_f)
    n_compute = n_rows * n_f

    def finish(rows, pre):
        y = _layer_norm(pre, g_ref[...], b_ref[...])
        o_ref[rows, :] = y
        if emit_bf16:
            ob_ref[rows, :] = y.astype(BF16)

    def compute():
        @pl.when(f == 0)
        def _():
            xb_ref[...] = x_ref[...].astype(BF16)
            acc_ref[...] = jnp.zeros_like(acc_ref)

        if lagged:
            @pl.when(step == 0)
            def _():
                pre_ref[...] = jnp.zeros_like(pre_ref)

        weights = [w_ref[...] for w_ref in (wg_ref, wu_ref, wd_ref)]
        if copy_weights:
            weights = [w.astype(BF16) for w in weights]
            for dst_ref, w in zip(w_copies, weights):
                dst_ref[...] = w
        wg, wu, wd = weights
        xb = xb_ref[...]
        gate = _dot(xb, wg)
        up = _dot(xb, wu)
        act = gate * jax.nn.sigmoid(gate) * up
        acc_ref[...] += _dot(act.astype(BF16), wd)

        if lagged:
            n_slice = -(-tm // (n_f * SUBLANE_BF16)) * SUBLANE_BF16
            rows = pl.ds(pl.multiple_of(jnp.minimum(f * n_slice, tm - n_slice), SUBLANE_BF16), n_slice)
            finish(rows, pre_ref[rows, :])

        @pl.when(f == n_f - 1)
        def _():
            pre = ALPHA * x_ref[...] + 0.5 * acc_ref[...]
            if lagged:
                pre_ref[...] = pre
            else:
                finish(slice(None), pre)

    if not lagged and n_pass == 0:
        compute()
        return
    pl.when((step >= 0) & (step < n_compute))(compute)

    if lagged:
        @pl.when(step == n_compute)
        def _():
            finish(slice(None), pre_ref[...])

    if n_pass:
        @pl.when(step < 0)
        def _():
            o_ref[...] = head_ref[...]
            ob_ref[...] = head_bf16_ref[...]


def _ffn_ln(x, wg, wu, wd, g, b, emit_bf16=False, copy_weights=False, head=None, tm=512, tf=512):
    t, d = x.shape
    ff = wg.shape[1]
    n_i, n_f = (1 if copy_weights else t // tm), ff // tf
    n_pass = head[0].shape[0] // tm if head else 0
    assert emit_bf16 or not head

    n_rows = n_i - n_pass
    lagged = n_rows > 1
    n_compute = n_rows * n_f

    def row_f(s, lag=0):
        c = jnp.clip(s - n_pass - lag, 0, n_compute - 1)
        return n_pass + c // n_f, c % n_f

    resident = dict(pipeline_mode=pl.Buffered(1))
    lone = resident if n_i == 1 else {}
    vec = pl.BlockSpec((1, d), lambda s: (0, 0))
    in_specs = [
        pl.BlockSpec((tm, d), lambda s: (row_f(s)[0], 0), **lone),
        pl.BlockSpec((d, tf), lambda s: (0, row_f(s)[1])),
        pl.BlockSpec((d, tf), lambda s: (0, row_f(s)[1])),
        pl.BlockSpec((tf, d), lambda s: (row_f(s)[1], 0)),
        vec, vec,
    ] + [pl.BlockSpec((tm, d), lambda s: (jnp.minimum(s, n_pass - 1), 0), **resident)] * (2 * int(n_pass > 0))
    n_res = 1 + int(emit_bf16)
    out_lag = n_f if lagged else 0
    out_specs = [pl.BlockSpec((tm, d), lambda s: (jnp.where(s < n_pass, s, row_f(s, out_lag)[0]), 0), **lone)] * n_res
    out_shape = [jax.ShapeDtypeStruct((n_i * tm, d), dt) for dt in (F32, BF16)[:n_res]]
    if copy_weights:
        out_specs += [pl.BlockSpec(spec.block_shape, spec.index_map) for spec in in_specs[1:4]]
        out_shape += [jax.ShapeDtypeStruct(w.shape, BF16) for w in (wg, wu, wd)]
    outs = pl.pallas_call(
        functools.partial(_ffn_ln_kernel, n_f=n_f, n_rows=n_rows, n_pass=n_pass, emit_bf16=emit_bf16,
                          copy_weights=copy_weights),
        grid=(n_pass + n_compute + int(lagged),),
        in_specs=in_specs,
        out_specs=out_specs,
        out_shape=out_shape,
        scratch_shapes=[pltpu.VMEM((tm, d), BF16)] + [pltpu.VMEM((tm, d), F32)] * (2 * int(lagged)),
        compiler_params=_cparams(("arbitrary",)),
        name="ffn_ln",
    )(x, wg, wu, wd, g, b, *(head or ()))
    return outs[0], list(outs[1:n_res]), list(outs[n_res:])


def _proj_kernel(x_ref, w_ref, wt_ref, o_ref, ot_ref):
    @pl.when(pl.program_id(1) == 0)
    def _():
        ot_ref[...] = _dot_tb(x_ref[...], wt_ref[...].astype(BF16))

    o_ref[...] = _dot_tb(x_ref[...], w_ref[...].astype(BF16))


def _proj(x, w_t, n, w_tail_t, tm=2048, tn=512):
    t, d = x.shape
    n_tail = w_tail_t.shape[0]
    return pl.pallas_call(
        _proj_kernel,
        grid=(t // tm, n // tn),
        in_specs=[
            pl.BlockSpec((tm, d), lambda i, j: (i, 0)),
            pl.BlockSpec((tn, d), lambda i, j: (j, 0)),
            pl.BlockSpec((n_tail, d), lambda i, j: (0, 0), pipeline_mode=pl.Buffered(1)),
        ],
        out_specs=[pl.BlockSpec((tm, tn), lambda i, j: (i, j)), pl.BlockSpec((tm, n_tail), lambda i, j: (i, 0))],
        out_shape=[jax.ShapeDtypeStruct((t, n), F32), jax.ShapeDtypeStruct((t, n_tail), F32)],
        compiler_params=_cparams(("parallel", "arbitrary")),
        name="proj",
    )(x, w_t, w_tail_t)


def _rel_bucket(dist):
    max_exact = REL_BUCKETS // 2
    n = jnp.maximum(dist, 0)
    nf = jnp.maximum(n, 1).astype(F32)
    large = max_exact + (jnp.log(nf / max_exact) / math.log(REL_MAX_DIST / max_exact)
                         * (REL_BUCKETS - max_exact)).astype(jnp.int32)
    large = jnp.minimum(large, REL_BUCKETS - 1)
    return jnp.where(n < max_exact, n, large)


def _bias_tab_kernel(rb_ref, o_ref):
    p = pl.program_id(0)
    blk = MOBA_BLOCK
    diff = lax.broadcasted_iota(jnp.int32, (SUBLANE, 2 * blk), 1) - blk
    for kind in range(2):
        dist = diff + kind * blk
        bucket = _rel_bucket(dist)
        for hh in range(PAIR):
            h = PAIR * p + hh
            vec = jnp.zeros(dist.shape, F32)
            for bkt in range(REL_BUCKETS):
                vec = jnp.where(bucket == bkt, rb_ref[bkt, h], vec)
            vec = vec * LOG2E
            if kind == 0:
                vec = jnp.where(dist >= 0, vec, NEG)
            rows = jnp.broadcast_to(vec[0:1], (blk, 2 * blk))
            o_ref[0, 2 * hh + kind] = pltpu.roll(rows, 0, 1, stride=1, stride_axis=0)[:, blk:]


def _bias_tab(rel_bias):
    npair = ATT_HEADS // PAIR
    return pl.pallas_call(
        _bias_tab_kernel,
        grid=(npair,),
        in_specs=[pl.BlockSpec(memory_space=pltpu.SMEM)],
        out_specs=pl.BlockSpec((1, 2 * PAIR, MOBA_BLOCK, MOBA_BLOCK), lambda p: (p, 0, 0, 0)),
        out_shape=jax.ShapeDtypeStruct((npair, 2 * PAIR, MOBA_BLOCK, MOBA_BLOCK), F32),
        compiler_params=_cparams(("arbitrary",)),
        name="bias_tab",
    )(rel_bias)


def _rwkv_prep_tile(tiles, prev_rows, first, mixm_ref, mixt_ref, w0_ref, w2_ref, a0_ref, a2_ref, g2_ref,
                    kk_ref, ka_ref, esum_ref, out):
    width = RWKV_WIDTH

    def shifted(g, mix):
        x = tiles[g]()
        prev = pltpu.roll(x, 1, 0)
        row = lax.broadcasted_iota(jnp.int32, x.shape, 0)
        last = 0.0 if first is True else jnp.where(first, 0.0, prev_rows[g]())
        prev = jnp.where(row == 0, last, prev)
        return x + (prev - x) * mix

    tail = shifted(3, mixt_ref[...])
    wd = tail[:, TAIL_WD:TAIL_WD + LANE]
    ad = tail[:, TAIL_AD:TAIL_AD + LANE]
    gd = tail[:, TAIL_GD:TAIL_GD + GD_PAD]
    y = w0_ref[...] + _dot(jnp.tanh(wd).astype(BF16), w2_ref[...])
    yield
    lw = -math.exp(-0.5) * _sigmoid(y)
    yield
    a = _sigmoid(a0_ref[...] + _dot(ad.astype(BF16), a2_ref[...]))
    yield
    g = _dot(_sigmoid(gd).astype(BF16), g2_ref[...])
    yield
    k = shifted(1, mixm_ref[:, width:2 * width])
    kk = k * kk_ref[...]
    kk2 = kk * kk
    yield
    ss = []
    for c in range(0, width, QUAD):
        ss.append(_dot(kk2[:, c:c + QUAD].astype(BF16), esum_ref[...]))
        yield
    kk = kk * jnp.minimum(lax.rsqrt(jnp.concatenate(ss, axis=1)), 1e12)
    k = k * (1.0 + (a - 1.0) * ka_ref[...])
    yield
    r = shifted(0, mixm_ref[:, 0:width])
    yield
    v = shifted(2, mixm_ref[:, 2 * width:3 * width])
    out.extend((r, k, v, lw, -kk, kk * a, g))


def _moba_kernel(rb_ref, q_ref, k_ref, v_ref, tab_ref, *refs, n_cast):
    cast_in, o_ref, cast_out = refs[:n_cast], refs[n_cast], refs[n_cast + 1:2 * n_cast + 1]
    kb_scr, vt_scr, kmean_scr = refs[2 * n_cast + 1:]
    p = pl.program_id(1)
    i0 = pl.program_id(2) * MOBA_QBLOCKS
    nb = kb_scr.shape[0]
    blk = MOBA_BLOCK

    @pl.when(pl.program_id(2) % MOBA_CAST_RUN == 0)
    def _():
        for src_ref, dst_ref in zip(cast_in, cast_out):
            dst_ref[...] = src_ref[...].astype(BF16)

    @pl.when(i0 == 0)
    def _():
        lane_k = lax.broadcasted_iota(jnp.int32, (blk, LANE), 1)
        lane_m = lax.broadcasted_iota(jnp.int32, (1, LANE), 1)
        ones = jnp.ones((ONES_ROWS, blk), BF16)
        for n in range(nb):
            kblk = k_ref[0, n * blk:(n + 1) * blk, :]
            kb_scr[n, :, :LANE] = kblk.astype(BF16)
            hot = (lane_k == n) | (lane_k == nb + n) | (lane_k == 2 * nb + n)
            kb_scr[n, :, LANE:] = jnp.where(hot, 1.0, 0.0).astype(BF16)
            kmean = jnp.mean(kblk, axis=0, keepdims=True)
            v_t = v_ref[0, n * blk:(n + 1) * blk, :].T.astype(BF16)
            for hh in range(PAIR):
                in_head = (lane_m >= hh * HEAD_DIM) & (lane_m < (hh + 1) * HEAD_DIM)
                kmean_scr[hh * nb + n:hh * nb + n + 1, :] = jnp.where(in_head, kmean, 0.0)
                vt_scr[n, hh * VT_ROWS:hh * VT_ROWS + HEAD_DIM, :] = v_t[hh * HEAD_DIM:(hh + 1) * HEAD_DIM, :]
                vt_scr[n, hh * VT_ROWS + HEAD_DIM:(hh + 1) * VT_ROWS, :] = ones

    q8 = q_ref[0] * (HEAD_DIM ** -0.5)
    qs = q8 * LOG2E
    nq = q8.shape[0]
    lane = lax.broadcasted_iota(jnp.int32, q8.shape, 1)
    nidx = lax.broadcasted_iota(jnp.int32, (nb, nq), 0)
    own = i0 + lax.broadcasted_iota(jnp.int32, (1, nq), 1) // blk
    gate_all = _dot_tb(kmean_scr[...], q8, precision=HI)

    q_tail, q_far, adj_mask = [], [], []
    for hh in range(PAIR):
        gate = gate_all[hh * nb:(hh + 1) * nb]
        avail = nidx < own
        sel = jnp.zeros((nb, nq), jnp.bool_)
        for _ in range(MOBA_TOPK):
            best = jnp.max(jnp.where(avail, gate, -jnp.inf), axis=0, keepdims=True)
            first = jnp.min(jnp.where(avail & (gate == best), nidx, nb), axis=0, keepdims=True)
            pick = nidx == first
            sel = sel | pick
            avail = avail & jnp.logical_not(pick)
        b_far = rb_ref[REL_BUCKETS - 1, PAIR * p + hh] * LOG2E
        rowb = jnp.where(sel & (nidx < own - 1), b_far, NEG)
        adj_mask.append(jnp.max(jnp.where(sel & (nidx == own - 1), 0.0, NEG), axis=0, keepdims=True))
        hi, mid, lo = _split3(rowb)
        parts = jnp.concatenate([hi.astype(F32), mid.astype(F32), lo.astype(F32),
                                 jnp.zeros((LANE - 3 * nb, nq), F32)], axis=0)
        in_head = (lane >= hh * HEAD_DIM) & (lane < (hh + 1) * HEAD_DIM)
        q_h = jnp.where(in_head, qs, 0.0).astype(BF16)
        q_tail.append(q_h)
        q_far.append(jnp.concatenate([q_h, parts.T.astype(BF16)], axis=1))

    def vt(hh, n):
        return vt_scr[n, hh * VT_ROWS:(hh + 1) * VT_ROWS, :]

    def block_chain(hh, n, scores, parts):
        s = scores()
        yield
        m_s = jnp.max(s, axis=0, keepdims=True)
        yield
        parts.append((m_s, _dot(vt(hh, n), jnp.exp2(s - m_s).astype(BF16))))

    def run_chains(gens):
        for _ in itertools.zip_longest(*gens):
            pass

    def merge(prev, parts):
        m_new = parts[0][0] if prev is None else jnp.maximum(prev[0], parts[0][0])
        for m_s, _ in parts[1:]:
            m_new = jnp.maximum(m_new, m_s)
        acc = None if prev is None else jnp.exp2(prev[0] - m_new) * prev[1]
        for m_s, acc_s in parts:
            term = jnp.exp2(m_s - m_new) * acc_s
            acc = term if acc is None else acc + term
        return m_new, acc

    def tail_blocks(hh, c):
        cols = slice(c * blk, (c + 1) * blk)
        i_c = i0 + c
        n_adj = jnp.maximum(i_c - 1, 0)
        q_c = q_tail[hh][cols]

        def own_blk():
            return _dot_tb(kb_scr[i_c, :, :LANE], q_c) + tab_ref[0, 2 * hh]

        def prev_blk():
            return _dot_tb(kb_scr[n_adj, :, :LANE], q_c) + tab_ref[0, 2 * hh + 1] + adj_mask[hh][:, cols]

        return [(i_c, own_blk), (n_adj, prev_blk)]

    tails = {(hh, c): [] for hh in range(PAIR) for c in range(MOBA_QBLOCKS)}
    run_chains(block_chain(hh, n, scores, tails[hh, c])
               for hh in range(PAIR) for c in range(MOBA_QBLOCKS) for n, scores in tail_blocks(hh, c))
    tails = {key: merge(None, parts) for key, parts in tails.items()}
    state = [tuple(jnp.concatenate([tails[hh, c][s] for c in range(MOBA_QBLOCKS)], axis=1) for s in range(2))
             for hh in range(PAIR)]

    def far_trip(first_block, n_blocks, carry):
        parts = [[] for _ in range(PAIR)]
        chains = []
        for hh in range(PAIR):
            for u in range(n_blocks):
                nc = first_block + u
                chains.append(block_chain(hh, nc, lambda nc=nc, hh=hh: _dot_tb(kb_scr[nc], q_far[hh]), parts[hh]))
        run_chains(chains)
        return tuple(x for hh in range(PAIR) for x in merge(carry[2 * hh:2 * hh + 2], parts[hh]))

    n_far = jnp.maximum(i0 + MOBA_QBLOCKS - 2, 0)
    full = n_far // MOBA_UNROLL
    flat = lax.fori_loop(0, full, lambda j, c: far_trip(j * MOBA_UNROLL, MOBA_UNROLL, c),
                         tuple(x for st in state for x in st))
    flat = lax.cond(n_far > full * MOBA_UNROLL,
                    lambda c: far_trip(full * MOBA_UNROLL, MOBA_UNROLL // 2, c), lambda c: c, flat)
    out_t = jnp.concatenate(
        [flat[2 * hh + 1][:HEAD_DIM] / flat[2 * hh + 1][HEAD_DIM:HEAD_DIM + 1] for hh in range(PAIR)], axis=0)
    o_ref[0] = out_t.T


def _moba(rel_bias, u_main, tab, to_cast=()):
    batch, seq, _ = u_main.shape
    npair = ATT_HEADS // PAIR
    nb = seq // MOBA_BLOCK
    assert 3 * nb <= LANE and nb % MOBA_QBLOCKS == 0
    assert MOBA_QBLOCKS == 2 and MOBA_UNROLL == 2 * MOBA_QBLOCKS
    kcol = ATT_WIDTH // LANE
    qrows = MOBA_QBLOCKS * MOBA_BLOCK
    nsteps = nb // MOBA_QBLOCKS
    assert nsteps % MOBA_CAST_RUN == 0
    total_steps = batch * npair * nsteps

    def cast_spec(w):
        rows, cols = w.shape
        n_blocks = min(total_steps // MOBA_CAST_RUN, rows // SUBLANE_BF16)
        assert rows % (n_blocks * SUBLANE_BF16) == 0 and total_steps % n_blocks == 0, w.shape
        run = total_steps // n_blocks
        return pl.BlockSpec((rows // n_blocks, cols), lambda b, p, i: (((b * npair + p) * nsteps + i) // run, 0))

    cast_specs = [cast_spec(w) for w in to_cast]
    outs = pl.pallas_call(
        functools.partial(_moba_kernel, n_cast=len(to_cast)),
        grid=(batch, npair, nsteps),
        in_specs=[
            pl.BlockSpec(memory_space=pltpu.SMEM),
            pl.BlockSpec((1, qrows, LANE), lambda b, p, i: (b, i, p)),
            pl.BlockSpec((1, seq, LANE), lambda b, p, i: (b, 0, kcol + p)),
            pl.BlockSpec((1, seq, LANE), lambda b, p, i: (b, 0, 2 * kcol + p)),
            pl.BlockSpec((1, 2 * PAIR, MOBA_BLOCK, MOBA_BLOCK), lambda b, p, i: (p, 0, 0, 0)),
        ] + cast_specs,
        out_specs=[pl.BlockSpec((1, qrows, LANE), lambda b, p, i: (b, i, p))] + cast_specs,
        out_shape=[jax.ShapeDtypeStruct((batch, seq, ATT_WIDTH), F32)]
        + [jax.ShapeDtypeStruct(w.shape, BF16) for w in to_cast],
        scratch_shapes=[
            pltpu.VMEM((nb, MOBA_BLOCK, 2 * LANE), BF16),
            pltpu.VMEM((nb, PAIR * VT_ROWS, MOBA_BLOCK), BF16),
            pltpu.VMEM((PAIR * nb, LANE), F32),
        ],
        compiler_params=_cparams(("arbitrary", "arbitrary", "arbitrary")),
        name="moba",
    )(rel_bias, u_main, u_main, u_main, tab, *to_cast)
    return outs[0], list(outs[1:])


def _rwkv_chunk_kernel(*refs):
    next_tiles, next_prev, first_tiles, prep_params = refs[0:4], refs[4:8], refs[8:12], refs[12:22]
    rk_ref, gng_ref, gnb_ref, o_ref, st_scr, prep_scr = refs[22:]
    eq_ref = prep_params[-1]
    c = pl.program_id(0)
    nquad = RWKV_WIDTH // QUAD
    nbatch = o_ref.shape[0]

    def prep_chain(bi, tiles, prevs, first):
        vals = []
        yield from _rwkv_prep_tile([lambda t=t: t[bi] for t in tiles], [lambda p=p: p[bi, SUBLANE - 1:SUBLANE, :] for p in prevs],
                                   first, *prep_params, vals)
        yield
        for q, val in enumerate(vals):
            prep_scr[q, bi] = val

    @pl.when(c == 0)
    def _():
        st_scr[...] = jnp.zeros_like(st_scr)
        for _ in itertools.zip_longest(*[prep_chain(bi, first_tiles, next_prev, True) for bi in range(nbatch)]):
            pass

    eq = eq_ref[...]
    levels = int(math.log2(CHUNK))
    row_h = lax.broadcasted_iota(jnp.int32, (QUAD, QUAD), 0) // HEAD_DIM
    col_h = lax.broadcasted_iota(jnp.int32, (QUAD, QUAD), 1) // HEAD_DIM
    bdmask = row_h == col_h
    eye = (lax.broadcasted_iota(jnp.int32, (QUAD, QUAD), 0)
           == lax.broadcasted_iota(jnp.int32, (QUAD, QUAD), 1))
    t_idx = lax.broadcasted_iota(jnp.int32, (CHUNK, QUAD), 0)
    s_idx = lax.broadcasted_iota(jnp.int32, (CHUNK, QUAD), 1) % CHUNK
    strict = t_idx > s_idx
    incl = t_idx >= s_idx
    both = jnp.concatenate([strict, incl], axis=0)
    reps = QUAD // CHUNK

    def bds(x):
        xb = x.astype(BF16)
        return jnp.where(bdmask, jnp.concatenate([xb] * reps, axis=0), jnp.zeros((), BF16))

    def mm(a, b):
        return _dot(a.astype(BF16), b.astype(BF16))

    def quad_stages(bi, qd):
        sl = slice(qd * QUAD, (qd + 1) * QUAD)
        st_ref = st_scr.at[bi * nquad + qd]
        r, k, v, lw, aa, bb, g = (prep_scr[q, bi, :, sl] for q in range(7))

        cum = lw
        for sh in (1 << b for b in range(levels)):
            cum = cum + jnp.where(t_idx >= sh, pltpu.roll(cum, sh, 0), 0.0)
        e_pos = jnp.exp(cum)
        e_neg = jnp.exp(-cum)
        gamma = e_pos[CHUNK - 1:CHUNK, :]
        e_prev = jnp.where(t_idx == 0, 1.0, pltpu.roll(e_pos, 1, 0))
        a_t = aa * e_prev
        r_t = r * e_pos
        b_c = bb * e_neg
        k_c = k * e_neg
        b_h = b_c * gamma
        k_h = k_c * gamma

        lhs = jnp.concatenate([a_t, r_t], axis=0).astype(BF16)
        pb = _dot_tb(lhs, bds(b_c))
        pk = _dot_tb(lhs, bds(k_c))
        bk_t = jnp.concatenate([b_h, k_h], axis=0).T.astype(BF16)
        yield
        a_rb = jnp.where(incl, pb[CHUNK:], 0.0).astype(BF16)
        akrk = mm(jnp.where(both, pk, 0.0), bds(v))
        n = jnp.where(strict, pb[:CHUNK], 0.0)
        t_inv = jnp.where(t_idx == s_idx, 1.0, 0.0) + n
        n = mm(n, bds(n))
        yield
        for lvl in range(1, levels):
            bd_n = bds(n)
            if lvl < levels - 1:
                tn = mm(jnp.concatenate([t_inv, n], axis=0), bd_n)
                t_inv = t_inv + tn[:CHUNK]
                n = tn[CHUNK:]
            else:
                t_inv = t_inv + mm(t_inv, bd_n)
            yield
        w = mm(t_inv, bds(a_t))
        u = mm(t_inv, bds(akrk[:CHUNK]))
        yield
        q_t = r_t + mm(a_rb, bds(w))
        y0 = mm(a_rb, bds(u)) + akrk[CHUNK:]
        w0 = jnp.concatenate([w, jnp.zeros_like(w)], axis=0)
        g_mat = jnp.where(bdmask, mm(bk_t, w0), 0.0) + jnp.where(eye, gamma, 0.0)
        h_mat = jnp.where(bdmask, mm(bk_t, jnp.concatenate([u, v], axis=0)), 0.0)
        yield
        gy = mm(jnp.concatenate([g_mat, q_t], axis=0), st_ref[...])
        st_ref[...] = gy[:QUAD] + h_mat
        y = gy[QUAD:] + y0
        yield

        inv_n = 1.0 / HEAD_DIM
        stats = mm(jnp.concatenate([y, r * k * rk_ref[:, sl]], axis=0), eq)
        mu = stats[:CHUNK] * inv_n
        bonus = stats[CHUNK:] * v
        yield
        d = y - mu
        var = mm(d * d, eq) * inv_n
        yield
        yn = d * lax.rsqrt(var + GN_EPS) * gng_ref[:, sl] + gnb_ref[:, sl]
        o_ref[bi, :, sl] = (yn + bonus) * g

    chains = [quad_stages(bi, qd) for bi in range(nbatch) for qd in range(nquad)]
    chains += [prep_chain(bi, next_tiles, next_prev, False) for bi in range(nbatch)]
    for _ in itertools.zip_longest(*chains):
        pass


def _rwkv(u_main, u_tail, prep_params, r_k, gn_g, gn_b):
    batch, seq, _ = u_main.shape
    width = RWKV_WIDTH
    nchunks = seq // CHUNK
    rcol = 3 * ATT_WIDTH // width
    nxt = lambda c: jnp.minimum(c + 1, nchunks - 1)
    prev_tile = lambda c: nxt(c) * (CHUNK // SUBLANE) - 1
    const = lambda a: pl.BlockSpec(a.shape, lambda c: (0,) * a.ndim)

    def tiles(rows, row_index):
        return ([pl.BlockSpec((batch, rows, width), lambda c, g=g: (0, row_index(c), rcol + g)) for g in range(3)]
                + [pl.BlockSpec((batch, rows, TAIL_COLS), lambda c: (0, row_index(c), 0))])

    u_args = (u_main, u_main, u_main, u_tail)
    return pl.pallas_call(
        _rwkv_chunk_kernel,
        grid=(nchunks,),
        in_specs=tiles(CHUNK, nxt) + tiles(SUBLANE, prev_tile) + tiles(CHUNK, lambda c: 0)
        + [const(a) for a in prep_params] + [const(r_k), const(gn_g), const(gn_b)],
        out_specs=pl.BlockSpec((batch, CHUNK, width), lambda c: (0, c, 0)),
        out_shape=jax.ShapeDtypeStruct((batch, seq, width), F32),
        scratch_shapes=[pltpu.VMEM((batch * width // QUAD, QUAD, QUAD), F32),
                        pltpu.VMEM((7, batch, CHUNK, width), F32)],
        compiler_params=_cparams(("arbitrary",)),
        name="rwkv_chunk",
    )(*u_args, *u_args, *u_args, *prep_params, r_k, gn_g, gn_b)


def _row_pipelined(n_rows, matmul, finish):
    pending = None
    for r0 in range(0, n_rows, SUB_ROWS):
        rows = slice(r0, r0 + SUB_ROWS)
        acc = matmul(rows)
        if pending is not None:
            finish(*pending)
        pending = (rows, acc)
    finish(*pending)


def _out_ln_kernel(h_ref, att_ref, rw_ref, wa_ref, wr_ref, g_ref, b_ref, o_ref):
    def matmul(rows):
        return (_dot(att_ref[rows, :].astype(BF16), wa_ref[...])
                + _dot(rw_ref[rows, :].astype(BF16), wr_ref[...]))

    def finish(rows, mix):
        o_ref[rows, :] = _layer_norm(ALPHA * h_ref[rows, :] + mix, g_ref[...], b_ref[...])

    _row_pipelined(h_ref.shape[0], matmul, finish)


def _out_ln(h, att, rw, w_out, g, b, tm=512):
    t, d = h.shape
    wa, wr = att.shape[1], rw.shape[1]
    assert wa == wr and w_out.shape[0] == wa + wr
    row = lambda n: pl.BlockSpec((tm, n), lambda i: (i, 0))
    full = lambda shape: pl.BlockSpec(shape, lambda i: (0,) * len(shape))
    return pl.pallas_call(
        _out_ln_kernel,
        grid=(t // tm,),
        in_specs=[row(d), row(wa), row(wr),
                  pl.BlockSpec((wa, d), lambda i: (0, 0)), pl.BlockSpec((wr, d), lambda i: (1, 0)),
                  full((1, d)), full((1, d))],
        out_specs=row(d),
        out_shape=jax.ShapeDtypeStruct((t, d), F32),
        compiler_params=_cparams(("parallel",)),
        name="out_ln",
    )(h, att, rw, w_out, w_out, g, b)


def _ple_ln_kernel(h_ref, p_ref, wg_ref, bg_ref, wp_ref, g_ref, b_ref, o_ref):
    def matmul(rows):
        return (_dot(h_ref[rows, :].astype(BF16), wg_ref[...]), _dot(p_ref[rows, :].astype(BF16), wp_ref[...]))

    def finish(rows, acc):
        pre_gate, e = acc
        y = ALPHA * h_ref[rows, :] + _sigmoid(pre_gate + bg_ref[...]) * e
        o_ref[rows, :] = _layer_norm(y, g_ref[...], b_ref[...])

    _row_pipelined(h_ref.shape[0], matmul, finish)


def _ple_ln(h, p, w_gate, b_gate, w_up, g, b, tm=512):
    t, d = h.shape
    row = lambda n: pl.BlockSpec((tm, n), lambda i: (i, 0))
    full = lambda shape: pl.BlockSpec(shape, lambda i: (0,) * len(shape))
    return pl.pallas_call(
        _ple_ln_kernel,
        grid=(t // tm,),
        in_specs=[row(d), row(p.shape[1]), full(w_gate.shape), full((1, d)), full(w_up.shape),
                  full((1, d)), full((1, d))],
        out_specs=row(d),
        out_shape=jax.ShapeDtypeStruct((t, d), F32),
        compiler_params=_cparams(("parallel",)),
        name="ple_ln",
    )(h, p, w_gate, b_gate, w_up, g, b)


def _pad_rows(w, rows):
    return jnp.pad(w, ((0, rows - w.shape[0]), (0, 0)))


def _pack_tail(cols, axis):
    wd, ad, gd = jnp.split(cols, [DECAY_LORA, DECAY_LORA + AAA_LORA], axis=axis)

    def pad(x, n):
        widths = [(0, 0)] * x.ndim
        widths[axis] = (0, n - x.shape[axis])
        return jnp.pad(x, widths)

    return jnp.concatenate([pad(wd, LANE), pad(ad, LANE), pad(gd, GD_PAD)], axis=axis)


def kernel(x, p, ffn1_w_gate, ffn1_w_up, ffn1_w_down, ln1_g, ln1_b, w_in, rel_bias, shift_mix, decay_w0, decay_w2, a_a0, a_a2, gate_g2, k_k, k_a, r_k, gn_g, gn_b, w_out, ln2_g, ln2_b, ffn2_w_gate, ffn2_w_up, ffn2_w_down, ln3_g, ln3_b, ple_w_up, ple_w_gate, ple_b_gate, ln4_g, ln4_b):
    batch, seq, d = x.shape
    t = batch * seq
    row = lambda a: a.reshape(1, -1)
    main_cols = 3 * ATT_WIDTH + 3 * RWKV_WIDTH

    heads = jnp.arange(QUAD) // HEAD_DIM
    esum = (heads[:, None] == heads[None, :]).astype(BF16)
    tab = _bias_tab(rel_bias)

    h = x.reshape(t, d).astype(F32)
    for i in range(DEPTH):
        w_in_t = w_in[i].T
        later_weights = (ffn2_w_gate[i], ffn2_w_up[i], ffn2_w_down[i], w_out[i], ple_w_gate[i], ple_w_up[i])
        ln1 = (row(ln1_g[i]), row(ln1_b[i]))
        head, (head_bf16,), ffn1_bf16 = _ffn_ln(h, ffn1_w_gate[i], ffn1_w_up[i], ffn1_w_down[i], *ln1,
                                                emit_bf16=True, copy_weights=True, tm=FFN_HEAD_TM, tf=FFN_HEAD_TF)
        h, (h_bf16,), _ = _ffn_ln(h, *ffn1_bf16, *ln1, emit_bf16=True, head=(head, head_bf16))
        u_main, u_tail = _proj(h_bf16, w_in_t, main_cols, _pack_tail(w_in_t[main_cols:], axis=0))
        mix = shift_mix[i]
        prep_params = (row(mix[:3 * RWKV_WIDTH]), row(_pack_tail(mix[3 * RWKV_WIDTH:], axis=0)),
                       row(decay_w0[i]), _pad_rows(decay_w2[i], LANE).astype(BF16), row(a_a0[i]),
                       _pad_rows(a_a2[i], LANE).astype(BF16), _pad_rows(gate_g2[i], GD_PAD).astype(BF16),
                       row(k_k[i]), row(k_a[i]), esum)
        u_main3 = u_main.reshape(batch, seq, main_cols)
        att, (wg2, wu2, wd2, wo, wpg, wpu) = _moba(rel_bias, u_main3, tab, to_cast=later_weights)
        rw = _rwkv(u_main3, u_tail.reshape(batch, seq, TAIL_COLS), prep_params,
                   row(r_k[i]), row(gn_g[i]), row(gn_b[i]))
        h = _out_ln(h, att.reshape(t, ATT_WIDTH), rw.reshape(t, RWKV_WIDTH), wo,
                    row(ln2_g[i]), row(ln2_b[i]))
        h, _, _ = _ffn_ln(h, wg2, wu2, wd2, row(ln3_g[i]), row(ln3_b[i]))
        h = _ple_ln(h, p[i].reshape(t, PLE_DIM).astype(F32), wpg, row(ple_b_gate[i]), wpu,
                    row(ln4_g[i]), row(ln4_b[i]))
    return h.reshape(batch, seq, d).astype(x.dtype)
```

```python
import functools
import itertools
import math

import jax
import jax.numpy as jnp
from jax import lax
from jax.experimental import pallas as pl
from jax.experimental.pallas import tpu as pltpu

DEPTH = 1
HEAD_DIM = 64
ATT_HEADS = 16
RWKV_HEADS = 16
ATT_WIDTH = ATT_HEADS * HEAD_DIM
RWKV_WIDTH = RWKV_HEADS * HEAD_DIM
MOBA_BLOCK = 256
MOBA_TOPK = 3
MOBA_UNROLL = 4
MOBA_QBLOCKS = 2
MOBA_CAST_RUN = 4
LOG2E = 1.4426950408889634
REL_BUCKETS = 32
REL_MAX_DIST = 128
DECAY_LORA = 64
AAA_LORA = 64
PLE_DIM = 256
LN_EPS = 1e-5
GN_EPS = 64e-5
NEG = -1e30
ALPHA = (2 * DEPTH) ** 0.25

LANE = 128
SUBLANE = 8
SUBLANE_BF16 = 16
PAIR = LANE // HEAD_DIM
CHUNK = 64
QUAD = 256
ONES_ROWS = SUBLANE_BF16
VT_ROWS = HEAD_DIM + ONES_ROWS
FFN_HEAD_TM = 1024
FFN_HEAD_TF = 256
SUB_ROWS = 128
TAIL_COLS = 512
TAIL_WD, TAIL_AD, TAIL_GD = 0, 128, 256
GD_PAD = 256
VMEM_LIMIT = 56 * 1024 * 1024

F32 = jnp.float32
BF16 = jnp.bfloat16
HI = lax.Precision.HIGHEST


def _cparams(sem):
    return pltpu.CompilerParams(dimension_semantics=sem, vmem_limit_bytes=VMEM_LIMIT)


def _layer_norm(y, g, b):
    mu = jnp.mean(y, axis=-1, keepdims=True)
    d = y - mu
    var = jnp.mean(d * d, axis=-1, keepdims=True)
    return d * lax.rsqrt(var + LN_EPS) * g + b


def _sigmoid(x):
    return 0.5 * jnp.tanh(0.5 * x) + 0.5


def _dot(a, b):
    return jnp.dot(a, b, preferred_element_type=F32)


def _dot_tb(a, b, precision=None):
    return lax.dot_general(a, b, (((1,), (1,)), ((), ())), preferred_element_type=F32,
                           precision=precision)


def _split3(x):
    hi = x.astype(BF16)
    r1 = x - hi.astype(F32)
    mid = r1.astype(BF16)
    lo = (r1 - mid.astype(F32)).astype(BF16)
    return hi, mid, lo


def _ffn_ln_kernel(*refs, n_f, n_pass, emit_bf16, copy_weights):
    x_ref, wg_ref, wu_ref, wd_ref, g_ref, b_ref = refs[:6]
    rest = list(refs[6:])
    head_ref, head_bf16_ref = (rest.pop(0), rest.pop(0)) if n_pass else (None, None)
    o_ref = rest.pop(0)
    ob_ref = rest.pop(0) if emit_bf16 else None
    w_copies = [rest.pop(0) for _ in range(3)] if copy_weights else []
    (xb_ref,) = rest
    step = pl.program_id(0) - n_pass
    f = lax.rem(step, n_f)
    last = n_f - 1

    def compute():
        @pl.when(f == 0)
        def _():
            xb_ref[...] = x_ref[...].astype(BF16)
            o_ref[...] = jnp.zeros_like(o_ref)

        weights = [w_ref[...] for w_ref in (wg_ref, wu_ref, wd_ref)]
        if copy_weights:
            weights = [w.astype(BF16) for w in weights]
            for dst_ref, w in zip(w_copies, weights):
                dst_ref[...] = w
        wg, wu, wd = weights
        xb = xb_ref[...]
        gate = _dot(xb, wg)
        up = _dot(xb, wu)
        act = gate * jax.nn.sigmoid(gate) * up
        o_ref[...] += _dot(act.astype(BF16), wd)

        @pl.when(f == last)
        def _():
            y = _layer_norm(ALPHA * x_ref[...] + 0.5 * o_ref[...], g_ref[...], b_ref[...])
            o_ref[...] = y
            if emit_bf16:
                ob_ref[...] = y.astype(BF16)

    if n_pass == 0:
        compute()
        return
    pl.when(step >= 0)(compute)

    @pl.when(step < 0)
    def _():
        o_ref[...] = head_ref[...]
        ob_ref[...] = head_bf16_ref[...]


def _ffn_ln(x, wg, wu, wd, g, b, emit_bf16=False, copy_weights=False, head=None, tm=512, tf=512):
    t, d = x.shape
    ff = wg.shape[1]
    n_i, n_f = (1 if copy_weights else t // tm), ff // tf
    n_pass = head[0].shape[0] // tm if head else 0
    assert emit_bf16 or not head

    def row_f(s):
        c = jnp.maximum(s - n_pass, 0)
        return n_pass + c // n_f, c % n_f

    resident = dict(pipeline_mode=pl.Buffered(1)) if n_i == 1 else {}
    vec = pl.BlockSpec((1, d), lambda s: (0, 0))
    in_specs = [
        pl.BlockSpec((tm, d), lambda s: (row_f(s)[0], 0), pipeline_mode=pl.Buffered(1 if tm > 512 else 2)),
        pl.BlockSpec((d, tf), lambda s: (0, row_f(s)[1])),
        pl.BlockSpec((d, tf), lambda s: (0, row_f(s)[1])),
        pl.BlockSpec((tf, d), lambda s: (row_f(s)[1], 0)),
        vec, vec,
    ] + [pl.BlockSpec((tm, d), lambda s: (jnp.minimum(s, n_pass - 1), 0))] * (2 * int(n_pass > 0))
    n_res = 1 + int(emit_bf16)
    out_specs = [pl.BlockSpec((tm, d), lambda s: (jnp.where(s < n_pass, s, row_f(s)[0]), 0), **resident)] * n_res
    out_shape = [jax.ShapeDtypeStruct((n_i * tm, d), dt) for dt in (F32, BF16)[:n_res]]
    if copy_weights:
        out_specs += [pl.BlockSpec(spec.block_shape, spec.index_map) for spec in in_specs[1:4]]
        out_shape += [jax.ShapeDtypeStruct(w.shape, BF16) for w in (wg, wu, wd)]
    outs = pl.pallas_call(
        functools.partial(_ffn_ln_kernel, n_f=n_f, n_pass=n_pass, emit_bf16=emit_bf16, copy_weights=copy_weights),
        grid=(n_pass + (n_i - n_pass) * n_f,),
        in_specs=in_specs,
        out_specs=out_specs,
        out_shape=out_shape,
        scratch_shapes=[pltpu.VMEM((tm, d), BF16)],
        compiler_params=_cparams(("arbitrary",)),
        name="ffn_ln",
    )(x, wg, wu, wd, g, b, *(head or ()))
    return outs[0], list(outs[1:n_res]), list(outs[n_res:])


def _proj_kernel(x_ref, w_ref, wt_ref, o_ref, ot_ref):
    @pl.when(pl.program_id(1) == 0)
    def _():
        ot_ref[...] = _dot_tb(x_ref[...], wt_ref[...].astype(BF16))

    o_ref[...] = _dot_tb(x_ref[...], w_ref[...].astype(BF16))


def _proj(x, w_t, n, w_tail_t, tm=2048, tn=512):
    t, d = x.shape
    n_tail = w_tail_t.shape[0]
    return pl.pallas_call(
        _proj_kernel,
        grid=(t // tm, n // tn),
        in_specs=[
            pl.BlockSpec((tm, d), lambda i, j: (i, 0)),
            pl.BlockSpec((tn, d), lambda i, j: (j, 0)),
            pl.BlockSpec((n_tail, d), lambda i, j: (0, 0), pipeline_mode=pl.Buffered(1)),
        ],
        out_specs=[pl.BlockSpec((tm, tn), lambda i, j: (i, j)), pl.BlockSpec((tm, n_tail), lambda i, j: (i, 0))],
        out_shape=[jax.ShapeDtypeStruct((t, n), F32), jax.ShapeDtypeStruct((t, n_tail), F32)],
        compiler_params=_cparams(("parallel", "arbitrary")),
        name="proj",
    )(x, w_t, w_tail_t)


def _rel_bucket(dist):
    max_exact = REL_BUCKETS // 2
    n = jnp.maximum(dist, 0)
    nf = jnp.maximum(n, 1).astype(F32)
    large = max_exact + (jnp.log(nf / max_exact) / math.log(REL_MAX_DIST / max_exact)
                         * (REL_BUCKETS - max_exact)).astype(jnp.int32)
    large = jnp.minimum(large, REL_BUCKETS - 1)
    return jnp.where(n < max_exact, n, large)


def _bias_tab_kernel(rb_ref, o_ref):
    p = pl.program_id(0)
    blk = MOBA_BLOCK
    diff = lax.broadcasted_iota(jnp.int32, (SUBLANE, 2 * blk), 1) - blk
    for kind in range(2):
        dist = diff + kind * blk
        bucket = _rel_bucket(dist)
        for hh in range(PAIR):
            h = PAIR * p + hh
            vec = jnp.zeros(dist.shape, F32)
            for bkt in range(REL_BUCKETS):
                vec = jnp.where(bucket == bkt, rb_ref[bkt, h], vec)
            vec = vec * LOG2E
            if kind == 0:
                vec = jnp.where(dist >= 0, vec, NEG)
            rows = jnp.broadcast_to(vec[0:1], (blk, 2 * blk))
            o_ref[0, 2 * hh + kind] = pltpu.roll(rows, 0, 1, stride=1, stride_axis=0)[:, blk:]


def _bias_tab(rel_bias):
    npair = ATT_HEADS // PAIR
    return pl.pallas_call(
        _bias_tab_kernel,
        grid=(npair,),
        in_specs=[pl.BlockSpec(memory_space=pltpu.SMEM)],
        out_specs=pl.BlockSpec((1, 2 * PAIR, MOBA_BLOCK, MOBA_BLOCK), lambda p: (p, 0, 0, 0)),
        out_shape=jax.ShapeDtypeStruct((npair, 2 * PAIR, MOBA_BLOCK, MOBA_BLOCK), F32),
        compiler_params=_cparams(("arbitrary",)),
        name="bias_tab",
    )(rel_bias)


def _rwkv_prep_tile(tiles, prev_rows, first, mixm_ref, mixt_ref, w0_ref, w2_ref, a0_ref, a2_ref, g2_ref,
                    kk_ref, ka_ref, esum_ref, out):
    width = RWKV_WIDTH

    def shifted(g, mix):
        x = tiles[g]()
        prev = pltpu.roll(x, 1, 0)
        row = lax.broadcasted_iota(jnp.int32, x.shape, 0)
        last = 0.0 if first is True else jnp.where(first, 0.0, prev_rows[g]())
        prev = jnp.where(row == 0, last, prev)
        return x + (prev - x) * mix

    tail = shifted(3, mixt_ref[...])
    wd = tail[:, TAIL_WD:TAIL_WD + LANE]
    ad = tail[:, TAIL_AD:TAIL_AD + LANE]
    gd = tail[:, TAIL_GD:TAIL_GD + GD_PAD]
    y = w0_ref[...] + _dot(jnp.tanh(wd).astype(BF16), w2_ref[...])
    yield
    lw = -math.exp(-0.5) * _sigmoid(y)
    yield
    a = _sigmoid(a0_ref[...] + _dot(ad.astype(BF16), a2_ref[...]))
    yield
    g = _dot(_sigmoid(gd).astype(BF16), g2_ref[...])
    yield
    k = shifted(1, mixm_ref[:, width:2 * width])
    kk = k * kk_ref[...]
    kk2 = kk * kk
    yield
    ss = []
    for c in range(0, width, QUAD):
        ss.append(_dot(kk2[:, c:c + QUAD].astype(BF16), esum_ref[...]))
        yield
    kk = kk * jnp.minimum(lax.rsqrt(jnp.concatenate(ss, axis=1)), 1e12)
    k = k * (1.0 + (a - 1.0) * ka_ref[...])
    yield
    r = shifted(0, mixm_ref[:, 0:width])
    yield
    v = shifted(2, mixm_ref[:, 2 * width:3 * width])
    out.extend((r, k, v, lw, -kk, kk * a, g))


def _moba_kernel(rb_ref, q_ref, k_ref, v_ref, tab_ref, *refs, n_cast):
    cast_in, o_ref, cast_out = refs[:n_cast], refs[n_cast], refs[n_cast + 1:2 * n_cast + 1]
    kb_scr, vt_scr, kmean_scr = refs[2 * n_cast + 1:]
    p = pl.program_id(1)
    i0 = pl.program_id(2) * MOBA_QBLOCKS
    nb = kb_scr.shape[0]
    blk = MOBA_BLOCK

    @pl.when(pl.program_id(2) % MOBA_CAST_RUN == 0)
    def _():
        for src_ref, dst_ref in zip(cast_in, cast_out):
            dst_ref[...] = src_ref[...].astype(BF16)

    @pl.when(i0 == 0)
    def _():
        lane_k = lax.broadcasted_iota(jnp.int32, (blk, LANE), 1)
        lane_m = lax.broadcasted_iota(jnp.int32, (1, LANE), 1)
        ones = jnp.ones((ONES_ROWS, blk), BF16)
        for n in range(nb):
            kblk = k_ref[0, n * blk:(n + 1) * blk, :]
            kb_scr[n, :, :LANE] = kblk.astype(BF16)
            hot = (lane_k == n) | (lane_k == nb + n) | (lane_k == 2 * nb + n)
            kb_scr[n, :, LANE:] = jnp.where(hot, 1.0, 0.0).astype(BF16)
            kmean = jnp.mean(kblk, axis=0, keepdims=True)
            v_t = v_ref[0, n * blk:(n + 1) * blk, :].T.astype(BF16)
            for hh in range(PAIR):
                in_head = (lane_m >= hh * HEAD_DIM) & (lane_m < (hh + 1) * HEAD_DIM)
                kmean_scr[hh * nb + n:hh * nb + n + 1, :] = jnp.where(in_head, kmean, 0.0)
                vt_scr[n, hh * VT_ROWS:hh * VT_ROWS + HEAD_DIM, :] = v_t[hh * HEAD_DIM:(hh + 1) * HEAD_DIM, :]
                vt_scr[n, hh * VT_ROWS + HEAD_DIM:(hh + 1) * VT_ROWS, :] = ones

    q8 = q_ref[0] * (HEAD_DIM ** -0.5)
    qs = q8 * LOG2E
    nq = q8.shape[0]
    lane = lax.broadcasted_iota(jnp.int32, q8.shape, 1)
    nidx = lax.broadcasted_iota(jnp.int32, (nb, nq), 0)
    own = i0 + lax.broadcasted_iota(jnp.int32, (1, nq), 1) // blk
    gate_all = _dot_tb(kmean_scr[...], q8, precision=HI)

    q_tail, q_far, adj_mask = [], [], []
    for hh in range(PAIR):
        gate = gate_all[hh * nb:(hh + 1) * nb]
        avail = nidx < own
        sel = jnp.zeros((nb, nq), jnp.bool_)
        for _ in range(MOBA_TOPK):
            best = jnp.max(jnp.where(avail, gate, -jnp.inf), axis=0, keepdims=True)
            first = jnp.min(jnp.where(avail & (gate == best), nidx, nb), axis=0, keepdims=True)
            pick = nidx == first
            sel = sel | pick
            avail = avail & jnp.logical_not(pick)
        b_far = rb_ref[REL_BUCKETS - 1, PAIR * p + hh] * LOG2E
        rowb = jnp.where(sel & (nidx < own - 1), b_far, NEG)
        adj_mask.append(jnp.max(jnp.where(sel & (nidx == own - 1), 0.0, NEG), axis=0, keepdims=True))
        hi, mid, lo = _split3(rowb)
        parts = jnp.concatenate([hi.astype(F32), mid.astype(F32), lo.astype(F32),
                                 jnp.zeros((LANE - 3 * nb, nq), F32)], axis=0)
        in_head = (lane >= hh * HEAD_DIM) & (lane < (hh + 1) * HEAD_DIM)
        q_h = jnp.where(in_head, qs, 0.0).astype(BF16)
        q_tail.append(q_h)
        q_far.append(jnp.concatenate([q_h, parts.T.astype(BF16)], axis=1))

    def vt(hh, n):
        return vt_scr[n, hh * VT_ROWS:(hh + 1) * VT_ROWS, :]

    def block_chain(hh, n, scores, parts):
        s = scores()
        yield
        m_s = jnp.max(s, axis=0, keepdims=True)
        yield
        parts.append((m_s, _dot(vt(hh, n), jnp.exp2(s - m_s).astype(BF16))))

    def run_chains(gens):
        for _ in itertools.zip_longest(*gens):
            pass

    def merge(prev, parts):
        m_new = parts[0][0] if prev is None else jnp.maximum(prev[0], parts[0][0])
        for m_s, _ in parts[1:]:
            m_new = jnp.maximum(m_new, m_s)
        acc = None if prev is None else jnp.exp2(prev[0] - m_new) * prev[1]
        for m_s, acc_s in parts:
            term = jnp.exp2(m_s - m_new) * acc_s
            acc = term if acc is None else acc + term
        return m_new, acc

    def tail_blocks(hh, c):
        cols = slice(c * blk, (c + 1) * blk)
        i_c = i0 + c
        n_adj = jnp.maximum(i_c - 1, 0)
        q_c = q_tail[hh][cols]

        def own_blk():
            return _dot_tb(kb_scr[i_c, :, :LANE], q_c) + tab_ref[0, 2 * hh]

        def prev_blk():
            return _dot_tb(kb_scr[n_adj, :, :LANE], q_c) + tab_ref[0, 2 * hh + 1] + adj_mask[hh][:, cols]

        return [(i_c, own_blk), (n_adj, prev_blk)]

    tails = {(hh, c): [] for hh in range(PAIR) for c in range(MOBA_QBLOCKS)}
    run_chains(block_chain(hh, n, scores, tails[hh, c])
               for hh in range(PAIR) for c in range(MOBA_QBLOCKS) for n, scores in tail_blocks(hh, c))
    tails = {key: merge(None, parts) for key, parts in tails.items()}
    state = [tuple(jnp.concatenate([tails[hh, c][s] for c in range(MOBA_QBLOCKS)], axis=1) for s in range(2))
             for hh in range(PAIR)]

    def far_trip(first_block, n_blocks, carry):
        parts = [[] for _ in range(PAIR)]
        chains = []
        for hh in range(PAIR):
            for u in range(n_blocks):
                nc = first_block + u
                chains.append(block_chain(hh, nc, lambda nc=nc, hh=hh: _dot_tb(kb_scr[nc], q_far[hh]), parts[hh]))
        run_chains(chains)
        return tuple(x for hh in range(PAIR) for x in merge(carry[2 * hh:2 * hh + 2], parts[hh]))

    n_far = jnp.maximum(i0 + MOBA_QBLOCKS - 2, 0)
    full = n_far // MOBA_UNROLL
    flat = lax.fori_loop(0, full, lambda j, c: far_trip(j * MOBA_UNROLL, MOBA_UNROLL, c),
                         tuple(x for st in state for x in st))
    flat = lax.cond(n_far > full * MOBA_UNROLL,
                    lambda c: far_trip(full * MOBA_UNROLL, MOBA_UNROLL // 2, c), lambda c: c, flat)
    out_t = jnp.concatenate(
        [flat[2 * hh + 1][:HEAD_DIM] / flat[2 * hh + 1][HEAD_DIM:HEAD_DIM + 1] for hh in range(PAIR)], axis=0)
    o_ref[0] = out_t.T


def _moba(rel_bias, u_main, tab, to_cast=()):
    batch, seq, _ = u_main.shape
    npair = ATT_HEADS // PAIR
    nb = seq // MOBA_BLOCK
    assert 3 * nb <= LANE and nb % MOBA_QBLOCKS == 0
    assert MOBA_QBLOCKS == 2 and MOBA_UNROLL == 2 * MOBA_QBLOCKS
    kcol = ATT_WIDTH // LANE
    qrows = MOBA_QBLOCKS * MOBA_BLOCK
    nsteps = nb // MOBA_QBLOCKS
    assert nsteps % MOBA_CAST_RUN == 0
    total_steps = batch * npair * nsteps

    def cast_spec(w):
        rows, cols = w.shape
        n_blocks = min(total_steps // MOBA_CAST_RUN, rows // SUBLANE_BF16)
        assert rows % (n_blocks * SUBLANE_BF16) == 0 and total_steps % n_blocks == 0, w.shape
        run = total_steps // n_blocks
        return pl.BlockSpec((rows // n_blocks, cols), lambda b, p, i: (((b * npair + p) * nsteps + i) // run, 0))

    cast_specs = [cast_spec(w) for w in to_cast]
    outs = pl.pallas_call(
        functools.partial(_moba_kernel, n_cast=len(to_cast)),
        grid=(batch, npair, nsteps),
        in_specs=[
            pl.BlockSpec(memory_space=pltpu.SMEM),
            pl.BlockSpec((1, qrows, LANE), lambda b, p, i: (b, i, p)),
            pl.BlockSpec((1, seq, LANE), lambda b, p, i: (b, 0, kcol + p)),
            pl.BlockSpec((1, seq, LANE), lambda b, p, i: (b, 0, 2 * kcol + p)),
            pl.BlockSpec((1, 2 * PAIR, MOBA_BLOCK, MOBA_BLOCK), lambda b, p, i: (p, 0, 0, 0)),
        ] + cast_specs,
        out_specs=[pl.BlockSpec((1, qrows, LANE), lambda b, p, i: (b, i, p))] + cast_specs,
        out_shape=[jax.ShapeDtypeStruct((batch, seq, ATT_WIDTH), F32)]
        + [jax.ShapeDtypeStruct(w.shape, BF16) for w in to_cast],
        scratch_shapes=[
            pltpu.VMEM((nb, MOBA_BLOCK, 2 * LANE), BF16),
            pltpu.VMEM((nb, PAIR * VT_ROWS, MOBA_BLOCK), BF16),
            pltpu.VMEM((PAIR * nb, LANE), F32),
        ],
        compiler_params=_cparams(("arbitrary", "arbitrary", "arbitrary")),
        name="moba",
    )(rel_bias, u_main, u_main, u_main, tab, *to_cast)
    return outs[0], list(outs[1:])


def _rwkv_chunk_kernel(*refs):
    next_tiles, next_prev, first_tiles, prep_params = refs[0:4], refs[4:8], refs[8:12], refs[12:22]
    rk_ref, gng_ref, gnb_ref, o_ref, st_scr, prep_scr = refs[22:]
    eq_ref = prep_params[-1]
    c = pl.program_id(0)
    nquad = RWKV_WIDTH // QUAD
    nbatch = o_ref.shape[0]

    def prep_chain(bi, tiles, prevs, first):
        vals = []
        yield from _rwkv_prep_tile([lambda t=t: t[bi] for t in tiles], [lambda p=p: p[bi, SUBLANE - 1:SUBLANE, :] for p in prevs],
                                   first, *prep_params, vals)
        yield
        for q, val in enumerate(vals):
            prep_scr[q, bi] = val

    @pl.when(c == 0)
    def _():
        st_scr[...] = jnp.zeros_like(st_scr)
        for _ in itertools.zip_longest(*[prep_chain(bi, first_tiles, next_prev, True) for bi in range(nbatch)]):
            pass

    eq = eq_ref[...]
    levels = int(math.log2(CHUNK))
    row_h = lax.broadcasted_iota(jnp.int32, (QUAD, QUAD), 0) // HEAD_DIM
    col_h = lax.broadcasted_iota(jnp.int32, (QUAD, QUAD), 1) // HEAD_DIM
    bdmask = row_h == col_h
    eye = (lax.broadcasted_iota(jnp.int32, (QUAD, QUAD), 0)
           == lax.broadcasted_iota(jnp.int32, (QUAD, QUAD), 1))
    t_idx = lax.broadcasted_iota(jnp.int32, (CHUNK, QUAD), 0)
    s_idx = lax.broadcasted_iota(jnp.int32, (CHUNK, QUAD), 1) % CHUNK
    strict = t_idx > s_idx
    incl = t_idx >= s_idx
    both = jnp.concatenate([strict, incl], axis=0)
    reps = QUAD // CHUNK

    def bds(x):
        xb = x.astype(BF16)
        return jnp.where(bdmask, jnp.concatenate([xb] * reps, axis=0), jnp.zeros((), BF16))

    def mm(a, b):
        return _dot(a.astype(BF16), b.astype(BF16))

    def quad_stages(bi, qd):
        sl = slice(qd * QUAD, (qd + 1) * QUAD)
        st_ref = st_scr.at[bi * nquad + qd]
        r, k, v, lw, aa, bb, g = (prep_scr[q, bi, :, sl] for q in range(7))

        cum = lw
        for sh in (1 << b for b in range(levels)):
            cum = cum + jnp.where(t_idx >= sh, pltpu.roll(cum, sh, 0), 0.0)
        e_pos = jnp.exp(cum)
        e_neg = jnp.exp(-cum)
        gamma = e_pos[CHUNK - 1:CHUNK, :]
        e_prev = jnp.where(t_idx == 0, 1.0, pltpu.roll(e_pos, 1, 0))
        a_t = aa * e_prev
        r_t = r * e_pos
        b_c = bb * e_neg
        k_c = k * e_neg
        b_h = b_c * gamma
        k_h = k_c * gamma

        lhs = jnp.concatenate([a_t, r_t], axis=0).astype(BF16)
        pb = _dot_tb(lhs, bds(b_c))
        pk = _dot_tb(lhs, bds(k_c))
        bk_t = jnp.concatenate([b_h, k_h], axis=0).T.astype(BF16)
        yield
        a_rb = jnp.where(incl, pb[CHUNK:], 0.0).astype(BF16)
        akrk = mm(jnp.where(both, pk, 0.0), bds(v))
        n = jnp.where(strict, pb[:CHUNK], 0.0)
        t_inv = jnp.where(t_idx == s_idx, 1.0, 0.0) + n
        n = mm(n, bds(n))
        yield
        for lvl in range(1, levels):
            bd_n = bds(n)
            if lvl < levels - 1:
                tn = mm(jnp.concatenate([t_inv, n], axis=0), bd_n)
                t_inv = t_inv + tn[:CHUNK]
                n = tn[CHUNK:]
            else:
                t_inv = t_inv + mm(t_inv, bd_n)
            yield
        w = mm(t_inv, bds(a_t))
        u = mm(t_inv, bds(akrk[:CHUNK]))
        yield
        q_t = r_t + mm(a_rb, bds(w))
        y0 = mm(a_rb, bds(u)) + akrk[CHUNK:]
        w0 = jnp.concatenate([w, jnp.zeros_like(w)], axis=0)
        g_mat = jnp.where(bdmask, mm(bk_t, w0), 0.0) + jnp.where(eye, gamma, 0.0)
        h_mat = jnp.where(bdmask, mm(bk_t, jnp.concatenate([u, v], axis=0)), 0.0)
        yield
        gy = mm(jnp.concatenate([g_mat, q_t], axis=0), st_ref[...])
        st_ref[...] = gy[:QUAD] + h_mat
        y = gy[QUAD:] + y0
        yield

        inv_n = 1.0 / HEAD_DIM
        stats = mm(jnp.concatenate([y, r * k * rk_ref[:, sl]], axis=0), eq)
        mu = stats[:CHUNK] * inv_n
        bonus = stats[CHUNK:] * v
        yield
        d = y - mu
        var = mm(d * d, eq) * inv_n
        yield
        yn = d * lax.rsqrt(var + GN_EPS) * gng_ref[:, sl] + gnb_ref[:, sl]
        o_ref[bi, :, sl] = (yn + bonus) * g

    chains = [quad_stages(bi, qd) for bi in range(nbatch) for qd in range(nquad)]
    chains += [prep_chain(bi, next_tiles, next_prev, False) for bi in range(nbatch)]
    for _ in itertools.zip_longest(*chains):
        pass


def _rwkv(u_main, u_tail, prep_params, r_k, gn_g, gn_b):
    batch, seq, _ = u_main.shape
    width = RWKV_WIDTH
    nchunks = seq // CHUNK
    rcol = 3 * ATT_WIDTH // width
    nxt = lambda c: jnp.minimum(c + 1, nchunks - 1)
    prev_tile = lambda c: nxt(c) * (CHUNK // SUBLANE) - 1
    const = lambda a: pl.BlockSpec(a.shape, lambda c: (0,) * a.ndim)

    def tiles(rows, row_index):
        return ([pl.BlockSpec((batch, rows, width), lambda c, g=g: (0, row_index(c), rcol + g)) for g in range(3)]
                + [pl.BlockSpec((batch, rows, TAIL_COLS), lambda c: (0, row_index(c), 0))])

    u_args = (u_main, u_main, u_main, u_tail)
    return pl.pallas_call(
        _rwkv_chunk_kernel,
        grid=(nchunks,),
        in_specs=tiles(CHUNK, nxt) + tiles(SUBLANE, prev_tile) + tiles(CHUNK, lambda c: 0)
        + [const(a) for a in prep_params] + [const(r_k), const(gn_g), const(gn_b)],
        out_specs=pl.BlockSpec((batch, CHUNK, width), lambda c: (0, c, 0)),
        out_shape=jax.ShapeDtypeStruct((batch, seq, width), F32),
        scratch_shapes=[pltpu.VMEM((batch * width // QUAD, QUAD, QUAD), F32),
                        pltpu.VMEM((7, batch, CHUNK, width), F32)],
        compiler_params=_cparams(("arbitrary",)),
        name="rwkv_chunk",
    )(*u_args, *u_args, *u_args, *prep_params, r_k, gn_g, gn_b)


def _row_pipelined(n_rows, matmul, finish):
    pending = None
    for r0 in range(0, n_rows, SUB_ROWS):
        rows = slice(r0, r0 + SUB_ROWS)
        acc = matmul(rows)
        if pending is not None:
            finish(*pending)
        pending = (rows, acc)
    finish(*pending)


def _out_ln_kernel(h_ref, att_ref, rw_ref, wa_ref, wr_ref, g_ref, b_ref, o_ref):
    def matmul(rows):
        return (_dot(att_ref[rows, :].astype(BF16), wa_ref[...])
                + _dot(rw_ref[rows, :].astype(BF16), wr_ref[...]))

    def finish(rows, mix):
        o_ref[rows, :] = _layer_norm(ALPHA * h_ref[rows, :] + mix, g_ref[...], b_ref[...])

    _row_pipelined(h_ref.shape[0], matmul, finish)


def _out_ln(h, att, rw, w_out, g, b, tm=512):
    t, d = h.shape
    wa, wr = att.shape[1], rw.shape[1]
    assert wa == wr and w_out.shape[0] == wa + wr
    row = lambda n: pl.BlockSpec((tm, n), lambda i: (i, 0))
    full = lambda shape: pl.BlockSpec(shape, lambda i: (0,) * len(shape))
    return pl.pallas_call(
        _out_ln_kernel,
        grid=(t // tm,),
        in_specs=[row(d), row(wa), row(wr),
                  pl.BlockSpec((wa, d), lambda i: (0, 0)), pl.BlockSpec((wr, d), lambda i: (1, 0)),
                  full((1, d)), full((1, d))],
        out_specs=row(d),
        out_shape=jax.ShapeDtypeStruct((t, d), F32),
        compiler_params=_cparams(("parallel",)),
        name="out_ln",
    )(h, att, rw, w_out, w_out, g, b)


def _ple_ln_kernel(h_ref, p_ref, wg_ref, bg_ref, wp_ref, g_ref, b_ref, o_ref):
    def matmul(rows):
        return (_dot(h_ref[rows, :].astype(BF16), wg_ref[...]), _dot(p_ref[rows, :].astype(BF16), wp_ref[...]))

    def finish(rows, acc):
        pre_gate, e = acc
        y = ALPHA * h_ref[rows, :] + _sigmoid(pre_gate + bg_ref[...]) * e
        o_ref[rows, :] = _layer_norm(y, g_ref[...], b_ref[...])

    _row_pipelined(h_ref.shape[0], matmul, finish)


def _ple_ln(h, p, w_gate, b_gate, w_up, g, b, tm=512):
    t, d = h.shape
    row = lambda n: pl.BlockSpec((tm, n), lambda i: (i, 0))
    full = lambda shape: pl.BlockSpec(shape, lambda i: (0,) * len(shape))
    return pl.pallas_call(
        _ple_ln_kernel,
        grid=(t // tm,),
        in_specs=[row(d), row(p.shape[1]), full(w_gate.shape), full((1, d)), full(w_up.shape),
                  full((1, d)), full((1, d))],
        out_specs=row(d),
        out_shape=jax.ShapeDtypeStruct((t, d), F32),
        compiler_params=_cparams(("parallel",)),
        name="ple_ln",
    )(h, p, w_gate, b_gate, w_up, g, b)


def _pad_rows(w, rows):
    return jnp.pad(w, ((0, rows - w.shape[0]), (0, 0)))


def _pack_tail(cols, axis):
    wd, ad, gd = jnp.split(cols, [DECAY_LORA, DECAY_LORA + AAA_LORA], axis=axis)

    def pad(x, n):
        widths = [(0, 0)] * x.ndim
        widths[axis] = (0, n - x.shape[axis])
        return jnp.pad(x, widths)

    return jnp.concatenate([pad(wd, LANE), pad(ad, LANE), pad(gd, GD_PAD)], axis=axis)


def kernel(x, p, ffn1_w_gate, ffn1_w_up, ffn1_w_down, ln1_g, ln1_b, w_in, rel_bias, shift_mix, decay_w0, decay_w2, a_a0, a_a2, gate_g2, k_k, k_a, r_k, gn_g, gn_b, w_out, ln2_g, ln2_b, ffn2_w_gate, ffn2_w_up, ffn2_w_down, ln3_g, ln3_b, ple_w_up, ple_w_gate, ple_b_gate, ln4_g, ln4_b):
    batch, seq, d = x.shape
    t = batch * seq
    row = lambda a: a.reshape(1, -1)
    main_cols = 3 * ATT_WIDTH + 3 * RWKV_WIDTH

    heads = jnp.arange(QUAD) // HEAD_DIM
    esum = (heads[:, None] == heads[None, :]).astype(BF16)
    tab = _bias_tab(rel_bias)

    h = x.reshape(t, d).astype(F32)
    for i in range(DEPTH):
        w_in_t = w_in[i].T
        later_weights = (ffn2_w_gate[i], ffn2_w_up[i], ffn2_w_down[i], w_out[i], ple_w_gate[i], ple_w_up[i])
        ln1 = (row(ln1_g[i]), row(ln1_b[i]))
        head, (head_bf16,), ffn1_bf16 = _ffn_ln(h, ffn1_w_gate[i], ffn1_w_up[i], ffn1_w_down[i], *ln1,
                                                emit_bf16=True, copy_weights=True, tm=FFN_HEAD_TM, tf=FFN_HEAD_TF)
        h, (h_bf16,), _ = _ffn_ln(h, *ffn1_bf16, *ln1, emit_bf16=True, head=(head, head_bf16))
        u_main, u_tail = _proj(h_bf16, w_in_t, main_cols, _pack_tail(w_in_t[main_cols:], axis=0))
        mix = shift_mix[i]
        prep_params = (row(mix[:3 * RWKV_WIDTH]), row(_pack_tail(mix[3 * RWKV_WIDTH:], axis=0)),
                       row(decay_w0[i]), _pad_rows(decay_w2[i], LANE).astype(BF16), row(a_a0[i]),
                       _pad_rows(a_a2[i], LANE).astype(BF16), _pad_rows(gate_g2[i], GD_PAD).astype(BF16),
                       row(k_k[i]), row(k_a[i]), esum)
        u_main3 = u_main.reshape(batch, seq, main_cols)
        att, (wg2, wu2, wd2, wo, wpg, wpu) = _moba(rel_bias, u_main3, tab, to_cast=later_weights)
        rw = _rwkv(u_main3, u_tail.reshape(batch, seq, TAIL_COLS), prep_params,
                   row(r_k[i]), row(gn_g[i]), row(gn_b[i]))
        h = _out_ln(h, att.reshape(t, ATT_WIDTH), rw.reshape(t, RWKV_WIDTH), wo,
                    row(ln2_g[i]), row(ln2_b[i]))
        h, _, _ = _ffn_ln(h, wg2, wu2, wd2, row(ln3_g[i]), row(ln3_b[i]), tm=1024)
        h = _ple_ln(h, p[i].reshape(t, PLE_DIM).astype(F32), wpg, row(ple_b_gate[i]), wpu,
                    row(ln4_g[i]), row(ln4_b[i]))
    return h.reshape(batch, seq, d).astype(x.dtype)
```

```python
import functools
import itertools
import math

import jax
import jax.numpy as jnp
from jax import lax
from jax.experimental import pallas as pl
from jax.experimental.pallas import tpu as pltpu

DEPTH = 1
HEAD_DIM = 64
ATT_HEADS = 16
RWKV_HEADS = 16
ATT_WIDTH = ATT_HEADS * HEAD_DIM
RWKV_WIDTH = RWKV_HEADS * HEAD_DIM
MOBA_BLOCK = 256
MOBA_TOPK = 3
MOBA_UNROLL = 4
MOBA_QBLOCKS = 2
MOBA_CAST_RUN = 4
LOG2E = 1.4426950408889634
REL_BUCKETS = 32
REL_MAX_DIST = 128
DECAY_LORA = 64
AAA_LORA = 64
PLE_DIM = 256
LN_EPS = 1e-5
GN_EPS = 64e-5
NEG = -1e30
ALPHA = (2 * DEPTH) ** 0.25

LANE = 128
SUBLANE = 8
SUBLANE_BF16 = 16
PAIR = LANE // HEAD_DIM
CHUNK = 64
QUAD = 256
ONES_ROWS = SUBLANE_BF16
VT_ROWS = HEAD_DIM + ONES_ROWS
FFN_HEAD_TM = 1024
FFN_HEAD_TF = 256
SUB_ROWS = 128
TAIL_COLS = 512
TAIL_WD, TAIL_AD, TAIL_GD = 0, 128, 256
GD_PAD = 256
VMEM_LIMIT = 56 * 1024 * 1024

F32 = jnp.float32
BF16 = jnp.bfloat16
HI = lax.Precision.HIGHEST


def _cparams(sem):
    return pltpu.CompilerParams(dimension_semantics=sem, vmem_limit_bytes=VMEM_LIMIT)


def _layer_norm(y, g, b):
    mu = jnp.mean(y, axis=-1, keepdims=True)
    d = y - mu
    var = jnp.mean(d * d, axis=-1, keepdims=True)
    return d * lax.rsqrt(var + LN_EPS) * g + b


def _sigmoid(x):
    return 0.5 * jnp.tanh(0.5 * x) + 0.5


def _dot(a, b):
    return jnp.dot(a, b, preferred_element_type=F32)


def _dot_tb(a, b, precision=None):
    return lax.dot_general(a, b, (((1,), (1,)), ((), ())), preferred_element_type=F32,
                           precision=precision)


def _split3(x):
    hi = x.astype(BF16)
    r1 = x - hi.astype(F32)
    mid = r1.astype(BF16)
    lo = (r1 - mid.astype(F32)).astype(BF16)
    return hi, mid, lo


def _ffn_ln_kernel(*refs, n_f, n_pass, emit_bf16, copy_weights):
    x_ref, wg_ref, wu_ref, wd_ref, g_ref, b_ref = refs[:6]
    rest = list(refs[6:])
    head_ref, head_bf16_ref = (rest.pop(0), rest.pop(0)) if n_pass else (None, None)
    o_ref = rest.pop(0)
    ob_ref = rest.pop(0) if emit_bf16 else None
    w_copies = [rest.pop(0) for _ in range(3)] if copy_weights else []
    (xb_ref,) = rest
    step = pl.program_id(0) - n_pass
    f = lax.rem(step, n_f)
    last = n_f - 1

    def compute():
        @pl.when(f == 0)
        def _():
            xb_ref[...] = x_ref[...].astype(BF16)
            o_ref[...] = jnp.zeros_like(o_ref)

        weights = [w_ref[...] for w_ref in (wg_ref, wu_ref, wd_ref)]
        if copy_weights:
            weights = [w.astype(BF16) for w in weights]
            for dst_ref, w in zip(w_copies, weights):
                dst_ref[...] = w
        wg, wu, wd = weights
        xb = xb_ref[...]
        gate = _dot(xb, wg)
        up = _dot(xb, wu)
        act = gate * jax.nn.sigmoid(gate) * up
        o_ref[...] += _dot(act.astype(BF16), wd)

        @pl.when(f == last)
        def _():
            y = _layer_norm(ALPHA * x_ref[...] + 0.5 * o_ref[...], g_ref[...], b_ref[...])
            o_ref[...] = y
            if emit_bf16:
                ob_ref[...] = y.astype(BF16)

    if n_pass == 0:
        compute()
        return
    pl.when(step >= 0)(compute)

    @pl.when(step < 0)
    def _():
        o_ref[...] = head_ref[...]
        ob_ref[...] = head_bf16_ref[...]


def _ffn_ln(x, wg, wu, wd, g, b, emit_bf16=False, copy_weights=False, head=None, tm=512, tf=512):
    t, d = x.shape
    ff = wg.shape[1]
    n_i, n_f = (1 if copy_weights else t // tm), ff // tf
    n_pass = head[0].shape[0] // tm if head else 0
    assert emit_bf16 or not head

    def row_f(s):
        c = jnp.maximum(s - n_pass, 0)
        return n_pass + c // n_f, c % n_f

    resident = dict(pipeline_mode=pl.Buffered(1)) if n_i == 1 else {}
    vec = pl.BlockSpec((1, d), lambda s: (0, 0))
    in_specs = [
        pl.BlockSpec((tm, d), lambda s: (row_f(s)[0], 0), **resident),
        pl.BlockSpec((d, tf), lambda s: (0, row_f(s)[1])),
        pl.BlockSpec((d, tf), lambda s: (0, row_f(s)[1])),
        pl.BlockSpec((tf, d), lambda s: (row_f(s)[1], 0)),
        vec, vec,
    ] + [pl.BlockSpec((tm, d), lambda s: (jnp.minimum(s, n_pass - 1), 0))] * (2 * int(n_pass > 0))
    n_res = 1 + int(emit_bf16)
    out_specs = [pl.BlockSpec((tm, d), lambda s: (jnp.where(s < n_pass, s, row_f(s)[0]), 0), **resident)] * n_res
    out_shape = [jax.ShapeDtypeStruct((n_i * tm, d), dt) for dt in (F32, BF16)[:n_res]]
    if copy_weights:
        out_specs += [pl.BlockSpec(spec.block_shape, spec.index_map) for spec in in_specs[1:4]]
        out_shape += [jax.ShapeDtypeStruct(w.shape, BF16) for w in (wg, wu, wd)]
    body = functools.partial(_ffn_ln_kernel, n_f=n_f, n_pass=n_pass, emit_bf16=emit_bf16, copy_weights=copy_weights)
    grid = (n_pass + (n_i - n_pass) * n_f,)
    n_io = len(in_specs) + len(out_specs)

    def streamed(*refs):
        pltpu.emit_pipeline(body, grid=grid, in_specs=in_specs, out_specs=out_specs)(
            *refs[:n_io], scratches=refs[n_io:])

    whole = pl.BlockSpec(memory_space=pl.ANY)
    if copy_weights:
        call = dict(grid=grid, in_specs=in_specs, out_specs=out_specs, compiler_params=_cparams(("arbitrary",)))
    else:
        call = dict(in_specs=[whole] * len(in_specs), out_specs=[whole] * len(out_specs),
                    compiler_params=pltpu.CompilerParams(vmem_limit_bytes=VMEM_LIMIT))
    outs = pl.pallas_call(
        body if copy_weights else streamed,
        out_shape=out_shape,
        scratch_shapes=[pltpu.VMEM((tm, d), BF16)],
        name="ffn_ln",
        **call,
    )(x, wg, wu, wd, g, b, *(head or ()))
    return outs[0], list(outs[1:n_res]), list(outs[n_res:])


def _proj_kernel(x_ref, w_ref, wt_ref, o_ref, ot_ref):
    @pl.when(pl.program_id(1) == 0)
    def _():
        ot_ref[...] = _dot_tb(x_ref[...], wt_ref[...].astype(BF16))

    o_ref[...] = _dot_tb(x_ref[...], w_ref[...].astype(BF16))


def _proj(x, w_t, n, w_tail_t, tm=2048, tn=512):
    t, d = x.shape
    n_tail = w_tail_t.shape[0]
    return pl.pallas_call(
        _proj_kernel,
        grid=(t // tm, n // tn),
        in_specs=[
            pl.BlockSpec((tm, d), lambda i, j: (i, 0)),
            pl.BlockSpec((tn, d), lambda i, j: (j, 0)),
            pl.BlockSpec((n_tail, d), lambda i, j: (0, 0), pipeline_mode=pl.Buffered(1)),
        ],
        out_specs=[pl.BlockSpec((tm, tn), lambda i, j: (i, j)), pl.BlockSpec((tm, n_tail), lambda i, j: (i, 0))],
        out_shape=[jax.ShapeDtypeStruct((t, n), F32), jax.ShapeDtypeStruct((t, n_tail), F32)],
        compiler_params=_cparams(("parallel", "arbitrary")),
        name="proj",
    )(x, w_t, w_tail_t)


def _rel_bucket(dist):
    max_exact = REL_BUCKETS // 2
    n = jnp.maximum(dist, 0)
    nf = jnp.maximum(n, 1).astype(F32)
    large = max_exact + (jnp.log(nf / max_exact) / math.log(REL_MAX_DIST / max_exact)
                         * (REL_BUCKETS - max_exact)).astype(jnp.int32)
    large = jnp.minimum(large, REL_BUCKETS - 1)
    return jnp.where(n < max_exact, n, large)


def _bias_tab_kernel(rb_ref, o_ref):
    p = pl.program_id(0)
    blk = MOBA_BLOCK
    diff = lax.broadcasted_iota(jnp.int32, (SUBLANE, 2 * blk), 1) - blk
    for kind in range(2):
        dist = diff + kind * blk
        bucket = _rel_bucket(dist)
        for hh in range(PAIR):
            h = PAIR * p + hh
            vec = jnp.zeros(dist.shape, F32)
            for bkt in range(REL_BUCKETS):
                vec = jnp.where(bucket == bkt, rb_ref[bkt, h], vec)
            vec = vec * LOG2E
            if kind == 0:
                vec = jnp.where(dist >= 0, vec, NEG)
            rows = jnp.broadcast_to(vec[0:1], (blk, 2 * blk))
            o_ref[0, 2 * hh + kind] = pltpu.roll(rows, 0, 1, stride=1, stride_axis=0)[:, blk:]


def _bias_tab(rel_bias):
    npair = ATT_HEADS // PAIR
    return pl.pallas_call(
        _bias_tab_kernel,
        grid=(npair,),
        in_specs=[pl.BlockSpec(memory_space=pltpu.SMEM)],
        out_specs=pl.BlockSpec((1, 2 * PAIR, MOBA_BLOCK, MOBA_BLOCK), lambda p: (p, 0, 0, 0)),
        out_shape=jax.ShapeDtypeStruct((npair, 2 * PAIR, MOBA_BLOCK, MOBA_BLOCK), F32),
        compiler_params=_cparams(("arbitrary",)),
        name="bias_tab",
    )(rel_bias)


def _rwkv_prep_tile(tiles, prev_rows, first, mixm_ref, mixt_ref, w0_ref, w2_ref, a0_ref, a2_ref, g2_ref,
                    kk_ref, ka_ref, esum_ref, out):
    width = RWKV_WIDTH

    def shifted(g, mix):
        x = tiles[g]()
        prev = pltpu.roll(x, 1, 0)
        row = lax.broadcasted_iota(jnp.int32, x.shape, 0)
        last = 0.0 if first is True else jnp.where(first, 0.0, prev_rows[g]())
        prev = jnp.where(row == 0, last, prev)
        return x + (prev - x) * mix

    tail = shifted(3, mixt_ref[...])
    wd = tail[:, TAIL_WD:TAIL_WD + LANE]
    ad = tail[:, TAIL_AD:TAIL_AD + LANE]
    gd = tail[:, TAIL_GD:TAIL_GD + GD_PAD]
    y = w0_ref[...] + _dot(jnp.tanh(wd).astype(BF16), w2_ref[...])
    yield
    lw = -math.exp(-0.5) * _sigmoid(y)
    yield
    a = _sigmoid(a0_ref[...] + _dot(ad.astype(BF16), a2_ref[...]))
    yield
    g = _dot(_sigmoid(gd).astype(BF16), g2_ref[...])
    yield
    k = shifted(1, mixm_ref[:, width:2 * width])
    kk = k * kk_ref[...]
    kk2 = kk * kk
    yield
    ss = []
    for c in range(0, width, QUAD):
        ss.append(_dot(kk2[:, c:c + QUAD].astype(BF16), esum_ref[...]))
        yield
    kk = kk * jnp.minimum(lax.rsqrt(jnp.concatenate(ss, axis=1)), 1e12)
    k = k * (1.0 + (a - 1.0) * ka_ref[...])
    yield
    r = shifted(0, mixm_ref[:, 0:width])
    yield
    v = shifted(2, mixm_ref[:, 2 * width:3 * width])
    out.extend((r, k, v, lw, -kk, kk * a, g))


def _moba_kernel(rb_ref, q_ref, k_ref, v_ref, tab_ref, *refs, n_cast):
    cast_in, o_ref, cast_out = refs[:n_cast], refs[n_cast], refs[n_cast + 1:2 * n_cast + 1]
    kb_scr, vt_scr, kmean_scr = refs[2 * n_cast + 1:]
    p = pl.program_id(1)
    i0 = pl.program_id(2) * MOBA_QBLOCKS
    nb = kb_scr.shape[0]
    blk = MOBA_BLOCK

    @pl.when(pl.program_id(2) % MOBA_CAST_RUN == 0)
    def _():
        for src_ref, dst_ref in zip(cast_in, cast_out):
            dst_ref[...] = src_ref[...].astype(BF16)

    @pl.when(i0 == 0)
    def _():
        lane_k = lax.broadcasted_iota(jnp.int32, (blk, LANE), 1)
        lane_m = lax.broadcasted_iota(jnp.int32, (1, LANE), 1)
        ones = jnp.ones((ONES_ROWS, blk), BF16)
        for n in range(nb):
            kblk = k_ref[0, n * blk:(n + 1) * blk, :]
            kb_scr[n, :, :LANE] = kblk.astype(BF16)
            hot = (lane_k == n) | (lane_k == nb + n) | (lane_k == 2 * nb + n)
            kb_scr[n, :, LANE:] = jnp.where(hot, 1.0, 0.0).astype(BF16)
            kmean = jnp.mean(kblk, axis=0, keepdims=True)
            v_t = v_ref[0, n * blk:(n + 1) * blk, :].T.astype(BF16)
            for hh in range(PAIR):
                in_head = (lane_m >= hh * HEAD_DIM) & (lane_m < (hh + 1) * HEAD_DIM)
                kmean_scr[hh * nb + n:hh * nb + n + 1, :] = jnp.where(in_head, kmean, 0.0)
                vt_scr[n, hh * VT_ROWS:hh * VT_ROWS + HEAD_DIM, :] = v_t[hh * HEAD_DIM:(hh + 1) * HEAD_DIM, :]
                vt_scr[n, hh * VT_ROWS + HEAD_DIM:(hh + 1) * VT_ROWS, :] = ones

    q8 = q_ref[0] * (HEAD_DIM ** -0.5)
    qs = q8 * LOG2E
    nq = q8.shape[0]
    lane = lax.broadcasted_iota(jnp.int32, q8.shape, 1)
    nidx = lax.broadcasted_iota(jnp.int32, (nb, nq), 0)
    own = i0 + lax.broadcasted_iota(jnp.int32, (1, nq), 1) // blk
    gate_all = _dot_tb(kmean_scr[...], q8, precision=HI)

    q_tail, q_far, adj_mask = [], [], []
    for hh in range(PAIR):
        gate = gate_all[hh * nb:(hh + 1) * nb]
        avail = nidx < own
        sel = jnp.zeros((nb, nq), jnp.bool_)
        for _ in range(MOBA_TOPK):
            best = jnp.max(jnp.where(avail, gate, -jnp.inf), axis=0, keepdims=True)
            first = jnp.min(jnp.where(avail & (gate == best), nidx, nb), axis=0, keepdims=True)
            pick = nidx == first
            sel = sel | pick
            avail = avail & jnp.logical_not(pick)
        b_far = rb_ref[REL_BUCKETS - 1, PAIR * p + hh] * LOG2E
        rowb = jnp.where(sel & (nidx < own - 1), b_far, NEG)
        adj_mask.append(jnp.max(jnp.where(sel & (nidx == own - 1), 0.0, NEG), axis=0, keepdims=True))
        hi, mid, lo = _split3(rowb)
        parts = jnp.concatenate([hi.astype(F32), mid.astype(F32), lo.astype(F32),
                                 jnp.zeros((LANE - 3 * nb, nq), F32)], axis=0)
        in_head = (lane >= hh * HEAD_DIM) & (lane < (hh + 1) * HEAD_DIM)
        q_h = jnp.where(in_head, qs, 0.0).astype(BF16)
        q_tail.append(q_h)
        q_far.append(jnp.concatenate([q_h, parts.T.astype(BF16)], axis=1))

    def vt(hh, n):
        return vt_scr[n, hh * VT_ROWS:(hh + 1) * VT_ROWS, :]

    def block_chain(hh, n, scores, parts):
        s = scores()
        yield
        m_s = jnp.max(s, axis=0, keepdims=True)
        yield
        parts.append((m_s, _dot(vt(hh, n), jnp.exp2(s - m_s).astype(BF16))))

    def run_chains(gens):
        for _ in itertools.zip_longest(*gens):
            pass

    def merge(prev, parts):
        m_new = parts[0][0] if prev is None else jnp.maximum(prev[0], parts[0][0])
        for m_s, _ in parts[1:]:
            m_new = jnp.maximum(m_new, m_s)
        acc = None if prev is None else jnp.exp2(prev[0] - m_new) * prev[1]
        for m_s, acc_s in parts:
            term = jnp.exp2(m_s - m_new) * acc_s
            acc = term if acc is None else acc + term
        return m_new, acc

    def tail_blocks(hh, c):
        cols = slice(c * blk, (c + 1) * blk)
        i_c = i0 + c
        n_adj = jnp.maximum(i_c - 1, 0)
        q_c = q_tail[hh][cols]

        def own_blk():
            return _dot_tb(kb_scr[i_c, :, :LANE], q_c) + tab_ref[0, 2 * hh]

        def prev_blk():
            return _dot_tb(kb_scr[n_adj, :, :LANE], q_c) + tab_ref[0, 2 * hh + 1] + adj_mask[hh][:, cols]

        return [(i_c, own_blk), (n_adj, prev_blk)]

    tails = {(hh, c): [] for hh in range(PAIR) for c in range(MOBA_QBLOCKS)}
    run_chains(block_chain(hh, n, scores, tails[hh, c])
               for hh in range(PAIR) for c in range(MOBA_QBLOCKS) for n, scores in tail_blocks(hh, c))
    tails = {key: merge(None, parts) for key, parts in tails.items()}
    state = [tuple(jnp.concatenate([tails[hh, c][s] for c in range(MOBA_QBLOCKS)], axis=1) for s in range(2))
             for hh in range(PAIR)]

    def far_trip(first_block, n_blocks, carry):
        parts = [[] for _ in range(PAIR)]
        chains = []
        for hh in range(PAIR):
            for u in range(n_blocks):
                nc = first_block + u
                chains.append(block_chain(hh, nc, lambda nc=nc, hh=hh: _dot_tb(kb_scr[nc], q_far[hh]), parts[hh]))
        run_chains(chains)
        return tuple(x for hh in range(PAIR) for x in merge(carry[2 * hh:2 * hh + 2], parts[hh]))

    n_far = jnp.maximum(i0 + MOBA_QBLOCKS - 2, 0)
    full = n_far // MOBA_UNROLL
    flat = lax.fori_loop(0, full, lambda j, c: far_trip(j * MOBA_UNROLL, MOBA_UNROLL, c),
                         tuple(x for st in state for x in st))
    flat = lax.cond(n_far > full * MOBA_UNROLL,
                    lambda c: far_trip(full * MOBA_UNROLL, MOBA_UNROLL // 2, c), lambda c: c, flat)
    out_t = jnp.concatenate(
        [flat[2 * hh + 1][:HEAD_DIM] / flat[2 * hh + 1][HEAD_DIM:HEAD_DIM + 1] for hh in range(PAIR)], axis=0)
    o_ref[0] = out_t.T


def _moba(rel_bias, u_main, tab, to_cast=()):
    batch, seq, _ = u_main.shape
    npair = ATT_HEADS // PAIR
    nb = seq // MOBA_BLOCK
    assert 3 * nb <= LANE and nb % MOBA_QBLOCKS == 0
    assert MOBA_QBLOCKS == 2 and MOBA_UNROLL == 2 * MOBA_QBLOCKS
    kcol = ATT_WIDTH // LANE
    qrows = MOBA_QBLOCKS * MOBA_BLOCK
    nsteps = nb // MOBA_QBLOCKS
    assert nsteps % MOBA_CAST_RUN == 0
    total_steps = batch * npair * nsteps

    def cast_spec(w):
        rows, cols = w.shape
        n_blocks = min(total_steps // MOBA_CAST_RUN, rows // SUBLANE_BF16)
        assert rows % (n_blocks * SUBLANE_BF16) == 0 and total_steps % n_blocks == 0, w.shape
        run = total_steps // n_blocks
        return pl.BlockSpec((rows // n_blocks, cols), lambda b, p, i: (((b * npair + p) * nsteps + i) // run, 0))

    cast_specs = [cast_spec(w) for w in to_cast]
    outs = pl.pallas_call(
        functools.partial(_moba_kernel, n_cast=len(to_cast)),
        grid=(batch, npair, nsteps),
        in_specs=[
            pl.BlockSpec(memory_space=pltpu.SMEM),
            pl.BlockSpec((1, qrows, LANE), lambda b, p, i: (b, i, p)),
            pl.BlockSpec((1, seq, LANE), lambda b, p, i: (b, 0, kcol + p)),
            pl.BlockSpec((1, seq, LANE), lambda b, p, i: (b, 0, 2 * kcol + p)),
            pl.BlockSpec((1, 2 * PAIR, MOBA_BLOCK, MOBA_BLOCK), lambda b, p, i: (p, 0, 0, 0)),
        ] + cast_specs,
        out_specs=[pl.BlockSpec((1, qrows, LANE), lambda b, p, i: (b, i, p))] + cast_specs,
        out_shape=[jax.ShapeDtypeStruct((batch, seq, ATT_WIDTH), F32)]
        + [jax.ShapeDtypeStruct(w.shape, BF16) for w in to_cast],
        scratch_shapes=[
            pltpu.VMEM((nb, MOBA_BLOCK, 2 * LANE), BF16),
            pltpu.VMEM((nb, PAIR * VT_ROWS, MOBA_BLOCK), BF16),
            pltpu.VMEM((PAIR * nb, LANE), F32),
        ],
        compiler_params=_cparams(("arbitrary", "arbitrary", "arbitrary")),
        name="moba",
    )(rel_bias, u_main, u_main, u_main, tab, *to_cast)
    return outs[0], list(outs[1:])


def _rwkv_chunk_kernel(*refs):
    next_tiles, next_prev, first_tiles, prep_params = refs[0:4], refs[4:8], refs[8:12], refs[12:22]
    rk_ref, gng_ref, gnb_ref, o_ref, st_scr, prep_scr = refs[22:]
    eq_ref = prep_params[-1]
    c = pl.program_id(0)
    nquad = RWKV_WIDTH // QUAD
    nbatch = o_ref.shape[0]

    def prep_chain(bi, tiles, prevs, first):
        vals = []
        yield from _rwkv_prep_tile([lambda t=t: t[bi] for t in tiles], [lambda p=p: p[bi, SUBLANE - 1:SUBLANE, :] for p in prevs],
                                   first, *prep_params, vals)
        yield
        for q, val in enumerate(vals):
            prep_scr[q, bi] = val

    @pl.when(c == 0)
    def _():
        st_scr[...] = jnp.zeros_like(st_scr)
        for _ in itertools.zip_longest(*[prep_chain(bi, first_tiles, next_prev, True) for bi in range(nbatch)]):
            pass

    eq = eq_ref[...]
    levels = int(math.log2(CHUNK))
    row_h = lax.broadcasted_iota(jnp.int32, (QUAD, QUAD), 0) // HEAD_DIM
    col_h = lax.broadcasted_iota(jnp.int32, (QUAD, QUAD), 1) // HEAD_DIM
    bdmask = row_h == col_h
    eye = (lax.broadcasted_iota(jnp.int32, (QUAD, QUAD), 0)
           == lax.broadcasted_iota(jnp.int32, (QUAD, QUAD), 1))
    t_idx = lax.broadcasted_iota(jnp.int32, (CHUNK, QUAD), 0)
    s_idx = lax.broadcasted_iota(jnp.int32, (CHUNK, QUAD), 1) % CHUNK
    strict = t_idx > s_idx
    incl = t_idx >= s_idx
    both = jnp.concatenate([strict, incl], axis=0)
    reps = QUAD // CHUNK

    def bds(x):
        xb = x.astype(BF16)
        return jnp.where(bdmask, jnp.concatenate([xb] * reps, axis=0), jnp.zeros((), BF16))

    def mm(a, b):
        return _dot(a.astype(BF16), b.astype(BF16))

    def quad_stages(bi, qd):
        sl = slice(qd * QUAD, (qd + 1) * QUAD)
        st_ref = st_scr.at[bi * nquad + qd]
        r, k, v, lw, aa, bb, g = (prep_scr[q, bi, :, sl] for q in range(7))

        cum = lw
        for sh in (1 << b for b in range(levels)):
            cum = cum + jnp.where(t_idx >= sh, pltpu.roll(cum, sh, 0), 0.0)
        e_pos = jnp.exp(cum)
        e_neg = jnp.exp(-cum)
        gamma = e_pos[CHUNK - 1:CHUNK, :]
        e_prev = jnp.where(t_idx == 0, 1.0, pltpu.roll(e_pos, 1, 0))
        a_t = aa * e_prev
        r_t = r * e_pos
        b_c = bb * e_neg
        k_c = k * e_neg
        b_h = b_c * gamma
        k_h = k_c * gamma

        lhs = jnp.concatenate([a_t, r_t], axis=0).astype(BF16)
        pb = _dot_tb(lhs, bds(b_c))
        pk = _dot_tb(lhs, bds(k_c))
        bk_t = jnp.concatenate([b_h, k_h], axis=0).T.astype(BF16)
        yield
        a_rb = jnp.where(incl, pb[CHUNK:], 0.0).astype(BF16)
        akrk = mm(jnp.where(both, pk, 0.0), bds(v))
        n = jnp.where(strict, pb[:CHUNK], 0.0)
        t_inv = jnp.where(t_idx == s_idx, 1.0, 0.0) + n
        n = mm(n, bds(n))
        yield
        for lvl in range(1, levels):
            bd_n = bds(n)
            if lvl < levels - 1:
                tn = mm(jnp.concatenate([t_inv, n], axis=0), bd_n)
                t_inv = t_inv + tn[:CHUNK]
                n = tn[CHUNK:]
            else:
                t_inv = t_inv + mm(t_inv, bd_n)
            yield
        w = mm(t_inv, bds(a_t))
        u = mm(t_inv, bds(akrk[:CHUNK]))
        yield
        q_t = r_t + mm(a_rb, bds(w))
        y0 = mm(a_rb, bds(u)) + akrk[CHUNK:]
        w0 = jnp.concatenate([w, jnp.zeros_like(w)], axis=0)
        g_mat = jnp.where(bdmask, mm(bk_t, w0), 0.0) + jnp.where(eye, gamma, 0.0)
        h_mat = jnp.where(bdmask, mm(bk_t, jnp.concatenate([u, v], axis=0)), 0.0)
        yield
        gy = mm(jnp.concatenate([g_mat, q_t], axis=0), st_ref[...])
        st_ref[...] = gy[:QUAD] + h_mat
        y = gy[QUAD:] + y0
        yield

        inv_n = 1.0 / HEAD_DIM
        stats = mm(jnp.concatenate([y, r * k * rk_ref[:, sl]], axis=0), eq)
        mu = stats[:CHUNK] * inv_n
        bonus = stats[CHUNK:] * v
        yield
        d = y - mu
        var = mm(d * d, eq) * inv_n
        yield
        yn = d * lax.rsqrt(var + GN_EPS) * gng_ref[:, sl] + gnb_ref[:, sl]
        o_ref[bi, :, sl] = (yn + bonus) * g

    chains = [quad_stages(bi, qd) for bi in range(nbatch) for qd in range(nquad)]
    chains += [prep_chain(bi, next_tiles, next_prev, False) for bi in range(nbatch)]
    for _ in itertools.zip_longest(*chains):
        pass


def _rwkv(u_main, u_tail, prep_params, r_k, gn_g, gn_b):
    batch, seq, _ = u_main.shape
    width = RWKV_WIDTH
    nchunks = seq // CHUNK
    rcol = 3 * ATT_WIDTH // width
    nxt = lambda c: jnp.minimum(c + 1, nchunks - 1)
    prev_tile = lambda c: nxt(c) * (CHUNK // SUBLANE) - 1
    const = lambda a: pl.BlockSpec(a.shape, lambda c: (0,) * a.ndim)

    def tiles(rows, row_index):
        return ([pl.BlockSpec((batch, rows, width), lambda c, g=g: (0, row_index(c), rcol + g)) for g in range(3)]
                + [pl.BlockSpec((batch, rows, TAIL_COLS), lambda c: (0, row_index(c), 0))])

    u_args = (u_main, u_main, u_main, u_tail)
    return pl.pallas_call(
        _rwkv_chunk_kernel,
        grid=(nchunks,),
        in_specs=tiles(CHUNK, nxt) + tiles(SUBLANE, prev_tile) + tiles(CHUNK, lambda c: 0)
        + [const(a) for a in prep_params] + [const(r_k), const(gn_g), const(gn_b)],
        out_specs=pl.BlockSpec((batch, CHUNK, width), lambda c: (0, c, 0)),
        out_shape=jax.ShapeDtypeStruct((batch, seq, width), F32),
        scratch_shapes=[pltpu.VMEM((batch * width // QUAD, QUAD, QUAD), F32),
                        pltpu.VMEM((7, batch, CHUNK, width), F32)],
        compiler_params=_cparams(("arbitrary",)),
        name="rwkv_chunk",
    )(*u_args, *u_args, *u_args, *prep_params, r_k, gn_g, gn_b)


def _row_pipelined(n_rows, matmul, finish):
    pending = None
    for r0 in range(0, n_rows, SUB_ROWS):
        rows = slice(r0, r0 + SUB_ROWS)
        acc = matmul(rows)
        if pending is not None:
            finish(*pending)
        pending = (rows, acc)
    finish(*pending)


def _out_ln_kernel(h_ref, att_ref, rw_ref, wa_ref, wr_ref, g_ref, b_ref, o_ref):
    def matmul(rows):
        return (_dot(att_ref[rows, :].astype(BF16), wa_ref[...])
                + _dot(rw_ref[rows, :].astype(BF16), wr_ref[...]))

    def finish(rows, mix):
        o_ref[rows, :] = _layer_norm(ALPHA * h_ref[rows, :] + mix, g_ref[...], b_ref[...])

    _row_pipelined(h_ref.shape[0], matmul, finish)


def _out_ln(h, att, rw, w_out, g, b, tm=512):
    t, d = h.shape
    wa, wr = att.shape[1], rw.shape[1]
    assert wa == wr and w_out.shape[0] == wa + wr
    row = lambda n: pl.BlockSpec((tm, n), lambda i: (i, 0))
    full = lambda shape: pl.BlockSpec(shape, lambda i: (0,) * len(shape))
    return pl.pallas_call(
        _out_ln_kernel,
        grid=(t // tm,),
        in_specs=[row(d), row(wa), row(wr),
                  pl.BlockSpec((wa, d), lambda i: (0, 0)), pl.BlockSpec((wr, d), lambda i: (1, 0)),
                  full((1, d)), full((1, d))],
        out_specs=row(d),
        out_shape=jax.ShapeDtypeStruct((t, d), F32),
        compiler_params=_cparams(("parallel",)),
        name="out_ln",
    )(h, att, rw, w_out, w_out, g, b)


def _ple_ln_kernel(h_ref, p_ref, wg_ref, bg_ref, wp_ref, g_ref, b_ref, o_ref):
    def matmul(rows):
        return (_dot(h_ref[rows, :].astype(BF16), wg_ref[...]), _dot(p_ref[rows, :].astype(BF16), wp_ref[...]))

    def finish(rows, acc):
        pre_gate, e = acc
        y = ALPHA * h_ref[rows, :] + _sigmoid(pre_gate + bg_ref[...]) * e
        o_ref[rows, :] = _layer_norm(y, g_ref[...], b_ref[...])

    _row_pipelined(h_ref.shape[0], matmul, finish)


def _ple_ln(h, p, w_gate, b_gate, w_up, g, b, tm=512):
    t, d = h.shape
    row = lambda n: pl.BlockSpec((tm, n), lambda i: (i, 0))
    full = lambda shape: pl.BlockSpec(shape, lambda i: (0,) * len(shape))
    return pl.pallas_call(
        _ple_ln_kernel,
        grid=(t // tm,),
        in_specs=[row(d), row(p.shape[1]), full(w_gate.shape), full((1, d)), full(w_up.shape),
                  full((1, d)), full((1, d))],
        out_specs=row(d),
        out_shape=jax.ShapeDtypeStruct((t, d), F32),
        compiler_params=_cparams(("parallel",)),
        name="ple_ln",
    )(h, p, w_gate, b_gate, w_up, g, b)


def _pad_rows(w, rows):
    return jnp.pad(w, ((0, rows - w.shape[0]), (0, 0)))


def _pack_tail(cols, axis):
    wd, ad, gd = jnp.split(cols, [DECAY_LORA, DECAY_LORA + AAA_LORA], axis=axis)

    def pad(x, n):
        widths = [(0, 0)] * x.ndim
        widths[axis] = (0, n - x.shape[axis])
        return jnp.pad(x, widths)

    return jnp.concatenate([pad(wd, LANE), pad(ad, LANE), pad(gd, GD_PAD)], axis=axis)


def kernel(x, p, ffn1_w_gate, ffn1_w_up, ffn1_w_down, ln1_g, ln1_b, w_in, rel_bias, shift_mix, decay_w0, decay_w2, a_a0, a_a2, gate_g2, k_k, k_a, r_k, gn_g, gn_b, w_out, ln2_g, ln2_b, ffn2_w_gate, ffn2_w_up, ffn2_w_down, ln3_g, ln3_b, ple_w_up, ple_w_gate, ple_b_gate, ln4_g, ln4_b):
    batch, seq, d = x.shape
    t = batch * seq
    row = lambda a: a.reshape(1, -1)
    main_cols = 3 * ATT_WIDTH + 3 * RWKV_WIDTH

    heads = jnp.arange(QUAD) // HEAD_DIM
    esum = (heads[:, None] == heads[None, :]).astype(BF16)
    tab = _bias_tab(rel_bias)

    h = x.reshape(t, d).astype(F32)
    for i in range(DEPTH):
        w_in_t = w_in[i].T
        later_weights = (ffn2_w_gate[i], ffn2_w_up[i], ffn2_w_down[i], w_out[i], ple_w_gate[i], ple_w_up[i])
        ln1 = (row(ln1_g[i]), row(ln1_b[i]))
        head, (head_bf16,), ffn1_bf16 = _ffn_ln(h, ffn1_w_gate[i], ffn1_w_up[i], ffn1_w_down[i], *ln1,
                                                emit_bf16=True, copy_weights=True, tm=FFN_HEAD_TM, tf=FFN_HEAD_TF)
        h, (h_bf16,), _ = _ffn_ln(h, *ffn1_bf16, *ln1, emit_bf16=True, head=(head, head_bf16))
        u_main, u_tail = _proj(h_bf16, w_in_t, main_cols, _pack_tail(w_in_t[main_cols:], axis=0))
        mix = shift_mix[i]
        prep_params = (row(mix[:3 * RWKV_WIDTH]), row(_pack_tail(mix[3 * RWKV_WIDTH:], axis=0)),
                       row(decay_w0[i]), _pad_rows(decay_w2[i], LANE).astype(BF16), row(a_a0[i]),
                       _pad_rows(a_a2[i], LANE).astype(BF16), _pad_rows(gate_g2[i], GD_PAD).astype(BF16),
                       row(k_k[i]), row(k_a[i]), esum)
        u_main3 = u_main.reshape(batch, seq, main_cols)
        att, (wg2, wu2, wd2, wo, wpg, wpu) = _moba(rel_bias, u_main3, tab, to_cast=later_weights)
        rw = _rwkv(u_main3, u_tail.reshape(batch, seq, TAIL_COLS), prep_params,
                   row(r_k[i]), row(gn_g[i]), row(gn_b[i]))
        h = _out_ln(h, att.reshape(t, ATT_WIDTH), rw.reshape(t, RWKV_WIDTH), wo,
                    row(ln2_g[i]), row(ln2_b[i]))
        h, _, _ = _ffn_ln(h, wg2, wu2, wd2, row(ln3_g[i]), row(ln3_b[i]))
        h = _ple_ln(h, p[i].reshape(t, PLE_DIM).astype(F32), wpg, row(ple_b_gate[i]), wpu,
                    row(ln4_g[i]), row(ln4_b[i]))
    return h.reshape(batch, seq, d).astype(x.dtype)
```

```python
import functools
import itertools
import math

import jax
import jax.numpy as jnp
from jax import lax
from jax.experimental import pallas as pl
from jax.experimental.pallas import tpu as pltpu

DEPTH = 1
HEAD_DIM = 64
ATT_HEADS = 16
RWKV_HEADS = 16
ATT_WIDTH = ATT_HEADS * HEAD_DIM
RWKV_WIDTH = RWKV_HEADS * HEAD_DIM
MOBA_BLOCK = 256
MOBA_TOPK = 3
MOBA_UNROLL = 4
MOBA_QBLOCKS = 2
MOBA_CAST_RUN = 4
LOG2E = 1.4426950408889634
REL_BUCKETS = 32
REL_MAX_DIST = 128
DECAY_LORA = 64
AAA_LORA = 64
PLE_DIM = 256
LN_EPS = 1e-5
GN_EPS = 64e-5
NEG = -1e30
ALPHA = (2 * DEPTH) ** 0.25

LANE = 128
SUBLANE = 8
SUBLANE_BF16 = 16
PAIR = LANE // HEAD_DIM
CHUNK = 64
QUAD = 256
ONES_ROWS = SUBLANE_BF16
VT_ROWS = HEAD_DIM + ONES_ROWS
FFN_HEAD_TM = 1024
FFN_HEAD_TF = 256
SUB_ROWS = 128
TAIL_COLS = 512
TAIL_WD, TAIL_AD, TAIL_GD = 0, 128, 256
GD_PAD = 256
VMEM_LIMIT = 56 * 1024 * 1024

F32 = jnp.float32
BF16 = jnp.bfloat16
HI = lax.Precision.HIGHEST


def _cparams(sem):
    return pltpu.CompilerParams(dimension_semantics=sem, vmem_limit_bytes=VMEM_LIMIT)


def _layer_norm(y, g, b):
    mu = jnp.mean(y, axis=-1, keepdims=True)
    d = y - mu
    var = jnp.mean(d * d, axis=-1, keepdims=True)
    return d * lax.rsqrt(var + LN_EPS) * g + b


def _sigmoid(x):
    return 0.5 * jnp.tanh(0.5 * x) + 0.5


def _dot(a, b):
    return jnp.dot(a, b, preferred_element_type=F32)


def _dot_tb(a, b, precision=None):
    return lax.dot_general(a, b, (((1,), (1,)), ((), ())), preferred_element_type=F32,
                           precision=precision)


def _split3(x):
    hi = x.astype(BF16)
    r1 = x - hi.astype(F32)
    mid = r1.astype(BF16)
    lo = (r1 - mid.astype(F32)).astype(BF16)
    return hi, mid, lo


def _ffn_ln_kernel(*refs, n_f, n_pass, emit_bf16, copy_weights):
    x_ref, wg_ref, wu_ref, wd_ref, g_ref, b_ref = refs[:6]
    rest = list(refs[6:])
    head_ref, head_bf16_ref = (rest.pop(0), rest.pop(0)) if n_pass else (None, None)
    o_ref = rest.pop(0)
    ob_ref = rest.pop(0) if emit_bf16 else None
    w_copies = [rest.pop(0) for _ in range(3)] if copy_weights else []
    (xb_ref,) = rest
    step = pl.program_id(0) - n_pass
    f = lax.rem(step, n_f)
    last = n_f - 1

    def compute():
        @pl.when(f == 0)
        def _():
            x = x_ref[...]
            xb_ref[...] = x.astype(BF16)
            o_ref[...] = ALPHA * x

        weights = [w_ref[...] for w_ref in (wg_ref, wu_ref, wd_ref)]
        if copy_weights:
            weights = [w.astype(BF16) for w in weights]
            for dst_ref, w in zip(w_copies, weights):
                dst_ref[...] = w
        wg, wu, wd = weights
        xb = xb_ref[...]
        gate = _dot(xb, wg)
        up = _dot(xb, wu)
        act = gate * jax.nn.sigmoid(gate) * up
        o_ref[...] += 0.5 * _dot(act.astype(BF16), wd)

        @pl.when(f == last)
        def _():
            y = _layer_norm(o_ref[...], g_ref[...], b_ref[...])
            o_ref[...] = y
            if emit_bf16:
                ob_ref[...] = y.astype(BF16)

    if n_pass == 0:
        compute()
        return
    pl.when(step >= 0)(compute)

    @pl.when(step < 0)
    def _():
        o_ref[...] = head_ref[...]
        ob_ref[...] = head_bf16_ref[...]


def _ffn_ln(x, wg, wu, wd, g, b, emit_bf16=False, copy_weights=False, head=None, tm=512, tf=512):
    t, d = x.shape
    ff = wg.shape[1]
    n_i, n_f = (1 if copy_weights else t // tm), ff // tf
    n_pass = head[0].shape[0] // tm if head else 0
    assert emit_bf16 or not head

    def row_f(s):
        c = jnp.maximum(s - n_pass, 0)
        return n_pass + c // n_f, c % n_f

    resident = dict(pipeline_mode=pl.Buffered(1)) if n_i == 1 else {}
    vec = pl.BlockSpec((1, d), lambda s: (0, 0))
    in_specs = [
        pl.BlockSpec((tm, d), lambda s: (row_f(s)[0], 0), **resident),
        pl.BlockSpec((d, tf), lambda s: (0, row_f(s)[1])),
        pl.BlockSpec((d, tf), lambda s: (0, row_f(s)[1])),
        pl.BlockSpec((tf, d), lambda s: (row_f(s)[1], 0)),
        vec, vec,
    ] + [pl.BlockSpec((tm, d), lambda s: (jnp.minimum(s, n_pass - 1), 0))] * (2 * int(n_pass > 0))
    n_res = 1 + int(emit_bf16)
    out_specs = [pl.BlockSpec((tm, d), lambda s: (jnp.where(s < n_pass, s, row_f(s)[0]), 0), **resident)] * n_res
    out_shape = [jax.ShapeDtypeStruct((n_i * tm, d), dt) for dt in (F32, BF16)[:n_res]]
    if copy_weights:
        out_specs += [pl.BlockSpec(spec.block_shape, spec.index_map) for spec in in_specs[1:4]]
        out_shape += [jax.ShapeDtypeStruct(w.shape, BF16) for w in (wg, wu, wd)]
    outs = pl.pallas_call(
        functools.partial(_ffn_ln_kernel, n_f=n_f, n_pass=n_pass, emit_bf16=emit_bf16, copy_weights=copy_weights),
        grid=(n_pass + (n_i - n_pass) * n_f,),
        in_specs=in_specs,
        out_specs=out_specs,
        out_shape=out_shape,
        scratch_shapes=[pltpu.VMEM((tm, d), BF16)],
        compiler_params=_cparams(("arbitrary",)),
        name="ffn_ln",
    )(x, wg, wu, wd, g, b, *(head or ()))
    return outs[0], list(outs[1:n_res]), list(outs[n_res:])


def _proj_kernel(x_ref, w_ref, wt_ref, o_ref, ot_ref):
    @pl.when(pl.program_id(1) == 0)
    def _():
        ot_ref[...] = _dot_tb(x_ref[...], wt_ref[...].astype(BF16))

    o_ref[...] = _dot_tb(x_ref[...], w_ref[...].astype(BF16))


def _proj(x, w_t, n, w_tail_t, tm=2048, tn=512):
    t, d = x.shape
    n_tail = w_tail_t.shape[0]
    return pl.pallas_call(
        _proj_kernel,
        grid=(t // tm, n // tn),
        in_specs=[
            pl.BlockSpec((tm, d), lambda i, j: (i, 0)),
            pl.BlockSpec((tn, d), lambda i, j: (j, 0)),
            pl.BlockSpec((n_tail, d), lambda i, j: (0, 0), pipeline_mode=pl.Buffered(1)),
        ],
        out_specs=[pl.BlockSpec((tm, tn), lambda i, j: (i, j)), pl.BlockSpec((tm, n_tail), lambda i, j: (i, 0))],
        out_shape=[jax.ShapeDtypeStruct((t, n), F32), jax.ShapeDtypeStruct((t, n_tail), F32)],
        compiler_params=_cparams(("parallel", "arbitrary")),
        name="proj",
    )(x, w_t, w_tail_t)


def _rel_bucket(dist):
    max_exact = REL_BUCKETS // 2
    n = jnp.maximum(dist, 0)
    nf = jnp.maximum(n, 1).astype(F32)
    large = max_exact + (jnp.log(nf / max_exact) / math.log(REL_MAX_DIST / max_exact)
                         * (REL_BUCKETS - max_exact)).astype(jnp.int32)
    large = jnp.minimum(large, REL_BUCKETS - 1)
    return jnp.where(n < max_exact, n, large)


def _bias_tab_kernel(rb_ref, o_ref):
    p = pl.program_id(0)
    blk = MOBA_BLOCK
    diff = lax.broadcasted_iota(jnp.int32, (SUBLANE, 2 * blk), 1) - blk
    for kind in range(2):
        dist = diff + kind * blk
        bucket = _rel_bucket(dist)
        for hh in range(PAIR):
            h = PAIR * p + hh
            vec = jnp.zeros(dist.shape, F32)
            for bkt in range(REL_BUCKETS):
                vec = jnp.where(bucket == bkt, rb_ref[bkt, h], vec)
            vec = vec * LOG2E
            if kind == 0:
                vec = jnp.where(dist >= 0, vec, NEG)
            rows = jnp.broadcast_to(vec[0:1], (blk, 2 * blk))
            o_ref[0, 2 * hh + kind] = pltpu.roll(rows, 0, 1, stride=1, stride_axis=0)[:, blk:]


def _bias_tab(rel_bias):
    npair = ATT_HEADS // PAIR
    return pl.pallas_call(
        _bias_tab_kernel,
        grid=(npair,),
        in_specs=[pl.BlockSpec(memory_space=pltpu.SMEM)],
        out_specs=pl.BlockSpec((1, 2 * PAIR, MOBA_BLOCK, MOBA_BLOCK), lambda p: (p, 0, 0, 0)),
        out_shape=jax.ShapeDtypeStruct((npair, 2 * PAIR, MOBA_BLOCK, MOBA_BLOCK), F32),
        compiler_params=_cparams(("arbitrary",)),
        name="bias_tab",
    )(rel_bias)


def _rwkv_prep_tile(tiles, prev_rows, first, mixm_ref, mixt_ref, w0_ref, w2_ref, a0_ref, a2_ref, g2_ref,
                    kk_ref, ka_ref, esum_ref, out):
    width = RWKV_WIDTH

    def shifted(g, mix):
        x = tiles[g]()
        prev = pltpu.roll(x, 1, 0)
        row = lax.broadcasted_iota(jnp.int32, x.shape, 0)
        last = 0.0 if first is True else jnp.where(first, 0.0, prev_rows[g]())
        prev = jnp.where(row == 0, last, prev)
        return x + (prev - x) * mix

    tail = shifted(3, mixt_ref[...])
    wd = tail[:, TAIL_WD:TAIL_WD + LANE]
    ad = tail[:, TAIL_AD:TAIL_AD + LANE]
    gd = tail[:, TAIL_GD:TAIL_GD + GD_PAD]
    y = w0_ref[...] + _dot(jnp.tanh(wd).astype(BF16), w2_ref[...])
    yield
    lw = -math.exp(-0.5) * _sigmoid(y)
    yield
    a = _sigmoid(a0_ref[...] + _dot(ad.astype(BF16), a2_ref[...]))
    yield
    g = _dot(_sigmoid(gd).astype(BF16), g2_ref[...])
    yield
    k = shifted(1, mixm_ref[:, width:2 * width])
    kk = k * kk_ref[...]
    kk2 = kk * kk
    yield
    ss = []
    for c in range(0, width, QUAD):
        ss.append(_dot(kk2[:, c:c + QUAD].astype(BF16), esum_ref[...]))
        yield
    kk = kk * jnp.minimum(lax.rsqrt(jnp.concatenate(ss, axis=1)), 1e12)
    k = k * (1.0 + (a - 1.0) * ka_ref[...])
    yield
    r = shifted(0, mixm_ref[:, 0:width])
    yield
    v = shifted(2, mixm_ref[:, 2 * width:3 * width])
    out.extend((r, k, v, lw, -kk, kk * a, g))


def _moba_kernel(rb_ref, q_ref, k_ref, v_ref, tab_ref, *refs, n_cast):
    cast_in, o_ref, cast_out = refs[:n_cast], refs[n_cast], refs[n_cast + 1:2 * n_cast + 1]
    kb_scr, vt_scr, kmean_scr = refs[2 * n_cast + 1:]
    p = pl.program_id(1)
    i0 = pl.program_id(2) * MOBA_QBLOCKS
    nb = kb_scr.shape[0]
    blk = MOBA_BLOCK

    @pl.when(pl.program_id(2) % MOBA_CAST_RUN == 0)
    def _():
        for src_ref, dst_ref in zip(cast_in, cast_out):
            dst_ref[...] = src_ref[...].astype(BF16)

    @pl.when(i0 == 0)
    def _():
        lane_k = lax.broadcasted_iota(jnp.int32, (blk, LANE), 1)
        lane_m = lax.broadcasted_iota(jnp.int32, (1, LANE), 1)
        ones = jnp.ones((ONES_ROWS, blk), BF16)
        for n in range(nb):
            kblk = k_ref[0, n * blk:(n + 1) * blk, :]
            kb_scr[n, :, :LANE] = kblk.astype(BF16)
            hot = (lane_k == n) | (lane_k == nb + n) | (lane_k == 2 * nb + n)
            kb_scr[n, :, LANE:] = jnp.where(hot, 1.0, 0.0).astype(BF16)
            kmean = jnp.mean(kblk, axis=0, keepdims=True)
            v_t = v_ref[0, n * blk:(n + 1) * blk, :].T.astype(BF16)
            for hh in range(PAIR):
                in_head = (lane_m >= hh * HEAD_DIM) & (lane_m < (hh + 1) * HEAD_DIM)
                kmean_scr[hh * nb + n:hh * nb + n + 1, :] = jnp.where(in_head, kmean, 0.0)
                vt_scr[n, hh * VT_ROWS:hh * VT_ROWS + HEAD_DIM, :] = v_t[hh * HEAD_DIM:(hh + 1) * HEAD_DIM, :]
                vt_scr[n, hh * VT_ROWS + HEAD_DIM:(hh + 1) * VT_ROWS, :] = ones

    q8 = q_ref[0] * (HEAD_DIM ** -0.5)
    qs = q8 * LOG2E
    nq = q8.shape[0]
    lane = lax.broadcasted_iota(jnp.int32, q8.shape, 1)
    nidx = lax.broadcasted_iota(jnp.int32, (nb, nq), 0)
    own = i0 + lax.broadcasted_iota(jnp.int32, (1, nq), 1) // blk
    gate_all = _dot_tb(kmean_scr[...], q8, precision=HI)

    q_tail, q_far, adj_mask = [], [], []
    for hh in range(PAIR):
        gate = gate_all[hh * nb:(hh + 1) * nb]
        avail = nidx < own
        sel = jnp.zeros((nb, nq), jnp.bool_)
        for _ in range(MOBA_TOPK):
            best = jnp.max(jnp.where(avail, gate, -jnp.inf), axis=0, keepdims=True)
            first = jnp.min(jnp.where(avail & (gate == best), nidx, nb), axis=0, keepdims=True)
            pick = nidx == first
            sel = sel | pick
            avail = avail & jnp.logical_not(pick)
        b_far = rb_ref[REL_BUCKETS - 1, PAIR * p + hh] * LOG2E
        rowb = jnp.where(sel & (nidx < own - 1), b_far, NEG)
        adj_mask.append(jnp.max(jnp.where(sel & (nidx == own - 1), 0.0, NEG), axis=0, keepdims=True))
        hi, mid, lo = _split3(rowb)
        parts = jnp.concatenate([hi.astype(F32), mid.astype(F32), lo.astype(F32),
                                 jnp.zeros((LANE - 3 * nb, nq), F32)], axis=0)
        in_head = (lane >= hh * HEAD_DIM) & (lane < (hh + 1) * HEAD_DIM)
        q_h = jnp.where(in_head, qs, 0.0).astype(BF16)
        q_tail.append(q_h)
        q_far.append(jnp.concatenate([q_h, parts.T.astype(BF16)], axis=1))

    def vt(hh, n):
        return vt_scr[n, hh * VT_ROWS:(hh + 1) * VT_ROWS, :]

    def block_chain(hh, n, scores, parts):
        s = scores()
        yield
        m_s = jnp.max(s, axis=0, keepdims=True)
        yield
        parts.append((m_s, _dot(vt(hh, n), jnp.exp2(s - m_s).astype(BF16))))

    def run_chains(gens):
        for _ in itertools.zip_longest(*gens):
            pass

    def merge(prev, parts):
        m_new = parts[0][0] if prev is None else jnp.maximum(prev[0], parts[0][0])
        for m_s, _ in parts[1:]:
            m_new = jnp.maximum(m_new, m_s)
        acc = None if prev is None else jnp.exp2(prev[0] - m_new) * prev[1]
        for m_s, acc_s in parts:
            term = jnp.exp2(m_s - m_new) * acc_s
            acc = term if acc is None else acc + term
        return m_new, acc

    def tail_blocks(hh, c):
        cols = slice(c * blk, (c + 1) * blk)
        i_c = i0 + c
        n_adj = jnp.maximum(i_c - 1, 0)
        q_c = q_tail[hh][cols]

        def own_blk():
            return _dot_tb(kb_scr[i_c, :, :LANE], q_c) + tab_ref[0, 2 * hh]

        def prev_blk():
            return _dot_tb(kb_scr[n_adj, :, :LANE], q_c) + tab_ref[0, 2 * hh + 1] + adj_mask[hh][:, cols]

        return [(i_c, own_blk), (n_adj, prev_blk)]

    tails = {(hh, c): [] for hh in range(PAIR) for c in range(MOBA_QBLOCKS)}
    run_chains(block_chain(hh, n, scores, tails[hh, c])
               for hh in range(PAIR) for c in range(MOBA_QBLOCKS) for n, scores in tail_blocks(hh, c))
    tails = {key: merge(None, parts) for key, parts in tails.items()}
    state = [tuple(jnp.concatenate([tails[hh, c][s] for c in range(MOBA_QBLOCKS)], axis=1) for s in range(2))
             for hh in range(PAIR)]

    def far_trip(first_block, n_blocks, carry):
        parts = [[] for _ in range(PAIR)]
        chains = []
        for hh in range(PAIR):
            for u in range(n_blocks):
                nc = first_block + u
                chains.append(block_chain(hh, nc, lambda nc=nc, hh=hh: _dot_tb(kb_scr[nc], q_far[hh]), parts[hh]))
        run_chains(chains)
        return tuple(x for hh in range(PAIR) for x in merge(carry[2 * hh:2 * hh + 2], parts[hh]))

    n_far = jnp.maximum(i0 + MOBA_QBLOCKS - 2, 0)
    full = n_far // MOBA_UNROLL
    flat = lax.fori_loop(0, full, lambda j, c: far_trip(j * MOBA_UNROLL, MOBA_UNROLL, c),
                         tuple(x for st in state for x in st))
    flat = lax.cond(n_far > full * MOBA_UNROLL,
                    lambda c: far_trip(full * MOBA_UNROLL, MOBA_UNROLL // 2, c), lambda c: c, flat)
    out_t = jnp.concatenate(
        [flat[2 * hh + 1][:HEAD_DIM] / flat[2 * hh + 1][HEAD_DIM:HEAD_DIM + 1] for hh in range(PAIR)], axis=0)
    o_ref[0] = out_t.T


def _moba(rel_bias, u_main, tab, to_cast=()):
    batch, seq, _ = u_main.shape
    npair = ATT_HEADS // PAIR
    nb = seq // MOBA_BLOCK
    assert 3 * nb <= LANE and nb % MOBA_QBLOCKS == 0
    assert MOBA_QBLOCKS == 2 and MOBA_UNROLL == 2 * MOBA_QBLOCKS
    kcol = ATT_WIDTH // LANE
    qrows = MOBA_QBLOCKS * MOBA_BLOCK
    nsteps = nb // MOBA_QBLOCKS
    assert nsteps % MOBA_CAST_RUN == 0
    total_steps = batch * npair * nsteps

    def cast_spec(w):
        rows, cols = w.shape
        n_blocks = min(total_steps // MOBA_CAST_RUN, rows // SUBLANE_BF16)
        assert rows % (n_blocks * SUBLANE_BF16) == 0 and total_steps % n_blocks == 0, w.shape
        run = total_steps // n_blocks
        return pl.BlockSpec((rows // n_blocks, cols), lambda b, p, i: (((b * npair + p) * nsteps + i) // run, 0))

    cast_specs = [cast_spec(w) for w in to_cast]
    outs = pl.pallas_call(
        functools.partial(_moba_kernel, n_cast=len(to_cast)),
        grid=(batch, npair, nsteps),
        in_specs=[
            pl.BlockSpec(memory_space=pltpu.SMEM),
            pl.BlockSpec((1, qrows, LANE), lambda b, p, i: (b, i, p)),
            pl.BlockSpec((1, seq, LANE), lambda b, p, i: (b, 0, kcol + p)),
            pl.BlockSpec((1, seq, LANE), lambda b, p, i: (b, 0, 2 * kcol + p)),
            pl.BlockSpec((1, 2 * PAIR, MOBA_BLOCK, MOBA_BLOCK), lambda b, p, i: (p, 0, 0, 0)),
        ] + cast_specs,
        out_specs=[pl.BlockSpec((1, qrows, LANE), lambda b, p, i: (b, i, p))] + cast_specs,
        out_shape=[jax.ShapeDtypeStruct((batch, seq, ATT_WIDTH), F32)]
        + [jax.ShapeDtypeStruct(w.shape, BF16) for w in to_cast],
        scratch_shapes=[
            pltpu.VMEM((nb, MOBA_BLOCK, 2 * LANE), BF16),
            pltpu.VMEM((nb, PAIR * VT_ROWS, MOBA_BLOCK), BF16),
            pltpu.VMEM((PAIR * nb, LANE), F32),
        ],
        compiler_params=_cparams(("arbitrary", "arbitrary", "arbitrary")),
        name="moba",
    )(rel_bias, u_main, u_main, u_main, tab, *to_cast)
    return outs[0], list(outs[1:])


def _rwkv_chunk_kernel(*refs):
    next_tiles, next_prev, first_tiles, prep_params = refs[0:4], refs[4:8], refs[8:12], refs[12:22]
    rk_ref, gng_ref, gnb_ref, o_ref, st_scr, prep_scr = refs[22:]
    eq_ref = prep_params[-1]
    c = pl.program_id(0)
    nquad = RWKV_WIDTH // QUAD
    nbatch = o_ref.shape[0]

    def prep_chain(bi, tiles, prevs, first):
        vals = []
        yield from _rwkv_prep_tile([lambda t=t: t[bi] for t in tiles], [lambda p=p: p[bi, SUBLANE - 1:SUBLANE, :] for p in prevs],
                                   first, *prep_params, vals)
        yield
        for q, val in enumerate(vals):
            prep_scr[q, bi] = val

    @pl.when(c == 0)
    def _():
        st_scr[...] = jnp.zeros_like(st_scr)
        for _ in itertools.zip_longest(*[prep_chain(bi, first_tiles, next_prev, True) for bi in range(nbatch)]):
            pass

    eq = eq_ref[...]
    levels = int(math.log2(CHUNK))
    row_h = lax.broadcasted_iota(jnp.int32, (QUAD, QUAD), 0) // HEAD_DIM
    col_h = lax.broadcasted_iota(jnp.int32, (QUAD, QUAD), 1) // HEAD_DIM
    bdmask = row_h == col_h
    eye = (lax.broadcasted_iota(jnp.int32, (QUAD, QUAD), 0)
           == lax.broadcasted_iota(jnp.int32, (QUAD, QUAD), 1))
    t_idx = lax.broadcasted_iota(jnp.int32, (CHUNK, QUAD), 0)
    s_idx = lax.broadcasted_iota(jnp.int32, (CHUNK, QUAD), 1) % CHUNK
    strict = t_idx > s_idx
    incl = t_idx >= s_idx
    both = jnp.concatenate([strict, incl], axis=0)
    reps = QUAD // CHUNK

    def bds(x):
        xb = x.astype(BF16)
        return jnp.where(bdmask, jnp.concatenate([xb] * reps, axis=0), jnp.zeros((), BF16))

    def mm(a, b):
        return _dot(a.astype(BF16), b.astype(BF16))

    def quad_stages(bi, qd):
        sl = slice(qd * QUAD, (qd + 1) * QUAD)
        st_ref = st_scr.at[bi * nquad + qd]
        r, k, v, lw, aa, bb, g = (prep_scr[q, bi, :, sl] for q in range(7))

        cum = lw
        for sh in (1 << b for b in range(levels)):
            cum = cum + jnp.where(t_idx >= sh, pltpu.roll(cum, sh, 0), 0.0)
        e_pos = jnp.exp(cum)
        e_neg = jnp.exp(-cum)
        gamma = e_pos[CHUNK - 1:CHUNK, :]
        e_prev = jnp.where(t_idx == 0, 1.0, pltpu.roll(e_pos, 1, 0))
        a_t = aa * e_prev
        r_t = r * e_pos
        b_c = bb * e_neg
        k_c = k * e_neg
        b_h = b_c * gamma
        k_h = k_c * gamma

        lhs = jnp.concatenate([a_t, r_t], axis=0).astype(BF16)
        pb = _dot_tb(lhs, bds(b_c))
        pk = _dot_tb(lhs, bds(k_c))
        bk_t = jnp.concatenate([b_h, k_h], axis=0).T.astype(BF16)
        yield
        a_rb = jnp.where(incl, pb[CHUNK:], 0.0).astype(BF16)
        akrk = mm(jnp.where(both, pk, 0.0), bds(v))
        n = jnp.where(strict, pb[:CHUNK], 0.0)
        t_inv = jnp.where(t_idx == s_idx, 1.0, 0.0) + n
        n = mm(n, bds(n))
        yield
        for lvl in range(1, levels):
            bd_n = bds(n)
            if lvl < levels - 1:
                tn = mm(jnp.concatenate([t_inv, n], axis=0), bd_n)
                t_inv = t_inv + tn[:CHUNK]
                n = tn[CHUNK:]
            else:
                t_inv = t_inv + mm(t_inv, bd_n)
            yield
        w = mm(t_inv, bds(a_t))
        u = mm(t_inv, bds(akrk[:CHUNK]))
        yield
        q_t = r_t + mm(a_rb, bds(w))
        y0 = mm(a_rb, bds(u)) + akrk[CHUNK:]
        w0 = jnp.concatenate([w, jnp.zeros_like(w)], axis=0)
        g_mat = jnp.where(bdmask, mm(bk_t, w0), 0.0) + jnp.where(eye, gamma, 0.0)
        h_mat = jnp.where(bdmask, mm(bk_t, jnp.concatenate([u, v], axis=0)), 0.0)
        yield
        gy = mm(jnp.concatenate([g_mat, q_t], axis=0), st_ref[...])
        st_ref[...] = gy[:QUAD] + h_mat
        y = gy[QUAD:] + y0
        yield

        inv_n = 1.0 / HEAD_DIM
        stats = mm(jnp.concatenate([y, r * k * rk_ref[:, sl]], axis=0), eq)
        mu = stats[:CHUNK] * inv_n
        bonus = stats[CHUNK:] * v
        yield
        d = y - mu
        var = mm(d * d, eq) * inv_n
        yield
        yn = d * lax.rsqrt(var + GN_EPS) * gng_ref[:, sl] + gnb_ref[:, sl]
        o_ref[bi, :, sl] = (yn + bonus) * g

    chains = [quad_stages(bi, qd) for bi in range(nbatch) for qd in range(nquad)]
    chains += [prep_chain(bi, next_tiles, next_prev, False) for bi in range(nbatch)]
    for _ in itertools.zip_longest(*chains):
        pass


def _rwkv(u_main, u_tail, prep_params, r_k, gn_g, gn_b):
    batch, seq, _ = u_main.shape
    width = RWKV_WIDTH
    nchunks = seq // CHUNK
    rcol = 3 * ATT_WIDTH // width
    nxt = lambda c: jnp.minimum(c + 1, nchunks - 1)
    prev_tile = lambda c: nxt(c) * (CHUNK // SUBLANE) - 1
    const = lambda a: pl.BlockSpec(a.shape, lambda c: (0,) * a.ndim)

    def tiles(rows, row_index):
        return ([pl.BlockSpec((batch, rows, width), lambda c, g=g: (0, row_index(c), rcol + g)) for g in range(3)]
                + [pl.BlockSpec((batch, rows, TAIL_COLS), lambda c: (0, row_index(c), 0))])

    u_args = (u_main, u_main, u_main, u_tail)
    return pl.pallas_call(
        _rwkv_chunk_kernel,
        grid=(nchunks,),
        in_specs=tiles(CHUNK, nxt) + tiles(SUBLANE, prev_tile) + tiles(CHUNK, lambda c: 0)
        + [const(a) for a in prep_params] + [const(r_k), const(gn_g), const(gn_b)],
        out_specs=pl.BlockSpec((batch, CHUNK, width), lambda c: (0, c, 0)),
        out_shape=jax.ShapeDtypeStruct((batch, seq, width), F32),
        scratch_shapes=[pltpu.VMEM((batch * width // QUAD, QUAD, QUAD), F32),
                        pltpu.VMEM((7, batch, CHUNK, width), F32)],
        compiler_params=_cparams(("arbitrary",)),
        name="rwkv_chunk",
    )(*u_args, *u_args, *u_args, *prep_params, r_k, gn_g, gn_b)


def _row_pipelined(n_rows, matmul, finish):
    pending = None
    for r0 in range(0, n_rows, SUB_ROWS):
        rows = slice(r0, r0 + SUB_ROWS)
        acc = matmul(rows)
        if pending is not None:
            finish(*pending)
        pending = (rows, acc)
    finish(*pending)


def _out_ln_kernel(h_ref, att_ref, rw_ref, wa_ref, wr_ref, g_ref, b_ref, o_ref):
    def matmul(rows):
        return (_dot(att_ref[rows, :].astype(BF16), wa_ref[...])
                + _dot(rw_ref[rows, :].astype(BF16), wr_ref[...]))

    def finish(rows, mix):
        o_ref[rows, :] = _layer_norm(ALPHA * h_ref[rows, :] + mix, g_ref[...], b_ref[...])

    _row_pipelined(h_ref.shape[0], matmul, finish)


def _out_ln(h, att, rw, w_out, g, b, tm=512):
    t, d = h.shape
    wa, wr = att.shape[1], rw.shape[1]
    assert wa == wr and w_out.shape[0] == wa + wr
    row = lambda n: pl.BlockSpec((tm, n), lambda i: (i, 0))
    full = lambda shape: pl.BlockSpec(shape, lambda i: (0,) * len(shape))
    return pl.pallas_call(
        _out_ln_kernel,
        grid=(t // tm,),
        in_specs=[row(d), row(wa), row(wr),
                  pl.BlockSpec((wa, d), lambda i: (0, 0)), pl.BlockSpec((wr, d), lambda i: (1, 0)),
                  full((1, d)), full((1, d))],
        out_specs=row(d),
        out_shape=jax.ShapeDtypeStruct((t, d), F32),
        compiler_params=_cparams(("parallel",)),
        name="out_ln",
    )(h, att, rw, w_out, w_out, g, b)


def _ple_ln_kernel(h_ref, p_ref, wg_ref, bg_ref, wp_ref, g_ref, b_ref, o_ref):
    def matmul(rows):
        return (_dot(h_ref[rows, :].astype(BF16), wg_ref[...]), _dot(p_ref[rows, :].astype(BF16), wp_ref[...]))

    def finish(rows, acc):
        pre_gate, e = acc
        y = ALPHA * h_ref[rows, :] + _sigmoid(pre_gate + bg_ref[...]) * e
        o_ref[rows, :] = _layer_norm(y, g_ref[...], b_ref[...])

    _row_pipelined(h_ref.shape[0], matmul, finish)


def _ple_ln(h, p, w_gate, b_gate, w_up, g, b, tm=512):
    t, d = h.shape
    row = lambda n: pl.BlockSpec((tm, n), lambda i: (i, 0))
    full = lambda shape: pl.BlockSpec(shape, lambda i: (0,) * len(shape))
    return pl.pallas_call(
        _ple_ln_kernel,
        grid=(t // tm,),
        in_specs=[row(d), row(p.shape[1]), full(w_gate.shape), full((1, d)), full(w_up.shape),
                  full((1, d)), full((1, d))],
        out_specs=row(d),
        out_shape=jax.ShapeDtypeStruct((t, d), F32),
        compiler_params=_cparams(("parallel",)),
        name="ple_ln",
    )(h, p, w_gate, b_gate, w_up, g, b)


def _pad_rows(w, rows):
    return jnp.pad(w, ((0, rows - w.shape[0]), (0, 0)))


def _pack_tail(cols, axis):
    wd, ad, gd = jnp.split(cols, [DECAY_LORA, DECAY_LORA + AAA_LORA], axis=axis)

    def pad(x, n):
        widths = [(0, 0)] * x.ndim
        widths[axis] = (0, n - x.shape[axis])
        return jnp.pad(x, widths)

    return jnp.concatenate([pad(wd, LANE), pad(ad, LANE), pad(gd, GD_PAD)], axis=axis)


def kernel(x, p, ffn1_w_gate, ffn1_w_up, ffn1_w_down, ln1_g, ln1_b, w_in, rel_bias, shift_mix, decay_w0, decay_w2, a_a0, a_a2, gate_g2, k_k, k_a, r_k, gn_g, gn_b, w_out, ln2_g, ln2_b, ffn2_w_gate, ffn2_w_up, ffn2_w_down, ln3_g, ln3_b, ple_w_up, ple_w_gate, ple_b_gate, ln4_g, ln4_b):
    batch, seq, d = x.shape
    t = batch * seq
    row = lambda a: a.reshape(1, -1)
    main_cols = 3 * ATT_WIDTH + 3 * RWKV_WIDTH

    heads = jnp.arange(QUAD) // HEAD_DIM
    esum = (heads[:, None] == heads[None, :]).astype(BF16)
    tab = _bias_tab(rel_bias)

    h = x.reshape(t, d).astype(F32)
    for i in range(DEPTH):
        w_in_t = w_in[i].T
        later_weights = (ffn2_w_gate[i], ffn2_w_up[i], ffn2_w_down[i], w_out[i], ple_w_gate[i], ple_w_up[i])
        ln1 = (row(ln1_g[i]), row(ln1_b[i]))
        head, (head_bf16,), ffn1_bf16 = _ffn_ln(h, ffn1_w_gate[i], ffn1_w_up[i], ffn1_w_down[i], *ln1,
                                                emit_bf16=True, copy_weights=True, tm=FFN_HEAD_TM, tf=FFN_HEAD_TF)
        h, (h_bf16,), _ = _ffn_ln(h, *ffn1_bf16, *ln1, emit_bf16=True, head=(head, head_bf16))
        u_main, u_tail = _proj(h_bf16, w_in_t, main_cols, _pack_tail(w_in_t[main_cols:], axis=0))
        mix = shift_mix[i]
        prep_params = (row(mix[:3 * RWKV_WIDTH]), row(_pack_tail(mix[3 * RWKV_WIDTH:], axis=0)),
                       row(decay_w0[i]), _pad_rows(decay_w2[i], LANE).astype(BF16), row(a_a0[i]),
                       _pad_rows(a_a2[i], LANE).astype(BF16), _pad_rows(gate_g2[i], GD_PAD).astype(BF16),
                       row(k_k[i]), row(k_a[i]), esum)
        u_main3 = u_main.reshape(batch, seq, main_cols)
        att, (wg2, wu2, wd2, wo, wpg, wpu) = _moba(rel_bias, u_main3, tab, to_cast=later_weights)
        rw = _rwkv(u_main3, u_tail.reshape(batch, seq, TAIL_COLS), prep_params,
                   row(r_k[i]), row(gn_g[i]), row(gn_b[i]))
        h = _out_ln(h, att.reshape(t, ATT_WIDTH), rw.reshape(t, RWKV_WIDTH), wo,
                    row(ln2_g[i]), row(ln2_b[i]))
        h, _, _ = _ffn_ln(h, wg2, wu2, wd2, row(ln3_g[i]), row(ln3_b[i]))
        h = _ple_ln(h, p[i].reshape(t, PLE_DIM).astype(F32), wpg, row(ple_b_gate[i]), wpu,
                    row(ln4_g[i]), row(ln4_b[i]))
    return h.reshape(batch, seq, d).astype(x.dtype)
```
